```python
import jax
import jax.numpy as jnp
from jax import lax
import numpy as np

D_MODEL = 1024
BATCH = 8
SEQ = 4096
DEPTH = 2
DEC_BATCH = 8
DEC_SEQ = 16
PAST_LEN = 1024

CHUNK = 64
Q_BLOCK = 128
EPS = 1e-6
HEAD_DIM = 64
N_GROUPS = 4
GROUP_WIDTH = D_MODEL // N_GROUPS
MIX_WIDTH = N_GROUPS * GROUP_WIDTH
N_HEADS = GROUP_WIDTH // HEAD_DIM
ROT_DIM = HEAD_DIM // 4
ROPE_THETA = 500000.0
A_BAND = 8 * CHUNK
REL_CLIP = 128
MLA_Q_LORA = 384
MLA_KV_LORA = 128
MLA_NOPE = 64
MLA_ROPE = 32
MLA_V = HEAD_DIM
IDX_HEADS = 8
IDX_DIM = 32
DSA_TOPK = 256
FORGET_BIAS = 3.0
PLE_DIM = 256
N_QK_GAINS = 8

IN_SPLITS = (
    ('a_q', N_HEADS * HEAD_DIM), ('a_k', N_HEADS * HEAD_DIM), ('a_v', N_HEADS * HEAD_DIM), ('a_g', GROUP_WIDTH),
    ('b_cq', MLA_Q_LORA), ('b_ckv', MLA_KV_LORA), ('b_kr', MLA_ROPE), ('b_g', GROUP_WIDTH),
    ('c_q', N_HEADS * HEAD_DIM), ('c_k', HEAD_DIM), ('c_v', HEAD_DIM), ('c_iq', IDX_HEADS * IDX_DIM),
    ('c_ik', IDX_DIM), ('c_iw', IDX_HEADS), ('c_g', GROUP_WIDTH),
    ('d_q', N_HEADS * HEAD_DIM), ('d_k', N_HEADS * HEAD_DIM), ('d_v', N_HEADS * HEAD_DIM), ('d_f', N_HEADS), ('d_g', GROUP_WIDTH),
)
IN_WIDTH = sum(n for _, n in IN_SPLITS)

kernel_name = 'hybrid_streaming_encoder_step'


def rms_norm(x, g):
    xf = x.astype(jnp.float32)
    y = xf * lax.rsqrt(jnp.mean(jnp.square(xf), axis=-1, keepdims=True) + EPS)
    return (y * g.astype(jnp.float32)).astype(x.dtype)


def split_heads(t, n):
    return t.reshape(t.shape[:-1] + (n, t.shape[-1] // n))


def split_columns(z):
    out, off = {}, 0
    for name, n in IN_SPLITS:
        out[name] = z[..., off:off + n]
        off += n
    return out


def rotary(x, pos, rot_dim):
    half = rot_dim // 2
    inv = jnp.float32(ROPE_THETA) ** (-jnp.arange(half, dtype=jnp.float32) / half)
    ang = pos.astype(jnp.float32)[:, None] * inv[None, :]
    cos = jnp.cos(ang)[None, :, None, :]
    sin = jnp.sin(ang)[None, :, None, :]
    xr = x[..., :rot_dim].astype(jnp.float32)
    x1, x2 = xr[..., :half], xr[..., half:]
    rot = jnp.concatenate([x1 * cos - x2 * sin, x2 * cos + x1 * sin], axis=-1).astype(x.dtype)
    return jnp.concatenate([rot, x[..., rot_dim:]], axis=-1)


def masked_softmax(s, mask):
    return jax.nn.softmax(jnp.where(mask, s.astype(jnp.float32), -jnp.inf), axis=-1)


def sweep_query_blocks(fn, q_pos, *q_side):
    T = q_pos.shape[0]
    blk = min(Q_BLOCK, T)
    nb = T // blk
    qs = tuple(jnp.moveaxis(a.reshape((a.shape[0], nb, blk) + a.shape[2:]), 1, 0) for a in q_side)
    out = lax.map(lambda args: fn(*args), (q_pos.reshape(nb, blk),) + qs)
    out = jnp.moveaxis(out, 0, 1)
    return out.reshape((out.shape[0], T) + out.shape[3:])


def chunk_band_attention(q, k, v, past_k, past_v, rel_bias):
    B, T, H, Dh = q.shape
    Pa = past_k.shape[1]
    n_chunks = -(-T // CHUNK)
    pad = n_chunks * CHUNK - T
    width = A_BAND + CHUNK

    def band_source(new, past):
        front = jnp.zeros((B, A_BAND - Pa, H, Dh), new.dtype)
        back = jnp.zeros((B, pad, H, Dh), new.dtype)
        return jnp.concatenate([front, past.astype(new.dtype), new, back], axis=1)

    band_idx = jnp.arange(n_chunks)[:, None] * CHUNK + jnp.arange(width)[None, :]
    kb = band_source(k, past_k)[:, band_idx]
    vb = band_source(v, past_v)[:, band_idx]
    valid = (band_idx >= A_BAND - Pa) & (band_idx < A_BAND + T)
    qc = jnp.pad(q, ((0, 0), (0, pad), (0, 0), (0, 0))).reshape(B, n_chunks, CHUNK, H, Dh)
    rel = jnp.clip(jnp.arange(CHUNK)[:, None] + A_BAND - jnp.arange(width)[None, :], -REL_CLIP, REL_CLIP) + REL_CLIP
    bias = rel_bias.astype(jnp.float32)[:, rel]
    s = jnp.einsum('bcqhd,bckhd->bchqk', qc, kb).astype(jnp.float32) * Dh ** -0.5 + bias[None, None]
    p = masked_softmax(s, valid[None, :, None, None, :])
    o = jnp.einsum('bchqk,bckhd->bcqhd', p.astype(v.dtype), vb)
    return o.reshape(B, n_chunks * CHUNK, H, Dh)[:, :T]


def latent_attention(q_nope, q_rope, q_pos, k_nope, k_rope, v, k_pos):
    scale = (MLA_NOPE + MLA_ROPE) ** -0.5
    k_chunk = k_pos // CHUNK

    def block(qp, qn, qr):
        s = (jnp.einsum('bqhd,bkhd->bhqk', qn, k_nope) + jnp.einsum('bqhr,bkr->bhqk', qr, k_rope)).astype(jnp.float32) * scale
        p = masked_softmax(s, k_chunk[None, :] <= (qp // CHUNK)[:, None])
        return jnp.einsum('bhqk,bkhd->bqhd', p.astype(v.dtype), v)

    return sweep_query_blocks(block, q_pos, q_nope, q_rope)


def indexed_sparse_attention(q, iq, iw, q_pos, k, v, ik, k_pos, k_sel):
    k_chunk = k_pos // CHUNK

    def block(qp, qb, iqb, iwb):
        logits = jnp.einsum('bqhd,bkd->bqhk', iqb, ik).astype(jnp.float32) * IDX_DIM ** -0.5
        score = jnp.einsum('bqh,bqhk->bqk', iwb.astype(jnp.float32) * IDX_HEADS ** -0.5, jax.nn.relu(logits))
        admissible = k_chunk[None, :] <= (qp // CHUNK)[:, None]
        score = jnp.where(admissible[None], score, -jnp.inf)
        top_val, top_idx = lax.top_k(score, k_sel)
        kg = jax.vmap(lambda kk, ii: kk[ii])(k, top_idx)
        vg = jax.vmap(lambda vv, ii: vv[ii])(v, top_idx)
        s = jnp.einsum('bqhd,bqkd->bhqk', qb, kg).astype(jnp.float32) * HEAD_DIM ** -0.5
        p = masked_softmax(s, jnp.isfinite(top_val)[:, None])
        return jnp.einsum('bhqk,bqkd->bqhd', p.astype(v.dtype), vg)

    return sweep_query_blocks(block, q_pos, q, iq, iw)


def forgetting_attention(q, q_pos, f_q, k, v, f_k, k_pos):
    scale = HEAD_DIM ** -0.5
    f_kt = jnp.moveaxis(f_k, 2, 1)

    def block(qp, qb, fqb):
        s = jnp.einsum('bqhd,bkhd->bhqk', qb, k).astype(jnp.float32) * scale
        s = s + jnp.moveaxis(fqb, 2, 1)[..., None] - f_kt[:, :, None, :]
        p = masked_softmax(s, k_pos[None, :] <= qp[:, None])
        return jnp.einsum('bhqk,bkhd->bqhd', p.astype(v.dtype), v)

    return sweep_query_blocks(block, q_pos, q, f_q)


def trunk_layer(x, p, past, w):
    pa_k, pa_v, pb_ckv, pb_kr, pc_k, pc_v, pc_ik, pd_k, pd_v, pd_lf = past
    B, T, _ = x.shape
    P = pb_ckv.shape[1]
    q_pos = P + jnp.arange(T)
    k_pos = jnp.arange(P + T)
    g = w['g_qk']
    cat = lambda a, b: jnp.concatenate([a.astype(b.dtype), b], axis=1)
    z = split_columns(rms_norm(x, w['g_in']) @ w['w_in'])

    a_q = rms_norm(split_heads(z['a_q'], N_HEADS), g[0])
    a_k = rms_norm(split_heads(z['a_k'], N_HEADS), g[1])
    a_v = split_heads(z['a_v'], N_HEADS)
    y_a = chunk_band_attention(a_q, a_k, a_v, pa_k, pa_v, w['a_rel_bias'])

    cq = rms_norm(z['b_cq'], w['b_g_cq'])
    qb = split_heads(cq @ w['b_w_uq'], N_HEADS)
    qb_nope = rms_norm(qb[..., :MLA_NOPE], g[6])
    qb_rope = rotary(rms_norm(qb[..., MLA_NOPE:], w['g_rope'][0]), q_pos, MLA_ROPE)
    b_ckv = rms_norm(z['b_ckv'], w['b_g_ckv'])
    b_kr = rotary(rms_norm(z['b_kr'], w['g_rope'][1])[:, :, None], q_pos, MLA_ROPE)[:, :, 0]
    kr_all = cat(pb_kr, b_kr)
    kv = split_heads(cat(pb_ckv, b_ckv) @ w['b_w_ukv'], N_HEADS)
    kb_nope = rms_norm(kv[..., :MLA_NOPE], g[7])
    y_b = latent_attention(qb_nope, qb_rope, q_pos, kb_nope, kr_all, kv[..., MLA_NOPE:], k_pos)

    c_q = rotary(rms_norm(split_heads(z['c_q'], N_HEADS), g[2]), q_pos, ROT_DIM)
    c_k = rotary(rms_norm(z['c_k'], g[3])[:, :, None], q_pos, ROT_DIM)[:, :, 0]
    c_v = z['c_v']
    c_ik = z['c_ik']
    k_sel = min(DSA_TOPK, (P + T) // 4)
    y_c = indexed_sparse_attention(c_q, split_heads(z['c_iq'], IDX_HEADS), z['c_iw'], q_pos,
                                   cat(pc_k, c_k), cat(pc_v, c_v), cat(pc_ik, c_ik), k_pos, k_sel)

    d_q = rms_norm(split_heads(z['d_q'], N_HEADS), g[4])
    d_k = rms_norm(split_heads(z['d_k'], N_HEADS), g[5])
    d_v = split_heads(z['d_v'], N_HEADS)
    d_lf = jax.nn.log_sigmoid(z['d_f'].astype(jnp.float32) + w['d_b_f'].astype(jnp.float32))
    f_cum = jnp.cumsum(cat(pd_lf, d_lf), axis=1)
    y_d = forgetting_attention(d_q, q_pos, f_cum[:, P:], cat(pd_k, d_k), cat(pd_v, d_v), f_cum, k_pos)

    mix = jnp.concatenate([
        y_a.reshape(B, T, GROUP_WIDTH) * jax.nn.silu(z['a_g']),
        y_b.reshape(B, T, GROUP_WIDTH) * jax.nn.silu(z['b_g']),
        y_c.reshape(B, T, GROUP_WIDTH) * jax.nn.silu(z['c_g']),
        y_d.reshape(B, T, GROUP_WIDTH) * jax.nn.silu(z['d_g']),
    ], axis=-1).astype(x.dtype)
    x = x + mix @ w['w_out']
    x = x + jax.nn.sigmoid(rms_norm(x, w['g_ple']) @ w['w_ple_gate']) * (p @ w['w_ple_proj'])
    return x, (a_k, a_v, b_ckv, b_kr, c_k, c_v, c_ik, d_k, d_v, d_lf)


def setup_inputs(seed: int = 0) -> dict:
    key = jax.random.key(seed)
    ks = iter(jax.random.split(key, 40))

    def nrm(shape, scale=1.0):
        return scale * jax.random.normal(next(ks), shape, jnp.float32)

    a_past = min(A_BAND, PAST_LEN)
    return {
        'x_prompt': nrm((BATCH, SEQ, D_MODEL)),
        'x_sample': nrm((DEC_BATCH, DEC_SEQ, D_MODEL)),
        'cache_a_k': nrm((DEPTH, DEC_BATCH, a_past, N_HEADS, HEAD_DIM)),
        'cache_a_v': nrm((DEPTH, DEC_BATCH, a_past, N_HEADS, HEAD_DIM)),
        'cache_b_ckv': nrm((DEPTH, DEC_BATCH, PAST_LEN, MLA_KV_LORA)),
        'cache_b_krope': nrm((DEPTH, DEC_BATCH, PAST_LEN, MLA_ROPE)),
        'cache_c_k': nrm((DEPTH, DEC_BATCH, PAST_LEN, HEAD_DIM)),
        'cache_c_v': nrm((DEPTH, DEC_BATCH, PAST_LEN, HEAD_DIM)),
        'cache_c_idx_k': nrm((DEPTH, DEC_BATCH, PAST_LEN, IDX_DIM)),
        'cache_d_k': nrm((DEPTH, DEC_BATCH, PAST_LEN, N_HEADS, HEAD_DIM)),
        'cache_d_v': nrm((DEPTH, DEC_BATCH, PAST_LEN, N_HEADS, HEAD_DIM)),
        'cache_d_logf': jax.nn.log_sigmoid(FORGET_BIAS + nrm((DEPTH, DEC_BATCH, PAST_LEN, N_HEADS))),
        'p_prompt': nrm((DEPTH, BATCH, SEQ, PLE_DIM)),
        'p_sample': nrm((DEPTH, DEC_BATCH, DEC_SEQ, PLE_DIM)),
        'g_in': 1.0 + nrm((DEPTH, D_MODEL), 0.1),
        'w_in': nrm((DEPTH, D_MODEL, IN_WIDTH), D_MODEL ** -0.5),
        'g_qk': 1.0 + nrm((DEPTH, N_QK_GAINS, HEAD_DIM), 0.1),
        'g_rope': 1.0 + nrm((DEPTH, 2, MLA_ROPE), 0.1),
        'a_rel_bias': nrm((DEPTH, N_HEADS, 2 * REL_CLIP + 1), 0.2),
        'b_g_cq': 1.0 + nrm((DEPTH, MLA_Q_LORA), 0.1),
        'b_w_uq': nrm((DEPTH, MLA_Q_LORA, N_HEADS * (MLA_NOPE + MLA_ROPE)), MLA_Q_LORA ** -0.5),
        'b_g_ckv': 1.0 + nrm((DEPTH, MLA_KV_LORA), 0.1),
        'b_w_ukv': nrm((DEPTH, MLA_KV_LORA, N_HEADS * (MLA_NOPE + MLA_V)), MLA_KV_LORA ** -0.5),
        'd_b_f': FORGET_BIAS + nrm((DEPTH, N_HEADS), 0.5),
        'w_out': nrm((DEPTH, MIX_WIDTH, D_MODEL), MIX_WIDTH ** -0.5),
        'g_ple': 1.0 + nrm((DEPTH, D_MODEL), 0.1),
        'w_ple_gate': nrm((DEPTH, D_MODEL, D_MODEL), D_MODEL ** -0.5),
        'w_ple_proj': nrm((DEPTH, PLE_DIM, D_MODEL), PLE_DIM ** -0.5),
    }


def reference(x_prompt, x_sample, cache_a_k, cache_a_v, cache_b_ckv, cache_b_krope, cache_c_k, cache_c_v,
              cache_c_idx_k, cache_d_k, cache_d_v, cache_d_logf, p_prompt, p_sample, g_in, w_in, g_qk, g_rope,
              a_rel_bias, b_g_cq, b_w_uq, b_g_ckv, b_w_ukv, d_b_f, w_out, g_ple, w_ple_gate, w_ple_proj):
    B, T = x_prompt.shape[:2]
    dt = x_prompt.dtype
    fresh = (
        jnp.zeros((B, 0, N_HEADS, HEAD_DIM), dt), jnp.zeros((B, 0, N_HEADS, HEAD_DIM), dt),
        jnp.zeros((B, 0, MLA_KV_LORA), dt), jnp.zeros((B, 0, MLA_ROPE), dt),
        jnp.zeros((B, 0, HEAD_DIM), dt), jnp.zeros((B, 0, HEAD_DIM), dt), jnp.zeros((B, 0, IDX_DIM), dt),
        jnp.zeros((B, 0, N_HEADS, HEAD_DIM), dt), jnp.zeros((B, 0, N_HEADS, HEAD_DIM), dt),
        jnp.zeros((B, 0, N_HEADS), jnp.float32),
    )
    xp, xs = x_prompt, x_sample
    states_p, states_s = [], []
    for i in range(DEPTH):
        w = {
            'g_in': g_in[i], 'w_in': w_in[i], 'g_qk': g_qk[i], 'g_rope': g_rope[i], 'a_rel_bias': a_rel_bias[i],
            'b_g_cq': b_g_cq[i], 'b_w_uq': b_w_uq[i], 'b_g_ckv': b_g_ckv[i], 'b_w_ukv': b_w_ukv[i],
            'd_b_f': d_b_f[i], 'w_out': w_out[i], 'g_ple': g_ple[i], 'w_ple_gate': w_ple_gate[i],
            'w_ple_proj': w_ple_proj[i],
        }
        xp, st_p = trunk_layer(xp, p_prompt[i], fresh, w)
        past = (cache_a_k[i], cache_a_v[i], cache_b_ckv[i], cache_b_krope[i], cache_c_k[i], cache_c_v[i],
                cache_c_idx_k[i], cache_d_k[i], cache_d_v[i], cache_d_logf[i])
        xs, st_s = trunk_layer(xs, p_sample[i], past, w)
        states_p.append(st_p)
        states_s.append(st_s)
    (a_k_p, a_v_p, b_ckv_p, b_krope_p, c_k_p, c_v_p, c_idx_k_p, d_k_p, d_v_p, d_logf_p) = [jnp.stack(t) for t in zip(*states_p)]
    (a_k_s, a_v_s, b_ckv_s, b_krope_s, c_k_s, c_v_s, c_idx_k_s, d_k_s, d_v_s, d_logf_s) = [jnp.stack(t) for t in zip(*states_s)]
    keep = min(A_BAND, T)
    return (xp, xs,
            a_k_p[:, :, T - keep:], a_v_p[:, :, T - keep:], b_ckv_p, b_krope_p, c_k_p, c_v_p, c_idx_k_p, d_k_p, d_v_p, d_logf_p,
            a_k_s, a_v_s, b_ckv_s, b_krope_s, c_k_s, c_v_s, c_idx_k_s, d_k_s, d_v_s, d_logf_s)
```

```python
import functools

import numpy as np
import jax
import jax.numpy as jnp
from jax import lax
from jax.experimental import pallas as pl
from jax.experimental.pallas import tpu as pltpu

F32 = jnp.float32
BF16 = jnp.bfloat16
I32 = jnp.int32

D_MODEL = 1024
CHUNK = 64
EPS = 1e-6
HEAD_DIM = 64
N_HEADS = 4
GROUP_WIDTH = 256
ROT_DIM = 16
ROPE_THETA = 500000.0
A_BAND = 8 * CHUNK
REL_CLIP = 128
MLA_Q_LORA = 384
MLA_KV_LORA = 128
MLA_NOPE = 64
MLA_ROPE = 32
IDX_HEADS = 8
IDX_DIM = 32
DSA_TOPK = 256
PLE_DIM = 256

_REF_SPLITS = (
    ('a_q', 256), ('a_k', 256), ('a_v', 256), ('a_g', 256),
    ('b_cq', 384), ('b_ckv', 128), ('b_kr', 32), ('b_g', 256),
    ('c_q', 256), ('c_k', 64), ('c_v', 64), ('c_iq', 256), ('c_ik', 32), ('c_iw', 8), ('c_g', 256),
    ('d_q', 256), ('d_k', 256), ('d_v', 256), ('d_f', 4), ('d_g', 256),
)
_REF_OFF = {}
_o = 0
for _n, _w in _REF_SPLITS:
    _REF_OFF[_n] = (_o, _w)
    _o += _w

_MY_ORDER = ('a_q', 'a_k', 'a_v', 'a_g', 'b_cq', 'b_ckv', 'b_g', 'c_q', 'c_iq', 'c_g', 'd_q', 'd_k', 'd_v', 'd_g',
             'c_k', 'c_v', 'b_kr', 'c_ik', 'c_iw', 'pad56')
_MY_OFF = {}
_o = 0
for _n in _MY_ORDER:
    _w = 56 if _n == 'pad56' else _REF_OFF[_n][1]
    _MY_OFF[_n] = _o
    _o += _w
W_IN_COLS = _o
S1_OFF = _MY_OFF['c_k']
S2_OFF = _MY_OFF['b_kr']
IW_LANE = 64

TAB_QB, TAB_CQ, TAB_S1, TAB_S2 = 0, 1536, 2304, 2688
TAB_W = 3072

NEG = -1e30
INT_MIN = -2 ** 31
VMEM_LIMIT = 56 * 1024 * 1024


def _cparams(n_axes):
    return pltpu.CompilerParams(dimension_semantics=("arbitrary",) * n_axes, vmem_limit_bytes=VMEM_LIMIT)


def _dot(a, b):
    return jnp.dot(a, b, preferred_element_type=F32)


def _dot_nt(a, b):
    return lax.dot_general(a, b, (((1,), (1,)), ((), ())), preferred_element_type=F32)


def _seg_mean_sq(t, seg):
    sq = t * t
    hi = sq.astype(BF16)
    lo = (sq - hi.astype(F32)).astype(BF16)
    return _dot(hi, seg) + _dot(lo, seg)


def _rotate(t, cos, sin_up, sin_dn, half):
    w = t.shape[-1]
    return t * cos + pltpu.roll(t, half, 1) * sin_up + pltpu.roll(t, w - half, 1) * sin_dn


def _silu(g):
    return g * (1.0 / (1.0 + jnp.exp(-g)))


def _log_sigmoid(v):
    return jnp.minimum(v, 0.0) - jnp.log1p(jnp.exp(-jnp.abs(v)))


def _inproj_kernel(x_ref, tab_ref, gin_ref, w_ref, wdf_ref, bcol_ref, g4_ref, hseg_ref, gcq_ref, wuq_ref, segq_ref,
                   gqb_ref, gckv_ref, srow_ref,
                   aq_o, akb_o, avb_o, akf_o, avf_o, qb_o, ckvf_o, cq_o, ciq_o, s1_o, s2_o,
                   dq_o, dkb_o, dvb_o, dkf_o, dvf_o, lft_o, gate_o):
    x = x_ref[0]
    ms = jnp.mean(x * x, axis=-1, keepdims=True)
    xn = (x * lax.rsqrt(ms + EPS) * gin_ref[...]).astype(BF16)

    def proj(name, n):
        c0 = _MY_OFF[name]
        return _dot(xn, w_ref[:, c0:c0 + n])

    hseg = hseg_ref[...]

    def headnorm(t, row):
        return t * lax.rsqrt(_seg_mean_sq(t, hseg) + EPS) * g4_ref[row:row + 1, :]

    def fullnorm(t, g):
        return t * lax.rsqrt(jnp.mean(t * t, axis=-1, keepdims=True) + EPS) * g

    aq_o[0] = headnorm(proj('a_q', 256), 0).astype(BF16)
    ak = headnorm(proj('a_k', 256), 1)
    akf_o[0] = ak
    akb_o[0] = ak.astype(BF16)
    av = proj('a_v', 256)
    avf_o[0] = av
    avb_o[0] = av.astype(BF16)
    gate_o[0, :, 0:256] = _silu(proj('a_g', 256)).astype(BF16)

    cqn = fullnorm(proj('b_cq', MLA_Q_LORA), gcq_ref[...]).astype(BF16)
    qb = _dot(cqn, wuq_ref[...])
    segq = segq_ref[...]
    ms_q = jnp.concatenate([_seg_mean_sq(qb[:, :256], segq), _seg_mean_sq(qb[:, 256:], segq)], axis=1)
    qbn = qb * lax.rsqrt(ms_q + EPS) * gqb_ref[...]
    qbn = _rotate(qbn, tab_ref[:, TAB_QB:TAB_QB + 512], tab_ref[:, TAB_QB + 512:TAB_QB + 1024],
                  tab_ref[:, TAB_QB + 1024:TAB_QB + 1536], MLA_ROPE // 2)
    qb_o[0] = qbn.astype(BF16)
    ckvf_o[0] = fullnorm(proj('b_ckv', MLA_KV_LORA), gckv_ref[...])
    gate_o[0, :, 256:512] = _silu(proj('b_g', 256)).astype(BF16)

    cq = headnorm(proj('c_q', 256), 2)
    cq = _rotate(cq, tab_ref[:, TAB_CQ:TAB_CQ + 256], tab_ref[:, TAB_CQ + 256:TAB_CQ + 512],
                 tab_ref[:, TAB_CQ + 512:TAB_CQ + 768], ROT_DIM // 2)
    cq_o[0] = cq.astype(BF16)
    ciq_o[0] = proj('c_iq', 256).astype(BF16)
    gate_o[0, :, 512:768] = _silu(proj('c_g', 256)).astype(BF16)

    dq_o[0] = headnorm(proj('d_q', 256), 3).astype(BF16)
    dk = headnorm(proj('d_k', 256), 4)
    dkf_o[0] = dk
    dkb_o[0] = dk.astype(BF16)
    dv = proj('d_v', 256)
    dvf_o[0] = dv
    dvb_o[0] = dv.astype(BF16)
    gate_o[0, :, 768:1024] = _silu(proj('d_g', 256)).astype(BF16)

    lane = lax.broadcasted_iota(I32, (1, 128), 1)
    t = proj('c_k', 128)
    m64 = lane < HEAD_DIM
    ms1 = jnp.sum(jnp.where(m64, t * t, 0.0), axis=-1, keepdims=True) * (1.0 / HEAD_DIM)
    t = jnp.where(m64, t * lax.rsqrt(ms1 + EPS), t) * srow_ref[0:1, :]
    t = _rotate(t, tab_ref[:, TAB_S1:TAB_S1 + 128], tab_ref[:, TAB_S1 + 128:TAB_S1 + 256],
                tab_ref[:, TAB_S1 + 256:TAB_S1 + 384], ROT_DIM // 2)
    s1_o[0] = t
    t = proj('b_kr', 128)
    m32 = lane < MLA_ROPE
    ms2 = jnp.sum(jnp.where(m32, t * t, 0.0), axis=-1, keepdims=True) * (1.0 / MLA_ROPE)
    t = jnp.where(m32, t * lax.rsqrt(ms2 + EPS), t) * srow_ref[1:2, :]
    t = _rotate(t, tab_ref[:, TAB_S2:TAB_S2 + 128], tab_ref[:, TAB_S2 + 128:TAB_S2 + 256],
                tab_ref[:, TAB_S2 + 256:TAB_S2 + 384], MLA_ROPE // 2)
    s2_o[0] = t

    dft = _dot_nt(wdf_ref[...], xn)
    lft_o[0] = _log_sigmoid(dft[:8] + bcol_ref[...])


def _inproj_call(x, tab, lw, tm):
    bk, tk_, _ = x.shape
    grid = (tk_ // tm, bk)

    def const(shape):
        return pl.BlockSpec(shape, lambda i, b: (0,) * len(shape))

    def rows(w):
        return pl.BlockSpec((1, tm, w), lambda i, b: (b, i, 0))

    in_specs = [
        rows(D_MODEL),
        pl.BlockSpec((tm, TAB_W), lambda i, b: (i, 0)),
        const((1, D_MODEL)), const((D_MODEL, W_IN_COLS)), const((16, D_MODEL)), const((8, 1)),
        const((8, 256)), const((256, 256)), const((1, MLA_Q_LORA)), const((MLA_Q_LORA, 512)), const((256, 256)),
        const((1, 512)), const((1, MLA_KV_LORA)), const((8, 128)),
    ]
    widths = [(256, BF16), (256, BF16), (256, BF16), (256, F32), (256, F32), (512, BF16), (128, F32), (256, BF16),
              (256, BF16), (128, F32), (128, F32), (256, BF16), (256, BF16), (256, BF16), (256, F32), (256, F32)]
    out_shape = [jax.ShapeDtypeStruct((bk, tk_, w), dt) for w, dt in widths]
    out_specs = [rows(w) for w, _ in widths]
    out_shape.append(jax.ShapeDtypeStruct((bk, 8, tk_), F32))
    out_specs.append(pl.BlockSpec((1, 8, tm), lambda i, b: (b, 0, i)))
    out_shape.append(jax.ShapeDtypeStruct((bk, tk_, 1024), BF16))
    out_specs.append(rows(1024))
    return pl.pallas_call(
        _inproj_kernel, grid=grid, in_specs=in_specs, out_specs=out_specs, out_shape=out_shape,
        compiler_params=_cparams(2), name="inproj",
    )(x, tab, lw['g_in'], lw['w_in'], lw['w_dft'], lw['b_col'], lw['g4'], lw['hseg'], lw['g_cq'], lw['w_uq'],
      lw['segq'], lw['g_qb'], lw['g_ckv'], lw['srow'])


def _bkv_kernel(ckv_ref, kr_ref, wk_ref, wv_ref, e_ref, segq_ref, gk_ref, kb_o, vb_o):
    c = ckv_ref[0].astype(BF16)
    kn = _dot(c, wk_ref[...])
    segq = segq_ref[...]
    ms = jnp.concatenate([_seg_mean_sq(kn[:, :256], segq), _seg_mean_sq(kn[:, 256:], segq)], axis=1)
    kn = kn * lax.rsqrt(ms + EPS) * gk_ref[...]
    kr = _dot(kr_ref[0].astype(BF16), e_ref[...])
    kb_o[0] = (kn + kr).astype(BF16)
    vb_o[0] = _dot(c, wv_ref[...]).astype(BF16)


def _bkv_call(ckv, kr, lw, tm):
    b, lp, _ = ckv.shape

    def const(shape):
        return pl.BlockSpec(shape, lambda bb, i: (0,) * len(shape))

    def rows(w):
        return pl.BlockSpec((1, tm, w), lambda bb, i: (bb, i, 0))

    return pl.pallas_call(
        _bkv_kernel, grid=(b, lp // tm),
        in_specs=[rows(128), rows(128), const((128, 512)), const((128, 256)), const((128, 512)), const((256, 256)),
                  const((1, 512))],
        out_specs=[rows(512), rows(256)],
        out_shape=[jax.ShapeDtypeStruct((b, lp, 512), BF16), jax.ShapeDtypeStruct((b, lp, 256), BF16)],
        compiler_params=_cparams(2), name="mla_kv",
    )(ckv, kr, lw['w_uk'], lw['w_uv'], lw['e_kr'], lw['segq'], lw['g_kb'])


def _head_lane_id(width=256):
    return lax.broadcasted_iota(I32, (1, width), 1) // HEAD_DIM


def _keep_lanes(x, pred):
    return jnp.where(pred, x.astype(F32), 0.0).astype(BF16)


def _by_head(cols, hid):
    out = cols[N_HEADS - 1]
    for h in range(N_HEADS - 2, -1, -1):
        out = jnp.where(hid == h, cols[h], out)
    return out


def _softmax_step(h, s, valid, v_blk, m_ref, l_ref):
    s = jnp.where(valid, s, NEG)
    m_prev = m_ref[h]
    m_new = jnp.maximum(m_prev, jnp.max(s, axis=-1, keepdims=True))
    alpha = jnp.exp(m_prev - m_new)
    p = jnp.where(valid, jnp.exp(s - m_new), 0.0)
    l_ref[h] = alpha * l_ref[h] + jnp.sum(p, axis=-1, keepdims=True)
    m_ref[h] = m_new
    return alpha, _dot(p.astype(BF16), v_blk)


def _init_state(m_ref, l_ref, acc_ref):
    m_ref[...] = jnp.full(m_ref.shape, NEG, F32)
    l_ref[...] = jnp.zeros(l_ref.shape, F32)
    acc_ref[...] = jnp.zeros(acc_ref.shape, F32)


def _finish(o_ref, g_ref, l_ref, acc_ref, hid):
    l_full = _by_head([l_ref[h] for h in range(N_HEADS)], hid)
    y = acc_ref[...] / l_full
    o_ref[0] = (y * g_ref[0].astype(F32)).astype(BF16)


def _attn_scratch(tq):
    return [pltpu.VMEM((N_HEADS, tq, 1), F32), pltpu.VMEM((N_HEADS, tq, 1), F32), pltpu.VMEM((tq, 256), F32)]


A_WIN = A_BAND + 2 * CHUNK


def _attn_a_kernel(q_ref, k_ref, v_ref, bias_ref, g_ref, o_ref, *, lo_valid, hi_valid):
    c = pl.program_id(1)
    start = pl.multiple_of(c * CHUNK, CHUNK)
    q = q_ref[0]
    hid = _head_lane_id()
    qs = jnp.concatenate([_keep_lanes(q, hid == h) for h in range(N_HEADS)], axis=0)
    kb = k_ref[0, pl.ds(start, A_WIN), :]
    vb = v_ref[0, pl.ds(start, A_WIN), :]
    s = _dot_nt(qs, kb) + bias_ref[...]
    row = start + lax.broadcasted_iota(I32, (1, A_WIN), 1)
    valid = (row >= lo_valid) & (row < hi_valid)
    s = jnp.where(valid, s, NEG)
    m = jnp.max(s, axis=-1, keepdims=True)
    p = jnp.where(valid, jnp.exp(s - m), 0.0)
    l = jnp.sum(p, axis=-1, keepdims=True)
    o = _dot(p.astype(BF16), vb) / l
    y = o[(N_HEADS - 1) * CHUNK:]
    for h in range(N_HEADS - 2, -1, -1):
        y = jnp.where(hid == h, o[h * CHUNK:(h + 1) * CHUNK], y)
    o_ref[0] = (y * g_ref[0].astype(F32)).astype(BF16)


def _attn_a_call(q, kfull, vfull, bias, gates, lo_valid, hi_valid):
    b, tp, _ = q.shape
    rows_kv = kfull.shape[1]
    kern = functools.partial(_attn_a_kernel, lo_valid=lo_valid, hi_valid=hi_valid)
    return pl.pallas_call(
        kern, grid=(b, tp // CHUNK),
        in_specs=[pl.BlockSpec((1, CHUNK, 256), lambda bb, c: (bb, c, 0)),
                  pl.BlockSpec((1, rows_kv, 256), lambda bb, c: (bb, 0, 0)),
                  pl.BlockSpec((1, rows_kv, 256), lambda bb, c: (bb, 0, 0)),
                  pl.BlockSpec((N_HEADS * CHUNK, A_WIN), lambda bb, c: (0, 0)),
                  pl.BlockSpec((1, CHUNK, 256), lambda bb, c: (bb, c, 0))],
        out_specs=pl.BlockSpec((1, CHUNK, 256), lambda bb, c: (bb, c, 0)),
        out_shape=jax.ShapeDtypeStruct((b, tp, 256), BF16),
        compiler_params=_cparams(2), name="attn_band",
    )(q, kfull, vfull, bias, gates)


def _attn_b_kernel(q_ref, k_ref, v_ref, g_ref, o_ref, m_ref, l_ref, acc_ref, *, p_len, l_len, tq, tk, nq):
    i = pl.program_id(1) if nq > 1 else 0
    q0 = p_len + i * tq
    hid = _head_lane_id()
    lane = lax.broadcasted_iota(I32, (1, 256), 1)
    qm = []
    for h in range(N_HEADS):
        qg = q_ref[0, :, (h // 2) * 256:(h // 2) * 256 + 256]
        lo = (h % 2) * (MLA_NOPE + MLA_ROPE)
        qm.append(_keep_lanes(qg, (lane >= lo) & (lane < lo + MLA_NOPE + MLA_ROPE)))
    _init_state(m_ref, l_ref, acc_ref)
    qchunk = (q0 + lax.broadcasted_iota(I32, (tq, 1), 0)) // CHUNK
    nkb = jnp.minimum((q0 + tq + tk - 1) // tk, k_ref.shape[1] // tk)

    def body(j, carry):
        ks = pl.multiple_of(j * tk, tk)
        kpos = ks + lax.broadcasted_iota(I32, (1, tk), 1)
        valid = ((kpos // CHUNK) <= qchunk) & (kpos < l_len)
        vblk = v_ref[0, pl.ds(ks, tk), :]
        alphas, pvs = [], []
        for h in range(N_HEADS):
            kblk = k_ref[0, pl.ds(ks, tk), (h // 2) * 256:(h // 2) * 256 + 256]
            a, pv = _softmax_step(h, _dot_nt(qm[h], kblk), valid, vblk, m_ref, l_ref)
            alphas.append(a)
            pvs.append(pv)
        acc_ref[...] = acc_ref[...] * _by_head(alphas, hid) + _by_head(pvs, hid)
        return carry

    lax.fori_loop(0, nkb, body, 0)
    _finish(o_ref, g_ref, l_ref, acc_ref, hid)


def _attn_b_call(qb, kb, vb, gates, p_len, l_len, tq, tk):
    b, t, _ = qb.shape
    lp = kb.shape[1]
    nq = t // tq
    kern = functools.partial(_attn_b_kernel, p_len=p_len, l_len=l_len, tq=tq, tk=tk, nq=nq)
    return pl.pallas_call(
        kern, grid=(b, nq),
        in_specs=[pl.BlockSpec((1, tq, 512), lambda bb, i: (bb, i, 0)),
                  pl.BlockSpec((1, lp, 512), lambda bb, i: (bb, 0, 0)),
                  pl.BlockSpec((1, lp, 256), lambda bb, i: (bb, 0, 0)),
                  pl.BlockSpec((1, tq, 256), lambda bb, i: (bb, i, 1))],
        out_specs=pl.BlockSpec((1, tq, 256), lambda bb, i: (bb, i, 0)),
        out_shape=jax.ShapeDtypeStruct((b, t, 256), BF16),
        scratch_shapes=_attn_scratch(tq),
        compiler_params=_cparams(2), name="attn_latent",
    )(qb, kb, vb, gates)


def _attn_c_kernel(q_ref, iq_ref, s2_ref, k_ref, v_ref, ik_ref, g_ref, o_ref,
                   key_ref, m_ref, l_ref, acc_ref, *, p_len, l_len, k_sel, tq, tk, nq):
    i = pl.program_id(1) if nq > 1 else 0
    q0 = p_len + i * tq
    nkb_max = k_ref.shape[1] // tk
    nkb = jnp.minimum((q0 + tq + tk - 1) // tk, nkb_max)
    qchunk = (q0 + lax.broadcasted_iota(I32, (tq, 1), 0)) // CHUNK
    hid = _head_lane_id()

    iq = iq_ref[0]
    ihid = lax.broadcasted_iota(I32, (1, 256), 1) // IDX_DIM
    iqm = [_keep_lanes(iq, ihid == h) for h in range(IDX_HEADS)]
    iw = s2_ref[0][:, IW_LANE:IW_LANE + IDX_HEADS] * (IDX_DIM ** -0.5 * IDX_HEADS ** -0.5)

    def score_body(j, carry):
        ks = pl.multiple_of(j * tk, tk)
        ikb = ik_ref[0, pl.ds(ks, tk), :]
        score = jnp.zeros((tq, tk), F32)
        for h in range(IDX_HEADS):
            score = score + iw[:, h:h + 1] * jnp.maximum(_dot_nt(iqm[h], ikb), 0.0)
        kpos = ks + lax.broadcasted_iota(I32, (1, tk), 1)
        adm = ((kpos // CHUNK) <= qchunk) & (kpos < l_len)
        bits = pltpu.bitcast(score, I32)
        key = jnp.where(bits < 0, bits ^ 0x7FFFFFFF, bits)
        key = jnp.where(key == -1, 0, key)
        key_ref[j] = jnp.where(adm, key, INT_MIN)
        return carry

    lax.fori_loop(0, nkb, score_body, 0)

    def count(pred_fn):
        def cbody(j, acc):
            pf = jnp.where(pred_fn(key_ref[j], j), 1.0, 0.0)
            for c in range(tk // 128):
                acc = acc + pf[:, c * 128:(c + 1) * 128]
            return acc
        acc = lax.fori_loop(0, nkb, cbody, jnp.zeros((tq, 128), F32))
        return jnp.sum(acc, axis=-1, keepdims=True)

    def bis_body(it, t):
        cand = t + jnp.left_shift(jnp.int32(1), 31 - it)
        cnt = count(lambda kb, j: kb >= cand)
        return jnp.where(cnt >= k_sel, cand, t)

    thr = lax.fori_loop(0, 32, bis_body, jnp.full((tq, 1), INT_MIN, I32))
    thr = jnp.maximum(thr, INT_MIN + 1)
    n_gt = count(lambda kb, j: kb > thr)
    need = k_sel - n_gt

    def idx_of(j):
        return j * tk + lax.broadcasted_iota(I32, (1, tk), 1)

    def tie_body(it, jc):
        cand = jc + jnp.left_shift(jnp.int32(1), 12 - it)
        cnt = count(lambda kb, j: (kb == thr) & (idx_of(j) < cand))
        return jnp.where(cnt < need, cand, jc)

    jcut = lax.fori_loop(0, 13, tie_body, jnp.zeros((tq, 1), I32))

    q = q_ref[0]
    qm = [_keep_lanes(q, hid == h) for h in range(N_HEADS)]
    _init_state(m_ref, l_ref, acc_ref)

    def body(j, carry):
        ks = pl.multiple_of(j * tk, tk)
        kb = key_ref[j]
        sel = (kb > thr) | ((kb == thr) & (idx_of(j) <= jcut))
        kblk = k_ref[0, pl.ds(ks, tk), :]
        vblk = v_ref[0, pl.ds(ks, tk), :]
        alphas, pvs = [], []
        for h in range(N_HEADS):
            a, pv = _softmax_step(h, _dot_nt(qm[h], kblk), sel, vblk, m_ref, l_ref)
            alphas.append(a)
            pvs.append(pv)
        acc_ref[...] = acc_ref[...] * _by_head(alphas, hid) + _by_head(pvs, hid)
        return carry

    lax.fori_loop(0, nkb, body, 0)
    _finish(o_ref, g_ref, l_ref, acc_ref, hid)


def _attn_c_call(cq, ciq, s2, krep, vrep, ikrep, gates, p_len, l_len, k_sel, tq, tk):
    b, t, _ = cq.shape
    lp = krep.shape[1]
    assert lp <= 8192
    nq = t // tq
    kern = functools.partial(_attn_c_kernel, p_len=p_len, l_len=l_len, k_sel=float(k_sel), tq=tq, tk=tk, nq=nq)
    return pl.pallas_call(
        kern, grid=(b, nq),
        in_specs=[pl.BlockSpec((1, tq, 256), lambda bb, i: (bb, i, 0)),
                  pl.BlockSpec((1, tq, 256), lambda bb, i: (bb, i, 0)),
                  pl.BlockSpec((1, tq, 128), lambda bb, i: (bb, i, 0)),
                  pl.BlockSpec((1, lp, 256), lambda bb, i: (bb, 0, 0)),
                  pl.BlockSpec((1, lp, 256), lambda bb, i: (bb, 0, 0)),
                  pl.BlockSpec((1, lp, 256), lambda bb, i: (bb, 0, 0)),
                  pl.BlockSpec((1, tq, 256), lambda bb, i: (bb, i, 2))],
        out_specs=pl.BlockSpec((1, tq, 256), lambda bb, i: (bb, i, 0)),
        out_shape=jax.ShapeDtypeStruct((b, t, 256), BF16),
        scratch_shapes=[pltpu.VMEM((lp // tk, tq, tk), I32)] + _attn_scratch(tq),
        compiler_params=_cparams(2), name="attn_sparse",
    )(cq, ciq, s2, krep, vrep, ikrep, gates)


def _attn_d_kernel(q_ref, k_ref, v_ref, lf_ref, g_ref, o_ref, fc_ref, m_ref, l_ref, acc_ref,
                   *, p_len, l_len, tq, tk, nq):
    i = pl.program_id(1) if nq > 1 else 0
    lp = k_ref.shape[1]
    nkb_max = lp // tk

    def cumulate():
        x = lf_ref[0]
        lane = lax.broadcasted_iota(I32, (1, lp), 1)
        step = 1
        while step < lp:
            x = x + jnp.where(lane >= step, pltpu.roll(x, step, 1), 0.0)
            step *= 2
        for jb in range(nkb_max):
            fc_ref[jb] = x[:, jb * tk:(jb + 1) * tk]

    if nq > 1:
        pl.when(i == 0)(cumulate)
    else:
        cumulate()

    q0 = p_len + i * tq
    hid = _head_lane_id()
    q = q_ref[0]
    qm = [_keep_lanes(q, hid == h) for h in range(N_HEADS)]
    win = fc_ref[q0 // tk]
    off = q0 % tk
    pick = (lax.broadcasted_iota(I32, (tq, tk), 0) + off) == lax.broadcasted_iota(I32, (tq, tk), 1)
    fq = [jnp.sum(jnp.where(pick, win[h:h + 1, :], 0.0), axis=-1, keepdims=True) for h in range(N_HEADS)]
    _init_state(m_ref, l_ref, acc_ref)
    qpos = q0 + lax.broadcasted_iota(I32, (tq, 1), 0)
    nkb = jnp.minimum((q0 + tq + tk - 1) // tk, nkb_max)

    def body(j, carry):
        ks = pl.multiple_of(j * tk, tk)
        kpos = ks + lax.broadcasted_iota(I32, (1, tk), 1)
        valid = kpos <= qpos
        kblk = k_ref[0, pl.ds(ks, tk), :]
        vblk = v_ref[0, pl.ds(ks, tk), :]
        fk = fc_ref[j]
        alphas, pvs = [], []
        for h in range(N_HEADS):
            s = _dot_nt(qm[h], kblk) + fq[h] - fk[h:h + 1, :]
            a, pv = _softmax_step(h, s, valid, vblk, m_ref, l_ref)
            alphas.append(a)
            pvs.append(pv)
        acc_ref[...] = acc_ref[...] * _by_head(alphas, hid) + _by_head(pvs, hid)
        return carry

    lax.fori_loop(0, nkb, body, 0)
    _finish(o_ref, g_ref, l_ref, acc_ref, hid)


def _attn_d_call(dq, kd, vd, lft, gates, p_len, l_len, tq, tk):
    b, t, _ = dq.shape
    lp = kd.shape[1]
    nq = t // tq
    assert tk % tq == 0 and p_len % tq == 0
    kern = functools.partial(_attn_d_kernel, p_len=p_len, l_len=l_len, tq=tq, tk=tk, nq=nq)
    return pl.pallas_call(
        kern, grid=(b, nq),
        in_specs=[pl.BlockSpec((1, tq, 256), lambda bb, i: (bb, i, 0)),
                  pl.BlockSpec((1, lp, 256), lambda bb, i: (bb, 0, 0)),
                  pl.BlockSpec((1, lp, 256), lambda bb, i: (bb, 0, 0)),
                  pl.BlockSpec((1, 8, lp), lambda bb, i: (bb, 0, 0)),
                  pl.BlockSpec((1, tq, 256), lambda bb, i: (bb, i, 3))],
        out_specs=pl.BlockSpec((1, tq, 256), lambda bb, i: (bb, i, 0)),
        out_shape=jax.ShapeDtypeStruct((b, t, 256), BF16),
        scratch_shapes=[pltpu.VMEM((lp // tk, 8, tk), F32)] + _attn_scratch(tq),
        compiler_params=_cparams(2), name="attn_forget",
    )(dq, kd, vd, lft, gates)


def _out_kernel(ya_ref, yb_ref, yc_ref, yd_ref, x_ref, p_ref, wo_ref, gple_ref, wg_ref, wp_ref, o_ref):
    mixed = (_dot(ya_ref[0], wo_ref[0:256, :]) + _dot(yb_ref[0], wo_ref[256:512, :])
             + _dot(yc_ref[0], wo_ref[512:768, :]) + _dot(yd_ref[0], wo_ref[768:1024, :]))
    x1 = x_ref[0] + mixed
    ms = jnp.mean(x1 * x1, axis=-1, keepdims=True)
    xn = (x1 * lax.rsqrt(ms + EPS) * gple_ref[...]).astype(BF16)
    gate = 1.0 / (1.0 + jnp.exp(-_dot(xn, wg_ref[...])))
    o_ref[0] = x1 + gate * _dot(p_ref[0].astype(BF16), wp_ref[...])


def _out_call(ys, x, p, lw, tm):
    b, t, _ = x.shape

    def const(shape):
        return pl.BlockSpec(shape, lambda bb, i: (0,) * len(shape))

    def rows(w):
        return pl.BlockSpec((1, tm, w), lambda bb, i: (bb, i, 0))

    return pl.pallas_call(
        _out_kernel, grid=(b, t // tm),
        in_specs=[rows(256)] * 4 + [rows(D_MODEL), rows(PLE_DIM), const((1024, D_MODEL)), const((1, D_MODEL)),
                                    const((D_MODEL, D_MODEL)), const((PLE_DIM, D_MODEL))],
        out_specs=rows(D_MODEL),
        out_shape=jax.ShapeDtypeStruct((b, t, D_MODEL), F32),
        compiler_params=_cparams(2), name="out_proj",
    )(*ys, x, p, lw['w_out'], lw['g_ple'], lw['w_ple_gate'], lw['w_ple_proj'])


def _seg_matrix(segments):
    m = np.zeros((256, 256), np.float32)
    for lo, n in segments:
        m[lo:lo + n, lo:lo + n] = 1.0 / n
    return jnp.asarray(m, BF16)


_PAIR_SEGS = ((0, 64), (64, 32), (96, 64), (160, 32), (192, 64))


def _pair_cols(head):
    return (head // 2) * 256 + (head % 2) * (MLA_NOPE + MLA_ROPE)


def _prep_layer(i, g_in, w_in, g_qk, g_rope, a_rel_bias, b_g_cq, b_w_uq, b_g_ckv, b_w_ukv, d_b_f, w_out, g_ple,
                w_ple_gate, w_ple_proj):
    w = w_in[i]
    cols = []
    for n in _MY_ORDER:
        if n == 'pad56':
            cols.append(jnp.zeros((D_MODEL, 56), F32))
        else:
            o, wd = _REF_OFF[n]
            cols.append(w[:, o:o + wd])
    lw = {'w_in': jnp.concatenate(cols, axis=1).astype(BF16), 'g_in': g_in[i][None, :]}
    o, wd = _REF_OFF['d_f']
    lw['w_dft'] = jnp.zeros((16, D_MODEL), F32).at[:N_HEADS].set(w[:, o:o + wd].T).astype(BF16)
    lw['b_col'] = jnp.zeros((8, 1), F32).at[:N_HEADS, 0].set(d_b_f[i])
    g = g_qk[i]
    sc = HEAD_DIM ** -0.5
    rows = [jnp.tile(g[0], 4) * sc, jnp.tile(g[1], 4), jnp.tile(g[2], 4) * sc, jnp.tile(g[4], 4) * sc,
            jnp.tile(g[5], 4)]
    lw['g4'] = jnp.zeros((8, 256), F32).at[:5].set(jnp.stack(rows))
    lw['hseg'] = _seg_matrix(tuple((h * 64, 64) for h in range(4)))
    lw['segq'] = _seg_matrix(_PAIR_SEGS)
    lw['g_cq'] = b_g_cq[i][None, :]
    lw['g_ckv'] = b_g_ckv[i][None, :]
    wuq = jnp.zeros((MLA_Q_LORA, 512), F32)
    gqb = jnp.zeros((512,), F32)
    qscale = (MLA_NOPE + MLA_ROPE) ** -0.5
    wuk = jnp.zeros((MLA_KV_LORA, 512), F32)
    gkb = jnp.zeros((512,), F32)
    e_kr = np.zeros((128, 512), np.float32)
    wuv = []
    for h in range(N_HEADS):
        c = _pair_cols(h)
        src = h * (MLA_NOPE + MLA_ROPE)
        wuq = wuq.at[:, c:c + MLA_NOPE + MLA_ROPE].set(b_w_uq[i][:, src:src + MLA_NOPE + MLA_ROPE])
        gqb = gqb.at[c:c + MLA_NOPE].set(g[6] * qscale).at[c + MLA_NOPE:c + MLA_NOPE + MLA_ROPE].set(g_rope[i][0] * qscale)
        srck = h * (MLA_NOPE + HEAD_DIM)
        wuk = wuk.at[:, c:c + MLA_NOPE].set(b_w_ukv[i][:, srck:srck + MLA_NOPE])
        gkb = gkb.at[c:c + MLA_NOPE].set(g[7])
        e_kr[np.arange(MLA_ROPE), c + MLA_NOPE + np.arange(MLA_ROPE)] = 1.0
        wuv.append(b_w_ukv[i][:, srck + MLA_NOPE:srck + MLA_NOPE + HEAD_DIM])
    lw['w_uq'] = wuq.astype(BF16)
    lw['g_qb'] = gqb[None, :]
    lw['w_uk'] = wuk.astype(BF16)
    lw['g_kb'] = gkb[None, :]
    lw['e_kr'] = jnp.asarray(e_kr, BF16)
    lw['w_uv'] = jnp.concatenate(wuv, axis=1).astype(BF16)
    srow = jnp.ones((8, 128), F32)
    srow = srow.at[0, :HEAD_DIM].set(g[3])
    srow = srow.at[1, :MLA_ROPE].set(g_rope[i][1])
    lw['srow'] = srow
    qi = np.arange(CHUNK)[:, None]
    wi = np.arange(A_WIN)[None, :]
    rel = np.clip(qi + A_BAND + CHUNK - wi, -REL_CLIP, REL_CLIP) + REL_CLIP
    lw['a_bias'] = a_rel_bias[i][:, rel].reshape(N_HEADS * CHUNK, A_WIN)
    lw['w_out'] = w_out[i].astype(BF16)
    lw['g_ple'] = g_ple[i][None, :]
    lw['w_ple_gate'] = w_ple_gate[i].astype(BF16)
    lw['w_ple_proj'] = w_ple_proj[i].astype(BF16)
    return lw


def _rot_section(pos, width, starts, rot_dim):
    half = rot_dim // 2
    inv = jnp.float32(ROPE_THETA) ** (-jnp.arange(half, dtype=F32) / half)
    ang = pos.astype(F32)[:, None] * inv[None, :]
    cos, sin = jnp.cos(ang), jnp.sin(ang)
    t = pos.shape[0]
    c = jnp.ones((t, width), F32)
    s_up = jnp.zeros((t, width), F32)
    s_dn = jnp.zeros((t, width), F32)
    for st in starts:
        c = c.at[:, st:st + half].set(cos).at[:, st + half:st + rot_dim].set(cos)
        s_up = s_up.at[:, st + half:st + rot_dim].set(sin)
        s_dn = s_dn.at[:, st:st + half].set(-sin)
    return [c, s_up, s_dn]


def _rot_tables(pos):
    secs = (_rot_section(pos, 512, [_pair_cols(h) + MLA_NOPE for h in range(N_HEADS)], MLA_ROPE)
            + _rot_section(pos, 256, [h * HEAD_DIM for h in range(N_HEADS)], ROT_DIM)
            + _rot_section(pos, 128, [0], ROT_DIM)
            + _rot_section(pos, 128, [0], MLA_ROPE))
    return jnp.concatenate(secs, axis=1)


def _pad_rows(a, rows):
    return jnp.pad(a, ((0, 0), (0, rows - a.shape[1])) + ((0, 0),) * (a.ndim - 2))


def _cat_rows(past, new, rows):
    a = new if past is None else jnp.concatenate([past.astype(new.dtype), new], axis=1)
    return _pad_rows(a, rows)


def _layer(x, p, past, lw, tab, *, fold_batch, tm, tq, tk):
    b, t, _ = x.shape
    p_len = 0 if past is None else past[2].shape[1]
    l_len = p_len + t
    lp = -(-l_len // tk) * tk

    xin = x.reshape(1, b * t, D_MODEL) if fold_batch else x
    outs = _inproj_call(xin, tab, lw, tm)
    if fold_batch:
        lft = outs[16][0].reshape(8, b, t).transpose(1, 0, 2)
        outs = [o.reshape((b, t) + o.shape[2:]) for o in outs[:16]] + [lft, outs[17].reshape(b, t, 1024)]
    (aq, akb, avb, akf, avf, qb, ckvf, cq, ciq, s1, s2, dq, dkb, dvb, dkf, dvf, lft, gates) = outs

    c_k, c_v = s1[..., :HEAD_DIM], s1[..., HEAD_DIM:]
    b_kr, c_ik = s2[..., :MLA_ROPE], s2[..., MLA_ROPE:MLA_ROPE + IDX_DIM]
    d_lf = lft[:, :N_HEADS, :].transpose(0, 2, 1)
    state = (akf.reshape(b, t, N_HEADS, HEAD_DIM), avf.reshape(b, t, N_HEADS, HEAD_DIM), ckvf, b_kr, c_k, c_v, c_ik,
             dkf.reshape(b, t, N_HEADS, HEAD_DIM), dvf.reshape(b, t, N_HEADS, HEAD_DIM), d_lf)

    if past is None:
        pa_k = pa_v = pb_ckv = pb_kr = pc_k = pc_v = pc_ik = pd_k = pd_v = pd_lf = None
        pa = 0
    else:
        pa_k, pa_v, pb_ckv, pb_kr, pc_k, pc_v, pc_ik, pd_k, pd_v, pd_lf = past
        pa = pa_k.shape[1]
        pa_k = pa_k.reshape(b, pa, 256)
        pa_v = pa_v.reshape(b, pa, 256)
        pd_k = pd_k.reshape(b, p_len, 256)
        pd_v = pd_v.reshape(b, p_len, 256)

    tp = -(-t // CHUNK) * CHUNK
    front = CHUNK + A_BAND - pa
    a_rows = CHUNK + A_BAND + tp

    def band_src(pst, new):
        parts = [jnp.zeros((b, front, 256), BF16)]
        if pst is not None:
            parts.append(pst.astype(BF16))
        parts.append(new)
        return _pad_rows(jnp.concatenate(parts, axis=1), a_rows)

    ya = _attn_a_call(_pad_rows(aq, tp), band_src(pa_k, akb), band_src(pa_v, avb), lw['a_bias'],
                      _pad_rows(gates, tp), front, CHUNK + A_BAND + t)[:, :t]

    ckv_all = _cat_rows(pb_ckv, ckvf, lp)
    kr_new = s2 if past is None else jnp.pad(b_kr, ((0, 0), (0, 0), (0, 128 - MLA_ROPE)))
    kr_past = None if past is None else jnp.pad(pb_kr, ((0, 0), (0, 0), (0, 128 - MLA_ROPE)))
    kr_all = _cat_rows(kr_past, kr_new, lp)
    kb, vb = _bkv_call(ckv_all, kr_all, lw, tk)
    yb = _attn_b_call(qb, kb, vb, gates, p_len, l_len, tq, tk)

    krep = jnp.tile(_cat_rows(pc_k, c_k, lp).astype(BF16), (1, 1, N_HEADS))
    vrep = jnp.tile(_cat_rows(pc_v, c_v, lp).astype(BF16), (1, 1, N_HEADS))
    ikrep = jnp.tile(_cat_rows(pc_ik, c_ik, lp).astype(BF16), (1, 1, IDX_HEADS))
    k_sel = min(DSA_TOPK, l_len // 4)
    yc = _attn_c_call(cq, ciq, s2, krep, vrep, ikrep, gates, p_len, l_len, k_sel, min(tq, 128), tk)

    kd = _cat_rows(None if past is None else pd_k.astype(BF16), dkb, lp)
    vd = _cat_rows(None if past is None else pd_v.astype(BF16), dvb, lp)
    if past is None:
        lf_all = lft
    else:
        lf_past = jnp.pad(pd_lf.transpose(0, 2, 1), ((0, 0), (0, 8 - N_HEADS), (0, 0)))
        lf_all = jnp.concatenate([lf_past, lft], axis=2)
    lf_all = jnp.pad(lf_all, ((0, 0), (0, 0), (0, lp - l_len)))
    yd = _attn_d_call(dq, kd, vd, lf_all, gates, p_len, l_len, tq, tk)

    if fold_batch:
        x_new = _out_call([y.reshape(1, b * t, 256) for y in (ya, yb, yc, yd)], x.reshape(1, b * t, D_MODEL),
                          p.reshape(1, b * t, PLE_DIM), lw, tm).reshape(b, t, D_MODEL)
    else:
        x_new = _out_call([ya, yb, yc, yd], x, p, lw, tm)
    return x_new, state


def kernel(x_prompt, x_sample, cache_a_k, cache_a_v, cache_b_ckv, cache_b_krope, cache_c_k, cache_c_v, cache_c_idx_k,
           cache_d_k, cache_d_v, cache_d_logf, p_prompt, p_sample, g_in, w_in, g_qk, g_rope, a_rel_bias, b_g_cq,
           b_w_uq, b_g_ckv, b_w_ukv, d_b_f, w_out, g_ple, w_ple_gate, w_ple_proj):
    depth = w_in.shape[0]
    b, t = x_prompt.shape[:2]
    bs, ts = x_sample.shape[:2]
    past_len = cache_b_ckv.shape[2]
    tab_p = _rot_tables(jnp.arange(t))
    tab_s = jnp.tile(_rot_tables(past_len + jnp.arange(ts)), (bs, 1))
    xp, xs = x_prompt, x_sample
    states_p, states_s = [], []
    for i in range(depth):
        lw = _prep_layer(i, g_in, w_in, g_qk, g_rope, a_rel_bias, b_g_cq, b_w_uq, b_g_ckv, b_w_ukv, d_b_f, w_out,
                         g_ple, w_ple_gate, w_ple_proj)
        xp, st_p = _layer(xp, p_prompt[i], None, lw, tab_p, fold_batch=False, tm=256, tq=256, tk=256)
        past = (cache_a_k[i], cache_a_v[i], cache_b_ckv[i], cache_b_krope[i], cache_c_k[i], cache_c_v[i],
                cache_c_idx_k[i], cache_d_k[i], cache_d_v[i], cache_d_logf[i])
        xs, st_s = _layer(xs, p_sample[i], past, lw, tab_s, fold_batch=True, tm=bs * ts, tq=ts, tk=256)
        states_p.append(st_p)
        states_s.append(st_s)
    sp = [jnp.stack(z) for z in zip(*states_p)]
    ss = [jnp.stack(z) for z in zip(*states_s)]
    keep = min(A_BAND, t)
    sp[0] = sp[0][:, :, t - keep:]
    sp[1] = sp[1][:, :, t - keep:]
    return (xp, xs, *sp, *ss)
```

```python
import functools

import numpy as np
import jax
import jax.numpy as jnp
from jax import lax
from jax.experimental import pallas as pl
from jax.experimental.pallas import tpu as pltpu

F32 = jnp.float32
BF16 = jnp.bfloat16
I32 = jnp.int32

D_MODEL = 1024
CHUNK = 64
EPS = 1e-6
HEAD_DIM = 64
N_HEADS = 4
GROUP_WIDTH = 256
ROT_DIM = 16
ROPE_THETA = 500000.0
A_BAND = 8 * CHUNK
REL_CLIP = 128
MLA_Q_LORA = 384
MLA_KV_LORA = 128
MLA_NOPE = 64
MLA_ROPE = 32
IDX_HEADS = 8
IDX_DIM = 32
DSA_TOPK = 256
PLE_DIM = 256

_REF_SPLITS = (
    ('a_q', 256), ('a_k', 256), ('a_v', 256), ('a_g', 256),
    ('b_cq', 384), ('b_ckv', 128), ('b_kr', 32), ('b_g', 256),
    ('c_q', 256), ('c_k', 64), ('c_v', 64), ('c_iq', 256), ('c_ik', 32), ('c_iw', 8), ('c_g', 256),
    ('d_q', 256), ('d_k', 256), ('d_v', 256), ('d_f', 4), ('d_g', 256),
)
_REF_OFF = {}
_o = 0
for _n, _w in _REF_SPLITS:
    _REF_OFF[_n] = (_o, _w)
    _o += _w

_MY_ORDER = ('a_q', 'a_k', 'a_v', 'a_g', 'b_cq', 'b_ckv', 'b_g', 'c_q', 'c_iq', 'c_g', 'd_q', 'd_k', 'd_v', 'd_g',
             'c_k', 'c_v', 'b_kr', 'c_ik', 'c_iw', 'pad56')
_MY_OFF = {}
_o = 0
for _n in _MY_ORDER:
    _w = 56 if _n == 'pad56' else _REF_OFF[_n][1]
    _MY_OFF[_n] = _o
    _o += _w
W_IN_COLS = _o

XT_DV, XT_CV, XT_IW, XT_DF, XT_ROWS = 0, 256, 320, 328, 336

TAB_QB, TAB_CQ, TAB_S1, TAB_S2 = 0, 1536, 2304, 2688
TAB_W = 3072

V_ROWS = HEAD_DIM + 16
NEG = -1e30
LOG2E = 1.4426950408889634
INT_MIN = -2 ** 31
VMEM_LIMIT = 56 * 1024 * 1024


def _cparams(n_axes):
    return pltpu.CompilerParams(dimension_semantics=("arbitrary",) * n_axes, vmem_limit_bytes=VMEM_LIMIT)


def _dot(a, b):
    return jnp.dot(a, b, preferred_element_type=F32)


def _dot_nt(a, b):
    return lax.dot_general(a, b, (((1,), (1,)), ((), ())), preferred_element_type=F32)


def _seg_mean_sq(t, seg):
    sq = t * t
    hi = sq.astype(BF16)
    lo = (sq - hi.astype(F32)).astype(BF16)
    return _dot(hi, seg) + _dot(lo, seg)


def _rotate(t, cos, sin_up, sin_dn, half):
    w = t.shape[-1]
    return t * cos + pltpu.roll(t, half, 1) * sin_up + pltpu.roll(t, w - half, 1) * sin_dn


def _silu(g):
    return g * (1.0 / (1.0 + jnp.exp(-g)))


def _log_sigmoid(v):
    return jnp.minimum(v, 0.0) - jnp.log1p(jnp.exp(-jnp.abs(v)))


def _inproj_kernel(x_ref, tab_ref, gin_ref, w_ref, wt_ref, bcol_ref, g4_ref, hseg_ref, gcq_ref, wuq_ref, segq_ref,
                   gqb_ref, gckv_ref, srow_ref,
                   aq_o, akb_o, avb_o, akf_o, avf_o, qb_o, ckvf_o, cq_o, ciq_o, s1_o, s2_o,
                   dq_o, dkb_o, dkf_o, dvf_o, xt_o, gate_o):
    x = x_ref[0]
    ms = jnp.mean(x * x, axis=-1, keepdims=True)
    xn = (x * lax.rsqrt(ms + EPS) * gin_ref[...]).astype(BF16)

    def proj(name, n):
        c0 = _MY_OFF[name]
        return _dot(xn, w_ref[:, c0:c0 + n])

    hseg = hseg_ref[...]

    def headnorm(t, row):
        return t * lax.rsqrt(_seg_mean_sq(t, hseg) + EPS) * g4_ref[row:row + 1, :]

    def fullnorm(t, g):
        return t * lax.rsqrt(jnp.mean(t * t, axis=-1, keepdims=True) + EPS) * g

    aq_o[0] = headnorm(proj('a_q', 256), 0).astype(BF16)
    ak = headnorm(proj('a_k', 256), 1)
    akf_o[0] = ak
    akb_o[0] = ak.astype(BF16)
    av = proj('a_v', 256)
    avf_o[0] = av
    avb_o[0] = av.astype(BF16)
    gate_o[0, :, 0:256] = _silu(proj('a_g', 256)).astype(BF16)

    cqn = fullnorm(proj('b_cq', MLA_Q_LORA), gcq_ref[...]).astype(BF16)
    qb = _dot(cqn, wuq_ref[...])
    segq = segq_ref[...]
    ms_q = jnp.concatenate([_seg_mean_sq(qb[:, :256], segq), _seg_mean_sq(qb[:, 256:], segq)], axis=1)
    qbn = qb * lax.rsqrt(ms_q + EPS) * gqb_ref[...]
    qbn = _rotate(qbn, tab_ref[:, TAB_QB:TAB_QB + 512], tab_ref[:, TAB_QB + 512:TAB_QB + 1024],
                  tab_ref[:, TAB_QB + 1024:TAB_QB + 1536], MLA_ROPE // 2)
    qb_o[0] = qbn.astype(BF16)
    ckvf_o[0] = fullnorm(proj('b_ckv', MLA_KV_LORA), gckv_ref[...])
    gate_o[0, :, 256:512] = _silu(proj('b_g', 256)).astype(BF16)

    cq = headnorm(proj('c_q', 256), 2)
    cq = _rotate(cq, tab_ref[:, TAB_CQ:TAB_CQ + 256], tab_ref[:, TAB_CQ + 256:TAB_CQ + 512],
                 tab_ref[:, TAB_CQ + 512:TAB_CQ + 768], ROT_DIM // 2)
    cq_o[0] = cq.astype(BF16)
    ciq_o[0] = proj('c_iq', 256).astype(BF16)
    gate_o[0, :, 512:768] = _silu(proj('c_g', 256)).astype(BF16)

    dq_o[0] = headnorm(proj('d_q', 256), 3).astype(BF16)
    dk = headnorm(proj('d_k', 256), 4)
    dkf_o[0] = dk
    dkb_o[0] = dk.astype(BF16)
    dvf_o[0] = proj('d_v', 256)
    gate_o[0, :, 768:1024] = _silu(proj('d_g', 256)).astype(BF16)

    lane = lax.broadcasted_iota(I32, (1, 128), 1)
    t = proj('c_k', 128)
    m64 = lane < HEAD_DIM
    ms1 = jnp.sum(jnp.where(m64, t * t, 0.0), axis=-1, keepdims=True) * (1.0 / HEAD_DIM)
    t = jnp.where(m64, t * lax.rsqrt(ms1 + EPS), t) * srow_ref[0:1, :]
    t = _rotate(t, tab_ref[:, TAB_S1:TAB_S1 + 128], tab_ref[:, TAB_S1 + 128:TAB_S1 + 256],
                tab_ref[:, TAB_S1 + 256:TAB_S1 + 384], ROT_DIM // 2)
    s1_o[0] = t
    t = proj('b_kr', 128)
    m32 = lane < MLA_ROPE
    ms2 = jnp.sum(jnp.where(m32, t * t, 0.0), axis=-1, keepdims=True) * (1.0 / MLA_ROPE)
    t = jnp.where(m32, t * lax.rsqrt(ms2 + EPS), t) * srow_ref[1:2, :]
    t = _rotate(t, tab_ref[:, TAB_S2:TAB_S2 + 128], tab_ref[:, TAB_S2 + 128:TAB_S2 + 256],
                tab_ref[:, TAB_S2 + 256:TAB_S2 + 384], MLA_ROPE // 2)
    s2_o[0] = t

    xt = _dot_nt(wt_ref[...], xn)
    xt_o[0, 0:XT_DF, :] = xt[0:XT_DF]
    xt_o[0, XT_DF:XT_ROWS, :] = _log_sigmoid(xt[XT_DF:XT_ROWS] + bcol_ref[...])


def _inproj_call(x, tab, lw, tm):
    bk, tk_, _ = x.shape
    grid = (tk_ // tm, bk)

    def const(shape):
        return pl.BlockSpec(shape, lambda i, b: (0,) * len(shape))

    def rows(w):
        return pl.BlockSpec((1, tm, w), lambda i, b: (b, i, 0))

    in_specs = [
        rows(D_MODEL),
        pl.BlockSpec((tm, TAB_W), lambda i, b: (i, 0)),
        const((1, D_MODEL)), const((D_MODEL, W_IN_COLS)), const((XT_ROWS, D_MODEL)), const((8, 1)),
        const((8, 256)), const((256, 256)), const((1, MLA_Q_LORA)), const((MLA_Q_LORA, 512)), const((256, 256)),
        const((1, 512)), const((1, MLA_KV_LORA)), const((8, 128)),
    ]
    widths = [(256, BF16), (256, BF16), (256, BF16), (256, F32), (256, F32), (512, BF16), (128, F32), (256, BF16),
              (256, BF16), (128, F32), (128, F32), (256, BF16), (256, BF16), (256, F32), (256, F32)]
    out_shape = [jax.ShapeDtypeStruct((bk, tk_, w), dt) for w, dt in widths]
    out_specs = [rows(w) for w, _ in widths]
    out_shape.append(jax.ShapeDtypeStruct((bk, XT_ROWS, tk_), F32))
    out_specs.append(pl.BlockSpec((1, XT_ROWS, tm), lambda i, b: (b, 0, i)))
    out_shape.append(jax.ShapeDtypeStruct((bk, tk_, 1024), BF16))
    out_specs.append(rows(1024))
    return pl.pallas_call(
        _inproj_kernel, grid=grid, in_specs=in_specs, out_specs=out_specs, out_shape=out_shape,
        compiler_params=_cparams(2), name="inproj",
    )(x, tab, lw['g_in'], lw['w_in'], lw['w_t'], lw['b_col'], lw['g4'], lw['hseg'], lw['g_cq'], lw['w_uq'],
      lw['segq'], lw['g_qb'], lw['g_ckv'], lw['srow'])


def _bkv_kernel(ckv_ref, kr_ref, wk_ref, wvt_ref, e_ref, segq_ref, gk_ref, kb_o, vt_o):
    c = ckv_ref[0].astype(BF16)
    kn = _dot(c, wk_ref[...])
    segq = segq_ref[...]
    ms = jnp.concatenate([_seg_mean_sq(kn[:, :256], segq), _seg_mean_sq(kn[:, 256:], segq)], axis=1)
    kn = kn * lax.rsqrt(ms + EPS) * gk_ref[...]
    kr = _dot(kr_ref[0].astype(BF16), e_ref[...])
    kb_o[0] = (kn + kr).astype(BF16)
    vt_o[0] = _dot_nt(wvt_ref[...], c).astype(BF16)


def _bkv_call(ckv, kr, lw, tm):
    b, lp, _ = ckv.shape

    def const(shape):
        return pl.BlockSpec(shape, lambda bb, i: (0,) * len(shape))

    def rows(w):
        return pl.BlockSpec((1, tm, w), lambda bb, i: (bb, i, 0))

    return pl.pallas_call(
        _bkv_kernel, grid=(b, lp // tm),
        in_specs=[rows(128), rows(128), const((128, 512)), const((256, 128)), const((128, 512)), const((256, 256)),
                  const((1, 512))],
        out_specs=[rows(512), pl.BlockSpec((1, 256, tm), lambda bb, i: (bb, 0, i))],
        out_shape=[jax.ShapeDtypeStruct((b, lp, 512), BF16), jax.ShapeDtypeStruct((b, 256, lp), BF16)],
        compiler_params=_cparams(2), name="mla_kv",
    )(ckv, kr, lw['w_uk'], lw['w_uvt'], lw['e_kr'], lw['segq'], lw['g_kb'])


def _head_lane_id(width=256):
    return lax.broadcasted_iota(I32, (1, width), 1) // HEAD_DIM


def _keep_lanes(x, pred):
    return jnp.where(pred, x.astype(F32), 0.0).astype(BF16)


def _flash_run(nfull, nkb, nkb_max, *, qm, k_block, v_block, col_sub, row_add, mask_fn, s_refs, acc_ref, tq):
    s_a, s_b = s_refs

    def produce(j, s_out):
        jc = jnp.minimum(j, nkb_max - 1)
        for h in range(N_HEADS):
            s_out[h] = _dot_nt(k_block(jc, h), qm[h])

    def half(j, c, s_in, s_out, masked):
        ms, ls = c
        produce(j + 1, s_out)
        valid = mask_fn(j) if masked else None
        new_m, new_l = [], []
        for h in range(N_HEADS):
            t = s_in[h]
            if col_sub is not None:
                t = t - col_sub(j, h)
            if masked:
                t = jnp.where(valid, t, NEG)
            m_cur = jnp.max(t, axis=0, keepdims=True)
            if row_add is not None:
                m_cur = m_cur + row_add[h]
            m_new = jnp.maximum(ms[h], m_cur)
            alpha = jnp.exp2(ms[h] - m_new)
            off = m_new if row_add is None else m_new - row_add[h]
            p = jnp.exp2(t - off).astype(BF16)
            pv = _dot(v_block(j, h), p)
            r0 = h * HEAD_DIM
            acc_ref[r0:r0 + HEAD_DIM, :] = acc_ref[r0:r0 + HEAD_DIM, :] * alpha + pv[:HEAD_DIM]
            new_l.append(ls[h] * alpha + pv[HEAD_DIM:HEAD_DIM + 1])
            new_m.append(m_new)
        return tuple(new_m), tuple(new_l)

    def step(j, c, s_in, s_out, may_end):
        def run(c):
            if nfull is None:
                return half(j, c, s_in, s_out, True)
            return lax.cond(j >= nfull, lambda cc: half(j, cc, s_in, s_out, True),
                            lambda cc: half(j, cc, s_in, s_out, False), c)
        if not may_end:
            return run(c)
        return lax.cond(j >= nkb, lambda cc: cc, run, c)

    def body(i, c):
        c = step(2 * i, c, s_a, s_b, False)
        return step(2 * i + 1, c, s_b, s_a, True)

    acc_ref[...] = jnp.zeros(acc_ref.shape, F32)
    init = (tuple(jnp.full((1, tq), NEG, F32) for _ in range(N_HEADS)),
            tuple(jnp.zeros((1, tq), F32) for _ in range(N_HEADS)))
    produce(0, s_a)
    _, ls = lax.fori_loop(0, (nkb + 1) // 2, body, init)
    return ls


def _flash_scratch(tq, tk):
    return [pltpu.VMEM((N_HEADS, tk, tq), F32), pltpu.VMEM((N_HEADS, tk, tq), F32), pltpu.VMEM((256, tq), F32)]


def _flash_finish(o_ref, g_ref, acc_ref, ls):
    for h in range(N_HEADS):
        r0 = h * HEAD_DIM
        acc_ref[r0:r0 + HEAD_DIM, :] = acc_ref[r0:r0 + HEAD_DIM, :] / ls[h]
    y = acc_ref[...].T
    o_ref[0] = (y * g_ref[0].astype(F32)).astype(BF16)


A_WIN = A_BAND + 2 * CHUNK


def _attn_a_kernel(q_ref, k_ref, v_ref, bias_ref, g_ref, o_ref, *, lo_valid, hi_valid):
    c = pl.program_id(1)
    start = pl.multiple_of(c * CHUNK, CHUNK)
    q = q_ref[0]
    hid = _head_lane_id()
    qs = jnp.concatenate([_keep_lanes(q, hid == h) for h in range(N_HEADS)], axis=0)
    kb = k_ref[0, pl.ds(start, A_WIN), :]
    vb = v_ref[0, pl.ds(start, A_WIN), :]
    s = _dot_nt(qs, kb) + bias_ref[...]
    row = start + lax.broadcasted_iota(I32, (1, A_WIN), 1)
    valid = (row >= lo_valid) & (row < hi_valid)
    s = jnp.where(valid, s, NEG)
    m = jnp.max(s, axis=-1, keepdims=True)
    p = jnp.where(valid, jnp.exp(s - m), 0.0)
    l = jnp.sum(p, axis=-1, keepdims=True)
    o = _dot(p.astype(BF16), vb) / l
    y = o[(N_HEADS - 1) * CHUNK:]
    for h in range(N_HEADS - 2, -1, -1):
        y = jnp.where(hid == h, o[h * CHUNK:(h + 1) * CHUNK], y)
    o_ref[0] = (y * g_ref[0].astype(F32)).astype(BF16)


def _attn_a_call(q, kfull, vfull, bias, gates, lo_valid, hi_valid):
    b, tp, _ = q.shape
    rows_kv = kfull.shape[1]
    kern = functools.partial(_attn_a_kernel, lo_valid=lo_valid, hi_valid=hi_valid)
    return pl.pallas_call(
        kern, grid=(b, tp // CHUNK),
        in_specs=[pl.BlockSpec((1, CHUNK, 256), lambda bb, c: (bb, c, 0)),
                  pl.BlockSpec((1, rows_kv, 256), lambda bb, c: (bb, 0, 0)),
                  pl.BlockSpec((1, rows_kv, 256), lambda bb, c: (bb, 0, 0)),
                  pl.BlockSpec((N_HEADS * CHUNK, A_WIN), lambda bb, c: (0, 0)),
                  pl.BlockSpec((1, CHUNK, 256), lambda bb, c: (bb, c, 0))],
        out_specs=pl.BlockSpec((1, CHUNK, 256), lambda bb, c: (bb, c, 0)),
        out_shape=jax.ShapeDtypeStruct((b, tp, 256), BF16),
        compiler_params=_cparams(2), name="attn_band",
    )(q, kfull, vfull, bias, gates)


def _attn_b_kernel(q_ref, k_ref, v_ref, g_ref, o_ref, sa_ref, sb_ref, acc_ref, *, p_len, l_len, tq, tk, nq):
    i = pl.program_id(1) if nq > 1 else 0
    q0 = p_len + i * tq
    lane = lax.broadcasted_iota(I32, (1, 256), 1)
    qm = []
    for h in range(N_HEADS):
        qg = q_ref[0, :, (h // 2) * 256:(h // 2) * 256 + 256]
        lo = (h % 2) * (MLA_NOPE + MLA_ROPE)
        qm.append(_keep_lanes(qg, (lane >= lo) & (lane < lo + MLA_NOPE + MLA_ROPE)))
    qchunk = (q0 + lax.broadcasted_iota(I32, (1, tq), 1)) // CHUNK
    nkb = jnp.minimum((q0 + tq + tk - 1) // tk, k_ref.shape[1] // tk)
    nfull = jnp.minimum(((q0 // CHUNK + 1) * CHUNK) // tk, nkb)

    def k_block(j, h):
        return k_ref[0, pl.ds(pl.multiple_of(j * tk, tk), tk), (h // 2) * 256:(h // 2) * 256 + 256]

    def mask_fn(j):
        kpos = j * tk + lax.broadcasted_iota(I32, (tk, 1), 0)
        return ((kpos // CHUNK) <= qchunk) & (kpos < l_len)

    ls = _flash_run(nfull, nkb, k_ref.shape[1] // tk, qm=qm, k_block=k_block, v_block=lambda j, h: v_ref[0, j, h],
                    col_sub=None, row_add=None, mask_fn=mask_fn, s_refs=(sa_ref, sb_ref), acc_ref=acc_ref, tq=tq)
    _flash_finish(o_ref, g_ref, acc_ref, ls)


def _attn_b_call(qb, kb, vaug, gates, p_len, l_len, tq, tk):
    b, t, _ = qb.shape
    lp = kb.shape[1]
    nq = t // tq
    kern = functools.partial(_attn_b_kernel, p_len=p_len, l_len=l_len, tq=tq, tk=tk, nq=nq)
    return pl.pallas_call(
        kern, grid=(b, nq),
        in_specs=[pl.BlockSpec((1, tq, 512), lambda bb, i: (bb, i, 0)),
                  pl.BlockSpec((1, lp, 512), lambda bb, i: (bb, 0, 0)),
                  pl.BlockSpec((1,) + vaug.shape[1:], lambda bb, i: (bb, 0, 0, 0, 0)),
                  pl.BlockSpec((1, tq, 256), lambda bb, i: (bb, i, 1))],
        out_specs=pl.BlockSpec((1, tq, 256), lambda bb, i: (bb, i, 0)),
        out_shape=jax.ShapeDtypeStruct((b, t, 256), BF16),
        scratch_shapes=_flash_scratch(tq, tk),
        compiler_params=_cparams(2), name="attn_latent",
    )(qb, kb, vaug, gates)


def _attn_c_kernel(q_ref, iq_ref, iw_ref, k_ref, v_ref, ik_ref, g_ref, o_ref, key_ref, sa_ref, sb_ref, acc_ref,
                   *, p_len, l_len, k_sel, tq, tk, nq):
    i = pl.program_id(1) if nq > 1 else 0
    q0 = p_len + i * tq
    nkb_max = k_ref.shape[1] // tk
    nkb = jnp.minimum((q0 + tq + tk - 1) // tk, nkb_max)
    qchunk = (q0 + lax.broadcasted_iota(I32, (1, tq), 1)) // CHUNK
    hid = _head_lane_id()

    iq = iq_ref[0]
    ihid = lax.broadcasted_iota(I32, (1, 256), 1) // IDX_DIM
    iqm = [_keep_lanes(iq, ihid == h) for h in range(IDX_HEADS)]
    iw = iw_ref[0] * (IDX_DIM ** -0.5 * IDX_HEADS ** -0.5)

    def score_body(j, carry):
        ks = pl.multiple_of(j * tk, tk)
        ikb = ik_ref[0, pl.ds(ks, tk), :]
        score = jnp.zeros((tk, tq), F32)
        for h in range(IDX_HEADS):
            score = score + iw[h:h + 1, :] * jnp.maximum(_dot_nt(ikb, iqm[h]), 0.0)
        kpos = ks + lax.broadcasted_iota(I32, (tk, 1), 0)
        adm = ((kpos // CHUNK) <= qchunk) & (kpos < l_len)
        bits = pltpu.bitcast(score, I32)
        key = jnp.where(bits < 0, bits ^ 0x7FFFFFFF, bits)
        key = jnp.where(key == -1, 0, key)
        key_ref[j] = jnp.where(adm, key, INT_MIN)
        return carry

    lax.fori_loop(0, nkb, score_body, 0)

    def count(pred_fn):
        def cbody(j, acc):
            pf = jnp.where(pred_fn(key_ref[j], j), 1.0, 0.0)
            for r in range(tk // 8):
                acc = acc + pf[r * 8:(r + 1) * 8]
            return acc
        acc = lax.fori_loop(0, nkb, cbody, jnp.zeros((8, tq), F32))
        return jnp.sum(acc, axis=0, keepdims=True)

    def bis_body(it, c):
        t, n_ge = c
        cand = t + jnp.left_shift(jnp.int32(1), 31 - it)
        cnt = count(lambda kb, j: kb >= cand)
        ok = cnt >= k_sel
        return jnp.where(ok, cand, t), jnp.where(ok, cnt, n_ge)

    thr, n_ge = lax.fori_loop(0, 32, bis_body, (jnp.full((1, tq), INT_MIN, I32), jnp.zeros((1, tq), F32)))
    thr = jnp.maximum(thr, INT_MIN + 1)

    def idx_of(j):
        return j * tk + lax.broadcasted_iota(I32, (tk, tq), 0)

    def tie_cut(_):
        need = k_sel - count(lambda kb, j: kb > thr)

        def tie_body(it, jc):
            cand = jc + jnp.left_shift(jnp.int32(1), 12 - it)
            cnt = count(lambda kb, j: (kb == thr) & (idx_of(j) < cand))
            return jnp.where(cnt < need, cand, jc)

        return lax.fori_loop(0, 13, tie_body, jnp.zeros((1, tq), I32))

    has_tie = jnp.max(jnp.where(n_ge > k_sel, 1.0, 0.0)) > 0.0
    jcut = lax.cond(has_tie, tie_cut, lambda _: jnp.full((1, tq), 2 ** 30, I32), 0)

    q = q_ref[0]
    qm = [_keep_lanes(q, hid == h) for h in range(N_HEADS)]

    def mask_fn(j):
        kb = key_ref[j]
        return (kb > thr) | ((kb == thr) & (idx_of(j) <= jcut))

    ls = _flash_run(None, nkb, nkb_max, qm=qm,
                    k_block=lambda j, h: k_ref[0, pl.ds(pl.multiple_of(j * tk, tk), tk), :],
                    v_block=lambda j, h: v_ref[0, j, 0], col_sub=None, row_add=None, mask_fn=mask_fn,
                    s_refs=(sa_ref, sb_ref), acc_ref=acc_ref, tq=tq)
    _flash_finish(o_ref, g_ref, acc_ref, ls)


def _attn_c_call(cq, ciq, iwt, krep, vaug, ikrep, gates, p_len, l_len, k_sel, tq, tk):
    b, t, _ = cq.shape
    lp = krep.shape[1]
    assert lp <= 8192
    nq = t // tq
    kern = functools.partial(_attn_c_kernel, p_len=p_len, l_len=l_len, k_sel=float(k_sel), tq=tq, tk=tk, nq=nq)
    return pl.pallas_call(
        kern, grid=(b, nq),
        in_specs=[pl.BlockSpec((1, tq, 256), lambda bb, i: (bb, i, 0)),
                  pl.BlockSpec((1, tq, 256), lambda bb, i: (bb, i, 0)),
                  pl.BlockSpec((1, 8, tq), lambda bb, i: (bb, 0, i)),
                  pl.BlockSpec((1, lp, 256), lambda bb, i: (bb, 0, 0)),
                  pl.BlockSpec((1,) + vaug.shape[1:], lambda bb, i: (bb, 0, 0, 0, 0)),
                  pl.BlockSpec((1, lp, 256), lambda bb, i: (bb, 0, 0)),
                  pl.BlockSpec((1, tq, 256), lambda bb, i: (bb, i, 2))],
        out_specs=pl.BlockSpec((1, tq, 256), lambda bb, i: (bb, i, 0)),
        out_shape=jax.ShapeDtypeStruct((b, t, 256), BF16),
        scratch_shapes=[pltpu.VMEM((lp // tk, tk, tq), I32)] + _flash_scratch(tq, tk),
        compiler_params=_cparams(2), name="attn_sparse",
    )(cq, ciq, iwt, krep, vaug, ikrep, gates)


def _attn_d_kernel(q_ref, k_ref, v_ref, lf_ref, g_ref, o_ref, fc_ref, fk_ref, sa_ref, sb_ref, acc_ref,
                   *, p_len, tq, tk, nq):
    i = pl.program_id(1) if nq > 1 else 0
    lp = k_ref.shape[1]
    nch = lp // 128

    def cumulate():
        x = lf_ref[0]
        lane = lax.broadcasted_iota(I32, (1, lp), 1)
        step = 1
        while step < lp:
            x = x + jnp.where(lane >= step, pltpu.roll(x, step, 1), 0.0)
            step *= 2
        x = x * LOG2E
        for c in range(nch):
            fc_ref[c] = x[:, c * 128:(c + 1) * 128]

        eye = lax.broadcasted_iota(I32, (128, 128), 0) == lax.broadcasted_iota(I32, (128, 128), 1)

        def spread(c, carry):
            rows = fc_ref[c]
            for h in range(N_HEADS):
                col = jnp.sum(jnp.where(eye, rows[h:h + 1, :], 0.0), axis=1, keepdims=True)
                fk_ref[h, pl.ds(pl.multiple_of(c * 128, 128), 128), :] = jnp.broadcast_to(col, (128, 128))
            return carry

        lax.fori_loop(0, nch, spread, 0)

    if nq > 1:
        pl.when(i == 0)(cumulate)
    else:
        cumulate()

    q0 = p_len + i * tq
    hid = _head_lane_id()
    q = q_ref[0]
    qm = [_keep_lanes(q, hid == h) for h in range(N_HEADS)]
    c0 = q0 // 128
    fq_rows = [fc_ref[c0 + c] for c in range(tq // 128)]
    fq = [jnp.concatenate([r[h:h + 1, :] for r in fq_rows], axis=1) for h in range(N_HEADS)]
    qpos = q0 + lax.broadcasted_iota(I32, (1, tq), 1)
    nkb = jnp.minimum((q0 + tq + tk - 1) // tk, lp // tk)
    nfull = jnp.minimum((q0 + 1) // tk, nkb)

    def col_sub(j, h):
        fk = fk_ref[h, pl.ds(pl.multiple_of(j * tk, tk), tk), :]
        return jnp.concatenate([fk] * (tq // 128), axis=1)

    def mask_fn(j):
        return (j * tk + lax.broadcasted_iota(I32, (tk, 1), 0)) <= qpos

    ls = _flash_run(nfull, nkb, lp // tk, qm=qm,
                    k_block=lambda j, h: k_ref[0, pl.ds(pl.multiple_of(j * tk, tk), tk), :],
                    v_block=lambda j, h: v_ref[0, j, h], col_sub=col_sub, row_add=fq, mask_fn=mask_fn,
                    s_refs=(sa_ref, sb_ref), acc_ref=acc_ref, tq=tq)
    _flash_finish(o_ref, g_ref, acc_ref, ls)


def _attn_d_call(dq, kd, vaug, lft, gates, p_len, tq, tk):
    b, t, _ = dq.shape
    lp = kd.shape[1]
    nq = t // tq
    assert p_len % 128 == 0 and tq % 128 == 0 and p_len + t <= lp
    kern = functools.partial(_attn_d_kernel, p_len=p_len, tq=tq, tk=tk, nq=nq)
    return pl.pallas_call(
        kern, grid=(b, nq),
        in_specs=[pl.BlockSpec((1, tq, 256), lambda bb, i: (bb, i, 0)),
                  pl.BlockSpec((1, lp, 256), lambda bb, i: (bb, 0, 0)),
                  pl.BlockSpec((1,) + vaug.shape[1:], lambda bb, i: (bb, 0, 0, 0, 0)),
                  pl.BlockSpec((1, 8, lp), lambda bb, i: (bb, 0, 0)),
                  pl.BlockSpec((1, tq, 256), lambda bb, i: (bb, i, 3))],
        out_specs=pl.BlockSpec((1, tq, 256), lambda bb, i: (bb, i, 0)),
        out_shape=jax.ShapeDtypeStruct((b, t, 256), BF16),
        scratch_shapes=[pltpu.VMEM((lp // 128, 8, 128), F32), pltpu.VMEM((N_HEADS, lp, 128), F32)]
        + _flash_scratch(tq, tk),
        compiler_params=_cparams(2), name="attn_forget",
    )(dq, kd, vaug, lft, gates)


def _out_kernel(ya_ref, yb_ref, yc_ref, yd_ref, x_ref, p_ref, wo_ref, gple_ref, wg_ref, wp_ref, o_ref):
    mixed = (_dot(ya_ref[0], wo_ref[0:256, :]) + _dot(yb_ref[0], wo_ref[256:512, :])
             + _dot(yc_ref[0], wo_ref[512:768, :]) + _dot(yd_ref[0], wo_ref[768:1024, :]))
    x1 = x_ref[0] + mixed
    ms = jnp.mean(x1 * x1, axis=-1, keepdims=True)
    xn = (x1 * lax.rsqrt(ms + EPS) * gple_ref[...]).astype(BF16)
    gate = 1.0 / (1.0 + jnp.exp(-_dot(xn, wg_ref[...])))
    o_ref[0] = x1 + gate * _dot(p_ref[0].astype(BF16), wp_ref[...])


def _out_call(ys, x, p, lw, tm):
    b, t, _ = x.shape

    def const(shape):
        return pl.BlockSpec(shape, lambda bb, i: (0,) * len(shape))

    def rows(w):
        return pl.BlockSpec((1, tm, w), lambda bb, i: (bb, i, 0))

    return pl.pallas_call(
        _out_kernel, grid=(b, t // tm),
        in_specs=[rows(256)] * 4 + [rows(D_MODEL), rows(PLE_DIM), const((1024, D_MODEL)), const((1, D_MODEL)),
                                    const((D_MODEL, D_MODEL)), const((PLE_DIM, D_MODEL))],
        out_specs=rows(D_MODEL),
        out_shape=jax.ShapeDtypeStruct((b, t, D_MODEL), F32),
        compiler_params=_cparams(2), name="out_proj",
    )(*ys, x, p, lw['w_out'], lw['g_ple'], lw['w_ple_gate'], lw['w_ple_proj'])


def _seg_matrix(segments):
    m = np.zeros((256, 256), np.float32)
    for lo, n in segments:
        m[lo:lo + n, lo:lo + n] = 1.0 / n
    return jnp.asarray(m, BF16)


_PAIR_SEGS = ((0, 64), (64, 32), (96, 64), (160, 32), (192, 64))


def _pair_cols(head):
    return (head // 2) * 256 + (head % 2) * (MLA_NOPE + MLA_ROPE)


def _prep_layer(i, g_in, w_in, g_qk, g_rope, a_rel_bias, b_g_cq, b_w_uq, b_g_ckv, b_w_ukv, d_b_f, w_out, g_ple,
                w_ple_gate, w_ple_proj):
    w = w_in[i]
    cols = []
    for n in _MY_ORDER:
        if n == 'pad56':
            cols.append(jnp.zeros((D_MODEL, 56), F32))
        else:
            o, wd = _REF_OFF[n]
            cols.append(w[:, o:o + wd])
    lw = {'w_in': jnp.concatenate(cols, axis=1).astype(BF16), 'g_in': g_in[i][None, :]}
    wt = [w[:, _REF_OFF[n][0]:_REF_OFF[n][0] + _REF_OFF[n][1]] for n in ('d_v', 'c_v', 'c_iw', 'd_f')]
    wt.append(jnp.zeros((D_MODEL, XT_ROWS - XT_DF - N_HEADS), F32))
    lw['w_t'] = jnp.concatenate(wt, axis=1).T.astype(BF16)
    lw['b_col'] = jnp.zeros((8, 1), F32).at[:N_HEADS, 0].set(d_b_f[i])
    g = g_qk[i]
    sc = HEAD_DIM ** -0.5
    rows = [jnp.tile(g[0], 4) * sc, jnp.tile(g[1], 4), jnp.tile(g[2], 4) * (sc * LOG2E),
            jnp.tile(g[4], 4) * (sc * LOG2E), jnp.tile(g[5], 4)]
    lw['g4'] = jnp.zeros((8, 256), F32).at[:5].set(jnp.stack(rows))
    lw['hseg'] = _seg_matrix(tuple((h * 64, 64) for h in range(4)))
    lw['segq'] = _seg_matrix(_PAIR_SEGS)
    lw['g_cq'] = b_g_cq[i][None, :]
    lw['g_ckv'] = b_g_ckv[i][None, :]
    wuq = jnp.zeros((MLA_Q_LORA, 512), F32)
    gqb = jnp.zeros((512,), F32)
    qscale = (MLA_NOPE + MLA_ROPE) ** -0.5 * LOG2E
    wuk = jnp.zeros((MLA_KV_LORA, 512), F32)
    gkb = jnp.zeros((512,), F32)
    e_kr = np.zeros((128, 512), np.float32)
    wuv = []
    for h in range(N_HEADS):
        c = _pair_cols(h)
        src = h * (MLA_NOPE + MLA_ROPE)
        wuq = wuq.at[:, c:c + MLA_NOPE + MLA_ROPE].set(b_w_uq[i][:, src:src + MLA_NOPE + MLA_ROPE])
        gqb = gqb.at[c:c + MLA_NOPE].set(g[6] * qscale).at[c + MLA_NOPE:c + MLA_NOPE + MLA_ROPE].set(g_rope[i][0] * qscale)
        srck = h * (MLA_NOPE + HEAD_DIM)
        wuk = wuk.at[:, c:c + MLA_NOPE].set(b_w_ukv[i][:, srck:srck + MLA_NOPE])
        gkb = gkb.at[c:c + MLA_NOPE].set(g[7])
        e_kr[np.arange(MLA_ROPE), c + MLA_NOPE + np.arange(MLA_ROPE)] = 1.0
        wuv.append(b_w_ukv[i][:, srck + MLA_NOPE:srck + MLA_NOPE + HEAD_DIM])
    lw['w_uq'] = wuq.astype(BF16)
    lw['g_qb'] = gqb[None, :]
    lw['w_uk'] = wuk.astype(BF16)
    lw['g_kb'] = gkb[None, :]
    lw['e_kr'] = jnp.asarray(e_kr, BF16)
    lw['w_uvt'] = jnp.concatenate(wuv, axis=1).T.astype(BF16)
    srow = jnp.ones((8, 128), F32)
    srow = srow.at[0, :HEAD_DIM].set(g[3])
    srow = srow.at[1, :MLA_ROPE].set(g_rope[i][1])
    lw['srow'] = srow
    qi = np.arange(CHUNK)[:, None]
    wi = np.arange(A_WIN)[None, :]
    rel = np.clip(qi + A_BAND + CHUNK - wi, -REL_CLIP, REL_CLIP) + REL_CLIP
    lw['a_bias'] = a_rel_bias[i][:, rel].reshape(N_HEADS * CHUNK, A_WIN)
    lw['w_out'] = w_out[i].astype(BF16)
    lw['g_ple'] = g_ple[i][None, :]
    lw['w_ple_gate'] = w_ple_gate[i].astype(BF16)
    lw['w_ple_proj'] = w_ple_proj[i].astype(BF16)
    return lw


def _rot_section(pos, width, starts, rot_dim):
    half = rot_dim // 2
    inv = jnp.float32(ROPE_THETA) ** (-jnp.arange(half, dtype=F32) / half)
    ang = pos.astype(F32)[:, None] * inv[None, :]
    cos, sin = jnp.cos(ang), jnp.sin(ang)
    t = pos.shape[0]
    c = jnp.ones((t, width), F32)
    s_up = jnp.zeros((t, width), F32)
    s_dn = jnp.zeros((t, width), F32)
    for st in starts:
        c = c.at[:, st:st + half].set(cos).at[:, st + half:st + rot_dim].set(cos)
        s_up = s_up.at[:, st + half:st + rot_dim].set(sin)
        s_dn = s_dn.at[:, st:st + half].set(-sin)
    return [c, s_up, s_dn]


def _rot_tables(pos):
    secs = (_rot_section(pos, 512, [_pair_cols(h) + MLA_NOPE for h in range(N_HEADS)], MLA_ROPE)
            + _rot_section(pos, 256, [h * HEAD_DIM for h in range(N_HEADS)], ROT_DIM)
            + _rot_section(pos, 128, [0], ROT_DIM)
            + _rot_section(pos, 128, [0], MLA_ROPE))
    return jnp.concatenate(secs, axis=1)


def _pad_axis(a, size, axis):
    pads = [(0, 0)] * a.ndim
    pads[axis] = (0, size - a.shape[axis])
    return jnp.pad(a, pads)


def _cat_rows(past, new, rows):
    a = new if past is None else jnp.concatenate([past.astype(new.dtype), new], axis=1)
    return _pad_axis(a, rows, 1)


def _cat_lanes(past_t, new_t, lanes):
    a = new_t if past_t is None else jnp.concatenate([past_t.astype(new_t.dtype), new_t], axis=2)
    return _pad_axis(a, lanes, 2)


def _value_blocks(vt, tk):
    b, r, lp = vt.shape
    h = r // HEAD_DIM
    v = vt.astype(BF16).reshape(b, h, HEAD_DIM, lp)
    v = jnp.concatenate([v, jnp.ones((b, h, V_ROWS - HEAD_DIM, lp), BF16)], axis=2)
    return v.reshape(b, h, V_ROWS, lp // tk, tk).transpose(0, 3, 1, 2, 4)


def _layer(x, p, past, lw, tab, *, fold_batch, tm, tq, tk):
    b, t, _ = x.shape
    p_len = 0 if past is None else past[2].shape[1]
    l_len = p_len + t
    tqp = -(-t // tq) * tq
    lp = -(-(p_len + tqp) // tk) * tk

    xin = x.reshape(1, b * t, D_MODEL) if fold_batch else x
    outs = _inproj_call(xin, tab, lw, tm)
    if fold_batch:
        xt = outs[15][0].reshape(XT_ROWS, b, t).transpose(1, 0, 2)
        outs = [o.reshape((b, t) + o.shape[2:]) for o in outs[:15]] + [xt, outs[16].reshape(b, t, 1024)]
    (aq, akb, avb, akf, avf, qb, ckvf, cq, ciq, s1, s2, dq, dkb, dkf, dvf, xt, gates) = outs

    c_k, c_v = s1[..., :HEAD_DIM], s1[..., HEAD_DIM:]
    b_kr, c_ik = s2[..., :MLA_ROPE], s2[..., MLA_ROPE:MLA_ROPE + IDX_DIM]
    lft = xt[:, XT_DF:XT_ROWS]
    d_lf = lft[:, :N_HEADS, :].transpose(0, 2, 1)
    state = (akf.reshape(b, t, N_HEADS, HEAD_DIM), avf.reshape(b, t, N_HEADS, HEAD_DIM), ckvf, b_kr, c_k, c_v, c_ik,
             dkf.reshape(b, t, N_HEADS, HEAD_DIM), dvf.reshape(b, t, N_HEADS, HEAD_DIM), d_lf)

    if past is None:
        pa_k = pa_v = pb_ckv = pb_kr = pc_k = pc_v = pc_ik = pd_k = pd_v = pd_lf = None
        pa = 0
    else:
        pa_k, pa_v, pb_ckv, pb_kr, pc_k, pc_v, pc_ik, pd_k, pd_v, pd_lf = past
        pa = pa_k.shape[1]
        pa_k = pa_k.reshape(b, pa, 256)
        pa_v = pa_v.reshape(b, pa, 256)
        pd_k = pd_k.reshape(b, p_len, 256)
        pd_v = pd_v.reshape(b, p_len, 256)

    tp = -(-t // CHUNK) * CHUNK
    front = CHUNK + A_BAND - pa
    a_rows = CHUNK + A_BAND + tp

    def band_src(pst, new):
        parts = [jnp.zeros((b, front, 256), BF16)]
        if pst is not None:
            parts.append(pst.astype(BF16))
        parts.append(new)
        return _pad_axis(jnp.concatenate(parts, axis=1), a_rows, 1)

    ya = _attn_a_call(_pad_axis(aq, tp, 1), band_src(pa_k, akb), band_src(pa_v, avb), lw['a_bias'],
                      _pad_axis(gates, tp, 1), front, CHUNK + A_BAND + t)[:, :t]

    gates_q = _pad_axis(gates, tqp, 1)

    ckv_all = _cat_rows(pb_ckv, ckvf, lp)
    kr_new = s2 if past is None else _pad_axis(b_kr, 128, 2)
    kr_past = None if past is None else _pad_axis(pb_kr, 128, 2)
    kr_all = _cat_rows(kr_past, kr_new, lp)
    kb, vbt = _bkv_call(ckv_all, kr_all, lw, tk)
    yb = _attn_b_call(_pad_axis(qb, tqp, 1), kb, _value_blocks(vbt, tk), gates_q, p_len, l_len, tq, tk)[:, :t]

    krep = jnp.tile(_cat_rows(pc_k, c_k, lp).astype(BF16), (1, 1, N_HEADS))
    ikrep = jnp.tile(_cat_rows(pc_ik, c_ik, lp).astype(BF16), (1, 1, IDX_HEADS))
    cvt = _cat_lanes(None if past is None else pc_v.transpose(0, 2, 1), xt[:, XT_CV:XT_IW], lp)
    iwt = _pad_axis(xt[:, XT_IW:XT_DF], tqp, 2)
    k_sel = min(DSA_TOPK, l_len // 4)
    yc = _attn_c_call(_pad_axis(cq, tqp, 1), _pad_axis(ciq, tqp, 1), iwt, krep, _value_blocks(cvt, tk), ikrep,
                      gates_q, p_len, l_len, k_sel, tq, tk)[:, :t]

    kd = _cat_rows(None if past is None else pd_k.astype(BF16), dkb, lp)
    dvt = _cat_lanes(None if past is None else pd_v.transpose(0, 2, 1), xt[:, XT_DV:XT_CV], lp)
    lf_past = None if past is None else _pad_axis(pd_lf.transpose(0, 2, 1), 8, 1)
    lf_all = _cat_lanes(lf_past, lft, lp)
    yd = _attn_d_call(_pad_axis(dq, tqp, 1), kd, _value_blocks(dvt, tk), lf_all, gates_q, p_len, tq, tk)[:, :t]

    if fold_batch:
        x_new = _out_call([y.reshape(1, b * t, 256) for y in (ya, yb, yc, yd)], x.reshape(1, b * t, D_MODEL),
                          p.reshape(1, b * t, PLE_DIM), lw, tm).reshape(b, t, D_MODEL)
    else:
        x_new = _out_call([ya, yb, yc, yd], x, p, lw, tm)
    return x_new, state


def kernel(x_prompt, x_sample, cache_a_k, cache_a_v, cache_b_ckv, cache_b_krope, cache_c_k, cache_c_v, cache_c_idx_k,
           cache_d_k, cache_d_v, cache_d_logf, p_prompt, p_sample, g_in, w_in, g_qk, g_rope, a_rel_bias, b_g_cq,
           b_w_uq, b_g_ckv, b_w_ukv, d_b_f, w_out, g_ple, w_ple_gate, w_ple_proj):
    depth = w_in.shape[0]
    b, t = x_prompt.shape[:2]
    bs, ts = x_sample.shape[:2]
    past_len = cache_b_ckv.shape[2]
    tab_p = _rot_tables(jnp.arange(t))
    tab_s = jnp.tile(_rot_tables(past_len + jnp.arange(ts)), (bs, 1))
    xp, xs = x_prompt, x_sample
    states_p, states_s = [], []
    for i in range(depth):
        lw = _prep_layer(i, g_in, w_in, g_qk, g_rope, a_rel_bias, b_g_cq, b_w_uq, b_g_ckv, b_w_ukv, d_b_f, w_out,
                         g_ple, w_ple_gate, w_ple_proj)
        xp, st_p = _layer(xp, p_prompt[i], None, lw, tab_p, fold_batch=False, tm=256, tq=256, tk=256)
        past = (cache_a_k[i], cache_a_v[i], cache_b_ckv[i], cache_b_krope[i], cache_c_k[i], cache_c_v[i],
                cache_c_idx_k[i], cache_d_k[i], cache_d_v[i], cache_d_logf[i])
        xs, st_s = _layer(xs, p_sample[i], past, lw, tab_s, fold_batch=True, tm=bs * ts, tq=128, tk=256)
        states_p.append(st_p)
        states_s.append(st_s)
    sp = [jnp.stack(z) for z in zip(*states_p)]
    ss = [jnp.stack(z) for z in zip(*states_s)]
    keep = min(A_BAND, t)
    sp[0] = sp[0][:, :, t - keep:]
    sp[1] = sp[1][:, :, t - keep:]
    return (xp, xs, *sp, *ss)
```

```python
import functools

import numpy as np
import jax
import jax.numpy as jnp
from jax import lax
from jax.experimental import pallas as pl
from jax.experimental.pallas import tpu as pltpu

F32 = jnp.float32
BF16 = jnp.bfloat16
I32 = jnp.int32

D_MODEL = 1024
CHUNK = 64
EPS = 1e-6
HEAD_DIM = 64
N_HEADS = 4
GROUP_WIDTH = 256
ROT_DIM = 16
ROPE_THETA = 500000.0
A_BAND = 8 * CHUNK
REL_CLIP = 128
MLA_Q_LORA = 384
MLA_KV_LORA = 128
MLA_NOPE = 64
MLA_ROPE = 32
IDX_HEADS = 8
IDX_DIM = 32
DSA_TOPK = 256
PLE_DIM = 256

_REF_SPLITS = (
    ('a_q', 256), ('a_k', 256), ('a_v', 256), ('a_g', 256),
    ('b_cq', 384), ('b_ckv', 128), ('b_kr', 32), ('b_g', 256),
    ('c_q', 256), ('c_k', 64), ('c_v', 64), ('c_iq', 256), ('c_ik', 32), ('c_iw', 8), ('c_g', 256),
    ('d_q', 256), ('d_k', 256), ('d_v', 256), ('d_f', 4), ('d_g', 256),
)
_REF_OFF = {}
_o = 0
for _n, _w in _REF_SPLITS:
    _REF_OFF[_n] = (_o, _w)
    _o += _w

_MY_ORDER = ('a_q', 'a_k', 'a_v', 'a_g', 'b_cq', 'b_ckv', 'b_g', 'c_q', 'c_iq', 'c_g', 'd_q', 'd_k', 'd_v', 'd_g',
             'c_k', 'c_v', 'b_kr', 'c_ik', 'c_iw', 'pad56')
_MY_OFF = {}
_o = 0
for _n in _MY_ORDER:
    _w = 56 if _n == 'pad56' else _REF_OFF[_n][1]
    _MY_OFF[_n] = _o
    _o += _w
W_IN_COLS = _o

XT_DV, XT_CV, XT_IW, XT_DF, XT_ROWS = 0, 256, 320, 328, 336

TAB_QB, TAB_CQ, TAB_S1, TAB_S2 = 0, 1536, 2304, 2688
TAB_W = 3072

V_ROWS = HEAD_DIM + 16
NEG = -1e30
LOG2E = 1.4426950408889634
INT_MIN = -2 ** 31
VMEM_LIMIT = 56 * 1024 * 1024


def _cparams(n_axes):
    return pltpu.CompilerParams(dimension_semantics=("arbitrary",) * n_axes, vmem_limit_bytes=VMEM_LIMIT)


def _dot(a, b):
    return jnp.dot(a, b, preferred_element_type=F32)


def _dot_nt(a, b):
    return lax.dot_general(a, b, (((1,), (1,)), ((), ())), preferred_element_type=F32)


def _seg_mean_sq(t, seg):
    sq = t * t
    hi = sq.astype(BF16)
    lo = (sq - hi.astype(F32)).astype(BF16)
    return _dot(hi, seg) + _dot(lo, seg)


def _rotate(t, cos, sin_up, sin_dn, half):
    w = t.shape[-1]
    return t * cos + pltpu.roll(t, half, 1) * sin_up + pltpu.roll(t, w - half, 1) * sin_dn


def _silu(g):
    return g * (1.0 / (1.0 + jnp.exp(-g)))


def _log_sigmoid(v):
    return jnp.minimum(v, 0.0) - jnp.log1p(jnp.exp(-jnp.abs(v)))


def _inproj_kernel(x_ref, tab_ref, gin_ref, w_ref, wt_ref, bcol_ref, g4_ref, hseg_ref, gcq_ref, wuq_ref, segq_ref,
                   gqb_ref, gckv_ref, srow_ref,
                   aq_o, akb_o, avb_o, akf_o, avf_o, qb_o, ckvf_o, cq_o, ciq_o, ck_o, cv_o, s2_o, bkr_o, cik_o,
                   dq_o, dkb_o, dkf_o, dvf_o, xt_o, gate_o):
    x = x_ref[0]
    ms = jnp.mean(x * x, axis=-1, keepdims=True)
    xn = (x * lax.rsqrt(ms + EPS) * gin_ref[...]).astype(BF16)

    def proj(name, n):
        c0 = _MY_OFF[name]
        return _dot(xn, w_ref[:, c0:c0 + n])

    hseg = hseg_ref[...]

    def headnorm(t, row):
        return t * lax.rsqrt(_seg_mean_sq(t, hseg) + EPS) * g4_ref[row:row + 1, :]

    def fullnorm(t, g):
        return t * lax.rsqrt(jnp.mean(t * t, axis=-1, keepdims=True) + EPS) * g

    aq_o[0] = headnorm(proj('a_q', 256), 0).astype(BF16)
    ak = headnorm(proj('a_k', 256), 1)
    akf_o[0] = ak
    akb_o[0] = ak.astype(BF16)
    av = proj('a_v', 256)
    avf_o[0] = av
    avb_o[0] = av.astype(BF16)
    gate_o[0, :, 0:256] = _silu(proj('a_g', 256)).astype(BF16)

    cqn = fullnorm(proj('b_cq', MLA_Q_LORA), gcq_ref[...]).astype(BF16)
    qb = _dot(cqn, wuq_ref[...])
    segq = segq_ref[...]
    ms_q = jnp.concatenate([_seg_mean_sq(qb[:, :256], segq), _seg_mean_sq(qb[:, 256:], segq)], axis=1)
    qbn = qb * lax.rsqrt(ms_q + EPS) * gqb_ref[...]
    qbn = _rotate(qbn, tab_ref[:, TAB_QB:TAB_QB + 512], tab_ref[:, TAB_QB + 512:TAB_QB + 1024],
                  tab_ref[:, TAB_QB + 1024:TAB_QB + 1536], MLA_ROPE // 2)
    qb_o[0] = qbn.astype(BF16)
    ckvf_o[0] = fullnorm(proj('b_ckv', MLA_KV_LORA), gckv_ref[...])
    gate_o[0, :, 256:512] = _silu(proj('b_g', 256)).astype(BF16)

    cq = headnorm(proj('c_q', 256), 2)
    cq = _rotate(cq, tab_ref[:, TAB_CQ:TAB_CQ + 256], tab_ref[:, TAB_CQ + 256:TAB_CQ + 512],
                 tab_ref[:, TAB_CQ + 512:TAB_CQ + 768], ROT_DIM // 2)
    cq_o[0] = cq.astype(BF16)
    ciq_o[0] = proj('c_iq', 256).astype(BF16)
    gate_o[0, :, 512:768] = _silu(proj('c_g', 256)).astype(BF16)

    dq_o[0] = headnorm(proj('d_q', 256), 3).astype(BF16)
    dk = headnorm(proj('d_k', 256), 4)
    dkf_o[0] = dk
    dkb_o[0] = dk.astype(BF16)
    dvf_o[0] = proj('d_v', 256)
    gate_o[0, :, 768:1024] = _silu(proj('d_g', 256)).astype(BF16)

    lane = lax.broadcasted_iota(I32, (1, 128), 1)
    t = proj('c_k', 128)
    m64 = lane < HEAD_DIM
    ms1 = jnp.sum(jnp.where(m64, t * t, 0.0), axis=-1, keepdims=True) * (1.0 / HEAD_DIM)
    t = jnp.where(m64, t * lax.rsqrt(ms1 + EPS), t) * srow_ref[0:1, :]
    t = _rotate(t, tab_ref[:, TAB_S1:TAB_S1 + 128], tab_ref[:, TAB_S1 + 128:TAB_S1 + 256],
                tab_ref[:, TAB_S1 + 256:TAB_S1 + 384], ROT_DIM // 2)
    ck_o[0] = t[:, :HEAD_DIM]
    cv_o[0] = t[:, HEAD_DIM:]
    t = proj('b_kr', 128)
    m32 = lane < MLA_ROPE
    ms2 = jnp.sum(jnp.where(m32, t * t, 0.0), axis=-1, keepdims=True) * (1.0 / MLA_ROPE)
    t = jnp.where(m32, t * lax.rsqrt(ms2 + EPS), t) * srow_ref[1:2, :]
    t = _rotate(t, tab_ref[:, TAB_S2:TAB_S2 + 128], tab_ref[:, TAB_S2 + 128:TAB_S2 + 256],
                tab_ref[:, TAB_S2 + 256:TAB_S2 + 384], MLA_ROPE // 2)
    s2_o[0] = t
    bkr_o[0] = t[:, :MLA_ROPE]
    cik_o[0] = t[:, MLA_ROPE:MLA_ROPE + IDX_DIM]

    xt = _dot_nt(wt_ref[...], xn)
    xt_o[0, 0:XT_DF, :] = xt[0:XT_DF]
    xt_o[0, XT_DF:XT_ROWS, :] = _log_sigmoid(xt[XT_DF:XT_ROWS] + bcol_ref[...])


def _inproj_call(x, tab, lw, tm):
    bk, tk_, _ = x.shape
    grid = (tk_ // tm, bk)

    def const(shape):
        return pl.BlockSpec(shape, lambda i, b: (0,) * len(shape))

    def rows(w):
        return pl.BlockSpec((1, tm, w), lambda i, b: (b, i, 0))

    in_specs = [
        rows(D_MODEL),
        pl.BlockSpec((tm, TAB_W), lambda i, b: (i, 0)),
        const((1, D_MODEL)), const((D_MODEL, W_IN_COLS)), const((XT_ROWS, D_MODEL)), const((8, 1)),
        const((8, 256)), const((256, 256)), const((1, MLA_Q_LORA)), const((MLA_Q_LORA, 512)), const((256, 256)),
        const((1, 512)), const((1, MLA_KV_LORA)), const((8, 128)),
    ]
    widths = [(256, BF16), (256, BF16), (256, BF16), (256, F32), (256, F32), (512, BF16), (128, F32), (256, BF16),
              (256, BF16), (64, F32), (64, F32), (128, F32), (32, F32), (32, F32), (256, BF16), (256, BF16), (256, F32),
              (256, F32)]
    out_shape = [jax.ShapeDtypeStruct((bk, tk_, w), dt) for w, dt in widths]
    out_specs = [rows(w) for w, _ in widths]
    out_shape.append(jax.ShapeDtypeStruct((bk, XT_ROWS, tk_), F32))
    out_specs.append(pl.BlockSpec((1, XT_ROWS, tm), lambda i, b: (b, 0, i)))
    out_shape.append(jax.ShapeDtypeStruct((bk, tk_, 1024), BF16))
    out_specs.append(rows(1024))
    return pl.pallas_call(
        _inproj_kernel, grid=grid, in_specs=in_specs, out_specs=out_specs, out_shape=out_shape,
        compiler_params=_cparams(2), name="inproj",
    )(x, tab, lw['g_in'], lw['w_in'], lw['w_t'], lw['b_col'], lw['g4'], lw['hseg'], lw['g_cq'], lw['w_uq'],
      lw['segq'], lw['g_qb'], lw['g_ckv'], lw['srow'])


def _bkv_kernel(ckv_ref, kr_ref, wk_ref, wvt_ref, e_ref, segq_ref, gk_ref, kb_o, vt_o):
    c = ckv_ref[0].astype(BF16)
    kn = _dot(c, wk_ref[...])
    segq = segq_ref[...]
    ms = jnp.concatenate([_seg_mean_sq(kn[:, :256], segq), _seg_mean_sq(kn[:, 256:], segq)], axis=1)
    kn = kn * lax.rsqrt(ms + EPS) * gk_ref[...]
    kr = _dot(kr_ref[0].astype(BF16), e_ref[...])
    kb_o[0] = (kn + kr).astype(BF16)
    vt_o[0] = _dot_nt(wvt_ref[...], c).astype(BF16)


def _bkv_call(ckv, kr, lw, tm):
    b, lp, _ = ckv.shape

    def const(shape):
        return pl.BlockSpec(shape, lambda bb, i: (0,) * len(shape))

    def rows(w):
        return pl.BlockSpec((1, tm, w), lambda bb, i: (bb, i, 0))

    return pl.pallas_call(
        _bkv_kernel, grid=(b, lp // tm),
        in_specs=[rows(128), rows(128), const((128, 512)), const((256, 128)), const((128, 512)), const((256, 256)),
                  const((1, 512))],
        out_specs=[rows(512), pl.BlockSpec((1, 256, tm), lambda bb, i: (bb, 0, i))],
        out_shape=[jax.ShapeDtypeStruct((b, lp, 512), BF16), jax.ShapeDtypeStruct((b, 256, lp), BF16)],
        compiler_params=_cparams(2), name="mla_kv",
    )(ckv, kr, lw['w_uk'], lw['w_uvt'], lw['e_kr'], lw['segq'], lw['g_kb'])


def _head_lane_id(width=256):
    return lax.broadcasted_iota(I32, (1, width), 1) // HEAD_DIM


def _keep_lanes(x, pred):
    return jnp.where(pred, x.astype(F32), 0.0).astype(BF16)


def _flash_run(nfull, nkb, nkb_max, *, qm, k_block, v_block, col_sub, row_add, mask_fn, s_refs, acc_ref, tq):
    s_a, s_b = s_refs

    def produce(j, s_out):
        jc = jnp.minimum(j, nkb_max - 1)
        for h in range(N_HEADS):
            s_out[h] = _dot_nt(k_block(jc, h), qm[h])

    def half(j, c, s_in, s_out, masked):
        ms, ls = c
        produce(j + 1, s_out)
        valid = mask_fn(j) if masked else None
        new_m, new_l = [], []
        for h in range(N_HEADS):
            t = s_in[h]
            if col_sub is not None:
                t = t - col_sub(j, h)
            if masked:
                t = jnp.where(valid, t, NEG)
            m_cur = jnp.max(t, axis=0, keepdims=True)
            if row_add is not None:
                m_cur = m_cur + row_add[h]
            m_new = jnp.maximum(ms[h], m_cur)
            alpha = jnp.exp2(ms[h] - m_new)
            off = m_new if row_add is None else m_new - row_add[h]
            p = jnp.exp2(t - off).astype(BF16)
            pv = _dot(v_block(j, h), p)
            r0 = h * HEAD_DIM
            acc_ref[r0:r0 + HEAD_DIM, :] = acc_ref[r0:r0 + HEAD_DIM, :] * alpha + pv[:HEAD_DIM]
            new_l.append(ls[h] * alpha + pv[HEAD_DIM:HEAD_DIM + 1])
            new_m.append(m_new)
        return tuple(new_m), tuple(new_l)

    def step(j, c, s_in, s_out, may_end):
        def run(c):
            if nfull is None:
                return half(j, c, s_in, s_out, True)
            return lax.cond(j >= nfull, lambda cc: half(j, cc, s_in, s_out, True),
                            lambda cc: half(j, cc, s_in, s_out, False), c)
        if not may_end:
            return run(c)
        return lax.cond(j >= nkb, lambda cc: cc, run, c)

    def body(i, c):
        c = step(2 * i, c, s_a, s_b, False)
        return step(2 * i + 1, c, s_b, s_a, True)

    acc_ref[...] = jnp.zeros(acc_ref.shape, F32)
    init = (tuple(jnp.full((1, tq), NEG, F32) for _ in range(N_HEADS)),
            tuple(jnp.zeros((1, tq), F32) for _ in range(N_HEADS)))
    produce(0, s_a)
    _, ls = lax.fori_loop(0, (nkb + 1) // 2, body, init)
    return ls


def _flash_scratch(tq, tk):
    return [pltpu.VMEM((N_HEADS, tk, tq), F32), pltpu.VMEM((N_HEADS, tk, tq), F32), pltpu.VMEM((256, tq), F32)]


def _flash_finish(o_ref, g_ref, acc_ref, ls):
    for h in range(N_HEADS):
        r0 = h * HEAD_DIM
        acc_ref[r0:r0 + HEAD_DIM, :] = acc_ref[r0:r0 + HEAD_DIM, :] / ls[h]
    y = acc_ref[...].T
    o_ref[0] = (y * g_ref[0].astype(F32)).astype(BF16)


A_WIN = A_BAND + 2 * CHUNK


def _attn_a_kernel(q_ref, k_ref, v_ref, bias_ref, g_ref, o_ref, *, lo_valid, hi_valid):
    c = pl.program_id(1)
    start = pl.multiple_of(c * CHUNK, CHUNK)
    q = q_ref[0]
    hid = _head_lane_id()
    qs = jnp.concatenate([_keep_lanes(q, hid == h) for h in range(N_HEADS)], axis=0)
    kb = k_ref[0, pl.ds(start, A_WIN), :]
    vb = v_ref[0, pl.ds(start, A_WIN), :]
    s = _dot_nt(qs, kb) + bias_ref[...]
    row = start + lax.broadcasted_iota(I32, (1, A_WIN), 1)
    valid = (row >= lo_valid) & (row < hi_valid)
    s = jnp.where(valid, s, NEG)
    m = jnp.max(s, axis=-1, keepdims=True)
    p = jnp.where(valid, jnp.exp(s - m), 0.0)
    l = jnp.sum(p, axis=-1, keepdims=True)
    o = _dot(p.astype(BF16), vb) / l
    y = o[(N_HEADS - 1) * CHUNK:]
    for h in range(N_HEADS - 2, -1, -1):
        y = jnp.where(hid == h, o[h * CHUNK:(h + 1) * CHUNK], y)
    o_ref[0] = (y * g_ref[0].astype(F32)).astype(BF16)


def _attn_a_call(q, kfull, vfull, bias, gates, lo_valid, hi_valid):
    b, tp, _ = q.shape
    rows_kv = kfull.shape[1]
    kern = functools.partial(_attn_a_kernel, lo_valid=lo_valid, hi_valid=hi_valid)
    return pl.pallas_call(
        kern, grid=(b, tp // CHUNK),
        in_specs=[pl.BlockSpec((1, CHUNK, 256), lambda bb, c: (bb, c, 0)),
                  pl.BlockSpec((1, rows_kv, 256), lambda bb, c: (bb, 0, 0)),
                  pl.BlockSpec((1, rows_kv, 256), lambda bb, c: (bb, 0, 0)),
                  pl.BlockSpec((N_HEADS * CHUNK, A_WIN), lambda bb, c: (0, 0)),
                  pl.BlockSpec((1, CHUNK, 256), lambda bb, c: (bb, c, 0))],
        out_specs=pl.BlockSpec((1, CHUNK, 256), lambda bb, c: (bb, c, 0)),
        out_shape=jax.ShapeDtypeStruct((b, tp, 256), BF16),
        compiler_params=_cparams(2), name="attn_band",
    )(q, kfull, vfull, bias, gates)


def _attn_b_kernel(q_ref, k_ref, v_ref, g_ref, o_ref, sa_ref, sb_ref, acc_ref, *, p_len, l_len, tq, tk, nq):
    i = pl.program_id(1) if nq > 1 else 0
    q0 = p_len + i * tq
    lane = lax.broadcasted_iota(I32, (1, 256), 1)
    qm = []
    for h in range(N_HEADS):
        qg = q_ref[0, :, (h // 2) * 256:(h // 2) * 256 + 256]
        lo = (h % 2) * (MLA_NOPE + MLA_ROPE)
        qm.append(_keep_lanes(qg, (lane >= lo) & (lane < lo + MLA_NOPE + MLA_ROPE)))
    qchunk = (q0 + lax.broadcasted_iota(I32, (1, tq), 1)) // CHUNK
    nkb = jnp.minimum((q0 + tq + tk - 1) // tk, k_ref.shape[1] // tk)
    nfull = jnp.minimum(((q0 // CHUNK + 1) * CHUNK) // tk, nkb)

    def k_block(j, h):
        return k_ref[0, pl.ds(pl.multiple_of(j * tk, tk), tk), (h // 2) * 256:(h // 2) * 256 + 256]

    def mask_fn(j):
        kpos = j * tk + lax.broadcasted_iota(I32, (tk, 1), 0)
        return ((kpos // CHUNK) <= qchunk) & (kpos < l_len)

    ls = _flash_run(nfull, nkb, k_ref.shape[1] // tk, qm=qm, k_block=k_block, v_block=lambda j, h: v_ref[0, j, h],
                    col_sub=None, row_add=None, mask_fn=mask_fn, s_refs=(sa_ref, sb_ref), acc_ref=acc_ref, tq=tq)
    _flash_finish(o_ref, g_ref, acc_ref, ls)


def _attn_b_call(qb, kb, vaug, gates, p_len, l_len, tq, tk):
    b, t, _ = qb.shape
    lp = kb.shape[1]
    nq = t // tq
    kern = functools.partial(_attn_b_kernel, p_len=p_len, l_len=l_len, tq=tq, tk=tk, nq=nq)
    return pl.pallas_call(
        kern, grid=(b, nq),
        in_specs=[pl.BlockSpec((1, tq, 512), lambda bb, i: (bb, i, 0)),
                  pl.BlockSpec((1, lp, 512), lambda bb, i: (bb, 0, 0)),
                  pl.BlockSpec((1,) + vaug.shape[1:], lambda bb, i: (bb, 0, 0, 0, 0)),
                  pl.BlockSpec((1, tq, 256), lambda bb, i: (bb, i, 1))],
        out_specs=pl.BlockSpec((1, tq, 256), lambda bb, i: (bb, i, 0)),
        out_shape=jax.ShapeDtypeStruct((b, t, 256), BF16),
        scratch_shapes=_flash_scratch(tq, tk),
        compiler_params=_cparams(2), name="attn_latent",
    )(qb, kb, vaug, gates)


def _attn_c_kernel(q_ref, iq_ref, iw_ref, k_ref, v_ref, ik_ref, g_ref, o_ref, key_ref, top_ref, sa_ref, sb_ref,
                   acc_ref,
                   *, p_len, l_len, k_sel, tq, tk, nq):
    i = pl.program_id(1) if nq > 1 else 0
    q0 = p_len + i * tq
    nkb_max = k_ref.shape[1] // tk
    nkb = jnp.minimum((q0 + tq + tk - 1) // tk, nkb_max)
    qchunk = (q0 + lax.broadcasted_iota(I32, (1, tq), 1)) // CHUNK
    hid = _head_lane_id()

    iq = iq_ref[0]
    ihid = lax.broadcasted_iota(I32, (1, 256), 1) // IDX_DIM
    iqm = [_keep_lanes(iq, ihid == h) for h in range(IDX_HEADS)]
    iw = iw_ref[0] * (IDX_DIM ** -0.5 * IDX_HEADS ** -0.5)

    def score_body(j, carry):
        ks = pl.multiple_of(j * tk, tk)
        ikb = ik_ref[0, pl.ds(ks, tk), :]
        score = jnp.zeros((tk, tq), F32)
        for h in range(IDX_HEADS):
            score = score + iw[h:h + 1, :] * jnp.maximum(_dot_nt(ikb, iqm[h]), 0.0)
        kpos = ks + lax.broadcasted_iota(I32, (tk, 1), 0)
        adm = ((kpos // CHUNK) <= qchunk) & (kpos < l_len)
        bits = pltpu.bitcast(score, I32)
        key = jnp.where(bits < 0, bits ^ 0x7FFFFFFF, bits)
        key = jnp.where(key == -1, 0, key)
        key_ref[j] = jnp.where(adm, key, INT_MIN)
        top = pltpu.bitcast(bits & -65536, F32)
        top_ref[j] = jnp.where(adm, top, -jnp.inf).astype(BF16)
        return carry

    lax.fori_loop(0, nkb, score_body, 0)

    def count(pred_fn):
        def cbody(j, acc):
            pf = jnp.where(pred_fn(key_ref[j], j), 1.0, 0.0)
            for r in range(tk // 8):
                acc = acc + pf[r * 8:(r + 1) * 8]
            return acc
        acc = lax.fori_loop(0, nkb, cbody, jnp.zeros((8, tq), F32))
        return jnp.sum(acc, axis=0, keepdims=True)

    def count_top(cand_b):
        one, zero = jnp.ones((16, tq), BF16), jnp.zeros((16, tq), BF16)

        def cbody(j, acc):
            part = zero
            for r in range(tk // 16):
                part = part + jnp.where(top_ref[j, r * 16:(r + 1) * 16, :] >= cand_b, one, zero)
            return acc + part.astype(F32)
        acc = lax.fori_loop(0, nkb, cbody, jnp.zeros((16, tq), F32))
        return jnp.sum(acc, axis=0, keepdims=True)

    def top_body(it, c):
        t, n_ge = c
        cand = t + jnp.left_shift(jnp.int32(1), 15 - it)
        cbits = jnp.where(cand < 0, cand ^ 0x7FFF, cand) & 0xFFFF
        cbits = jnp.where((cbits > 0) & (cbits < 0x80), 0x80, cbits)
        cval =pltpu.bitcast(jnp.left_shift(cbits, 16), F32)
        cnt = count_top(jnp.broadcast_to(cval, (16, tq)).astype(BF16))
        ok = cnt >= k_sel
        return jnp.where(ok, cand, t), jnp.where(ok, cnt, n_ge)

    t16, n_ge = lax.fori_loop(0, 16, top_body, (jnp.full((1, tq), -32768, I32), jnp.zeros((1, tq), F32)))

    def bis_body(it, c):
        t, n_ge = c
        cand = t + jnp.left_shift(jnp.int32(1), 15 - it)
        cnt = count(lambda kb, j: kb >= cand)
        ok = cnt >= k_sel
        return jnp.where(ok, cand, t), jnp.where(ok, cnt, n_ge)

    thr, n_ge = lax.fori_loop(0, 16, bis_body, (t16 * 65536, n_ge))
    thr = jnp.maximum(thr, INT_MIN + 1)

    def idx_of(j):
        return j * tk + lax.broadcasted_iota(I32, (tk, tq), 0)

    def tie_cut(_):
        need = k_sel - count(lambda kb, j: kb > thr)

        def tie_body(it, jc):
            cand = jc + jnp.left_shift(jnp.int32(1), 12 - it)
            cnt = count(lambda kb, j: (kb == thr) & (idx_of(j) < cand))
            return jnp.where(cnt < need, cand, jc)

        return lax.fori_loop(0, 13, tie_body, jnp.zeros((1, tq), I32))

    has_tie = jnp.max(jnp.where(n_ge > k_sel, 1.0, 0.0)) > 0.0
    jcut = lax.cond(has_tie, tie_cut, lambda _: jnp.full((1, tq), 2 ** 30, I32), 0)

    q = q_ref[0]
    qm = [_keep_lanes(q, hid == h) for h in range(N_HEADS)]

    def mask_fn(j):
        kb = key_ref[j]
        return (kb > thr) | ((kb == thr) & (idx_of(j) <= jcut))

    ls = _flash_run(None, nkb, nkb_max, qm=qm,
                    k_block=lambda j, h: k_ref[0, pl.ds(pl.multiple_of(j * tk, tk), tk), :],
                    v_block=lambda j, h: v_ref[0, j, 0], col_sub=None, row_add=None, mask_fn=mask_fn,
                    s_refs=(sa_ref, sb_ref), acc_ref=acc_ref, tq=tq)
    _flash_finish(o_ref, g_ref, acc_ref, ls)


def _attn_c_call(cq, ciq, iwt, krep, vaug, ikrep, gates, p_len, l_len, k_sel, tq, tk):
    b, t, _ = cq.shape
    lp = krep.shape[1]
    assert lp <= 8192
    nq = t // tq
    kern = functools.partial(_attn_c_kernel, p_len=p_len, l_len=l_len, k_sel=float(k_sel), tq=tq, tk=tk, nq=nq)
    return pl.pallas_call(
        kern, grid=(b, nq),
        in_specs=[pl.BlockSpec((1, tq, 256), lambda bb, i: (bb, i, 0)),
                  pl.BlockSpec((1, tq, 256), lambda bb, i: (bb, i, 0)),
                  pl.BlockSpec((1, 8, tq), lambda bb, i: (bb, 0, i)),
                  pl.BlockSpec((1, lp, 256), lambda bb, i: (bb, 0, 0)),
                  pl.BlockSpec((1,) + vaug.shape[1:], lambda bb, i: (bb, 0, 0, 0, 0)),
                  pl.BlockSpec((1, lp, 256), lambda bb, i: (bb, 0, 0)),
                  pl.BlockSpec((1, tq, 256), lambda bb, i: (bb, i, 2))],
        out_specs=pl.BlockSpec((1, tq, 256), lambda bb, i: (bb, i, 0)),
        out_shape=jax.ShapeDtypeStruct((b, t, 256), BF16),
        scratch_shapes=[pltpu.VMEM((lp // tk, tk, tq), I32), pltpu.VMEM((lp // tk, tk, tq), BF16)]
        + _flash_scratch(tq, tk),
        compiler_params=_cparams(2), name="attn_sparse",
    )(cq, ciq, iwt, krep, vaug, ikrep, gates)


def _attn_d_kernel(q_ref, k_ref, v_ref, lf_ref, g_ref, o_ref, fc_ref, fk_ref, sa_ref, sb_ref, acc_ref,
                   *, p_len, tq, tk, nq):
    i = pl.program_id(1) if nq > 1 else 0
    lp = k_ref.shape[1]
    nch = lp // 128

    def cumulate():
        x = lf_ref[0]
        lane = lax.broadcasted_iota(I32, (1, lp), 1)
        step = 1
        while step < lp:
            x = x + jnp.where(lane >= step, pltpu.roll(x, step, 1), 0.0)
            step *= 2
        x = x * LOG2E
        for c in range(nch):
            fc_ref[c] = x[:, c * 128:(c + 1) * 128]

        eye = lax.broadcasted_iota(I32, (128, 128), 0) == lax.broadcasted_iota(I32, (128, 128), 1)

        def spread(c, carry):
            rows = fc_ref[c]
            for h in range(N_HEADS):
                col = jnp.sum(jnp.where(eye, rows[h:h + 1, :], 0.0), axis=1, keepdims=True)
                fk_ref[h, pl.ds(pl.multiple_of(c * 128, 128), 128), :] = jnp.broadcast_to(col, (128, 128))
            return carry

        lax.fori_loop(0, nch, spread, 0)

    if nq > 1:
        pl.when(i == 0)(cumulate)
    else:
        cumulate()

    q0 = p_len + i * tq
    hid = _head_lane_id()
    q = q_ref[0]
    qm = [_keep_lanes(q, hid == h) for h in range(N_HEADS)]
    c0 = q0 // 128
    fq_rows = [fc_ref[c0 + c] for c in range(tq // 128)]
    fq = [jnp.concatenate([r[h:h + 1, :] for r in fq_rows], axis=1) for h in range(N_HEADS)]
    qpos = q0 + lax.broadcasted_iota(I32, (1, tq), 1)
    nkb = jnp.minimum((q0 + tq + tk - 1) // tk, lp // tk)
    nfull = jnp.minimum((q0 + 1) // tk, nkb)

    def col_sub(j, h):
        fk = fk_ref[h, pl.ds(pl.multiple_of(j * tk, tk), tk), :]
        return jnp.concatenate([fk] * (tq // 128), axis=1)

    def mask_fn(j):
        return (j * tk + lax.broadcasted_iota(I32, (tk, 1), 0)) <= qpos

    ls = _flash_run(nfull, nkb, lp // tk, qm=qm,
                    k_block=lambda j, h: k_ref[0, pl.ds(pl.multiple_of(j * tk, tk), tk), :],
                    v_block=lambda j, h: v_ref[0, j, h], col_sub=col_sub, row_add=fq, mask_fn=mask_fn,
                    s_refs=(sa_ref, sb_ref), acc_ref=acc_ref, tq=tq)
    _flash_finish(o_ref, g_ref, acc_ref, ls)


def _attn_d_call(dq, kd, vaug, lft, gates, p_len, tq, tk):
    b, t, _ = dq.shape
    lp = kd.shape[1]
    nq = t // tq
    assert p_len % 128 == 0 and tq % 128 == 0 and p_len + t <= lp
    kern = functools.partial(_attn_d_kernel, p_len=p_len, tq=tq, tk=tk, nq=nq)
    return pl.pallas_call(
        kern, grid=(b, nq),
        in_specs=[pl.BlockSpec((1, tq, 256), lambda bb, i: (bb, i, 0)),
                  pl.BlockSpec((1, lp, 256), lambda bb, i: (bb, 0, 0)),
                  pl.BlockSpec((1,) + vaug.shape[1:], lambda bb, i: (bb, 0, 0, 0, 0)),
                  pl.BlockSpec((1, 8, lp), lambda bb, i: (bb, 0, 0)),
                  pl.BlockSpec((1, tq, 256), lambda bb, i: (bb, i, 3))],
        out_specs=pl.BlockSpec((1, tq, 256), lambda bb, i: (bb, i, 0)),
        out_shape=jax.ShapeDtypeStruct((b, t, 256), BF16),
        scratch_shapes=[pltpu.VMEM((lp // 128, 8, 128), F32), pltpu.VMEM((N_HEADS, lp, 128), F32)]
        + _flash_scratch(tq, tk),
        compiler_params=_cparams(2), name="attn_forget",
    )(dq, kd, vaug, lft, gates)


def _out_kernel(ya_ref, yb_ref, yc_ref, yd_ref, x_ref, p_ref, wo_ref, gple_ref, wg_ref, wp_ref, o_ref):
    mixed = (_dot(ya_ref[0], wo_ref[0:256, :]) + _dot(yb_ref[0], wo_ref[256:512, :])
             + _dot(yc_ref[0], wo_ref[512:768, :]) + _dot(yd_ref[0], wo_ref[768:1024, :]))
    x1 = x_ref[0] + mixed
    ms = jnp.mean(x1 * x1, axis=-1, keepdims=True)
    xn = (x1 * lax.rsqrt(ms + EPS) * gple_ref[...]).astype(BF16)
    gate = 1.0 / (1.0 + jnp.exp(-_dot(xn, wg_ref[...])))
    o_ref[0] = x1 + gate * _dot(p_ref[0].astype(BF16), wp_ref[...])


def _out_call(ys, x, p, lw, tm):
    b, t, _ = x.shape

    def const(shape):
        return pl.BlockSpec(shape, lambda bb, i: (0,) * len(shape))

    def rows(w):
        return pl.BlockSpec((1, tm, w), lambda bb, i: (bb, i, 0))

    return pl.pallas_call(
        _out_kernel, grid=(b, t // tm),
        in_specs=[rows(256)] * 4 + [rows(D_MODEL), rows(PLE_DIM), const((1024, D_MODEL)), const((1, D_MODEL)),
                                    const((D_MODEL, D_MODEL)), const((PLE_DIM, D_MODEL))],
        out_specs=rows(D_MODEL),
        out_shape=jax.ShapeDtypeStruct((b, t, D_MODEL), F32),
        compiler_params=_cparams(2), name="out_proj",
    )(*ys, x, p, lw['w_out'], lw['g_ple'], lw['w_ple_gate'], lw['w_ple_proj'])


def _seg_matrix(segments):
    m = np.zeros((256, 256), np.float32)
    for lo, n in segments:
        m[lo:lo + n, lo:lo + n] = 1.0 / n
    return jnp.asarray(m, BF16)


_PAIR_SEGS = ((0, 64), (64, 32), (96, 64), (160, 32), (192, 64))


def _pair_cols(head):
    return (head // 2) * 256 + (head % 2) * (MLA_NOPE + MLA_ROPE)


def _prep_layer(i, g_in, w_in, g_qk, g_rope, a_rel_bias, b_g_cq, b_w_uq, b_g_ckv, b_w_ukv, d_b_f, w_out, g_ple,
                w_ple_gate, w_ple_proj):
    w = w_in[i]
    cols = []
    for n in _MY_ORDER:
        if n == 'pad56':
            cols.append(jnp.zeros((D_MODEL, 56), F32))
        else:
            o, wd = _REF_OFF[n]
            cols.append(w[:, o:o + wd])
    lw = {'w_in': jnp.concatenate(cols, axis=1).astype(BF16), 'g_in': g_in[i][None, :]}
    wt = [w[:, _REF_OFF[n][0]:_REF_OFF[n][0] + _REF_OFF[n][1]] for n in ('d_v', 'c_v', 'c_iw', 'd_f')]
    wt.append(jnp.zeros((D_MODEL, XT_ROWS - XT_DF - N_HEADS), F32))
    lw['w_t'] = jnp.concatenate(wt, axis=1).T.astype(BF16)
    lw['b_col'] = jnp.concatenate([d_b_f[i], jnp.zeros((8 - N_HEADS,), F32)])[:, None]
    g = g_qk[i]
    sc = HEAD_DIM ** -0.5
    rows = [jnp.tile(g[0], 4) * sc, jnp.tile(g[1], 4), jnp.tile(g[2], 4) * (sc * LOG2E),
            jnp.tile(g[4], 4) * (sc * LOG2E), jnp.tile(g[5], 4)]
    lw['g4'] = jnp.stack(rows + [jnp.zeros((256,), F32)] * 3)
    lw['hseg'] = _seg_matrix(tuple((h * 64, 64) for h in range(4)))
    lw['segq'] = _seg_matrix(_PAIR_SEGS)
    lw['g_cq'] = b_g_cq[i][None, :]
    lw['g_ckv'] = b_g_ckv[i][None, :]
    qscale = (MLA_NOPE + MLA_ROPE) ** -0.5 * LOG2E
    hw = MLA_NOPE + MLA_ROPE
    wuq, gqb, wuk, gkb, wuv = [], [], [], [], []
    e_kr = np.zeros((128, 512), np.float32)
    for h in range(N_HEADS):
        wuq.append(b_w_uq[i][:, h * hw:(h + 1) * hw])
        gqb += [g[6] * qscale, g_rope[i][0] * qscale]
        srck = h * (MLA_NOPE + HEAD_DIM)
        wuk += [b_w_ukv[i][:, srck:srck + MLA_NOPE], jnp.zeros((MLA_KV_LORA, MLA_ROPE), F32)]
        gkb += [g[7], jnp.zeros((MLA_ROPE,), F32)]
        wuv.append(b_w_ukv[i][:, srck + MLA_NOPE:srck + MLA_NOPE + HEAD_DIM])
        e_kr[np.arange(MLA_ROPE), _pair_cols(h) + MLA_NOPE + np.arange(MLA_ROPE)] = 1.0
        if h % 2 == 1:
            wuq.append(jnp.zeros((MLA_Q_LORA, 256 - 2 * hw), F32))
            wuk.append(jnp.zeros((MLA_KV_LORA, 256 - 2 * hw), F32))
            gqb.append(jnp.zeros((256 - 2 * hw,), F32))
            gkb.append(jnp.zeros((256 - 2 * hw,), F32))
    lw['w_uq'] = jnp.concatenate(wuq, axis=1).astype(BF16)
    lw['g_qb'] = jnp.concatenate(gqb)[None, :]
    lw['w_uk'] = jnp.concatenate(wuk, axis=1).astype(BF16)
    lw['g_kb'] = jnp.concatenate(gkb)[None, :]
    lw['e_kr'] = jnp.asarray(e_kr, BF16)
    lw['w_uvt'] = jnp.concatenate(wuv, axis=1).T.astype(BF16)
    one = jnp.ones((128,), F32)
    lw['srow'] = jnp.stack([jnp.concatenate([g[3], one[HEAD_DIM:]]), jnp.concatenate([g_rope[i][1], one[MLA_ROPE:]])]
                           + [one] * 6)
    ab = a_rel_bias[i]
    ext = jnp.concatenate([jnp.broadcast_to(ab[:, 2 * REL_CLIP:], (N_HEADS, A_WIN - REL_CLIP - 1)),
                           ab[:, REL_CLIP - CHUNK + 1:][:, ::-1]], axis=1)
    lw['a_bias'] = jnp.stack([ext[:, CHUNK - 1 - q:CHUNK - 1 - q + A_WIN] for q in range(CHUNK)],
                             axis=1).reshape(N_HEADS * CHUNK, A_WIN)
    lw['w_out'] = w_out[i].astype(BF16)
    lw['g_ple'] = g_ple[i][None, :]
    lw['w_ple_gate'] = w_ple_gate[i].astype(BF16)
    lw['w_ple_proj'] = w_ple_proj[i].astype(BF16)
    return lw


def _rot_section(pos, width, starts, rot_dim):
    half = rot_dim // 2
    inv = jnp.float32(ROPE_THETA) ** (-jnp.arange(half, dtype=F32) / half)
    ang = pos.astype(F32)[:, None] * inv[None, :]
    cos, sin = jnp.cos(ang), jnp.sin(ang)
    t = pos.shape[0]
    zh = jnp.zeros((t, half), F32)
    c, s_up, s_dn = [], [], []
    at = 0
    for st in list(starts) + [width]:
        gap = st - at
        c.append(jnp.ones((t, gap), F32))
        s_up.append(jnp.zeros((t, gap), F32))
        s_dn.append(jnp.zeros((t, gap), F32))
        if st < width:
            c += [cos, cos]
            s_up += [zh, sin]
            s_dn += [-sin, zh]
        at = st + rot_dim
    return c + s_up + s_dn


def _rot_tables(pos):
    secs = (_rot_section(pos, 512, [_pair_cols(h) + MLA_NOPE for h in range(N_HEADS)], MLA_ROPE)
            + _rot_section(pos, 256, [h * HEAD_DIM for h in range(N_HEADS)], ROT_DIM)
            + _rot_section(pos, 128, [0], ROT_DIM)
            + _rot_section(pos, 128, [0], MLA_ROPE))
    return jnp.concatenate(secs, axis=1)


def _pad_axis(a, size, axis):
    pads = [(0, 0)] * a.ndim
    pads[axis] = (0, size - a.shape[axis])
    return jnp.pad(a, pads)


def _cat_rows(past, new, rows):
    a = new if past is None else jnp.concatenate([past.astype(new.dtype), new], axis=1)
    return _pad_axis(a, rows, 1)


def _cat_lanes(past_t, new_t, lanes):
    a = new_t if past_t is None else jnp.concatenate([past_t.astype(new_t.dtype), new_t], axis=2)
    return _pad_axis(a, lanes, 2)


def _value_blocks(vt, tk):
    b, r, lp = vt.shape
    h = r // HEAD_DIM
    v = vt.astype(BF16).reshape(b, h, HEAD_DIM, lp)
    v = jnp.concatenate([v, jnp.ones((b, h, V_ROWS - HEAD_DIM, lp), BF16)], axis=2)
    return v.reshape(b, h, V_ROWS, lp // tk, tk).transpose(0, 3, 1, 2, 4)


def _layer(x, p, past, lw, tab, *, fold_batch, tm, tq, tk):
    b, t, _ = x.shape
    p_len = 0 if past is None else past[2].shape[1]
    l_len = p_len + t
    tqp = -(-t // tq) * tq
    lp = -(-(p_len + tqp) // tk) * tk

    xin = x.reshape(1, b * t, D_MODEL) if fold_batch else x
    outs = _inproj_call(xin, tab, lw, tm)
    if fold_batch:
        xt = outs[18][0].reshape(XT_ROWS, b, t).transpose(1, 0, 2)
        outs = [o.reshape((b, t) + o.shape[2:]) for o in outs[:18]] + [xt, outs[19].reshape(b, t, 1024)]
    (aq, akb, avb, akf, avf, qb, ckvf, cq, ciq, c_k, c_v, s2, b_kr, c_ik, dq, dkb, dkf, dvf, xt, gates) = outs

    lft = xt[:, XT_DF:XT_ROWS]
    d_lf = lft[:, :N_HEADS, :].transpose(0, 2, 1)
    state = (akf.reshape(b, t, N_HEADS, HEAD_DIM), avf.reshape(b, t, N_HEADS, HEAD_DIM), ckvf, b_kr, c_k, c_v, c_ik,
             dkf.reshape(b, t, N_HEADS, HEAD_DIM), dvf.reshape(b, t, N_HEADS, HEAD_DIM), d_lf)

    if past is None:
        pa_k = pa_v = pb_ckv = pb_kr = pc_k = pc_v = pc_ik = pd_k = pd_v = pd_lf = None
        pa = 0
    else:
        pa_k, pa_v, pb_ckv, pb_kr, pc_k, pc_v, pc_ik, pd_k, pd_v, pd_lf = past
        pa = pa_k.shape[1]
        pa_k = pa_k.reshape(b, pa, 256)
        pa_v = pa_v.reshape(b, pa, 256)
        pd_k = pd_k.reshape(b, p_len, 256)
        pd_v = pd_v.reshape(b, p_len, 256)

    tp = -(-t // CHUNK) * CHUNK
    front = CHUNK + A_BAND - pa
    a_rows = CHUNK + A_BAND + tp

    def band_src(pst, new):
        parts = [jnp.zeros((b, front, 256), BF16)]
        if pst is not None:
            parts.append(pst.astype(BF16))
        parts.append(new)
        return _pad_axis(jnp.concatenate(parts, axis=1), a_rows, 1)

    ya = _attn_a_call(_pad_axis(aq, tp, 1), band_src(pa_k, akb), band_src(pa_v, avb), lw['a_bias'],
                      _pad_axis(gates, tp, 1), front, CHUNK + A_BAND + t)[:, :t]

    gates_q = _pad_axis(gates, tqp, 1)

    ckv_all = _cat_rows(pb_ckv, ckvf, lp)
    kr_new = s2 if past is None else _pad_axis(b_kr, 128, 2)
    kr_past = None if past is None else _pad_axis(pb_kr, 128, 2)
    kr_all = _cat_rows(kr_past, kr_new, lp)
    kb, vbt = _bkv_call(ckv_all, kr_all, lw, tk)
    yb = _attn_b_call(_pad_axis(qb, tqp, 1), kb, _value_blocks(vbt, tk), gates_q, p_len, l_len, tq, tk)[:, :t]

    krep = jnp.tile(_cat_rows(pc_k, c_k, lp).astype(BF16), (1, 1, N_HEADS))
    ikrep = jnp.tile(_cat_rows(pc_ik, c_ik, lp).astype(BF16), (1, 1, IDX_HEADS))
    cvt = _cat_lanes(None if past is None else pc_v.transpose(0, 2, 1), xt[:, XT_CV:XT_IW], lp)
    iwt = _pad_axis(xt[:, XT_IW:XT_DF], tqp, 2)
    k_sel = min(DSA_TOPK, l_len // 4)
    yc = _attn_c_call(_pad_axis(cq, tqp, 1), _pad_axis(ciq, tqp, 1), iwt, krep, _value_blocks(cvt, tk), ikrep,
                      gates_q, p_len, l_len, k_sel, tq, tk)[:, :t]

    kd = _cat_rows(None if past is None else pd_k.astype(BF16), dkb, lp)
    dvt = _cat_lanes(None if past is None else pd_v.transpose(0, 2, 1), xt[:, XT_DV:XT_CV], lp)
    lf_past = None if past is None else _pad_axis(pd_lf.transpose(0, 2, 1), 8, 1)
    lf_all = _cat_lanes(lf_past, lft, lp)
    yd = _attn_d_call(_pad_axis(dq, tqp, 1), kd, _value_blocks(dvt, tk), lf_all, gates_q, p_len, tq, tk)[:, :t]

    if fold_batch:
        x_new = _out_call([y.reshape(1, b * t, 256) for y in (ya, yb, yc, yd)], x.reshape(1, b * t, D_MODEL),
                          p.reshape(1, b * t, PLE_DIM), lw, tm).reshape(b, t, D_MODEL)
    else:
        x_new = _out_call([ya, yb, yc, yd], x, p, lw, tm)
    return x_new, state


def kernel(x_prompt, x_sample, cache_a_k, cache_a_v, cache_b_ckv, cache_b_krope, cache_c_k, cache_c_v, cache_c_idx_k,
           cache_d_k, cache_d_v, cache_d_logf, p_prompt, p_sample, g_in, w_in, g_qk, g_rope, a_rel_bias, b_g_cq,
           b_w_uq, b_g_ckv, b_w_ukv, d_b_f, w_out, g_ple, w_ple_gate, w_ple_proj):
    depth = w_in.shape[0]
    b, t = x_prompt.shape[:2]
    bs, ts = x_sample.shape[:2]
    past_len = cache_b_ckv.shape[2]
    tab_p = _rot_tables(jnp.arange(t))
    tab_s = jnp.tile(_rot_tables(past_len + jnp.arange(ts)), (bs, 1))
    xp, xs = x_prompt, x_sample
    states_p, states_s = [], []
    for i in range(depth):
        lw = _prep_layer(i, g_in, w_in, g_qk, g_rope, a_rel_bias, b_g_cq, b_w_uq, b_g_ckv, b_w_ukv, d_b_f, w_out,
                         g_ple, w_ple_gate, w_ple_proj)
        xp, st_p = _layer(xp, p_prompt[i], None, lw, tab_p, fold_batch=False, tm=256, tq=256, tk=256)
        past = (cache_a_k[i], cache_a_v[i], cache_b_ckv[i], cache_b_krope[i], cache_c_k[i], cache_c_v[i],
                cache_c_idx_k[i], cache_d_k[i], cache_d_v[i], cache_d_logf[i])
        xs, st_s = _layer(xs, p_sample[i], past, lw, tab_s, fold_batch=True, tm=bs * ts, tq=128, tk=256)
        states_p.append(st_p)
        states_s.append(st_s)
    sp = [jnp.stack(z) for z in zip(*states_p)]
    ss = [jnp.stack(z) for z in zip(*states_s)]
    keep = min(A_BAND, t)
    sp[0] = sp[0][:, :, t - keep:]
    sp[1] = sp[1][:, :, t - keep:]
    return (xp, xs, *sp, *ss)
```

```python
import functools

import numpy as np
import jax
import jax.numpy as jnp
from jax import lax
from jax.experimental import pallas as pl
from jax.experimental.pallas import tpu as pltpu

F32 = jnp.float32
BF16 = jnp.bfloat16
I32 = jnp.int32

D_MODEL = 1024
CHUNK = 64
EPS = 1e-6
HEAD_DIM = 64
N_HEADS = 4
GROUP_WIDTH = 256
ROT_DIM = 16
ROPE_THETA = 500000.0
A_BAND = 8 * CHUNK
REL_CLIP = 128
MLA_Q_LORA = 384
MLA_KV_LORA = 128
MLA_NOPE = 64
MLA_ROPE = 32
IDX_HEADS = 8
IDX_DIM = 32
DSA_TOPK = 256
PLE_DIM = 256

_REF_SPLITS = (
    ('a_q', 256), ('a_k', 256), ('a_v', 256), ('a_g', 256),
    ('b_cq', 384), ('b_ckv', 128), ('b_kr', 32), ('b_g', 256),
    ('c_q', 256), ('c_k', 64), ('c_v', 64), ('c_iq', 256), ('c_ik', 32), ('c_iw', 8), ('c_g', 256),
    ('d_q', 256), ('d_k', 256), ('d_v', 256), ('d_f', 4), ('d_g', 256),
)
_REF_OFF = {}
_o = 0
for _n, _w in _REF_SPLITS:
    _REF_OFF[_n] = (_o, _w)
    _o += _w

_MY_ORDER = ('a_q', 'a_k', 'a_v', 'a_g', 'b_cq', 'b_ckv', 'b_g', 'c_q', 'c_iq', 'c_g', 'd_q', 'd_k', 'd_v', 'd_g',
             'c_k', 'c_v', 'b_kr', 'c_ik', 'c_iw', 'pad56')
_MY_OFF = {}
_o = 0
for _n in _MY_ORDER:
    _w = 56 if _n == 'pad56' else _REF_OFF[_n][1]
    _MY_OFF[_n] = _o
    _o += _w
W_IN_COLS = _o

XT_DV, XT_CV, XT_IW, XT_DF, XT_ROWS = 0, 256, 320, 328, 336

TAB_QB, TAB_CQ, TAB_S1, TAB_S2 = 0, 1536, 2304, 2688
TAB_W = 3072

V_ROWS = HEAD_DIM + 16
NEG = -1e30
LOG2E = 1.4426950408889634
INT_MIN = -2 ** 31
VMEM_LIMIT = 56 * 1024 * 1024


def _cparams(n_axes):
    return pltpu.CompilerParams(dimension_semantics=("arbitrary",) * n_axes, vmem_limit_bytes=VMEM_LIMIT)


def _dot(a, b):
    return jnp.dot(a, b, preferred_element_type=F32)


def _dot_nt(a, b):
    return lax.dot_general(a, b, (((1,), (1,)), ((), ())), preferred_element_type=F32)


def _seg_mean_sq(t, seg):
    sq = t * t
    hi = sq.astype(BF16)
    lo = (sq - hi.astype(F32)).astype(BF16)
    return _dot(hi, seg) + _dot(lo, seg)


def _rotate(t, cos, sin_up, sin_dn, half):
    w = t.shape[-1]
    return t * cos + pltpu.roll(t, half, 1) * sin_up + pltpu.roll(t, w - half, 1) * sin_dn


def _silu(g):
    return g * (1.0 / (1.0 + jnp.exp(-g)))


def _log_sigmoid(v):
    return jnp.minimum(v, 0.0) - jnp.log1p(jnp.exp(-jnp.abs(v)))


def _inproj_kernel(x_ref, tab_ref, gin_ref, w_ref, wt_ref, bcol_ref, g4_ref, hseg_ref, gcq_ref, wuq_ref, segq_ref,
                   gqb_ref, gckv_ref, srow_ref,
                   aq_o, akb_o, avb_o, akf_o, avf_o, qb_o, ckvf_o, cq_o, ciq_o, ck_o, cv_o, s2_o, bkr_o, cik_o,
                   dq_o, dkb_o, dkf_o, dvf_o, xt_o, gate_o):
    x = x_ref[0]
    ms = jnp.mean(x * x, axis=-1, keepdims=True)
    xn = (x * lax.rsqrt(ms + EPS) * gin_ref[...]).astype(BF16)

    def proj(name, n):
        c0 = _MY_OFF[name]
        return _dot(xn, w_ref[:, c0:c0 + n])

    hseg = hseg_ref[...]

    def headnorm(t, row):
        return t * lax.rsqrt(_seg_mean_sq(t, hseg) + EPS) * g4_ref[row:row + 1, :]

    def fullnorm(t, g):
        return t * lax.rsqrt(jnp.mean(t * t, axis=-1, keepdims=True) + EPS) * g

    aq_o[0] = headnorm(proj('a_q', 256), 0).astype(BF16)
    ak = headnorm(proj('a_k', 256), 1)
    akf_o[0] = ak
    akb_o[0] = ak.astype(BF16)
    av = proj('a_v', 256)
    avf_o[0] = av
    avb_o[0] = av.astype(BF16)
    gate_o[0, :, 0:256] = _silu(proj('a_g', 256)).astype(BF16)

    cqn = fullnorm(proj('b_cq', MLA_Q_LORA), gcq_ref[...]).astype(BF16)
    qb = _dot(cqn, wuq_ref[...])
    segq = segq_ref[...]
    ms_q = jnp.concatenate([_seg_mean_sq(qb[:, :256], segq), _seg_mean_sq(qb[:, 256:], segq)], axis=1)
    qbn = qb * lax.rsqrt(ms_q + EPS) * gqb_ref[...]
    qbn = _rotate(qbn, tab_ref[:, TAB_QB:TAB_QB + 512], tab_ref[:, TAB_QB + 512:TAB_QB + 1024],
                  tab_ref[:, TAB_QB + 1024:TAB_QB + 1536], MLA_ROPE // 2)
    qb_o[0] = qbn.astype(BF16)
    ckvf_o[0] = fullnorm(proj('b_ckv', MLA_KV_LORA), gckv_ref[...])
    gate_o[0, :, 256:512] = _silu(proj('b_g', 256)).astype(BF16)

    cq = headnorm(proj('c_q', 256), 2)
    cq = _rotate(cq, tab_ref[:, TAB_CQ:TAB_CQ + 256], tab_ref[:, TAB_CQ + 256:TAB_CQ + 512],
                 tab_ref[:, TAB_CQ + 512:TAB_CQ + 768], ROT_DIM // 2)
    cq_o[0] = cq.astype(BF16)
    ciq_o[0] = proj('c_iq', 256).astype(BF16)
    gate_o[0, :, 512:768] = _silu(proj('c_g', 256)).astype(BF16)

    dq_o[0] = headnorm(proj('d_q', 256), 3).astype(BF16)
    dk = headnorm(proj('d_k', 256), 4)
    dkf_o[0] = dk
    dkb_o[0] = dk.astype(BF16)
    dvf_o[0] = proj('d_v', 256)
    gate_o[0, :, 768:1024] = _silu(proj('d_g', 256)).astype(BF16)

    lane = lax.broadcasted_iota(I32, (1, 128), 1)
    t = proj('c_k', 128)
    m64 = lane < HEAD_DIM
    ms1 = jnp.sum(jnp.where(m64, t * t, 0.0), axis=-1, keepdims=True) * (1.0 / HEAD_DIM)
    t = jnp.where(m64, t * lax.rsqrt(ms1 + EPS), t) * srow_ref[0:1, :]
    t = _rotate(t, tab_ref[:, TAB_S1:TAB_S1 + 128], tab_ref[:, TAB_S1 + 128:TAB_S1 + 256],
                tab_ref[:, TAB_S1 + 256:TAB_S1 + 384], ROT_DIM // 2)
    ck_o[0] = t[:, :HEAD_DIM]
    cv_o[0] = t[:, HEAD_DIM:]
    t = proj('b_kr', 128)
    m32 = lane < MLA_ROPE
    ms2 = jnp.sum(jnp.where(m32, t * t, 0.0), axis=-1, keepdims=True) * (1.0 / MLA_ROPE)
    t = jnp.where(m32, t * lax.rsqrt(ms2 + EPS), t) * srow_ref[1:2, :]
    t = _rotate(t, tab_ref[:, TAB_S2:TAB_S2 + 128], tab_ref[:, TAB_S2 + 128:TAB_S2 + 256],
                tab_ref[:, TAB_S2 + 256:TAB_S2 + 384], MLA_ROPE // 2)
    s2_o[0] = t
    bkr_o[0] = t[:, :MLA_ROPE]
    cik_o[0] = t[:, MLA_ROPE:MLA_ROPE + IDX_DIM]

    xt = _dot_nt(wt_ref[...], xn)
    xt_o[0, 0:XT_DF, :] = xt[0:XT_DF]
    xt_o[0, XT_DF:XT_ROWS, :] = _log_sigmoid(xt[XT_DF:XT_ROWS] + bcol_ref[...])


def _inproj_call(x, tab, lw, tm):
    bk, tk_, _ = x.shape
    grid = (tk_ // tm, bk)

    def const(shape):
        return pl.BlockSpec(shape, lambda i, b: (0,) * len(shape))

    def rows(w):
        return pl.BlockSpec((1, tm, w), lambda i, b: (b, i, 0))

    in_specs = [
        rows(D_MODEL),
        pl.BlockSpec((tm, TAB_W), lambda i, b: (i, 0)),
        const((1, D_MODEL)), const((D_MODEL, W_IN_COLS)), const((XT_ROWS, D_MODEL)), const((8, 1)),
        const((8, 256)), const((256, 256)), const((1, MLA_Q_LORA)), const((MLA_Q_LORA, 512)), const((256, 256)),
        const((1, 512)), const((1, MLA_KV_LORA)), const((8, 128)),
    ]
    widths = [(256, BF16), (256, BF16), (256, BF16), (256, F32), (256, F32), (512, BF16), (128, F32), (256, BF16),
              (256, BF16), (64, F32), (64, F32), (128, F32), (32, F32), (32, F32), (256, BF16), (256, BF16), (256, F32),
              (256, F32)]
    out_shape = [jax.ShapeDtypeStruct((bk, tk_, w), dt) for w, dt in widths]
    out_specs = [rows(w) for w, _ in widths]
    out_shape.append(jax.ShapeDtypeStruct((bk, XT_ROWS, tk_), F32))
    out_specs.append(pl.BlockSpec((1, XT_ROWS, tm), lambda i, b: (b, 0, i)))
    out_shape.append(jax.ShapeDtypeStruct((bk, tk_, 1024), BF16))
    out_specs.append(rows(1024))
    return pl.pallas_call(
        _inproj_kernel, grid=grid, in_specs=in_specs, out_specs=out_specs, out_shape=out_shape,
        compiler_params=_cparams(2), name="inproj",
    )(x, tab, lw['g_in'], lw['w_in'], lw['w_t'], lw['b_col'], lw['g4'], lw['hseg'], lw['g_cq'], lw['w_uq'],
      lw['segq'], lw['g_qb'], lw['g_ckv'], lw['srow'])


def _bkv_kernel(ckv_ref, kr_ref, wk_ref, wvt_ref, e_ref, segq_ref, gk_ref, kb_o, vt_o):
    c = ckv_ref[0].astype(BF16)
    kn = _dot(c, wk_ref[...])
    segq = segq_ref[...]
    ms = jnp.concatenate([_seg_mean_sq(kn[:, :256], segq), _seg_mean_sq(kn[:, 256:], segq)], axis=1)
    kn = kn * lax.rsqrt(ms + EPS) * gk_ref[...]
    kr = _dot(kr_ref[0].astype(BF16), e_ref[...])
    kb_o[0] = (kn + kr).astype(BF16)
    vt_o[0] = _dot_nt(wvt_ref[...], c).astype(BF16)


def _bkv_call(ckv, kr, lw, tm):
    b, lp, _ = ckv.shape

    def const(shape):
        return pl.BlockSpec(shape, lambda bb, i: (0,) * len(shape))

    def rows(w):
        return pl.BlockSpec((1, tm, w), lambda bb, i: (bb, i, 0))

    return pl.pallas_call(
        _bkv_kernel, grid=(b, lp // tm),
        in_specs=[rows(128), rows(128), const((128, 512)), const((256, 128)), const((128, 512)), const((256, 256)),
                  const((1, 512))],
        out_specs=[rows(512), pl.BlockSpec((1, 256, tm), lambda bb, i: (bb, 0, i))],
        out_shape=[jax.ShapeDtypeStruct((b, lp, 512), BF16), jax.ShapeDtypeStruct((b, 256, lp), BF16)],
        compiler_params=_cparams(2), name="mla_kv",
    )(ckv, kr, lw['w_uk'], lw['w_uvt'], lw['e_kr'], lw['segq'], lw['g_kb'])


def _head_lane_id(width=256):
    return lax.broadcasted_iota(I32, (1, width), 1) // HEAD_DIM


def _keep_lanes(x, pred):
    return jnp.where(pred, x.astype(F32), 0.0).astype(BF16)


def _flash_run(nfull, nkb, nkb_max, *, qm, k_block, v_block, col_sub, row_add, mask_fn, s_refs, acc_ref, tq):
    s_a, s_b = s_refs

    def produce(j, s_out):
        jc = jnp.minimum(j, nkb_max - 1)
        for h in range(N_HEADS):
            s_out[h] = _dot_nt(k_block(jc, h), qm[h])

    def half(j, c, s_in, s_out, masked):
        ms, ls = c
        produce(j + 1, s_out)
        valid = mask_fn(j) if masked else None
        new_m, new_l = [], []
        for h in range(N_HEADS):
            t = s_in[h]
            if col_sub is not None:
                t = t - col_sub(j, h)
            if masked:
                t = jnp.where(valid, t, NEG)
            m_cur = jnp.max(t, axis=0, keepdims=True)
            if row_add is not None:
                m_cur = m_cur + row_add[h]
            m_new = jnp.maximum(ms[h], m_cur)
            alpha = jnp.exp2(ms[h] - m_new)
            off = m_new if row_add is None else m_new - row_add[h]
            p = jnp.exp2(t - off).astype(BF16)
            pv = _dot(v_block(j, h), p)
            r0 = h * HEAD_DIM
            acc_ref[r0:r0 + HEAD_DIM, :] = acc_ref[r0:r0 + HEAD_DIM, :] * alpha + pv[:HEAD_DIM]
            new_l.append(ls[h] * alpha + pv[HEAD_DIM:HEAD_DIM + 1])
            new_m.append(m_new)
        return tuple(new_m), tuple(new_l)

    def step(j, c, s_in, s_out, may_end):
        def run(c):
            if nfull is None:
                return half(j, c, s_in, s_out, True)
            return lax.cond(j >= nfull, lambda cc: half(j, cc, s_in, s_out, True),
                            lambda cc: half(j, cc, s_in, s_out, False), c)
        if not may_end:
            return run(c)
        return lax.cond(j >= nkb, lambda cc: cc, run, c)

    def body(i, c):
        c = step(2 * i, c, s_a, s_b, False)
        return step(2 * i + 1, c, s_b, s_a, True)

    acc_ref[...] = jnp.zeros(acc_ref.shape, F32)
    init = (tuple(jnp.full((1, tq), NEG, F32) for _ in range(N_HEADS)),
            tuple(jnp.zeros((1, tq), F32) for _ in range(N_HEADS)))
    produce(0, s_a)
    _, ls = lax.fori_loop(0, (nkb + 1) // 2, body, init)
    return ls


def _flash_scratch(tq, tk):
    return [pltpu.VMEM((N_HEADS, tk, tq), F32), pltpu.VMEM((N_HEADS, tk, tq), F32), pltpu.VMEM((256, tq), F32)]


def _flash_finish(o_ref, g_ref, acc_ref, ls):
    for h in range(N_HEADS):
        r0 = h * HEAD_DIM
        acc_ref[r0:r0 + HEAD_DIM, :] = acc_ref[r0:r0 + HEAD_DIM, :] / ls[h]
    y = acc_ref[...].T
    o_ref[0] = (y * g_ref[0].astype(F32)).astype(BF16)


A_WIN = A_BAND + 2 * CHUNK


def _attn_a_kernel(q_ref, k_ref, v_ref, bias_ref, g_ref, o_ref, *, lo_valid, hi_valid, group):
    hid = _head_lane_id()
    for gi in range(group):
        c = pl.program_id(1) * group + gi
        start = pl.multiple_of(c * CHUNK, CHUNK)
        q = q_ref[0, gi * CHUNK:(gi + 1) * CHUNK, :]
        qs = jnp.concatenate([_keep_lanes(q, hid == h) for h in range(N_HEADS)], axis=0)
        kb = k_ref[0, pl.ds(start, A_WIN), :]
        vb = v_ref[0, pl.ds(start, A_WIN), :]
        s = _dot_nt(qs, kb) + bias_ref[...]
        row = start + lax.broadcasted_iota(I32, (1, A_WIN), 1)
        valid = (row >= lo_valid) & (row < hi_valid)
        s = jnp.where(valid, s, NEG)
        m = jnp.max(s, axis=-1, keepdims=True)
        p = jnp.where(valid, jnp.exp(s - m), 0.0)
        l = jnp.sum(p, axis=-1, keepdims=True)
        o = _dot(p.astype(BF16), vb) / l
        y = o[(N_HEADS - 1) * CHUNK:]
        for h in range(N_HEADS - 2, -1, -1):
            y = jnp.where(hid == h, o[h * CHUNK:(h + 1) * CHUNK], y)
        gate = g_ref[0, gi * CHUNK:(gi + 1) * CHUNK, :].astype(F32)
        o_ref[0, gi * CHUNK:(gi + 1) * CHUNK, :] = (y * gate).astype(BF16)


def _attn_a_call(q, kfull, vfull, bias, gates, lo_valid, hi_valid, group):
    b, tp, _ = q.shape
    rows_kv = kfull.shape[1]
    kern = functools.partial(_attn_a_kernel, lo_valid=lo_valid, hi_valid=hi_valid, group=group)
    rows = CHUNK * group
    return pl.pallas_call(
        kern, grid=(b, tp // rows),
        in_specs=[pl.BlockSpec((1, rows, 256), lambda bb, c: (bb, c, 0)),
                  pl.BlockSpec((1, rows_kv, 256), lambda bb, c: (bb, 0, 0)),
                  pl.BlockSpec((1, rows_kv, 256), lambda bb, c: (bb, 0, 0)),
                  pl.BlockSpec((N_HEADS * CHUNK, A_WIN), lambda bb, c: (0, 0)),
                  pl.BlockSpec((1, rows, 256), lambda bb, c: (bb, c, 0))],
        out_specs=pl.BlockSpec((1, rows, 256), lambda bb, c: (bb, c, 0)),
        out_shape=jax.ShapeDtypeStruct((b, tp, 256), BF16),
        compiler_params=_cparams(2), name="attn_band",
    )(q, kfull, vfull, bias, gates)


def _attn_b_kernel(q_ref, k_ref, v_ref, g_ref, o_ref, sa_ref, sb_ref, acc_ref, *, p_len, l_len, tq, tk, nq):
    i = pl.program_id(1) if nq > 1 else 0
    q0 = p_len + i * tq
    lane = lax.broadcasted_iota(I32, (1, 256), 1)
    qm = []
    for h in range(N_HEADS):
        qg = q_ref[0, :, (h // 2) * 256:(h // 2) * 256 + 256]
        lo = (h % 2) * (MLA_NOPE + MLA_ROPE)
        qm.append(_keep_lanes(qg, (lane >= lo) & (lane < lo + MLA_NOPE + MLA_ROPE)))
    qchunk = (q0 + lax.broadcasted_iota(I32, (1, tq), 1)) // CHUNK
    nkb = jnp.minimum((q0 + tq + tk - 1) // tk, k_ref.shape[1] // tk)
    nfull = jnp.minimum(((q0 // CHUNK + 1) * CHUNK) // tk, nkb)

    def k_block(j, h):
        return k_ref[0, pl.ds(pl.multiple_of(j * tk, tk), tk), (h // 2) * 256:(h // 2) * 256 + 256]

    def mask_fn(j):
        kpos = j * tk + lax.broadcasted_iota(I32, (tk, 1), 0)
        return ((kpos // CHUNK) <= qchunk) & (kpos < l_len)

    ls = _flash_run(nfull, nkb, k_ref.shape[1] // tk, qm=qm, k_block=k_block, v_block=lambda j, h: v_ref[0, j, h],
                    col_sub=None, row_add=None, mask_fn=mask_fn, s_refs=(sa_ref, sb_ref), acc_ref=acc_ref, tq=tq)
    _flash_finish(o_ref, g_ref, acc_ref, ls)


def _attn_b_call(qb, kb, vaug, gates, p_len, l_len, tq, tk):
    b, t, _ = qb.shape
    lp = kb.shape[1]
    nq = t // tq
    kern = functools.partial(_attn_b_kernel, p_len=p_len, l_len=l_len, tq=tq, tk=tk, nq=nq)
    return pl.pallas_call(
        kern, grid=(b, nq),
        in_specs=[pl.BlockSpec((1, tq, 512), lambda bb, i: (bb, i, 0)),
                  pl.BlockSpec((1, lp, 512), lambda bb, i: (bb, 0, 0)),
                  pl.BlockSpec((1,) + vaug.shape[1:], lambda bb, i: (bb, 0, 0, 0, 0)),
                  pl.BlockSpec((1, tq, 256), lambda bb, i: (bb, i, 1))],
        out_specs=pl.BlockSpec((1, tq, 256), lambda bb, i: (bb, i, 0)),
        out_shape=jax.ShapeDtypeStruct((b, t, 256), BF16),
        scratch_shapes=_flash_scratch(tq, tk),
        compiler_params=_cparams(2), name="attn_latent",
    )(qb, kb, vaug, gates)


def _attn_c_kernel(q_ref, iq_ref, iw_ref, k_ref, v_ref, ik_ref, g_ref, o_ref, key_ref, top_ref, sa_ref, sb_ref,
                   acc_ref,
                   *, p_len, l_len, k_sel, tq, tk, nq):
    i = pl.program_id(1) if nq > 1 else 0
    q0 = p_len + i * tq
    nkb_max = k_ref.shape[1] // tk
    nkb = jnp.minimum((q0 + tq + tk - 1) // tk, nkb_max)
    qchunk = (q0 + lax.broadcasted_iota(I32, (1, tq), 1)) // CHUNK
    hid = _head_lane_id()

    iq = iq_ref[0]
    ihid = lax.broadcasted_iota(I32, (1, 256), 1) // IDX_DIM
    iqm = [_keep_lanes(iq, ihid == h) for h in range(IDX_HEADS)]
    iw = iw_ref[0] * (IDX_DIM ** -0.5 * IDX_HEADS ** -0.5)

    def score_body(j, carry):
        ks = pl.multiple_of(j * tk, tk)
        ikb = ik_ref[0, pl.ds(ks, tk), :]
        score = jnp.zeros((tk, tq), F32)
        for h in range(IDX_HEADS):
            score = score + iw[h:h + 1, :] * jnp.maximum(_dot_nt(ikb, iqm[h]), 0.0)
        kpos = ks + lax.broadcasted_iota(I32, (tk, 1), 0)
        adm = ((kpos // CHUNK) <= qchunk) & (kpos < l_len)
        bits = pltpu.bitcast(score, I32)
        key = jnp.where(bits < 0, bits ^ 0x7FFFFFFF, bits)
        key = jnp.where(key == -1, 0, key)
        key_ref[j] = jnp.where(adm, key, INT_MIN)
        top = pltpu.bitcast(bits & -65536, F32)
        top_ref[j] = jnp.where(adm, top, -jnp.inf).astype(BF16)
        return carry

    lax.fori_loop(0, nkb, score_body, 0)

    def count(pred_fn):
        def cbody(j, acc):
            pf = jnp.where(pred_fn(key_ref[j], j), 1.0, 0.0)
            for r in range(tk // 8):
                acc = acc + pf[r * 8:(r + 1) * 8]
            return acc
        acc = lax.fori_loop(0, nkb, cbody, jnp.zeros((8, tq), F32))
        return jnp.sum(acc, axis=0, keepdims=True)

    def count_top(cand_b):
        one, zero = jnp.ones((16, tq), BF16), jnp.zeros((16, tq), BF16)

        def cbody(j, acc):
            part = zero
            for r in range(tk // 16):
                part = part + jnp.where(top_ref[j, r * 16:(r + 1) * 16, :] >= cand_b, one, zero)
            return acc + part.astype(F32)
        acc = lax.fori_loop(0, nkb, cbody, jnp.zeros((16, tq), F32))
        return jnp.sum(acc, axis=0, keepdims=True)

    def top_body(it, c):
        t, n_ge = c
        cand = t + jnp.left_shift(jnp.int32(1), 15 - it)
        cbits = jnp.where(cand < 0, cand ^ 0x7FFF, cand) & 0xFFFF
        cbits = jnp.where((cbits > 0) & (cbits < 0x80), 0x80, cbits)
        cval =pltpu.bitcast(jnp.left_shift(cbits, 16), F32)
        cnt = count_top(jnp.broadcast_to(cval, (16, tq)).astype(BF16))
        ok = cnt >= k_sel
        return jnp.where(ok, cand, t), jnp.where(ok, cnt, n_ge)

    t16, n_ge = lax.fori_loop(0, 16, top_body, (jnp.full((1, tq), -32768, I32), jnp.zeros((1, tq), F32)))

    def bis_body(it, c):
        t, n_ge = c
        cand = t + jnp.left_shift(jnp.int32(1), 15 - it)
        cnt = count(lambda kb, j: kb >= cand)
        ok = cnt >= k_sel
        return jnp.where(ok, cand, t), jnp.where(ok, cnt, n_ge)

    thr, n_ge = lax.fori_loop(0, 16, bis_body, (t16 * 65536, n_ge))
    thr = jnp.maximum(thr, INT_MIN + 1)

    def idx_of(j):
        return j * tk + lax.broadcasted_iota(I32, (tk, tq), 0)

    def tie_cut(_):
        need = k_sel - count(lambda kb, j: kb > thr)

        def tie_body(it, jc):
            cand = jc + jnp.left_shift(jnp.int32(1), 12 - it)
            cnt = count(lambda kb, j: (kb == thr) & (idx_of(j) < cand))
            return jnp.where(cnt < need, cand, jc)

        return lax.fori_loop(0, 13, tie_body, jnp.zeros((1, tq), I32))

    has_tie = jnp.max(jnp.where(n_ge > k_sel, 1.0, 0.0)) > 0.0
    jcut = lax.cond(has_tie, tie_cut, lambda _: jnp.full((1, tq), 2 ** 30, I32), 0)

    q = q_ref[0]
    qm = [_keep_lanes(q, hid == h) for h in range(N_HEADS)]

    def mask_fn(j):
        kb = key_ref[j]
        return (kb > thr) | ((kb == thr) & (idx_of(j) <= jcut))

    ls = _flash_run(None, nkb, nkb_max, qm=qm,
                    k_block=lambda j, h: k_ref[0, pl.ds(pl.multiple_of(j * tk, tk), tk), :],
                    v_block=lambda j, h: v_ref[0, j, 0], col_sub=None, row_add=None, mask_fn=mask_fn,
                    s_refs=(sa_ref, sb_ref), acc_ref=acc_ref, tq=tq)
    _flash_finish(o_ref, g_ref, acc_ref, ls)


def _attn_c_call(cq, ciq, iwt, krep, vaug, ikrep, gates, p_len, l_len, k_sel, tq, tk):
    b, t, _ = cq.shape
    lp = krep.shape[1]
    assert lp <= 8192
    nq = t // tq
    kern = functools.partial(_attn_c_kernel, p_len=p_len, l_len=l_len, k_sel=float(k_sel), tq=tq, tk=tk, nq=nq)
    return pl.pallas_call(
        kern, grid=(b, nq),
        in_specs=[pl.BlockSpec((1, tq, 256), lambda bb, i: (bb, i, 0)),
                  pl.BlockSpec((1, tq, 256), lambda bb, i: (bb, i, 0)),
                  pl.BlockSpec((1, 8, tq), lambda bb, i: (bb, 0, i)),
                  pl.BlockSpec((1, lp, 256), lambda bb, i: (bb, 0, 0)),
                  pl.BlockSpec((1,) + vaug.shape[1:], lambda bb, i: (bb, 0, 0, 0, 0)),
                  pl.BlockSpec((1, lp, 256), lambda bb, i: (bb, 0, 0)),
                  pl.BlockSpec((1, tq, 256), lambda bb, i: (bb, i, 2))],
        out_specs=pl.BlockSpec((1, tq, 256), lambda bb, i: (bb, i, 0)),
        out_shape=jax.ShapeDtypeStruct((b, t, 256), BF16),
        scratch_shapes=[pltpu.VMEM((lp // tk, tk, tq), I32), pltpu.VMEM((lp // tk, tk, tq), BF16)]
        + _flash_scratch(tq, tk),
        compiler_params=_cparams(2), name="attn_sparse",
    )(cq, ciq, iwt, krep, vaug, ikrep, gates)


def _attn_d_kernel(q_ref, k_ref, v_ref, lf_ref, g_ref, o_ref, fc_ref, fk_ref, sa_ref, sb_ref, acc_ref,
                   *, p_len, tq, tk, nq):
    i = pl.program_id(1) if nq > 1 else 0
    lp = k_ref.shape[1]
    nch = lp // 128

    def cumulate():
        x = lf_ref[0]
        lane = lax.broadcasted_iota(I32, (1, lp), 1)
        step = 1
        while step < lp:
            x = x + jnp.where(lane >= step, pltpu.roll(x, step, 1), 0.0)
            step *= 2
        x = x * LOG2E
        for c in range(nch):
            fc_ref[c] = x[:, c * 128:(c + 1) * 128]

        eye = lax.broadcasted_iota(I32, (128, 128), 0) == lax.broadcasted_iota(I32, (128, 128), 1)

        def spread(c, carry):
            rows = fc_ref[c]
            for h in range(N_HEADS):
                col = jnp.sum(jnp.where(eye, rows[h:h + 1, :], 0.0), axis=1, keepdims=True)
                fk_ref[h, pl.ds(pl.multiple_of(c * 128, 128), 128), :] = jnp.broadcast_to(col, (128, 128))
            return carry

        lax.fori_loop(0, nch, spread, 0)

    if nq > 1:
        pl.when(i == 0)(cumulate)
    else:
        cumulate()

    q0 = p_len + i * tq
    hid = _head_lane_id()
    q = q_ref[0]
    qm = [_keep_lanes(q, hid == h) for h in range(N_HEADS)]
    c0 = q0 // 128
    fq_rows = [fc_ref[c0 + c] for c in range(tq // 128)]
    fq = [jnp.concatenate([r[h:h + 1, :] for r in fq_rows], axis=1) for h in range(N_HEADS)]
    qpos = q0 + lax.broadcasted_iota(I32, (1, tq), 1)
    nkb = jnp.minimum((q0 + tq + tk - 1) // tk, lp // tk)
    nfull = jnp.minimum((q0 + 1) // tk, nkb)

    def col_sub(j, h):
        fk = fk_ref[h, pl.ds(pl.multiple_of(j * tk, tk), tk), :]
        return jnp.concatenate([fk] * (tq // 128), axis=1)

    def mask_fn(j):
        return (j * tk + lax.broadcasted_iota(I32, (tk, 1), 0)) <= qpos

    ls = _flash_run(nfull, nkb, lp // tk, qm=qm,
                    k_block=lambda j, h: k_ref[0, pl.ds(pl.multiple_of(j * tk, tk), tk), :],
                    v_block=lambda j, h: v_ref[0, j, h], col_sub=col_sub, row_add=fq, mask_fn=mask_fn,
                    s_refs=(sa_ref, sb_ref), acc_ref=acc_ref, tq=tq)
    _flash_finish(o_ref, g_ref, acc_ref, ls)


def _attn_d_call(dq, kd, vaug, lft, gates, p_len, tq, tk):
    b, t, _ = dq.shape
    lp = kd.shape[1]
    nq = t // tq
    assert p_len % 128 == 0 and tq % 128 == 0 and p_len + t <= lp
    kern = functools.partial(_attn_d_kernel, p_len=p_len, tq=tq, tk=tk, nq=nq)
    return pl.pallas_call(
        kern, grid=(b, nq),
        in_specs=[pl.BlockSpec((1, tq, 256), lambda bb, i: (bb, i, 0)),
                  pl.BlockSpec((1, lp, 256), lambda bb, i: (bb, 0, 0)),
                  pl.BlockSpec((1,) + vaug.shape[1:], lambda bb, i: (bb, 0, 0, 0, 0)),
                  pl.BlockSpec((1, 8, lp), lambda bb, i: (bb, 0, 0)),
                  pl.BlockSpec((1, tq, 256), lambda bb, i: (bb, i, 3))],
        out_specs=pl.BlockSpec((1, tq, 256), lambda bb, i: (bb, i, 0)),
        out_shape=jax.ShapeDtypeStruct((b, t, 256), BF16),
        scratch_shapes=[pltpu.VMEM((lp // 128, 8, 128), F32), pltpu.VMEM((N_HEADS, lp, 128), F32)]
        + _flash_scratch(tq, tk),
        compiler_params=_cparams(2), name="attn_forget",
    )(dq, kd, vaug, lft, gates)


def _out_kernel(ya_ref, yb_ref, yc_ref, yd_ref, x_ref, p_ref, wo_ref, gple_ref, wg_ref, wp_ref, o_ref):
    mixed = (_dot(ya_ref[0], wo_ref[0:256, :]) + _dot(yb_ref[0], wo_ref[256:512, :])
             + _dot(yc_ref[0], wo_ref[512:768, :]) + _dot(yd_ref[0], wo_ref[768:1024, :]))
    x1 = x_ref[0] + mixed
    ms = jnp.mean(x1 * x1, axis=-1, keepdims=True)
    xn = (x1 * lax.rsqrt(ms + EPS) * gple_ref[...]).astype(BF16)
    gate = 1.0 / (1.0 + jnp.exp(-_dot(xn, wg_ref[...])))
    o_ref[0] = x1 + gate * _dot(p_ref[0].astype(BF16), wp_ref[...])


def _out_call(ys, x, p, lw, tm):
    b, t, _ = x.shape

    def const(shape):
        return pl.BlockSpec(shape, lambda bb, i: (0,) * len(shape))

    def rows(w):
        return pl.BlockSpec((1, tm, w), lambda bb, i: (bb, i, 0))

    return pl.pallas_call(
        _out_kernel, grid=(b, t // tm),
        in_specs=[rows(256)] * 4 + [rows(D_MODEL), rows(PLE_DIM), const((1024, D_MODEL)), const((1, D_MODEL)),
                                    const((D_MODEL, D_MODEL)), const((PLE_DIM, D_MODEL))],
        out_specs=rows(D_MODEL),
        out_shape=jax.ShapeDtypeStruct((b, t, D_MODEL), F32),
        compiler_params=_cparams(2), name="out_proj",
    )(*ys, x, p, lw['w_out'], lw['g_ple'], lw['w_ple_gate'], lw['w_ple_proj'])


def _seg_matrix(segments):
    m = np.zeros((256, 256), np.float32)
    for lo, n in segments:
        m[lo:lo + n, lo:lo + n] = 1.0 / n
    return jnp.asarray(m, BF16)


_PAIR_SEGS = ((0, 64), (64, 32), (96, 64), (160, 32), (192, 64))


def _pair_cols(head):
    return (head // 2) * 256 + (head % 2) * (MLA_NOPE + MLA_ROPE)


def _prep_layer(i, g_in, w_in, g_qk, g_rope, a_rel_bias, b_g_cq, b_w_uq, b_g_ckv, b_w_ukv, d_b_f, w_out, g_ple,
                w_ple_gate, w_ple_proj):
    w = w_in[i]
    cols = []
    for n in _MY_ORDER:
        if n == 'pad56':
            cols.append(jnp.zeros((D_MODEL, 56), F32))
        else:
            o, wd = _REF_OFF[n]
            cols.append(w[:, o:o + wd])
    lw = {'w_in': jnp.concatenate(cols, axis=1).astype(BF16), 'g_in': g_in[i][None, :]}
    wt = [w[:, _REF_OFF[n][0]:_REF_OFF[n][0] + _REF_OFF[n][1]] for n in ('d_v', 'c_v', 'c_iw', 'd_f')]
    wt.append(jnp.zeros((D_MODEL, XT_ROWS - XT_DF - N_HEADS), F32))
    lw['w_t'] = jnp.concatenate(wt, axis=1).T.astype(BF16)
    lw['b_col'] = jnp.concatenate([d_b_f[i], jnp.zeros((8 - N_HEADS,), F32)])[:, None]
    g = g_qk[i]
    sc = HEAD_DIM ** -0.5
    rows = [jnp.tile(g[0], 4) * sc, jnp.tile(g[1], 4), jnp.tile(g[2], 4) * (sc * LOG2E),
            jnp.tile(g[4], 4) * (sc * LOG2E), jnp.tile(g[5], 4)]
    lw['g4'] = jnp.stack(rows + [jnp.zeros((256,), F32)] * 3)
    lw['hseg'] = _seg_matrix(tuple((h * 64, 64) for h in range(4)))
    lw['segq'] = _seg_matrix(_PAIR_SEGS)
    lw['g_cq'] = b_g_cq[i][None, :]
    lw['g_ckv'] = b_g_ckv[i][None, :]
    qscale = (MLA_NOPE + MLA_ROPE) ** -0.5 * LOG2E
    hw = MLA_NOPE + MLA_ROPE
    wuq, gqb, wuk, gkb, wuv = [], [], [], [], []
    e_kr = np.zeros((128, 512), np.float32)
    for h in range(N_HEADS):
        wuq.append(b_w_uq[i][:, h * hw:(h + 1) * hw])
        gqb += [g[6] * qscale, g_rope[i][0] * qscale]
        srck = h * (MLA_NOPE + HEAD_DIM)
        wuk += [b_w_ukv[i][:, srck:srck + MLA_NOPE], jnp.zeros((MLA_KV_LORA, MLA_ROPE), F32)]
        gkb += [g[7], jnp.zeros((MLA_ROPE,), F32)]
        wuv.append(b_w_ukv[i][:, srck + MLA_NOPE:srck + MLA_NOPE + HEAD_DIM])
        e_kr[np.arange(MLA_ROPE), _pair_cols(h) + MLA_NOPE + np.arange(MLA_ROPE)] = 1.0
        if h % 2 == 1:
            wuq.append(jnp.zeros((MLA_Q_LORA, 256 - 2 * hw), F32))
            wuk.append(jnp.zeros((MLA_KV_LORA, 256 - 2 * hw), F32))
            gqb.append(jnp.zeros((256 - 2 * hw,), F32))
            gkb.append(jnp.zeros((256 - 2 * hw,), F32))
    lw['w_uq'] = jnp.concatenate(wuq, axis=1).astype(BF16)
    lw['g_qb'] = jnp.concatenate(gqb)[None, :]
    lw['w_uk'] = jnp.concatenate(wuk, axis=1).astype(BF16)
    lw['g_kb'] = jnp.concatenate(gkb)[None, :]
    lw['e_kr'] = jnp.asarray(e_kr, BF16)
    lw['w_uvt'] = jnp.concatenate(wuv, axis=1).T.astype(BF16)
    one = jnp.ones((128,), F32)
    lw['srow'] = jnp.stack([jnp.concatenate([g[3], one[HEAD_DIM:]]), jnp.concatenate([g_rope[i][1], one[MLA_ROPE:]])]
                           + [one] * 6)
    ab = a_rel_bias[i]
    ext = jnp.concatenate([jnp.broadcast_to(ab[:, 2 * REL_CLIP:], (N_HEADS, A_WIN - REL_CLIP - 1)),
                           ab[:, REL_CLIP - CHUNK + 1:][:, ::-1]], axis=1)
    lw['a_bias'] = jnp.stack([ext[:, CHUNK - 1 - q:CHUNK - 1 - q + A_WIN] for q in range(CHUNK)],
                             axis=1).reshape(N_HEADS * CHUNK, A_WIN)
    lw['w_out'] = w_out[i].astype(BF16)
    lw['g_ple'] = g_ple[i][None, :]
    lw['w_ple_gate'] = w_ple_gate[i].astype(BF16)
    lw['w_ple_proj'] = w_ple_proj[i].astype(BF16)
    return lw


def _cos_sin(pos, rot_dim):
    half = rot_dim // 2
    inv = jnp.float32(ROPE_THETA) ** (-jnp.arange(half, dtype=F32) / half)
    ang = pos.astype(F32)[:, None] * inv[None, :]
    return [jnp.cos(ang), jnp.sin(ang)]


def _rot_placement():
    src = {MLA_ROPE: 1, ROT_DIM: 1 + MLA_ROPE}
    e = np.zeros((1 + MLA_ROPE + ROT_DIM, TAB_W), np.float32)
    sections = ((TAB_QB, 512, [_pair_cols(h) + MLA_NOPE for h in range(N_HEADS)], MLA_ROPE),
                (TAB_CQ, 256, [h * HEAD_DIM for h in range(N_HEADS)], ROT_DIM),
                (TAB_S1, 128, [0], ROT_DIM), (TAB_S2, 128, [0], MLA_ROPE))
    for base, width, starts, rot_dim in sections:
        half = rot_dim // 2
        e[0, base:base + width] = 1.0
        for st in starts:
            for i in range(half):
                c, s = src[rot_dim] + i, src[rot_dim] + half + i
                e[0, base + st + i] = e[0, base + st + half + i] = 0.0
                e[c, base + st + i] = e[c, base + st + half + i] = 1.0
                e[s, base + width + st + half + i] = 1.0
                e[s, base + 2 * width + st + i] = -1.0
    return jnp.asarray(e)


def _rot_tables(pos):
    src = jnp.concatenate([jnp.ones((pos.shape[0], 1), F32)] + _cos_sin(pos, MLA_ROPE) + _cos_sin(pos, ROT_DIM),
                          axis=1)
    return jnp.dot(src, _rot_placement(), precision=lax.Precision.HIGHEST)


def _pad_axis(a, size, axis):
    pads = [(0, 0)] * a.ndim
    pads[axis] = (0, size - a.shape[axis])
    return jnp.pad(a, pads)


def _cat_rows(past, new, rows):
    a = new if past is None else jnp.concatenate([past.astype(new.dtype), new], axis=1)
    return _pad_axis(a, rows, 1)


def _cat_lanes(past_t, new_t, lanes):
    a = new_t if past_t is None else jnp.concatenate([past_t.astype(new_t.dtype), new_t], axis=2)
    return _pad_axis(a, lanes, 2)


def _value_blocks(vt, tk):
    b, r, lp = vt.shape
    h = r // HEAD_DIM
    v = vt.astype(BF16).reshape(b, h, HEAD_DIM, lp)
    v = jnp.concatenate([v, jnp.ones((b, h, V_ROWS - HEAD_DIM, lp), BF16)], axis=2)
    return v.reshape(b, h, V_ROWS, lp // tk, tk).transpose(0, 3, 1, 2, 4)


def _layer(x, p, past, lw, tab, *, fold_batch, tm, tq, tq_c, tk, a_group):
    b, t, _ = x.shape
    p_len = 0 if past is None else past[2].shape[1]
    l_len = p_len + t
    tqp = -(-t // tq) * tq
    tqp_c = -(-t // tq_c) * tq_c
    lp = -(-(p_len + max(tqp, tqp_c)) // tk) * tk

    xin = x.reshape(1, b * t, D_MODEL) if fold_batch else x
    outs = _inproj_call(xin, tab, lw, tm)
    if fold_batch:
        xt = outs[18][0].reshape(XT_ROWS, b, t).transpose(1, 0, 2)
        outs = [o.reshape((b, t) + o.shape[2:]) for o in outs[:18]] + [xt, outs[19].reshape(b, t, 1024)]
    (aq, akb, avb, akf, avf, qb, ckvf, cq, ciq, c_k, c_v, s2, b_kr, c_ik, dq, dkb, dkf, dvf, xt, gates) = outs

    lft = xt[:, XT_DF:XT_ROWS]
    d_lf = lft[:, :N_HEADS, :].transpose(0, 2, 1)
    state = (akf.reshape(b, t, N_HEADS, HEAD_DIM), avf.reshape(b, t, N_HEADS, HEAD_DIM), ckvf, b_kr, c_k, c_v, c_ik,
             dkf.reshape(b, t, N_HEADS, HEAD_DIM), dvf.reshape(b, t, N_HEADS, HEAD_DIM), d_lf)

    if past is None:
        pa_k = pa_v = pb_ckv = pb_kr = pc_k = pc_v = pc_ik = pd_k = pd_v = pd_lf = None
        pa = 0
    else:
        pa_k, pa_v, pb_ckv, pb_kr, pc_k, pc_v, pc_ik, pd_k, pd_v, pd_lf = past
        pa = pa_k.shape[1]
        pa_k = pa_k.reshape(b, pa, 256)
        pa_v = pa_v.reshape(b, pa, 256)
        pd_k = pd_k.reshape(b, p_len, 256)
        pd_v = pd_v.reshape(b, p_len, 256)

    tp = -(-t // (CHUNK * a_group)) * (CHUNK * a_group)
    front = CHUNK + A_BAND - pa
    a_rows = CHUNK + A_BAND + tp

    def band_src(pst, new):
        parts = [jnp.zeros((b, front, 256), BF16)]
        if pst is not None:
            parts.append(pst.astype(BF16))
        parts.append(new)
        return _pad_axis(jnp.concatenate(parts, axis=1), a_rows, 1)

    ya = _attn_a_call(_pad_axis(aq, tp, 1), band_src(pa_k, akb), band_src(pa_v, avb), lw['a_bias'],
                      _pad_axis(gates, tp, 1), front, CHUNK + A_BAND + t, a_group)[:, :t]

    gates_q = _pad_axis(gates, tqp, 1)

    ckv_all = _cat_rows(pb_ckv, ckvf, lp)
    kr_new = s2 if past is None else _pad_axis(b_kr, 128, 2)
    kr_past = None if past is None else _pad_axis(pb_kr, 128, 2)
    kr_all = _cat_rows(kr_past, kr_new, lp)
    kb, vbt = _bkv_call(ckv_all, kr_all, lw, tk)
    yb = _attn_b_call(_pad_axis(qb, tqp, 1), kb, _value_blocks(vbt, tk), gates_q, p_len, l_len, tq, tk)[:, :t]

    krep = jnp.tile(_cat_rows(pc_k, c_k, lp).astype(BF16), (1, 1, N_HEADS))
    ikrep = jnp.tile(_cat_rows(pc_ik, c_ik, lp).astype(BF16), (1, 1, IDX_HEADS))
    cvt = _cat_lanes(None if past is None else pc_v.transpose(0, 2, 1), xt[:, XT_CV:XT_IW], lp)
    iwt = _pad_axis(xt[:, XT_IW:XT_DF], tqp_c, 2)
    k_sel = min(DSA_TOPK, l_len // 4)
    yc = _attn_c_call(_pad_axis(cq, tqp_c, 1), _pad_axis(ciq, tqp_c, 1), iwt, krep, _value_blocks(cvt, tk), ikrep,
                      _pad_axis(gates, tqp_c, 1), p_len, l_len, k_sel, tq_c, tk)[:, :t]

    kd = _cat_rows(None if past is None else pd_k.astype(BF16), dkb, lp)
    dvt = _cat_lanes(None if past is None else pd_v.transpose(0, 2, 1), xt[:, XT_DV:XT_CV], lp)
    lf_past = None if past is None else _pad_axis(pd_lf.transpose(0, 2, 1), 8, 1)
    lf_all = _cat_lanes(lf_past, lft, lp)
    yd = _attn_d_call(_pad_axis(dq, tqp, 1), kd, _value_blocks(dvt, tk), lf_all, gates_q, p_len, tq, tk)[:, :t]

    if fold_batch:
        x_new = _out_call([y.reshape(1, b * t, 256) for y in (ya, yb, yc, yd)], x.reshape(1, b * t, D_MODEL),
                          p.reshape(1, b * t, PLE_DIM), lw, tm).reshape(b, t, D_MODEL)
    else:
        x_new = _out_call([ya, yb, yc, yd], x, p, lw, tm)
    return x_new, state


def kernel(x_prompt, x_sample, cache_a_k, cache_a_v, cache_b_ckv, cache_b_krope, cache_c_k, cache_c_v, cache_c_idx_k,
           cache_d_k, cache_d_v, cache_d_logf, p_prompt, p_sample, g_in, w_in, g_qk, g_rope, a_rel_bias, b_g_cq,
           b_w_uq, b_g_ckv, b_w_ukv, d_b_f, w_out, g_ple, w_ple_gate, w_ple_proj):
    depth = w_in.shape[0]
    b, t = x_prompt.shape[:2]
    bs, ts = x_sample.shape[:2]
    past_len = cache_b_ckv.shape[2]
    tab_p = _rot_tables(jnp.arange(t))
    tab_s = jnp.tile(_rot_tables(past_len + jnp.arange(ts)), (bs, 1))
    xp, xs = x_prompt, x_sample
    states_p, states_s = [], []
    for i in range(depth):
        lw = _prep_layer(i, g_in, w_in, g_qk, g_rope, a_rel_bias, b_g_cq, b_w_uq, b_g_ckv, b_w_ukv, d_b_f, w_out,
                         g_ple, w_ple_gate, w_ple_proj)
        xp, st_p = _layer(xp, p_prompt[i], None, lw, tab_p, fold_batch=False, tm=256, tq=256, tq_c=512, tk=256,
                           a_group=4)
        past = (cache_a_k[i], cache_a_v[i], cache_b_ckv[i], cache_b_krope[i], cache_c_k[i], cache_c_v[i],
                cache_c_idx_k[i], cache_d_k[i], cache_d_v[i], cache_d_logf[i])
        xs, st_s = _layer(xs, p_sample[i], past, lw, tab_s, fold_batch=True, tm=bs * ts, tq=128, tq_c=128, tk=256,
                           a_group=1)
        states_p.append(st_p)
        states_s.append(st_s)
    sp = [jnp.stack(z) for z in zip(*states_p)]
    ss = [jnp.stack(z) for z in zip(*states_s)]
    keep = min(A_BAND, t)
    sp[0] = sp[0][:, :, t - keep:]
    sp[1] = sp[1][:, :, t - keep:]
    return (xp, xs, *sp, *ss)
```

```python
import functools

import numpy as np
import jax
import jax.numpy as jnp
from jax import lax
from jax.experimental import pallas as pl
from jax.experimental.pallas import tpu as pltpu

F32 = jnp.float32
BF16 = jnp.bfloat16
I32 = jnp.int32

D_MODEL = 1024
CHUNK = 64
EPS = 1e-6
HEAD_DIM = 64
N_HEADS = 4
GROUP_WIDTH = 256
ROT_DIM = 16
ROPE_THETA = 500000.0
A_BAND = 8 * CHUNK
REL_CLIP = 128
MLA_Q_LORA = 384
MLA_KV_LORA = 128
MLA_NOPE = 64
MLA_ROPE = 32
IDX_HEADS = 8
IDX_DIM = 32
DSA_TOPK = 256
PLE_DIM = 256

_REF_SPLITS = (
    ('a_q', 256), ('a_k', 256), ('a_v', 256), ('a_g', 256),
    ('b_cq', 384), ('b_ckv', 128), ('b_kr', 32), ('b_g', 256),
    ('c_q', 256), ('c_k', 64), ('c_v', 64), ('c_iq', 256), ('c_ik', 32), ('c_iw', 8), ('c_g', 256),
    ('d_q', 256), ('d_k', 256), ('d_v', 256), ('d_f', 4), ('d_g', 256),
)
_REF_OFF = {}
_o = 0
for _n, _w in _REF_SPLITS:
    _REF_OFF[_n] = (_o, _w)
    _o += _w

_MY_ORDER = ('a_q', 'a_k', 'a_v', 'a_g', 'b_cq', 'b_ckv', 'b_g', 'c_q', 'c_iq', 'c_g', 'd_q', 'd_k', 'd_v', 'd_g',
             'c_k', 'c_v', 'b_kr', 'c_ik', 'c_iw', 'pad56')
_MY_OFF = {}
_o = 0
for _n in _MY_ORDER:
    _w = 56 if _n == 'pad56' else _REF_OFF[_n][1]
    _MY_OFF[_n] = _o
    _o += _w
W_IN_COLS = _o

XT_DV, XT_CV, XT_IW, XT_DF, XT_ROWS = 0, 256, 320, 328, 336

TAB_QB, TAB_CQ, TAB_S1, TAB_S2 = 0, 1536, 2304, 2688
TAB_W = 3072

V_ROWS = HEAD_DIM + 16
NEG = -1e30
LOG2E = 1.4426950408889634
MOST_NEGATIVE_CODE = -2139095040
VMEM_LIMIT = 56 * 1024 * 1024


def _cparams(n_axes):
    return pltpu.CompilerParams(dimension_semantics=("arbitrary",) * n_axes, vmem_limit_bytes=VMEM_LIMIT)


def _dot(a, b):
    return jnp.dot(a, b, preferred_element_type=F32)


def _dot_nt(a, b):
    return lax.dot_general(a, b, (((1,), (1,)), ((), ())), preferred_element_type=F32)


def _seg_mean_sq(t, seg):
    sq = t * t
    hi = sq.astype(BF16)
    lo = (sq - hi.astype(F32)).astype(BF16)
    return _dot(hi, seg) + _dot(lo, seg)


def _rotate(t, cos, sin_up, sin_dn, half):
    w = t.shape[-1]
    return t * cos + pltpu.roll(t, half, 1) * sin_up + pltpu.roll(t, w - half, 1) * sin_dn


def _silu(g):
    return g * (1.0 / (1.0 + jnp.exp(-g)))


def _log_sigmoid(v):
    return jnp.minimum(v, 0.0) - jnp.log1p(jnp.exp(-jnp.abs(v)))


def _inproj_kernel(x_ref, tab_ref, gin_ref, w_ref, wt_ref, bcol_ref, g4_ref, hseg_ref, gcq_ref, wuq_ref, segq_ref,
                   gqb_ref, gckv_ref, srow_ref,
                   aq_o, akb_o, avb_o, akf_o, avf_o, qb_o, ckvf_o, cq_o, ciq_o, ck_o, cv_o, s2_o, bkr_o, cik_o,
                   dq_o, dkb_o, dkf_o, dvf_o, xt_o, gate_o):
    x = x_ref[0]
    ms = jnp.mean(x * x, axis=-1, keepdims=True)
    xn = (x * lax.rsqrt(ms + EPS) * gin_ref[...]).astype(BF16)

    def proj(name, n):
        c0 = _MY_OFF[name]
        return _dot(xn, w_ref[:, c0:c0 + n])

    hseg = hseg_ref[...]

    def headnorm(t, row):
        return t * lax.rsqrt(_seg_mean_sq(t, hseg) + EPS) * g4_ref[row:row + 1, :]

    def fullnorm(t, g):
        return t * lax.rsqrt(jnp.mean(t * t, axis=-1, keepdims=True) + EPS) * g

    aq_o[0] = headnorm(proj('a_q', 256), 0).astype(BF16)
    ak = headnorm(proj('a_k', 256), 1)
    akf_o[0] = ak
    akb_o[0] = ak.astype(BF16)
    av = proj('a_v', 256)
    avf_o[0] = av
    avb_o[0] = av.astype(BF16)
    gate_o[0, :, 0:256] = _silu(proj('a_g', 256)).astype(BF16)

    cqn = fullnorm(proj('b_cq', MLA_Q_LORA), gcq_ref[...]).astype(BF16)
    qb = _dot(cqn, wuq_ref[...])
    segq = segq_ref[...]
    ms_q = jnp.concatenate([_seg_mean_sq(qb[:, :256], segq), _seg_mean_sq(qb[:, 256:], segq)], axis=1)
    qbn = qb * lax.rsqrt(ms_q + EPS) * gqb_ref[...]
    qbn = _rotate(qbn, tab_ref[:, TAB_QB:TAB_QB + 512], tab_ref[:, TAB_QB + 512:TAB_QB + 1024],
                  tab_ref[:, TAB_QB + 1024:TAB_QB + 1536], MLA_ROPE // 2)
    qb_o[0] = qbn.astype(BF16)
    ckvf_o[0] = fullnorm(proj('b_ckv', MLA_KV_LORA), gckv_ref[...])
    gate_o[0, :, 256:512] = _silu(proj('b_g', 256)).astype(BF16)

    cq = headnorm(proj('c_q', 256), 2)
    cq = _rotate(cq, tab_ref[:, TAB_CQ:TAB_CQ + 256], tab_ref[:, TAB_CQ + 256:TAB_CQ + 512],
                 tab_ref[:, TAB_CQ + 512:TAB_CQ + 768], ROT_DIM // 2)
    cq_o[0] = cq.astype(BF16)
    ciq_o[0] = proj('c_iq', 256).astype(BF16)
    gate_o[0, :, 512:768] = _silu(proj('c_g', 256)).astype(BF16)

    dq_o[0] = headnorm(proj('d_q', 256), 3).astype(BF16)
    dk = headnorm(proj('d_k', 256), 4)
    dkf_o[0] = dk
    dkb_o[0] = dk.astype(BF16)
    dvf_o[0] = proj('d_v', 256)
    gate_o[0, :, 768:1024] = _silu(proj('d_g', 256)).astype(BF16)

    lane = lax.broadcasted_iota(I32, (1, 128), 1)
    t = proj('c_k', 128)
    m64 = lane < HEAD_DIM
    ms1 = jnp.sum(jnp.where(m64, t * t, 0.0), axis=-1, keepdims=True) * (1.0 / HEAD_DIM)
    t = jnp.where(m64, t * lax.rsqrt(ms1 + EPS), t) * srow_ref[0:1, :]
    t = _rotate(t, tab_ref[:, TAB_S1:TAB_S1 + 128], tab_ref[:, TAB_S1 + 128:TAB_S1 + 256],
                tab_ref[:, TAB_S1 + 256:TAB_S1 + 384], ROT_DIM // 2)
    ck_o[0] = t[:, :HEAD_DIM]
    cv_o[0] = t[:, HEAD_DIM:]
    t = proj('b_kr', 128)
    m32 = lane < MLA_ROPE
    ms2 = jnp.sum(jnp.where(m32, t * t, 0.0), axis=-1, keepdims=True) * (1.0 / MLA_ROPE)
    t = jnp.where(m32, t * lax.rsqrt(ms2 + EPS), t) * srow_ref[1:2, :]
    t = _rotate(t, tab_ref[:, TAB_S2:TAB_S2 + 128], tab_ref[:, TAB_S2 + 128:TAB_S2 + 256],
                tab_ref[:, TAB_S2 + 256:TAB_S2 + 384], MLA_ROPE // 2)
    s2_o[0] = t
    bkr_o[0] = t[:, :MLA_ROPE]
    cik_o[0] = t[:, MLA_ROPE:MLA_ROPE + IDX_DIM]

    xt = _dot_nt(wt_ref[...], xn)
    xt_o[0, 0:XT_DF, :] = xt[0:XT_DF]
    xt_o[0, XT_DF:XT_ROWS, :] = _log_sigmoid(xt[XT_DF:XT_ROWS] + bcol_ref[...])


def _inproj_call(x, tab, lw, tm):
    bk, tk_, _ = x.shape
    grid = (tk_ // tm, bk)

    def const(shape):
        return pl.BlockSpec(shape, lambda i, b: (0,) * len(shape))

    def rows(w):
        return pl.BlockSpec((1, tm, w), lambda i, b: (b, i, 0))

    in_specs = [
        rows(D_MODEL),
        pl.BlockSpec((tm, TAB_W), lambda i, b: (i, 0)),
        const((1, D_MODEL)), const((D_MODEL, W_IN_COLS)), const((XT_ROWS, D_MODEL)), const((8, 1)),
        const((8, 256)), const((256, 256)), const((1, MLA_Q_LORA)), const((MLA_Q_LORA, 512)), const((256, 256)),
        const((1, 512)), const((1, MLA_KV_LORA)), const((8, 128)),
    ]
    widths = [(256, BF16), (256, BF16), (256, BF16), (256, F32), (256, F32), (512, BF16), (128, F32), (256, BF16),
              (256, BF16), (64, F32), (64, F32), (128, F32), (32, F32), (32, F32), (256, BF16), (256, BF16), (256, F32),
              (256, F32)]
    out_shape = [jax.ShapeDtypeStruct((bk, tk_, w), dt) for w, dt in widths]
    out_specs = [rows(w) for w, _ in widths]
    out_shape.append(jax.ShapeDtypeStruct((bk, XT_ROWS, tk_), F32))
    out_specs.append(pl.BlockSpec((1, XT_ROWS, tm), lambda i, b: (b, 0, i)))
    out_shape.append(jax.ShapeDtypeStruct((bk, tk_, 1024), BF16))
    out_specs.append(rows(1024))
    return pl.pallas_call(
        _inproj_kernel, grid=grid, in_specs=in_specs, out_specs=out_specs, out_shape=out_shape,
        compiler_params=_cparams(2), name="inproj",
    )(x, tab, lw['g_in'], lw['w_in'], lw['w_t'], lw['b_col'], lw['g4'], lw['hseg'], lw['g_cq'], lw['w_uq'],
      lw['segq'], lw['g_qb'], lw['g_ckv'], lw['srow'])


def _bkv_kernel(ckv_ref, kr_ref, wk_ref, wvt_ref, e_ref, segq_ref, gk_ref, kb_o, vt_o):
    c = ckv_ref[0].astype(BF16)
    kn = _dot(c, wk_ref[...])
    segq = segq_ref[...]
    ms = jnp.concatenate([_seg_mean_sq(kn[:, :256], segq), _seg_mean_sq(kn[:, 256:], segq)], axis=1)
    kn = kn * lax.rsqrt(ms + EPS) * gk_ref[...]
    kr = _dot(kr_ref[0].astype(BF16), e_ref[...])
    kb_o[0] = (kn + kr).astype(BF16)
    vt_o[0] = _dot_nt(wvt_ref[...], c).astype(BF16)


def _bkv_call(ckv, kr, lw, tm):
    b, lp, _ = ckv.shape

    def const(shape):
        return pl.BlockSpec(shape, lambda bb, i: (0,) * len(shape))

    def rows(w):
        return pl.BlockSpec((1, tm, w), lambda bb, i: (bb, i, 0))

    return pl.pallas_call(
        _bkv_kernel, grid=(b, lp // tm),
        in_specs=[rows(128), rows(128), const((128, 512)), const((256, 128)), const((128, 512)), const((256, 256)),
                  const((1, 512))],
        out_specs=[rows(512), pl.BlockSpec((1, 256, tm), lambda bb, i: (bb, 0, i))],
        out_shape=[jax.ShapeDtypeStruct((b, lp, 512), BF16), jax.ShapeDtypeStruct((b, 256, lp), BF16)],
        compiler_params=_cparams(2), name="mla_kv",
    )(ckv, kr, lw['w_uk'], lw['w_uvt'], lw['e_kr'], lw['segq'], lw['g_kb'])


def _head_lane_id(width=256):
    return lax.broadcasted_iota(I32, (1, width), 1) // HEAD_DIM


def _keep_lanes(x, pred):
    return jnp.where(pred, x.astype(F32), 0.0).astype(BF16)


def _flash_run(nfull, nkb, nkb_max, *, qm, k_block, v_block, col_sub, row_add, mask_fn, s_refs, acc_ref, tq):
    s_a, s_b = s_refs

    def produce(j, s_out):
        jc = jnp.minimum(j, nkb_max - 1)
        for h in range(N_HEADS):
            s_out[h] = _dot_nt(k_block(jc, h), qm[h])

    def half(j, c, s_in, s_out, masked):
        ms, ls = c
        produce(j + 1, s_out)
        valid = mask_fn(j) if masked else None
        new_m, new_l = [], []
        for h in range(N_HEADS):
            t = s_in[h]
            if col_sub is not None:
                t = t - col_sub(j, h)
            if masked:
                t = jnp.where(valid, t, NEG)
            m_cur = jnp.max(t, axis=0, keepdims=True)
            if row_add is not None:
                m_cur = m_cur + row_add[h]
            m_new = jnp.maximum(ms[h], m_cur)
            alpha = jnp.exp2(ms[h] - m_new)
            off = m_new if row_add is None else m_new - row_add[h]
            p = jnp.exp2(t - off).astype(BF16)
            pv = _dot(v_block(j, h), p)
            r0 = h * HEAD_DIM
            acc_ref[r0:r0 + HEAD_DIM, :] = acc_ref[r0:r0 + HEAD_DIM, :] * alpha + pv[:HEAD_DIM]
            new_l.append(ls[h] * alpha + pv[HEAD_DIM:HEAD_DIM + 1])
            new_m.append(m_new)
        return tuple(new_m), tuple(new_l)

    def step(j, c, s_in, s_out, may_end):
        def run(c):
            if nfull is None:
                return half(j, c, s_in, s_out, True)
            return lax.cond(j >= nfull, lambda cc: half(j, cc, s_in, s_out, True),
                            lambda cc: half(j, cc, s_in, s_out, False), c)
        if not may_end:
            return run(c)
        return lax.cond(j >= nkb, lambda cc: cc, run, c)

    main_masked = nfull is None
    n_main = (nkb if main_masked else nfull) // 2

    def main_body(i, c):
        c = half(2 * i, c, s_a, s_b, main_masked)
        return half(2 * i + 1, c, s_b, s_a, main_masked)

    def tail_body(i, c):
        c = step(2 * i, c, s_a, s_b, False)
        return step(2 * i + 1, c, s_b, s_a, True)

    acc_ref[...] = jnp.zeros(acc_ref.shape, F32)
    init = (tuple(jnp.full((1, tq), NEG, F32) for _ in range(N_HEADS)),
            tuple(jnp.zeros((1, tq), F32) for _ in range(N_HEADS)))
    produce(0, s_a)
    c = lax.fori_loop(0, n_main, main_body, init)
    _, ls = lax.fori_loop(n_main, (nkb + 1) // 2, tail_body, c)
    return ls


def _flash_scratch(tq, tk):
    return [pltpu.VMEM((N_HEADS, tk, tq), F32), pltpu.VMEM((N_HEADS, tk, tq), F32), pltpu.VMEM((256, tq), F32)]


def _flash_finish(o_ref, g_ref, acc_ref, ls):
    for h in range(N_HEADS):
        r0 = h * HEAD_DIM
        acc_ref[r0:r0 + HEAD_DIM, :] = acc_ref[r0:r0 + HEAD_DIM, :] / ls[h]
    y = acc_ref[...].T
    o_ref[0] = (y * g_ref[0].astype(F32)).astype(BF16)


A_WIN = A_BAND + 2 * CHUNK


def _attn_a_kernel(q_ref, k_ref, v_ref, bias_ref, g_ref, o_ref, *, lo_valid, hi_valid, group):
    hid = _head_lane_id()
    for gi in range(group):
        c = pl.program_id(1) * group + gi
        start = pl.multiple_of(c * CHUNK, CHUNK)
        q = q_ref[0, gi * CHUNK:(gi + 1) * CHUNK, :]
        qs = jnp.concatenate([_keep_lanes(q, hid == h) for h in range(N_HEADS)], axis=0)
        kb = k_ref[0, pl.ds(start, A_WIN), :]
        vb = v_ref[0, pl.ds(start, A_WIN), :]
        s = _dot_nt(qs, kb) + bias_ref[...]
        row = start + lax.broadcasted_iota(I32, (1, A_WIN), 1)
        valid = (row >= lo_valid) & (row < hi_valid)
        s = jnp.where(valid, s, NEG)
        m = jnp.max(s, axis=-1, keepdims=True)
        p = jnp.where(valid, jnp.exp(s - m), 0.0)
        l = jnp.sum(p, axis=-1, keepdims=True)
        o = _dot(p.astype(BF16), vb) / l
        y = o[(N_HEADS - 1) * CHUNK:]
        for h in range(N_HEADS - 2, -1, -1):
            y = jnp.where(hid == h, o[h * CHUNK:(h + 1) * CHUNK], y)
        gate = g_ref[0, gi * CHUNK:(gi + 1) * CHUNK, :].astype(F32)
        o_ref[0, gi * CHUNK:(gi + 1) * CHUNK, :] = (y * gate).astype(BF16)


def _attn_a_call(q, kfull, vfull, bias, gates, lo_valid, hi_valid, group):
    b, tp, _ = q.shape
    rows_kv = kfull.shape[1]
    kern = functools.partial(_attn_a_kernel, lo_valid=lo_valid, hi_valid=hi_valid, group=group)
    rows = CHUNK * group
    return pl.pallas_call(
        kern, grid=(b, tp // rows),
        in_specs=[pl.BlockSpec((1, rows, 256), lambda bb, c: (bb, c, 0)),
                  pl.BlockSpec((1, rows_kv, 256), lambda bb, c: (bb, 0, 0)),
                  pl.BlockSpec((1, rows_kv, 256), lambda bb, c: (bb, 0, 0)),
                  pl.BlockSpec((N_HEADS * CHUNK, A_WIN), lambda bb, c: (0, 0)),
                  pl.BlockSpec((1, rows, 256), lambda bb, c: (bb, c, 0))],
        out_specs=pl.BlockSpec((1, rows, 256), lambda bb, c: (bb, c, 0)),
        out_shape=jax.ShapeDtypeStruct((b, tp, 256), BF16),
        compiler_params=_cparams(2), name="attn_band",
    )(q, kfull, vfull, bias, gates)


def _attn_b_kernel(q_ref, k_ref, v_ref, g_ref, o_ref, sa_ref, sb_ref, acc_ref, *, p_len, l_len, tq, tk, nq):
    i = pl.program_id(1) if nq > 1 else 0
    q0 = p_len + i * tq
    lane = lax.broadcasted_iota(I32, (1, 256), 1)
    qm = []
    for h in range(N_HEADS):
        qg = q_ref[0, :, (h // 2) * 256:(h // 2) * 256 + 256]
        lo = (h % 2) * (MLA_NOPE + MLA_ROPE)
        qm.append(_keep_lanes(qg, (lane >= lo) & (lane < lo + MLA_NOPE + MLA_ROPE)))
    qchunk = (q0 + lax.broadcasted_iota(I32, (1, tq), 1)) // CHUNK
    nkb = jnp.minimum((q0 + tq + tk - 1) // tk, k_ref.shape[1] // tk)
    nfull = jnp.minimum(((q0 // CHUNK + 1) * CHUNK) // tk, nkb)

    def k_block(j, h):
        return k_ref[0, pl.ds(pl.multiple_of(j * tk, tk), tk), (h // 2) * 256:(h // 2) * 256 + 256]

    def mask_fn(j):
        kpos = j * tk + lax.broadcasted_iota(I32, (tk, 1), 0)
        return ((kpos // CHUNK) <= qchunk) & (kpos < l_len)

    ls = _flash_run(nfull, nkb, k_ref.shape[1] // tk, qm=qm, k_block=k_block, v_block=lambda j, h: v_ref[0, j, h],
                    col_sub=None, row_add=None, mask_fn=mask_fn, s_refs=(sa_ref, sb_ref), acc_ref=acc_ref, tq=tq)
    _flash_finish(o_ref, g_ref, acc_ref, ls)


def _attn_b_call(qb, kb, vaug, gates, p_len, l_len, tq, tk):
    b, t, _ = qb.shape
    lp = kb.shape[1]
    nq = t // tq
    kern = functools.partial(_attn_b_kernel, p_len=p_len, l_len=l_len, tq=tq, tk=tk, nq=nq)
    return pl.pallas_call(
        kern, grid=(b, nq),
        in_specs=[pl.BlockSpec((1, tq, 512), lambda bb, i: (bb, i, 0)),
                  pl.BlockSpec((1, lp, 512), lambda bb, i: (bb, 0, 0)),
                  pl.BlockSpec((1,) + vaug.shape[1:], lambda bb, i: (bb, 0, 0, 0, 0)),
                  pl.BlockSpec((1, tq, 256), lambda bb, i: (bb, i, 1))],
        out_specs=pl.BlockSpec((1, tq, 256), lambda bb, i: (bb, i, 0)),
        out_shape=jax.ShapeDtypeStruct((b, t, 256), BF16),
        scratch_shapes=_flash_scratch(tq, tk),
        compiler_params=_cparams(2), name="attn_latent",
    )(qb, kb, vaug, gates)


def _attn_c_kernel(q_ref, iq_ref, iw_ref, k_ref, v_ref, ik_ref, g_ref, o_ref, key_ref, top_ref, sa_ref, sb_ref,
                   acc_ref,
                   *, p_len, l_len, k_sel, tq, tk, nq):
    i = pl.program_id(1) if nq > 1 else 0
    q0 = p_len + i * tq
    nkb_max = k_ref.shape[1] // tk
    nkb = jnp.minimum((q0 + tq + tk - 1) // tk, nkb_max)
    qchunk = (q0 + lax.broadcasted_iota(I32, (1, tq), 1)) // CHUNK
    hid = _head_lane_id()

    iq = iq_ref[0]
    ihid = lax.broadcasted_iota(I32, (1, 256), 1) // IDX_DIM
    iqm = [_keep_lanes(iq, ihid == h) for h in range(IDX_HEADS)]
    iw = iw_ref[0] * (IDX_DIM ** -0.5 * IDX_HEADS ** -0.5)

    def score_body(edge):
        def body(j, carry):
            ks = pl.multiple_of(j * tk, tk)
            ikb = ik_ref[0, pl.ds(ks, tk), :]
            score = jnp.zeros((tk, tq), F32)
            for h in range(IDX_HEADS):
                score = score + iw[h:h + 1, :] * jnp.maximum(_dot_nt(ikb, iqm[h]), 0.0)
            if edge:
                kpos = ks + lax.broadcasted_iota(I32, (tk, 1), 0)
                score = jnp.where(((kpos // CHUNK) <= qchunk) & (kpos < l_len), score, -jnp.inf)
            key_ref[j] = score
            top_ref[j] = pltpu.bitcast(pltpu.bitcast(score, I32) & -65536, F32).astype(BF16)
            return carry
        return body

    n_free = jnp.minimum(((q0 // CHUNK + 1) * CHUNK) // tk, nkb)
    lax.fori_loop(0, n_free, score_body(False), 0)
    lax.fori_loop(n_free, nkb, score_body(True), 0)

    def cand_value(cand):
        bits = jnp.where(cand < 0, cand ^ 0x7FFFFFFF, cand)
        bits = jnp.where((bits > 0) & (bits < 0x00800000), 0x00800000, bits)
        return pltpu.bitcast(bits, F32)

    def count(pred_fn):
        def cbody(j, acc):
            pf = jnp.where(pred_fn(key_ref[j], j), 1.0, 0.0)
            for r in range(tk // 8):
                acc = acc + pf[r * 8:(r + 1) * 8]
            return acc
        acc = lax.fori_loop(0, nkb, cbody, jnp.zeros((8, tq), F32))
        return jnp.sum(acc, axis=0, keepdims=True)

    def count_top(cand_b):
        one, zero = jnp.ones((16, tq), BF16), jnp.zeros((16, tq), BF16)

        def cbody(j, acc):
            part = zero
            for r in range(tk // 16):
                part = part + jnp.where(top_ref[j, r * 16:(r + 1) * 16, :] >= cand_b, one, zero)
            return acc + part.astype(F32)
        acc = lax.fori_loop(0, nkb, cbody, jnp.zeros((16, tq), F32))
        return jnp.sum(acc, axis=0, keepdims=True)

    def top_body(it, c):
        t, n_ge = c
        cand = t + jnp.left_shift(jnp.int32(1), 15 - it)
        cbits = pltpu.bitcast(cand_value(cand * 65536), I32) & -65536
        cnt = count_top(jnp.broadcast_to(pltpu.bitcast(cbits, F32), (16, tq)).astype(BF16))
        ok = cnt >= k_sel
        return jnp.where(ok, cand, t), jnp.where(ok, cnt, n_ge)

    t16, n_ge = lax.fori_loop(0, 16, top_body, (jnp.full((1, tq), -32768, I32), jnp.zeros((1, tq), F32)))

    def bis_body(it, c):
        t, n_ge = c
        cand = t + jnp.left_shift(jnp.int32(1), 15 - it)
        cval = cand_value(cand)
        cnt = count(lambda kb, j: kb >= cval)
        ok = cnt >= k_sel
        return jnp.where(ok, cand, t), jnp.where(ok, cnt, n_ge)

    tcode, n_ge = lax.fori_loop(0, 16, bis_body, (t16 * 65536, n_ge))
    thr = cand_value(jnp.maximum(tcode, MOST_NEGATIVE_CODE))

    def idx_of(j):
        return j * tk + lax.broadcasted_iota(I32, (tk, tq), 0)

    def tie_cut(_):
        need = k_sel - count(lambda kb, j: kb > thr)

        def tie_body(it, jc):
            cand = jc + jnp.left_shift(jnp.int32(1), 12 - it)
            cnt = count(lambda kb, j: (kb == thr) & (idx_of(j) < cand))
            return jnp.where(cnt < need, cand, jc)

        return lax.fori_loop(0, 13, tie_body, jnp.zeros((1, tq), I32))

    has_tie = jnp.max(jnp.where(n_ge > k_sel, 1.0, 0.0)) > 0.0
    jcut = lax.cond(has_tie, tie_cut, lambda _: jnp.full((1, tq), 2 ** 30, I32), 0)

    q = q_ref[0]
    qm = [_keep_lanes(q, hid == h) for h in range(N_HEADS)]

    def mask_fn(j):
        kb = key_ref[j]
        return (kb > thr) | ((kb == thr) & (idx_of(j) <= jcut))

    ls = _flash_run(None, nkb, nkb_max, qm=qm,
                    k_block=lambda j, h: k_ref[0, pl.ds(pl.multiple_of(j * tk, tk), tk), :],
                    v_block=lambda j, h: v_ref[0, j, 0], col_sub=None, row_add=None, mask_fn=mask_fn,
                    s_refs=(sa_ref, sb_ref), acc_ref=acc_ref, tq=tq)
    _flash_finish(o_ref, g_ref, acc_ref, ls)


def _attn_c_call(cq, ciq, iwt, krep, vaug, ikrep, gates, p_len, l_len, k_sel, tq, tk):
    b, t, _ = cq.shape
    lp = krep.shape[1]
    assert lp <= 8192
    nq = t // tq
    kern = functools.partial(_attn_c_kernel, p_len=p_len, l_len=l_len, k_sel=float(k_sel), tq=tq, tk=tk, nq=nq)
    return pl.pallas_call(
        kern, grid=(b, nq),
        in_specs=[pl.BlockSpec((1, tq, 256), lambda bb, i: (bb, i, 0)),
                  pl.BlockSpec((1, tq, 256), lambda bb, i: (bb, i, 0)),
                  pl.BlockSpec((1, 8, tq), lambda bb, i: (bb, 0, i)),
                  pl.BlockSpec((1, lp, 256), lambda bb, i: (bb, 0, 0)),
                  pl.BlockSpec((1,) + vaug.shape[1:], lambda bb, i: (bb, 0, 0, 0, 0)),
                  pl.BlockSpec((1, lp, 256), lambda bb, i: (bb, 0, 0)),
                  pl.BlockSpec((1, tq, 256), lambda bb, i: (bb, i, 2))],
        out_specs=pl.BlockSpec((1, tq, 256), lambda bb, i: (bb, i, 0)),
        out_shape=jax.ShapeDtypeStruct((b, t, 256), BF16),
        scratch_shapes=[pltpu.VMEM((lp // tk, tk, tq), F32), pltpu.VMEM((lp // tk, tk, tq), BF16)]
        + _flash_scratch(tq, tk),
        compiler_params=_cparams(2), name="attn_sparse",
    )(cq, ciq, iwt, krep, vaug, ikrep, gates)


def _attn_d_kernel(q_ref, k_ref, v_ref, lf_ref, g_ref, o_ref, fc_ref, fk_ref, sa_ref, sb_ref, acc_ref,
                   *, p_len, tq, tk, nq):
    i = pl.program_id(1) if nq > 1 else 0
    lp = k_ref.shape[1]
    nch = lp // 128

    def cumulate():
        x = lf_ref[0]
        lane = lax.broadcasted_iota(I32, (1, lp), 1)
        step = 1
        while step < lp:
            x = x + jnp.where(lane >= step, pltpu.roll(x, step, 1), 0.0)
            step *= 2
        x = x * LOG2E
        for c in range(nch):
            fc_ref[c] = x[:, c * 128:(c + 1) * 128]

        eye = lax.broadcasted_iota(I32, (128, 128), 0) == lax.broadcasted_iota(I32, (128, 128), 1)

        def spread(c, carry):
            rows = fc_ref[c]
            for h in range(N_HEADS):
                col = jnp.sum(jnp.where(eye, rows[h:h + 1, :], 0.0), axis=1, keepdims=True)
                fk_ref[h, pl.ds(pl.multiple_of(c * 128, 128), 128), :] = jnp.broadcast_to(col, (128, 128))
            return carry

        lax.fori_loop(0, nch, spread, 0)

    if nq > 1:
        pl.when(i == 0)(cumulate)
    else:
        cumulate()

    q0 = p_len + i * tq
    hid = _head_lane_id()
    q = q_ref[0]
    qm = [_keep_lanes(q, hid == h) for h in range(N_HEADS)]
    c0 = q0 // 128
    fq_rows = [fc_ref[c0 + c] for c in range(tq // 128)]
    fq = [jnp.concatenate([r[h:h + 1, :] for r in fq_rows], axis=1) for h in range(N_HEADS)]
    qpos = q0 + lax.broadcasted_iota(I32, (1, tq), 1)
    nkb = jnp.minimum((q0 + tq + tk - 1) // tk, lp // tk)
    nfull = jnp.minimum((q0 + 1) // tk, nkb)

    def col_sub(j, h):
        fk = fk_ref[h, pl.ds(pl.multiple_of(j * tk, tk), tk), :]
        return jnp.concatenate([fk] * (tq // 128), axis=1)

    def mask_fn(j):
        return (j * tk + lax.broadcasted_iota(I32, (tk, 1), 0)) <= qpos

    ls = _flash_run(nfull, nkb, lp // tk, qm=qm,
                    k_block=lambda j, h: k_ref[0, pl.ds(pl.multiple_of(j * tk, tk), tk), :],
                    v_block=lambda j, h: v_ref[0, j, h], col_sub=col_sub, row_add=fq, mask_fn=mask_fn,
                    s_refs=(sa_ref, sb_ref), acc_ref=acc_ref, tq=tq)
    _flash_finish(o_ref, g_ref, acc_ref, ls)


def _attn_d_call(dq, kd, vaug, lft, gates, p_len, tq, tk):
    b, t, _ = dq.shape
    lp = kd.shape[1]
    nq = t // tq
    assert p_len % 128 == 0 and tq % 128 == 0 and p_len + t <= lp
    kern = functools.partial(_attn_d_kernel, p_len=p_len, tq=tq, tk=tk, nq=nq)
    return pl.pallas_call(
        kern, grid=(b, nq),
        in_specs=[pl.BlockSpec((1, tq, 256), lambda bb, i: (bb, i, 0)),
                  pl.BlockSpec((1, lp, 256), lambda bb, i: (bb, 0, 0)),
                  pl.BlockSpec((1,) + vaug.shape[1:], lambda bb, i: (bb, 0, 0, 0, 0)),
                  pl.BlockSpec((1, 8, lp), lambda bb, i: (bb, 0, 0)),
                  pl.BlockSpec((1, tq, 256), lambda bb, i: (bb, i, 3))],
        out_specs=pl.BlockSpec((1, tq, 256), lambda bb, i: (bb, i, 0)),
        out_shape=jax.ShapeDtypeStruct((b, t, 256), BF16),
        scratch_shapes=[pltpu.VMEM((lp // 128, 8, 128), F32), pltpu.VMEM((N_HEADS, lp, 128), F32)]
        + _flash_scratch(tq, tk),
        compiler_params=_cparams(2), name="attn_forget",
    )(dq, kd, vaug, lft, gates)


def _out_kernel(ya_ref, yb_ref, yc_ref, yd_ref, x_ref, p_ref, wo_ref, gple_ref, wg_ref, wp_ref, o_ref):
    mixed = (_dot(ya_ref[0], wo_ref[0:256, :]) + _dot(yb_ref[0], wo_ref[256:512, :])
             + _dot(yc_ref[0], wo_ref[512:768, :]) + _dot(yd_ref[0], wo_ref[768:1024, :]))
    x1 = x_ref[0] + mixed
    ms = jnp.mean(x1 * x1, axis=-1, keepdims=True)
    xn = (x1 * lax.rsqrt(ms + EPS) * gple_ref[...]).astype(BF16)
    gate = 1.0 / (1.0 + jnp.exp(-_dot(xn, wg_ref[...])))
    o_ref[0] = x1 + gate * _dot(p_ref[0].astype(BF16), wp_ref[...])


def _out_call(ys, x, p, lw, tm):
    b, t, _ = x.shape

    def const(shape):
        return pl.BlockSpec(shape, lambda bb, i: (0,) * len(shape))

    def rows(w):
        return pl.BlockSpec((1, tm, w), lambda bb, i: (bb, i, 0))

    return pl.pallas_call(
        _out_kernel, grid=(b, t // tm),
        in_specs=[rows(256)] * 4 + [rows(D_MODEL), rows(PLE_DIM), const((1024, D_MODEL)), const((1, D_MODEL)),
                                    const((D_MODEL, D_MODEL)), const((PLE_DIM, D_MODEL))],
        out_specs=rows(D_MODEL),
        out_shape=jax.ShapeDtypeStruct((b, t, D_MODEL), F32),
        compiler_params=_cparams(2), name="out_proj",
    )(*ys, x, p, lw['w_out'], lw['g_ple'], lw['w_ple_gate'], lw['w_ple_proj'])


def _seg_matrix(segments):
    m = np.zeros((256, 256), np.float32)
    for lo, n in segments:
        m[lo:lo + n, lo:lo + n] = 1.0 / n
    return jnp.asarray(m, BF16)


_PAIR_SEGS = ((0, 64), (64, 32), (96, 64), (160, 32), (192, 64))


def _pair_cols(head):
    return (head // 2) * 256 + (head % 2) * (MLA_NOPE + MLA_ROPE)


def _prep_layer(i, g_in, w_in, g_qk, g_rope, a_rel_bias, b_g_cq, b_w_uq, b_g_ckv, b_w_ukv, d_b_f, w_out, g_ple,
                w_ple_gate, w_ple_proj):
    w = w_in[i]
    cols = []
    for n in _MY_ORDER:
        if n == 'pad56':
            cols.append(jnp.zeros((D_MODEL, 56), F32))
        else:
            o, wd = _REF_OFF[n]
            cols.append(w[:, o:o + wd])
    lw = {'w_in': jnp.concatenate(cols, axis=1).astype(BF16), 'g_in': g_in[i][None, :]}
    wt = [w[:, _REF_OFF[n][0]:_REF_OFF[n][0] + _REF_OFF[n][1]] for n in ('d_v', 'c_v', 'c_iw', 'd_f')]
    wt.append(jnp.zeros((D_MODEL, XT_ROWS - XT_DF - N_HEADS), F32))
    lw['w_t'] = jnp.concatenate(wt, axis=1).T.astype(BF16)
    lw['b_col'] = jnp.concatenate([d_b_f[i], jnp.zeros((8 - N_HEADS,), F32)])[:, None]
    g = g_qk[i]
    sc = HEAD_DIM ** -0.5
    rows = [jnp.tile(g[0], 4) * sc, jnp.tile(g[1], 4), jnp.tile(g[2], 4) * (sc * LOG2E),
            jnp.tile(g[4], 4) * (sc * LOG2E), jnp.tile(g[5], 4)]
    lw['g4'] = jnp.stack(rows + [jnp.zeros((256,), F32)] * 3)
    lw['hseg'] = _seg_matrix(tuple((h * 64, 64) for h in range(4)))
    lw['segq'] = _seg_matrix(_PAIR_SEGS)
    lw['g_cq'] = b_g_cq[i][None, :]
    lw['g_ckv'] = b_g_ckv[i][None, :]
    qscale = (MLA_NOPE + MLA_ROPE) ** -0.5 * LOG2E
    hw = MLA_NOPE + MLA_ROPE
    wuq, gqb, wuk, gkb, wuv = [], [], [], [], []
    e_kr = np.zeros((128, 512), np.float32)
    for h in range(N_HEADS):
        wuq.append(b_w_uq[i][:, h * hw:(h + 1) * hw])
        gqb += [g[6] * qscale, g_rope[i][0] * qscale]
        srck = h * (MLA_NOPE + HEAD_DIM)
        wuk += [b_w_ukv[i][:, srck:srck + MLA_NOPE], jnp.zeros((MLA_KV_LORA, MLA_ROPE), F32)]
        gkb += [g[7], jnp.zeros((MLA_ROPE,), F32)]
        wuv.append(b_w_ukv[i][:, srck + MLA_NOPE:srck + MLA_NOPE + HEAD_DIM])
        e_kr[np.arange(MLA_ROPE), _pair_cols(h) + MLA_NOPE + np.arange(MLA_ROPE)] = 1.0
        if h % 2 == 1:
            wuq.append(jnp.zeros((MLA_Q_LORA, 256 - 2 * hw), F32))
            wuk.append(jnp.zeros((MLA_KV_LORA, 256 - 2 * hw), F32))
            gqb.append(jnp.zeros((256 - 2 * hw,), F32))
            gkb.append(jnp.zeros((256 - 2 * hw,), F32))
    lw['w_uq'] = jnp.concatenate(wuq, axis=1).astype(BF16)
    lw['g_qb'] = jnp.concatenate(gqb)[None, :]
    lw['w_uk'] = jnp.concatenate(wuk, axis=1).astype(BF16)
    lw['g_kb'] = jnp.concatenate(gkb)[None, :]
    lw['e_kr'] = jnp.asarray(e_kr, BF16)
    lw['w_uvt'] = jnp.concatenate(wuv, axis=1).T.astype(BF16)
    one = jnp.ones((128,), F32)
    lw['srow'] = jnp.stack([jnp.concatenate([g[3], one[HEAD_DIM:]]), jnp.concatenate([g_rope[i][1], one[MLA_ROPE:]])]
                           + [one] * 6)
    ab = a_rel_bias[i]
    ext = jnp.concatenate([jnp.broadcast_to(ab[:, 2 * REL_CLIP:], (N_HEADS, A_WIN - REL_CLIP - 1)),
                           ab[:, REL_CLIP - CHUNK + 1:][:, ::-1]], axis=1)
    lw['a_bias'] = jnp.stack([ext[:, CHUNK - 1 - q:CHUNK - 1 - q + A_WIN] for q in range(CHUNK)],
                             axis=1).reshape(N_HEADS * CHUNK, A_WIN)
    lw['w_out'] = w_out[i].astype(BF16)
    lw['g_ple'] = g_ple[i][None, :]
    lw['w_ple_gate'] = w_ple_gate[i].astype(BF16)
    lw['w_ple_proj'] = w_ple_proj[i].astype(BF16)
    return lw


def _cos_sin(pos, rot_dim):
    half = rot_dim // 2
    inv = jnp.float32(ROPE_THETA) ** (-jnp.arange(half, dtype=F32) / half)
    ang = pos.astype(F32)[:, None] * inv[None, :]
    return [jnp.cos(ang), jnp.sin(ang)]


def _rot_placement():
    src = {MLA_ROPE: 1, ROT_DIM: 1 + MLA_ROPE}
    e = np.zeros((1 + MLA_ROPE + ROT_DIM, TAB_W), np.float32)
    sections = ((TAB_QB, 512, [_pair_cols(h) + MLA_NOPE for h in range(N_HEADS)], MLA_ROPE),
                (TAB_CQ, 256, [h * HEAD_DIM for h in range(N_HEADS)], ROT_DIM),
                (TAB_S1, 128, [0], ROT_DIM), (TAB_S2, 128, [0], MLA_ROPE))
    for base, width, starts, rot_dim in sections:
        half = rot_dim // 2
        e[0, base:base + width] = 1.0
        for st in starts:
            for i in range(half):
                c, s = src[rot_dim] + i, src[rot_dim] + half + i
                e[0, base + st + i] = e[0, base + st + half + i] = 0.0
                e[c, base + st + i] = e[c, base + st + half + i] = 1.0
                e[s, base + width + st + half + i] = 1.0
                e[s, base + 2 * width + st + i] = -1.0
    return jnp.asarray(e)


def _rot_tables(pos):
    src = jnp.concatenate([jnp.ones((pos.shape[0], 1), F32)] + _cos_sin(pos, MLA_ROPE) + _cos_sin(pos, ROT_DIM),
                          axis=1)
    return jnp.dot(src, _rot_placement(), precision=lax.Precision.HIGHEST)


def _pad_axis(a, size, axis):
    pads = [(0, 0)] * a.ndim
    pads[axis] = (0, size - a.shape[axis])
    return jnp.pad(a, pads)


def _cat_rows(past, new, rows):
    a = new if past is None else jnp.concatenate([past.astype(new.dtype), new], axis=1)
    return _pad_axis(a, rows, 1)


def _cat_lanes(past_t, new_t, lanes):
    a = new_t if past_t is None else jnp.concatenate([past_t.astype(new_t.dtype), new_t], axis=2)
    return _pad_axis(a, lanes, 2)


def _value_blocks(vt, tk):
    b, r, lp = vt.shape
    h = r // HEAD_DIM
    v = vt.astype(BF16).reshape(b, h, HEAD_DIM, lp)
    v = jnp.concatenate([v, jnp.ones((b, h, V_ROWS - HEAD_DIM, lp), BF16)], axis=2)
    return v.reshape(b, h, V_ROWS, lp // tk, tk).transpose(0, 3, 1, 2, 4)


def _layer(x, p, past, lw, tab, *, fold_batch, tm, tq, tq_c, tk, a_group):
    b, t, _ = x.shape
    p_len = 0 if past is None else past[2].shape[1]
    l_len = p_len + t
    tqp = -(-t // tq) * tq
    tqp_c = -(-t // tq_c) * tq_c
    lp = -(-(p_len + max(tqp, tqp_c)) // tk) * tk

    xin = x.reshape(1, b * t, D_MODEL) if fold_batch else x
    outs = _inproj_call(xin, tab, lw, tm)
    if fold_batch:
        xt = outs[18][0].reshape(XT_ROWS, b, t).transpose(1, 0, 2)
        outs = [o.reshape((b, t) + o.shape[2:]) for o in outs[:18]] + [xt, outs[19].reshape(b, t, 1024)]
    (aq, akb, avb, akf, avf, qb, ckvf, cq, ciq, c_k, c_v, s2, b_kr, c_ik, dq, dkb, dkf, dvf, xt, gates) = outs

    lft = xt[:, XT_DF:XT_ROWS]
    d_lf = lft[:, :N_HEADS, :].transpose(0, 2, 1)
    state = (akf.reshape(b, t, N_HEADS, HEAD_DIM), avf.reshape(b, t, N_HEADS, HEAD_DIM), ckvf, b_kr, c_k, c_v, c_ik,
             dkf.reshape(b, t, N_HEADS, HEAD_DIM), dvf.reshape(b, t, N_HEADS, HEAD_DIM), d_lf)

    if past is None:
        pa_k = pa_v = pb_ckv = pb_kr = pc_k = pc_v = pc_ik = pd_k = pd_v = pd_lf = None
        pa = 0
    else:
        pa_k, pa_v, pb_ckv, pb_kr, pc_k, pc_v, pc_ik, pd_k, pd_v, pd_lf = past
        pa = pa_k.shape[1]
        pa_k = pa_k.reshape(b, pa, 256)
        pa_v = pa_v.reshape(b, pa, 256)
        pd_k = pd_k.reshape(b, p_len, 256)
        pd_v = pd_v.reshape(b, p_len, 256)

    tp = -(-t // (CHUNK * a_group)) * (CHUNK * a_group)
    front = CHUNK + A_BAND - pa
    a_rows = CHUNK + A_BAND + tp

    def band_src(pst, new):
        parts = [jnp.zeros((b, front, 256), BF16)]
        if pst is not None:
            parts.append(pst.astype(BF16))
        parts.append(new)
        return _pad_axis(jnp.concatenate(parts, axis=1), a_rows, 1)

    ya = _attn_a_call(_pad_axis(aq, tp, 1), band_src(pa_k, akb), band_src(pa_v, avb), lw['a_bias'],
                      _pad_axis(gates, tp, 1), front, CHUNK + A_BAND + t, a_group)[:, :t]

    gates_q = _pad_axis(gates, tqp, 1)

    ckv_all = _cat_rows(pb_ckv, ckvf, lp)
    kr_new = s2 if past is None else _pad_axis(b_kr, 128, 2)
    kr_past = None if past is None else _pad_axis(pb_kr, 128, 2)
    kr_all = _cat_rows(kr_past, kr_new, lp)
    kb, vbt = _bkv_call(ckv_all, kr_all, lw, tk)
    yb = _attn_b_call(_pad_axis(qb, tqp, 1), kb, _value_blocks(vbt, tk), gates_q, p_len, l_len, tq, tk)[:, :t]

    krep = jnp.tile(_cat_rows(pc_k, c_k, lp).astype(BF16), (1, 1, N_HEADS))
    ikrep = jnp.tile(_cat_rows(pc_ik, c_ik, lp).astype(BF16), (1, 1, IDX_HEADS))
    cvt = _cat_lanes(None if past is None else pc_v.transpose(0, 2, 1), xt[:, XT_CV:XT_IW], lp)
    iwt = _pad_axis(xt[:, XT_IW:XT_DF], tqp_c, 2)
    k_sel = min(DSA_TOPK, l_len // 4)
    yc = _attn_c_call(_pad_axis(cq, tqp_c, 1), _pad_axis(ciq, tqp_c, 1), iwt, krep, _value_blocks(cvt, tk), ikrep,
                      _pad_axis(gates, tqp_c, 1), p_len, l_len, k_sel, tq_c, tk)[:, :t]

    kd = _cat_rows(None if past is None else pd_k.astype(BF16), dkb, lp)
    dvt = _cat_lanes(None if past is None else pd_v.transpose(0, 2, 1), xt[:, XT_DV:XT_CV], lp)
    lf_past = None if past is None else _pad_axis(pd_lf.transpose(0, 2, 1), 8, 1)
    lf_all = _cat_lanes(lf_past, lft, lp)
    yd = _attn_d_call(_pad_axis(dq, tqp, 1), kd, _value_blocks(dvt, tk), lf_all, gates_q, p_len, tq, tk)[:, :t]

    if fold_batch:
        x_new = _out_call([y.reshape(1, b * t, 256) for y in (ya, yb, yc, yd)], x.reshape(1, b * t, D_MODEL),
                          p.reshape(1, b * t, PLE_DIM), lw, tm).reshape(b, t, D_MODEL)
    else:
        x_new = _out_call([ya, yb, yc, yd], x, p, lw, tm)
    return x_new, state


def kernel(x_prompt, x_sample, cache_a_k, cache_a_v, cache_b_ckv, cache_b_krope, cache_c_k, cache_c_v, cache_c_idx_k,
           cache_d_k, cache_d_v, cache_d_logf, p_prompt, p_sample, g_in, w_in, g_qk, g_rope, a_rel_bias, b_g_cq,
           b_w_uq, b_g_ckv, b_w_ukv, d_b_f, w_out, g_ple, w_ple_gate, w_ple_proj):
    depth = w_in.shape[0]
    b, t = x_prompt.shape[:2]
    bs, ts = x_sample.shape[:2]
    past_len = cache_b_ckv.shape[2]
    tab_p = _rot_tables(jnp.arange(t))
    tab_s = jnp.tile(_rot_tables(past_len + jnp.arange(ts)), (bs, 1))
    xp, xs = x_prompt, x_sample
    states_p, states_s = [], []
    for i in range(depth):
        lw = _prep_layer(i, g_in, w_in, g_qk, g_rope, a_rel_bias, b_g_cq, b_w_uq, b_g_ckv, b_w_ukv, d_b_f, w_out,
                         g_ple, w_ple_gate, w_ple_proj)
        xp, st_p = _layer(xp, p_prompt[i], None, lw, tab_p, fold_batch=False, tm=256, tq=256, tq_c=512, tk=256,
                           a_group=4)
        past = (cache_a_k[i], cache_a_v[i], cache_b_ckv[i], cache_b_krope[i], cache_c_k[i], cache_c_v[i],
                cache_c_idx_k[i], cache_d_k[i], cache_d_v[i], cache_d_logf[i])
        xs, st_s = _layer(xs, p_sample[i], past, lw, tab_s, fold_batch=True, tm=bs * ts, tq=128, tq_c=128, tk=256,
                           a_group=1)
        states_p.append(st_p)
        states_s.append(st_s)
    sp = [jnp.stack(z) for z in zip(*states_p)]
    ss = [jnp.stack(z) for z in zip(*states_s)]
    keep = min(A_BAND, t)
    sp[0] = sp[0][:, :, t - keep:]
    sp[1] = sp[1][:, :, t - keep:]
    return (xp, xs, *sp, *ss)
```

```python
import functools

import numpy as np
import jax
import jax.numpy as jnp
from jax import lax
from jax.experimental import pallas as pl
from jax.experimental.pallas import tpu as pltpu

F32 = jnp.float32
BF16 = jnp.bfloat16
I32 = jnp.int32

D_MODEL = 1024
CHUNK = 64
EPS = 1e-6
HEAD_DIM = 64
N_HEADS = 4
GROUP_WIDTH = 256
ROT_DIM = 16
ROPE_THETA = 500000.0
A_BAND = 8 * CHUNK
REL_CLIP = 128
MLA_Q_LORA = 384
MLA_KV_LORA = 128
MLA_NOPE = 64
MLA_ROPE = 32
IDX_HEADS = 8
IDX_DIM = 32
DSA_TOPK = 256
PLE_DIM = 256

_REF_SPLITS = (
    ('a_q', 256), ('a_k', 256), ('a_v', 256), ('a_g', 256),
    ('b_cq', 384), ('b_ckv', 128), ('b_kr', 32), ('b_g', 256),
    ('c_q', 256), ('c_k', 64), ('c_v', 64), ('c_iq', 256), ('c_ik', 32), ('c_iw', 8), ('c_g', 256),
    ('d_q', 256), ('d_k', 256), ('d_v', 256), ('d_f', 4), ('d_g', 256),
)
_REF_OFF = {}
_o = 0
for _n, _w in _REF_SPLITS:
    _REF_OFF[_n] = (_o, _w)
    _o += _w

_MY_ORDER = ('a_q', 'a_k', 'a_v', 'a_g', 'b_cq', 'b_ckv', 'b_g', 'c_q', 'c_iq', 'c_g', 'd_q', 'd_k', 'd_v', 'd_g',
             'c_k', 'c_v', 'b_kr', 'c_ik', 'c_iw', 'pad56')
_MY_OFF = {}
_o = 0
for _n in _MY_ORDER:
    _w = 56 if _n == 'pad56' else _REF_OFF[_n][1]
    _MY_OFF[_n] = _o
    _o += _w
W_IN_COLS = _o
_PROJ_GROUPS = ((0, _MY_OFF['b_cq']), (_MY_OFF['b_cq'], _MY_OFF['c_q']), (_MY_OFF['c_q'], _MY_OFF['d_q']),
                (_MY_OFF['d_q'], _MY_OFF['c_k']), (_MY_OFF['c_k'], W_IN_COLS))

XT_DV, XT_CV, XT_IW, XT_DF, XT_ROWS = 0, 256, 320, 328, 336

TAB_QB, TAB_CQ, TAB_S1, TAB_S2 = 0, 1536, 2304, 2688
TAB_W = 3072

V_ROWS = HEAD_DIM + 16
NEG = -1e30
LOG2E = 1.4426950408889634
MOST_NEGATIVE_CODE = -2139095040
VMEM_LIMIT = 56 * 1024 * 1024


def _cparams(n_axes):
    return pltpu.CompilerParams(dimension_semantics=("arbitrary",) * n_axes, vmem_limit_bytes=VMEM_LIMIT)


def _dot(a, b):
    return jnp.dot(a, b, preferred_element_type=F32)


def _dot_nt(a, b):
    return lax.dot_general(a, b, (((1,), (1,)), ((), ())), preferred_element_type=F32)


def _seg_mean_sq(t, seg):
    sq = t * t
    hi = sq.astype(BF16)
    lo = (sq - hi.astype(F32)).astype(BF16)
    return _dot(hi, seg) + _dot(lo, seg)


def _rotate(t, cos, sin_up, sin_dn, half):
    w = t.shape[-1]
    return t * cos + pltpu.roll(t, half, 1) * sin_up + pltpu.roll(t, w - half, 1) * sin_dn


def _silu(g):
    return g * (1.0 / (1.0 + jnp.exp(-g)))


def _log_sigmoid(v):
    return jnp.minimum(v, 0.0) - jnp.log1p(jnp.exp(-jnp.abs(v)))


def _inproj_kernel(x_ref, tab_ref, gin_ref, w_ref, wt_ref, bcol_ref, g4_ref, hseg_ref, gcq_ref, wuq_ref, segq_ref,
                   gqb_ref, gckv_ref, srow_ref,
                   aq_o, akb_o, avb_o, akf_o, avf_o, qb_o, ckvf_o, cq_o, ciq_o, ck_o, cv_o, s2_o, bkr_o, cik_o,
                   dq_o, dkb_o, dkf_o, dvf_o, xt_o, gate_o, dva_o, cva_o):
    x = x_ref[0]
    ms = jnp.mean(x * x, axis=-1, keepdims=True)
    xn = (x * lax.rsqrt(ms + EPS) * gin_ref[...]).astype(BF16)

    group_out = {}

    def proj(name, n):
        c0 = _MY_OFF[name]
        g0, g1 = next((a, b) for a, b in _PROJ_GROUPS if a <= c0 < b)
        if g0 not in group_out:
            group_out[g0] = _dot(xn, w_ref[:, g0:g1])
        return group_out[g0][:, c0 - g0:c0 - g0 + n]

    hseg = hseg_ref[...]

    def headnorm(t, row):
        return t * lax.rsqrt(_seg_mean_sq(t, hseg) + EPS) * g4_ref[row:row + 1, :]

    def fullnorm(t, g):
        return t * lax.rsqrt(jnp.mean(t * t, axis=-1, keepdims=True) + EPS) * g

    aq_o[0] = headnorm(proj('a_q', 256), 0).astype(BF16)
    ak = headnorm(proj('a_k', 256), 1)
    akf_o[0] = ak
    akb_o[0] = ak.astype(BF16)
    av = proj('a_v', 256)
    avf_o[0] = av
    avb_o[0] = av.astype(BF16)
    gate_o[0, :, 0:256] = _silu(proj('a_g', 256)).astype(BF16)

    cqn = fullnorm(proj('b_cq', MLA_Q_LORA), gcq_ref[...]).astype(BF16)
    qb = _dot(cqn, wuq_ref[...])
    segq = segq_ref[...]
    ms_q = jnp.concatenate([_seg_mean_sq(qb[:, :256], segq), _seg_mean_sq(qb[:, 256:], segq)], axis=1)
    qbn = qb * lax.rsqrt(ms_q + EPS) * gqb_ref[...]
    qbn = _rotate(qbn, tab_ref[:, TAB_QB:TAB_QB + 512], tab_ref[:, TAB_QB + 512:TAB_QB + 1024],
                  tab_ref[:, TAB_QB + 1024:TAB_QB + 1536], MLA_ROPE // 2)
    qb_o[0] = qbn.astype(BF16)
    ckvf_o[0] = fullnorm(proj('b_ckv', MLA_KV_LORA), gckv_ref[...])
    gate_o[0, :, 256:512] = _silu(proj('b_g', 256)).astype(BF16)

    cq = headnorm(proj('c_q', 256), 2)
    cq = _rotate(cq, tab_ref[:, TAB_CQ:TAB_CQ + 256], tab_ref[:, TAB_CQ + 256:TAB_CQ + 512],
                 tab_ref[:, TAB_CQ + 512:TAB_CQ + 768], ROT_DIM // 2)
    cq_o[0] = cq.astype(BF16)
    ciq_o[0] = proj('c_iq', 256).astype(BF16)
    gate_o[0, :, 512:768] = _silu(proj('c_g', 256)).astype(BF16)

    dq_o[0] = headnorm(proj('d_q', 256), 3).astype(BF16)
    dk = headnorm(proj('d_k', 256), 4)
    dkf_o[0] = dk
    dkb_o[0] = dk.astype(BF16)
    dvf_o[0] = proj('d_v', 256)
    gate_o[0, :, 768:1024] = _silu(proj('d_g', 256)).astype(BF16)

    lane = lax.broadcasted_iota(I32, (1, 128), 1)
    t = proj('c_k', 128)
    m64 = lane < HEAD_DIM
    ms1 = jnp.sum(jnp.where(m64, t * t, 0.0), axis=-1, keepdims=True) * (1.0 / HEAD_DIM)
    t = jnp.where(m64, t * lax.rsqrt(ms1 + EPS), t) * srow_ref[0:1, :]
    t = _rotate(t, tab_ref[:, TAB_S1:TAB_S1 + 128], tab_ref[:, TAB_S1 + 128:TAB_S1 + 256],
                tab_ref[:, TAB_S1 + 256:TAB_S1 + 384], ROT_DIM // 2)
    ck_o[0] = t[:, :HEAD_DIM]
    cv_o[0] = t[:, HEAD_DIM:]
    t = proj('b_kr', 128)
    m32 = lane < MLA_ROPE
    ms2 = jnp.sum(jnp.where(m32, t * t, 0.0), axis=-1, keepdims=True) * (1.0 / MLA_ROPE)
    t = jnp.where(m32, t * lax.rsqrt(ms2 + EPS), t) * srow_ref[1:2, :]
    t = _rotate(t, tab_ref[:, TAB_S2:TAB_S2 + 128], tab_ref[:, TAB_S2 + 128:TAB_S2 + 256],
                tab_ref[:, TAB_S2 + 256:TAB_S2 + 384], MLA_ROPE // 2)
    s2_o[0] = t
    bkr_o[0] = t[:, :MLA_ROPE]
    cik_o[0] = t[:, MLA_ROPE:MLA_ROPE + IDX_DIM]

    xt = _dot_nt(wt_ref[...], xn)
    xt_o[0, 0:XT_DF, :] = xt[0:XT_DF]
    xt_o[0, XT_DF:XT_ROWS, :] = _log_sigmoid(xt[XT_DF:XT_ROWS] + bcol_ref[...])
    _store_value_block(dva_o, xt[XT_DV:XT_CV].astype(BF16), N_HEADS)
    _store_value_block(cva_o, xt[XT_CV:XT_IW].astype(BF16), 1)


def _inproj_call(x, tab, lw, tm):
    bk, tk_, _ = x.shape
    grid = (tk_ // tm, bk)

    def const(shape):
        return pl.BlockSpec(shape, lambda i, b: (0,) * len(shape))

    def rows(w):
        return pl.BlockSpec((1, tm, w), lambda i, b: (b, i, 0))

    in_specs = [
        rows(D_MODEL),
        pl.BlockSpec((tm, TAB_W), lambda i, b: (i, 0)),
        const((1, D_MODEL)), const((D_MODEL, W_IN_COLS)), const((XT_ROWS, D_MODEL)), const((8, 1)),
        const((8, 256)), const((256, 256)), const((1, MLA_Q_LORA)), const((MLA_Q_LORA, 512)), const((256, 256)),
        const((1, 512)), const((1, MLA_KV_LORA)), const((8, 128)),
    ]
    widths = [(256, BF16), (256, BF16), (256, BF16), (256, F32), (256, F32), (512, BF16), (128, F32), (256, BF16),
              (256, BF16), (64, F32), (64, F32), (128, F32), (32, F32), (32, F32), (256, BF16), (256, BF16), (256, F32),
              (256, F32)]
    out_shape = [jax.ShapeDtypeStruct((bk, tk_, w), dt) for w, dt in widths]
    out_specs = [rows(w) for w, _ in widths]
    out_shape.append(jax.ShapeDtypeStruct((bk, XT_ROWS, tk_), F32))
    out_specs.append(pl.BlockSpec((1, XT_ROWS, tm), lambda i, b: (b, 0, i)))
    out_shape.append(jax.ShapeDtypeStruct((bk, tk_, 1024), BF16))
    out_specs.append(rows(1024))
    for heads in (N_HEADS, 1):
        out_shape.append(jax.ShapeDtypeStruct((bk, tk_ // tm, heads, V_ROWS, tm), BF16))
        out_specs.append(pl.BlockSpec((1, 1, heads, V_ROWS, tm), lambda i, b: (b, i, 0, 0, 0)))
    return pl.pallas_call(
        _inproj_kernel, grid=grid, in_specs=in_specs, out_specs=out_specs, out_shape=out_shape,
        compiler_params=_cparams(2), name="inproj",
    )(x, tab, lw['g_in'], lw['w_in'], lw['w_t'], lw['b_col'], lw['g4'], lw['hseg'], lw['g_cq'], lw['w_uq'],
      lw['segq'], lw['g_qb'], lw['g_ckv'], lw['srow'])


def _store_value_block(va_o, vt, heads):
    tk = vt.shape[1]
    for h in range(heads):
        va_o[0, 0, h, 0:HEAD_DIM, :] = vt[h * HEAD_DIM:(h + 1) * HEAD_DIM]
        va_o[0, 0, h, HEAD_DIM:V_ROWS, :] = jnp.ones((V_ROWS - HEAD_DIM, tk), BF16)


def _bkv_kernel(ckv_ref, kr_ref, wk_ref, wvt_ref, e_ref, segq_ref, gk_ref, kb_o, va_o):
    c = ckv_ref[0].astype(BF16)
    kn = _dot(c, wk_ref[...])
    segq = segq_ref[...]
    ms = jnp.concatenate([_seg_mean_sq(kn[:, :256], segq), _seg_mean_sq(kn[:, 256:], segq)], axis=1)
    kn = kn * lax.rsqrt(ms + EPS) * gk_ref[...]
    kr = _dot(kr_ref[0].astype(BF16), e_ref[...])
    kb_o[0] = (kn + kr).astype(BF16)
    vt = _dot_nt(wvt_ref[...], c).astype(BF16)
    _store_value_block(va_o, vt, N_HEADS)


def _bkv_call(ckv, kr, lw, tm):
    b, lp, _ = ckv.shape

    def const(shape):
        return pl.BlockSpec(shape, lambda bb, i: (0,) * len(shape))

    def rows(w):
        return pl.BlockSpec((1, tm, w), lambda bb, i: (bb, i, 0))

    return pl.pallas_call(
        _bkv_kernel, grid=(b, lp // tm),
        in_specs=[rows(128), rows(128), const((128, 512)), const((256, 128)), const((128, 512)), const((256, 256)),
                  const((1, 512))],
        out_specs=[rows(512), pl.BlockSpec((1, 1, N_HEADS, V_ROWS, tm), lambda bb, i: (bb, i, 0, 0, 0))],
        out_shape=[jax.ShapeDtypeStruct((b, lp, 512), BF16),
                   jax.ShapeDtypeStruct((b, lp // tm, N_HEADS, V_ROWS, tm), BF16)],
        compiler_params=_cparams(2), name="mla_kv",
    )(ckv, kr, lw['w_uk'], lw['w_uvt'], lw['e_kr'], lw['segq'], lw['g_kb'])


def _head_lane_id(width=256):
    return lax.broadcasted_iota(I32, (1, width), 1) // HEAD_DIM


def _keep_lanes(x, pred):
    return jnp.where(pred, x.astype(F32), 0.0).astype(BF16)


def _flash_run(nfull, nkb, nkb_max, *, qm, k_block, v_block, col_sub, row_add, mask_fn, s_refs, acc_ref, tq):
    s_a, s_b = s_refs

    def produce(j, s_out):
        jc = jnp.minimum(j, nkb_max - 1)
        for h in range(N_HEADS):
            s_out[h] = _dot_nt(k_block(jc, h), qm[h])

    def half(j, c, s_in, s_out, masked):
        ms, ls = c
        produce(j + 1, s_out)
        valid = mask_fn(j) if masked else None
        new_m, new_l = [], []
        for h in range(N_HEADS):
            t = s_in[h]
            if col_sub is not None:
                t = t - col_sub(j, h)
            if masked:
                t = jnp.where(valid, t, NEG)
            m_cur = jnp.max(t, axis=0, keepdims=True)
            if row_add is not None:
                m_cur = m_cur + row_add[h]
            m_new = jnp.maximum(ms[h], m_cur)
            alpha = jnp.exp2(ms[h] - m_new)
            off = m_new if row_add is None else m_new - row_add[h]
            p = jnp.exp2(t - off).astype(BF16)
            pv = _dot(v_block(j, h), p)
            r0 = h * HEAD_DIM
            acc_ref[r0:r0 + HEAD_DIM, :] = acc_ref[r0:r0 + HEAD_DIM, :] * alpha + pv[:HEAD_DIM]
            new_l.append(ls[h] * alpha + pv[HEAD_DIM:HEAD_DIM + 1])
            new_m.append(m_new)
        return tuple(new_m), tuple(new_l)

    def step(j, c, s_in, s_out, may_end):
        def run(c):
            if nfull is None:
                return half(j, c, s_in, s_out, True)
            return lax.cond(j >= nfull, lambda cc: half(j, cc, s_in, s_out, True),
                            lambda cc: half(j, cc, s_in, s_out, False), c)
        if not may_end:
            return run(c)
        return lax.cond(j >= nkb, lambda cc: cc, run, c)

    main_masked = nfull is None
    n_main = (nkb if main_masked else nfull) // 2

    def main_body(i, c):
        c = half(2 * i, c, s_a, s_b, main_masked)
        return half(2 * i + 1, c, s_b, s_a, main_masked)

    def tail_body(i, c):
        c = step(2 * i, c, s_a, s_b, False)
        return step(2 * i + 1, c, s_b, s_a, True)

    acc_ref[...] = jnp.zeros(acc_ref.shape, F32)
    init = (tuple(jnp.full((1, tq), NEG, F32) for _ in range(N_HEADS)),
            tuple(jnp.zeros((1, tq), F32) for _ in range(N_HEADS)))
    produce(0, s_a)
    c = lax.fori_loop(0, n_main, main_body, init)
    _, ls = lax.fori_loop(n_main, (nkb + 1) // 2, tail_body, c)
    return ls


def _flash_scratch(tq, tk):
    return [pltpu.VMEM((N_HEADS, tk, tq), F32), pltpu.VMEM((N_HEADS, tk, tq), F32), pltpu.VMEM((256, tq), F32)]


def _flash_finish(o_ref, g_ref, acc_ref, ls):
    for h in range(N_HEADS):
        r0 = h * HEAD_DIM
        acc_ref[r0:r0 + HEAD_DIM, :] = acc_ref[r0:r0 + HEAD_DIM, :] / ls[h]
    y = acc_ref[...].T
    o_ref[0] = (y * g_ref[0].astype(F32)).astype(BF16)


A_WIN = A_BAND + 2 * CHUNK


def _attn_a_kernel(q_ref, k_ref, v_ref, bias_ref, g_ref, o_ref, *, lo_valid, hi_valid, group):
    hid = _head_lane_id()
    for gi in range(group):
        c = pl.program_id(1) * group + gi
        start = pl.multiple_of(c * CHUNK, CHUNK)
        q = q_ref[0, gi * CHUNK:(gi + 1) * CHUNK, :]
        qs = jnp.concatenate([_keep_lanes(q, hid == h) for h in range(N_HEADS)], axis=0)
        kb = k_ref[0, pl.ds(start, A_WIN), :]
        vb = v_ref[0, pl.ds(start, A_WIN), :]
        s = _dot_nt(qs, kb) + bias_ref[...]
        row = start + lax.broadcasted_iota(I32, (1, A_WIN), 1)
        valid = (row >= lo_valid) & (row < hi_valid)
        s = jnp.where(valid, s, NEG)
        m = jnp.max(s, axis=-1, keepdims=True)
        p = jnp.where(valid, jnp.exp(s - m), 0.0)
        l = jnp.sum(p, axis=-1, keepdims=True)
        o = _dot(p.astype(BF16), vb) / l
        y = o[(N_HEADS - 1) * CHUNK:]
        for h in range(N_HEADS - 2, -1, -1):
            y = jnp.where(hid == h, o[h * CHUNK:(h + 1) * CHUNK], y)
        gate = g_ref[0, gi * CHUNK:(gi + 1) * CHUNK, :].astype(F32)
        o_ref[0, gi * CHUNK:(gi + 1) * CHUNK, :] = (y * gate).astype(BF16)


def _attn_a_call(q, kfull, vfull, bias, gates, lo_valid, hi_valid, group):
    b, tp, _ = q.shape
    rows_kv = kfull.shape[1]
    kern = functools.partial(_attn_a_kernel, lo_valid=lo_valid, hi_valid=hi_valid, group=group)
    rows = CHUNK * group
    return pl.pallas_call(
        kern, grid=(b, tp // rows),
        in_specs=[pl.BlockSpec((1, rows, 256), lambda bb, c: (bb, c, 0)),
                  pl.BlockSpec((1, rows_kv, 256), lambda bb, c: (bb, 0, 0)),
                  pl.BlockSpec((1, rows_kv, 256), lambda bb, c: (bb, 0, 0)),
                  pl.BlockSpec((N_HEADS * CHUNK, A_WIN), lambda bb, c: (0, 0)),
                  pl.BlockSpec((1, rows, 256), lambda bb, c: (bb, c, 0))],
        out_specs=pl.BlockSpec((1, rows, 256), lambda bb, c: (bb, c, 0)),
        out_shape=jax.ShapeDtypeStruct((b, tp, 256), BF16),
        compiler_params=_cparams(2), name="attn_band",
    )(q, kfull, vfull, bias, gates)


def _attn_b_kernel(q_ref, k_ref, v_ref, g_ref, o_ref, sa_ref, sb_ref, acc_ref, *, p_len, l_len, tq, tk, nq):
    i = pl.program_id(1) if nq > 1 else 0
    q0 = p_len + i * tq
    lane = lax.broadcasted_iota(I32, (1, 256), 1)
    qm = []
    for h in range(N_HEADS):
        qg = q_ref[0, :, (h // 2) * 256:(h // 2) * 256 + 256]
        lo = (h % 2) * (MLA_NOPE + MLA_ROPE)
        qm.append(_keep_lanes(qg, (lane >= lo) & (lane < lo + MLA_NOPE + MLA_ROPE)))
    qchunk = (q0 + lax.broadcasted_iota(I32, (1, tq), 1)) // CHUNK
    nkb = jnp.minimum((q0 + tq + tk - 1) // tk, k_ref.shape[1] // tk)
    nfull = jnp.minimum(((q0 // CHUNK + 1) * CHUNK) // tk, nkb)

    def k_block(j, h):
        return k_ref[0, pl.ds(pl.multiple_of(j * tk, tk), tk), (h // 2) * 256:(h // 2) * 256 + 256]

    def mask_fn(j):
        kpos = j * tk + lax.broadcasted_iota(I32, (tk, 1), 0)
        return ((kpos // CHUNK) <= qchunk) & (kpos < l_len)

    ls = _flash_run(nfull, nkb, k_ref.shape[1] // tk, qm=qm, k_block=k_block, v_block=lambda j, h: v_ref[0, j, h],
                    col_sub=None, row_add=None, mask_fn=mask_fn, s_refs=(sa_ref, sb_ref), acc_ref=acc_ref, tq=tq)
    _flash_finish(o_ref, g_ref, acc_ref, ls)


def _attn_b_call(qb, kb, vaug, gates, p_len, l_len, tq, tk):
    b, t, _ = qb.shape
    lp = kb.shape[1]
    nq = t // tq
    kern = functools.partial(_attn_b_kernel, p_len=p_len, l_len=l_len, tq=tq, tk=tk, nq=nq)
    return pl.pallas_call(
        kern, grid=(b, nq),
        in_specs=[pl.BlockSpec((1, tq, 512), lambda bb, i: (bb, i, 0)),
                  pl.BlockSpec((1, lp, 512), lambda bb, i: (bb, 0, 0)),
                  pl.BlockSpec((1,) + vaug.shape[1:], lambda bb, i: (bb, 0, 0, 0, 0)),
                  pl.BlockSpec((1, tq, 256), lambda bb, i: (bb, i, 1))],
        out_specs=pl.BlockSpec((1, tq, 256), lambda bb, i: (bb, i, 0)),
        out_shape=jax.ShapeDtypeStruct((b, t, 256), BF16),
        scratch_shapes=_flash_scratch(tq, tk),
        compiler_params=_cparams(2), name="attn_latent",
    )(qb, kb, vaug, gates)


def _attn_c_kernel(q_ref, iq_ref, iw_ref, k_ref, v_ref, ik_ref, g_ref, o_ref, key_ref, top_ref, sa_ref, sb_ref,
                   acc_ref,
                   *, p_len, l_len, k_sel, tq, tk, nq):
    i = pl.program_id(1) if nq > 1 else 0
    q0 = p_len + i * tq
    nkb_max = k_ref.shape[1] // tk
    nkb = jnp.minimum((q0 + tq + tk - 1) // tk, nkb_max)
    qchunk = (q0 + lax.broadcasted_iota(I32, (1, tq), 1)) // CHUNK
    hid = _head_lane_id()

    iq = iq_ref[0]
    ihid = lax.broadcasted_iota(I32, (1, 256), 1) // IDX_DIM
    iqm = [_keep_lanes(iq, ihid == h) for h in range(IDX_HEADS)]
    iw = iw_ref[0] * (IDX_DIM ** -0.5 * IDX_HEADS ** -0.5)

    def score_body(edge):
        def body(j, carry):
            ks = pl.multiple_of(j * tk, tk)
            ikb = ik_ref[0, pl.ds(ks, tk), :]
            score = jnp.zeros((tk, tq), F32)
            for h in range(IDX_HEADS):
                score = score + iw[h:h + 1, :] * jnp.maximum(_dot_nt(ikb, iqm[h]), 0.0)
            if edge:
                kpos = ks + lax.broadcasted_iota(I32, (tk, 1), 0)
                score = jnp.where(((kpos // CHUNK) <= qchunk) & (kpos < l_len), score, -jnp.inf)
            key_ref[j] = score
            top_ref[j] = pltpu.bitcast(pltpu.bitcast(score, I32) & -65536, F32).astype(BF16)
            return carry
        return body

    n_free = jnp.minimum(((q0 // CHUNK + 1) * CHUNK) // tk, nkb)
    lax.fori_loop(0, n_free, score_body(False), 0)
    lax.fori_loop(n_free, nkb, score_body(True), 0)

    def cand_value(cand):
        bits = jnp.where(cand < 0, cand ^ 0x7FFFFFFF, cand)
        bits = jnp.where((bits > 0) & (bits < 0x00800000), 0x00800000, bits)
        return pltpu.bitcast(bits, F32)

    def count(pred_fn):
        def cbody(j, acc):
            pf = jnp.where(pred_fn(key_ref[j], j), 1.0, 0.0)
            for r in range(tk // 8):
                acc = acc + pf[r * 8:(r + 1) * 8]
            return acc
        acc = lax.fori_loop(0, nkb, cbody, jnp.zeros((8, tq), F32))
        return jnp.sum(acc, axis=0, keepdims=True)

    def count_top(cand_b):
        one, zero = jnp.ones((16, tq), BF16), jnp.zeros((16, tq), BF16)

        def cbody(j, acc):
            part = zero
            for r in range(tk // 16):
                part = part + jnp.where(top_ref[j, r * 16:(r + 1) * 16, :] >= cand_b, one, zero)
            return acc + part.astype(F32)
        acc = lax.fori_loop(0, nkb, cbody, jnp.zeros((16, tq), F32))
        return jnp.sum(acc, axis=0, keepdims=True)

    def top_body(it, c):
        t, n_ge = c
        cand = t + jnp.left_shift(jnp.int32(1), 15 - it)
        cbits = pltpu.bitcast(cand_value(cand * 65536), I32) & -65536
        cnt = count_top(jnp.broadcast_to(pltpu.bitcast(cbits, F32), (16, tq)).astype(BF16))
        ok = cnt >= k_sel
        return jnp.where(ok, cand, t), jnp.where(ok, cnt, n_ge)

    t16, n_ge = lax.fori_loop(0, 16, top_body, (jnp.full((1, tq), -32768, I32), jnp.zeros((1, tq), F32)))

    def bis_body(it, c):
        t, n_ge = c
        cand = t + jnp.left_shift(jnp.int32(1), 15 - it)
        cval = cand_value(cand)
        cnt = count(lambda kb, j: kb >= cval)
        ok = cnt >= k_sel
        return jnp.where(ok, cand, t), jnp.where(ok, cnt, n_ge)

    tcode, n_ge = lax.fori_loop(0, 16, bis_body, (t16 * 65536, n_ge))
    thr = cand_value(jnp.maximum(tcode, MOST_NEGATIVE_CODE))

    def idx_of(j):
        return j * tk + lax.broadcasted_iota(I32, (tk, tq), 0)

    def tie_cut(_):
        need = k_sel - count(lambda kb, j: kb > thr)

        def tie_body(it, jc):
            cand = jc + jnp.left_shift(jnp.int32(1), 12 - it)
            cnt = count(lambda kb, j: (kb == thr) & (idx_of(j) < cand))
            return jnp.where(cnt < need, cand, jc)

        return lax.fori_loop(0, 13, tie_body, jnp.zeros((1, tq), I32))

    has_tie = jnp.max(jnp.where(n_ge > k_sel, 1.0, 0.0)) > 0.0
    jcut = lax.cond(has_tie, tie_cut, lambda _: jnp.full((1, tq), 2 ** 30, I32), 0)

    q = q_ref[0]
    qm = [_keep_lanes(q, hid == h) for h in range(N_HEADS)]

    def mask_fn(j):
        kb = key_ref[j]
        return (kb > thr) | ((kb == thr) & (idx_of(j) <= jcut))

    ls = _flash_run(None, nkb, nkb_max, qm=qm,
                    k_block=lambda j, h: k_ref[0, pl.ds(pl.multiple_of(j * tk, tk), tk), :],
                    v_block=lambda j, h: v_ref[0, j, 0], col_sub=None, row_add=None, mask_fn=mask_fn,
                    s_refs=(sa_ref, sb_ref), acc_ref=acc_ref, tq=tq)
    _flash_finish(o_ref, g_ref, acc_ref, ls)


def _attn_c_call(cq, ciq, iwt, krep, vaug, ikrep, gates, p_len, l_len, k_sel, tq, tk):
    b, t, _ = cq.shape
    lp = krep.shape[1]
    assert lp <= 8192
    nq = t // tq
    kern = functools.partial(_attn_c_kernel, p_len=p_len, l_len=l_len, k_sel=float(k_sel), tq=tq, tk=tk, nq=nq)
    return pl.pallas_call(
        kern, grid=(b, nq),
        in_specs=[pl.BlockSpec((1, tq, 256), lambda bb, i: (bb, i, 0)),
                  pl.BlockSpec((1, tq, 256), lambda bb, i: (bb, i, 0)),
                  pl.BlockSpec((1, 8, tq), lambda bb, i: (bb, 0, i)),
                  pl.BlockSpec((1, lp, 256), lambda bb, i: (bb, 0, 0)),
                  pl.BlockSpec((1,) + vaug.shape[1:], lambda bb, i: (bb, 0, 0, 0, 0)),
                  pl.BlockSpec((1, lp, 256), lambda bb, i: (bb, 0, 0)),
                  pl.BlockSpec((1, tq, 256), lambda bb, i: (bb, i, 2))],
        out_specs=pl.BlockSpec((1, tq, 256), lambda bb, i: (bb, i, 0)),
        out_shape=jax.ShapeDtypeStruct((b, t, 256), BF16),
        scratch_shapes=[pltpu.VMEM((lp // tk, tk, tq), F32), pltpu.VMEM((lp // tk, tk, tq), BF16)]
        + _flash_scratch(tq, tk),
        compiler_params=_cparams(2), name="attn_sparse",
    )(cq, ciq, iwt, krep, vaug, ikrep, gates)


def _attn_d_kernel(q_ref, k_ref, v_ref, lf_ref, g_ref, o_ref, fc_ref, fk_ref, sa_ref, sb_ref, acc_ref,
                   *, p_len, tq, tk, nq):
    i = pl.program_id(1) if nq > 1 else 0
    lp = k_ref.shape[1]
    nch = lp // 128

    def cumulate():
        x = lf_ref[0]
        lane = lax.broadcasted_iota(I32, (1, lp), 1)
        step = 1
        while step < lp:
            x = x + jnp.where(lane >= step, pltpu.roll(x, step, 1), 0.0)
            step *= 2
        x = x * LOG2E
        for c in range(nch):
            fc_ref[c] = x[:, c * 128:(c + 1) * 128]

        eye = lax.broadcasted_iota(I32, (128, 128), 0) == lax.broadcasted_iota(I32, (128, 128), 1)

        def spread(c, carry):
            rows = fc_ref[c]
            for h in range(N_HEADS):
                col = jnp.sum(jnp.where(eye, rows[h:h + 1, :], 0.0), axis=1, keepdims=True)
                fk_ref[h, pl.ds(pl.multiple_of(c * 128, 128), 128), :] = jnp.broadcast_to(col, (128, 128))
            return carry

        lax.fori_loop(0, nch, spread, 0)

    if nq > 1:
        pl.when(i == 0)(cumulate)
    else:
        cumulate()

    q0 = p_len + i * tq
    hid = _head_lane_id()
    q = q_ref[0]
    qm = [_keep_lanes(q, hid == h) for h in range(N_HEADS)]
    c0 = q0 // 128
    fq_rows = [fc_ref[c0 + c] for c in range(tq // 128)]
    fq = [jnp.concatenate([r[h:h + 1, :] for r in fq_rows], axis=1) for h in range(N_HEADS)]
    qpos = q0 + lax.broadcasted_iota(I32, (1, tq), 1)
    nkb = jnp.minimum((q0 + tq + tk - 1) // tk, lp // tk)
    nfull = jnp.minimum((q0 + 1) // tk, nkb)

    def col_sub(j, h):
        fk = fk_ref[h, pl.ds(pl.multiple_of(j * tk, tk), tk), :]
        return jnp.concatenate([fk] * (tq // 128), axis=1)

    def mask_fn(j):
        return (j * tk + lax.broadcasted_iota(I32, (tk, 1), 0)) <= qpos

    ls = _flash_run(nfull, nkb, lp // tk, qm=qm,
                    k_block=lambda j, h: k_ref[0, pl.ds(pl.multiple_of(j * tk, tk), tk), :],
                    v_block=lambda j, h: v_ref[0, j, h], col_sub=col_sub, row_add=fq, mask_fn=mask_fn,
                    s_refs=(sa_ref, sb_ref), acc_ref=acc_ref, tq=tq)
    _flash_finish(o_ref, g_ref, acc_ref, ls)


def _attn_d_call(dq, kd, vaug, lft, gates, p_len, tq, tk):
    b, t, _ = dq.shape
    lp = kd.shape[1]
    nq = t // tq
    assert p_len % 128 == 0 and tq % 128 == 0 and p_len + t <= lp
    kern = functools.partial(_attn_d_kernel, p_len=p_len, tq=tq, tk=tk, nq=nq)
    return pl.pallas_call(
        kern, grid=(b, nq),
        in_specs=[pl.BlockSpec((1, tq, 256), lambda bb, i: (bb, i, 0)),
                  pl.BlockSpec((1, lp, 256), lambda bb, i: (bb, 0, 0)),
                  pl.BlockSpec((1,) + vaug.shape[1:], lambda bb, i: (bb, 0, 0, 0, 0)),
                  pl.BlockSpec((1, 8, lp), lambda bb, i: (bb, 0, 0)),
                  pl.BlockSpec((1, tq, 256), lambda bb, i: (bb, i, 3))],
        out_specs=pl.BlockSpec((1, tq, 256), lambda bb, i: (bb, i, 0)),
        out_shape=jax.ShapeDtypeStruct((b, t, 256), BF16),
        scratch_shapes=[pltpu.VMEM((lp // 128, 8, 128), F32), pltpu.VMEM((N_HEADS, lp, 128), F32)]
        + _flash_scratch(tq, tk),
        compiler_params=_cparams(2), name="attn_forget",
    )(dq, kd, vaug, lft, gates)


def _out_kernel(ya_ref, yb_ref, yc_ref, yd_ref, x_ref, p_ref, wo_ref, gple_ref, wg_ref, wp_ref, o_ref):
    mixed = (_dot(ya_ref[0], wo_ref[0:256, :]) + _dot(yb_ref[0], wo_ref[256:512, :])
             + _dot(yc_ref[0], wo_ref[512:768, :]) + _dot(yd_ref[0], wo_ref[768:1024, :]))
    x1 = x_ref[0] + mixed
    ms = jnp.mean(x1 * x1, axis=-1, keepdims=True)
    xn = (x1 * lax.rsqrt(ms + EPS) * gple_ref[...]).astype(BF16)
    gate = 1.0 / (1.0 + jnp.exp(-_dot(xn, wg_ref[...])))
    o_ref[0] = x1 + gate * _dot(p_ref[0].astype(BF16), wp_ref[...])


def _out_call(ys, x, p, lw, tm):
    b, t, _ = x.shape

    def const(shape):
        return pl.BlockSpec(shape, lambda bb, i: (0,) * len(shape))

    def rows(w):
        return pl.BlockSpec((1, tm, w), lambda bb, i: (bb, i, 0))

    return pl.pallas_call(
        _out_kernel, grid=(b, t // tm),
        in_specs=[rows(256)] * 4 + [rows(D_MODEL), rows(PLE_DIM), const((1024, D_MODEL)), const((1, D_MODEL)),
                                    const((D_MODEL, D_MODEL)), const((PLE_DIM, D_MODEL))],
        out_specs=rows(D_MODEL),
        out_shape=jax.ShapeDtypeStruct((b, t, D_MODEL), F32),
        compiler_params=_cparams(2), name="out_proj",
    )(*ys, x, p, lw['w_out'], lw['g_ple'], lw['w_ple_gate'], lw['w_ple_proj'])


def _seg_matrix(segments):
    m = np.zeros((256, 256), np.float32)
    for lo, n in segments:
        m[lo:lo + n, lo:lo + n] = 1.0 / n
    return jnp.asarray(m, BF16)


_PAIR_SEGS = ((0, 64), (64, 32), (96, 64), (160, 32), (192, 64))


def _pair_cols(head):
    return (head // 2) * 256 + (head % 2) * (MLA_NOPE + MLA_ROPE)


def _prep_layer(i, g_in, w_in, g_qk, g_rope, a_rel_bias, b_g_cq, b_w_uq, b_g_ckv, b_w_ukv, d_b_f, w_out, g_ple,
                w_ple_gate, w_ple_proj):
    w = w_in[i]
    cols = []
    for n in _MY_ORDER:
        if n == 'pad56':
            cols.append(jnp.zeros((D_MODEL, 56), F32))
        else:
            o, wd = _REF_OFF[n]
            cols.append(w[:, o:o + wd])
    lw = {'w_in': jnp.concatenate(cols, axis=1).astype(BF16), 'g_in': g_in[i][None, :]}
    wt = [w[:, _REF_OFF[n][0]:_REF_OFF[n][0] + _REF_OFF[n][1]] for n in ('d_v', 'c_v', 'c_iw', 'd_f')]
    wt.append(jnp.zeros((D_MODEL, XT_ROWS - XT_DF - N_HEADS), F32))
    lw['w_t'] = jnp.concatenate(wt, axis=1).T.astype(BF16)
    lw['b_col'] = jnp.concatenate([d_b_f[i], jnp.zeros((8 - N_HEADS,), F32)])[:, None]
    g = g_qk[i]
    sc = HEAD_DIM ** -0.5
    rows = [jnp.tile(g[0], 4) * sc, jnp.tile(g[1], 4), jnp.tile(g[2], 4) * (sc * LOG2E),
            jnp.tile(g[4], 4) * (sc * LOG2E), jnp.tile(g[5], 4)]
    lw['g4'] = jnp.stack(rows + [jnp.zeros((256,), F32)] * 3)
    lw['hseg'] = _seg_matrix(tuple((h * 64, 64) for h in range(4)))
    lw['segq'] = _seg_matrix(_PAIR_SEGS)
    lw['g_cq'] = b_g_cq[i][None, :]
    lw['g_ckv'] = b_g_ckv[i][None, :]
    qscale = (MLA_NOPE + MLA_ROPE) ** -0.5 * LOG2E
    hw = MLA_NOPE + MLA_ROPE
    wuq, gqb, wuk, gkb, wuv = [], [], [], [], []
    e_kr = np.zeros((128, 512), np.float32)
    for h in range(N_HEADS):
        wuq.append(b_w_uq[i][:, h * hw:(h + 1) * hw])
        gqb += [g[6] * qscale, g_rope[i][0] * qscale]
        srck = h * (MLA_NOPE + HEAD_DIM)
        wuk += [b_w_ukv[i][:, srck:srck + MLA_NOPE], jnp.zeros((MLA_KV_LORA, MLA_ROPE), F32)]
        gkb += [g[7], jnp.zeros((MLA_ROPE,), F32)]
        wuv.append(b_w_ukv[i][:, srck + MLA_NOPE:srck + MLA_NOPE + HEAD_DIM])
        e_kr[np.arange(MLA_ROPE), _pair_cols(h) + MLA_NOPE + np.arange(MLA_ROPE)] = 1.0
        if h % 2 == 1:
            wuq.append(jnp.zeros((MLA_Q_LORA, 256 - 2 * hw), F32))
            wuk.append(jnp.zeros((MLA_KV_LORA, 256 - 2 * hw), F32))
            gqb.append(jnp.zeros((256 - 2 * hw,), F32))
            gkb.append(jnp.zeros((256 - 2 * hw,), F32))
    lw['w_uq'] = jnp.concatenate(wuq, axis=1).astype(BF16)
    lw['g_qb'] = jnp.concatenate(gqb)[None, :]
    lw['w_uk'] = jnp.concatenate(wuk, axis=1).astype(BF16)
    lw['g_kb'] = jnp.concatenate(gkb)[None, :]
    lw['e_kr'] = jnp.asarray(e_kr, BF16)
    lw['w_uvt'] = jnp.concatenate(wuv, axis=1).T.astype(BF16)
    one = jnp.ones((128,), F32)
    lw['srow'] = jnp.stack([jnp.concatenate([g[3], one[HEAD_DIM:]]), jnp.concatenate([g_rope[i][1], one[MLA_ROPE:]])]
                           + [one] * 6)
    ab = a_rel_bias[i]
    ext = jnp.concatenate([jnp.broadcast_to(ab[:, 2 * REL_CLIP:], (N_HEADS, A_WIN - REL_CLIP - 1)),
                           ab[:, REL_CLIP - CHUNK + 1:][:, ::-1]], axis=1)
    lw['a_bias'] = jnp.stack([ext[:, CHUNK - 1 - q:CHUNK - 1 - q + A_WIN] for q in range(CHUNK)],
                             axis=1).reshape(N_HEADS * CHUNK, A_WIN)
    lw['w_out'] = w_out[i].astype(BF16)
    lw['g_ple'] = g_ple[i][None, :]
    lw['w_ple_gate'] = w_ple_gate[i].astype(BF16)
    lw['w_ple_proj'] = w_ple_proj[i].astype(BF16)
    return lw


def _cos_sin(pos, rot_dim):
    half = rot_dim // 2
    inv = jnp.float32(ROPE_THETA) ** (-jnp.arange(half, dtype=F32) / half)
    ang = pos.astype(F32)[:, None] * inv[None, :]
    return [jnp.cos(ang), jnp.sin(ang)]


def _rot_placement():
    src = {MLA_ROPE: 1, ROT_DIM: 1 + MLA_ROPE}
    e = np.zeros((1 + MLA_ROPE + ROT_DIM, TAB_W), np.float32)
    sections = ((TAB_QB, 512, [_pair_cols(h) + MLA_NOPE for h in range(N_HEADS)], MLA_ROPE),
                (TAB_CQ, 256, [h * HEAD_DIM for h in range(N_HEADS)], ROT_DIM),
                (TAB_S1, 128, [0], ROT_DIM), (TAB_S2, 128, [0], MLA_ROPE))
    for base, width, starts, rot_dim in sections:
        half = rot_dim // 2
        e[0, base:base + width] = 1.0
        for st in starts:
            for i in range(half):
                c, s = src[rot_dim] + i, src[rot_dim] + half + i
                e[0, base + st + i] = e[0, base + st + half + i] = 0.0
                e[c, base + st + i] = e[c, base + st + half + i] = 1.0
                e[s, base + width + st + half + i] = 1.0
                e[s, base + 2 * width + st + i] = -1.0
    return jnp.asarray(e)


def _rot_tables(pos):
    src = jnp.concatenate([jnp.ones((pos.shape[0], 1), F32)] + _cos_sin(pos, MLA_ROPE) + _cos_sin(pos, ROT_DIM),
                          axis=1)
    return jnp.dot(src, _rot_placement(), precision=lax.Precision.HIGH)


def _pad_axis(a, size, axis):
    if a.shape[axis] == size:
        return a
    pads = [(0, 0)] * a.ndim
    pads[axis] = (0, size - a.shape[axis])
    return jnp.pad(a, pads)


def _cat_rows(past, new, rows):
    a = new if past is None else jnp.concatenate([past.astype(new.dtype), new], axis=1)
    return _pad_axis(a, rows, 1)


def _cat_lanes(past_t, new_t, lanes):
    a = new_t if past_t is None else jnp.concatenate([past_t.astype(new_t.dtype), new_t], axis=2)
    return _pad_axis(a, lanes, 2)


def _value_blocks(vt, tk):
    b, r, lp = vt.shape
    h = r // HEAD_DIM
    v = vt.astype(BF16).reshape(b, h, HEAD_DIM, lp)
    v = jnp.concatenate([v, jnp.ones((b, h, V_ROWS - HEAD_DIM, lp), BF16)], axis=2)
    return v.reshape(b, h, V_ROWS, lp // tk, tk).transpose(0, 3, 1, 2, 4)


def _layer(x, p, past, lw, tab, *, fold_batch, tm, tq, tq_c, tk, a_group):
    b, t, _ = x.shape
    p_len = 0 if past is None else past[2].shape[1]
    l_len = p_len + t
    tqp = -(-t // tq) * tq
    tqp_c = -(-t // tq_c) * tq_c
    lp = -(-(p_len + max(tqp, tqp_c)) // tk) * tk

    xin = x.reshape(1, b * t, D_MODEL) if fold_batch else x
    outs = _inproj_call(xin, tab, lw, tm)
    if fold_batch:
        xt = outs[18][0].reshape(XT_ROWS, b, t).transpose(1, 0, 2)
        outs = [o.reshape((b, t) + o.shape[2:]) for o in outs[:18]] + [xt, outs[19].reshape(b, t, 1024), None, None]
    (aq, akb, avb, akf, avf, qb, ckvf, cq, ciq, c_k, c_v, s2, b_kr, c_ik, dq, dkb, dkf, dvf, xt, gates,
     dva, cva) = outs
    direct_values = past is None and not fold_batch and tm == tk

    lft = xt[:, XT_DF:XT_ROWS]
    d_lf = lft[:, :N_HEADS, :].transpose(0, 2, 1)
    state = (akf.reshape(b, t, N_HEADS, HEAD_DIM), avf.reshape(b, t, N_HEADS, HEAD_DIM), ckvf, b_kr, c_k, c_v, c_ik,
             dkf.reshape(b, t, N_HEADS, HEAD_DIM), dvf.reshape(b, t, N_HEADS, HEAD_DIM), d_lf)

    if past is None:
        pa_k = pa_v = pb_ckv = pb_kr = pc_k = pc_v = pc_ik = pd_k = pd_v = pd_lf = None
        pa = 0
    else:
        pa_k, pa_v, pb_ckv, pb_kr, pc_k, pc_v, pc_ik, pd_k, pd_v, pd_lf = past
        pa = pa_k.shape[1]
        pa_k = pa_k.reshape(b, pa, 256)
        pa_v = pa_v.reshape(b, pa, 256)
        pd_k = pd_k.reshape(b, p_len, 256)
        pd_v = pd_v.reshape(b, p_len, 256)

    tp = -(-t // (CHUNK * a_group)) * (CHUNK * a_group)
    front = CHUNK + A_BAND - pa
    a_rows = CHUNK + A_BAND + tp

    def band_src(pst, new):
        parts = [jnp.zeros((b, front, 256), BF16)]
        if pst is not None:
            parts.append(pst.astype(BF16))
        parts.append(new)
        return _pad_axis(jnp.concatenate(parts, axis=1), a_rows, 1)

    ya = _attn_a_call(_pad_axis(aq, tp, 1), band_src(pa_k, akb), band_src(pa_v, avb), lw['a_bias'],
                      _pad_axis(gates, tp, 1), front, CHUNK + A_BAND + t, a_group)[:, :t]

    gates_q = _pad_axis(gates, tqp, 1)

    ckv_all = _cat_rows(pb_ckv, ckvf, lp)
    kr_new = s2 if past is None else _pad_axis(b_kr, 128, 2)
    kr_past = None if past is None else _pad_axis(pb_kr, 128, 2)
    kr_all = _cat_rows(kr_past, kr_new, lp)
    kb, vba = _bkv_call(ckv_all, kr_all, lw, tk)
    yb = _attn_b_call(_pad_axis(qb, tqp, 1), kb, vba, gates_q, p_len, l_len, tq, tk)[:, :t]

    krep = jnp.tile(_cat_rows(pc_k, c_k, lp).astype(BF16), (1, 1, N_HEADS))
    ikrep = jnp.tile(_cat_rows(pc_ik, c_ik, lp).astype(BF16), (1, 1, IDX_HEADS))
    if not direct_values:
        cva = _value_blocks(_cat_lanes(None if past is None else pc_v.transpose(0, 2, 1), xt[:, XT_CV:XT_IW], lp), tk)
    iwt = _pad_axis(xt[:, XT_IW:XT_DF], tqp_c, 2)
    k_sel = min(DSA_TOPK, l_len // 4)
    yc = _attn_c_call(_pad_axis(cq, tqp_c, 1), _pad_axis(ciq, tqp_c, 1), iwt, krep, cva, ikrep,
                      _pad_axis(gates, tqp_c, 1), p_len, l_len, k_sel, tq_c, tk)[:, :t]

    kd = _cat_rows(None if past is None else pd_k.astype(BF16), dkb, lp)
    if not direct_values:
        dva = _value_blocks(_cat_lanes(None if past is None else pd_v.transpose(0, 2, 1), xt[:, XT_DV:XT_CV], lp), tk)
    lf_past = None if past is None else _pad_axis(pd_lf.transpose(0, 2, 1), 8, 1)
    lf_all = _cat_lanes(lf_past, lft, lp)
    yd = _attn_d_call(_pad_axis(dq, tqp, 1), kd, dva, lf_all, gates_q, p_len, tq, tk)[:, :t]

    if fold_batch:
        x_new = _out_call([y.reshape(1, b * t, 256) for y in (ya, yb, yc, yd)], x.reshape(1, b * t, D_MODEL),
                          p.reshape(1, b * t, PLE_DIM), lw, tm).reshape(b, t, D_MODEL)
    else:
        x_new = _out_call([ya, yb, yc, yd], x, p, lw, tm)
    return x_new, state


def kernel(x_prompt, x_sample, cache_a_k, cache_a_v, cache_b_ckv, cache_b_krope, cache_c_k, cache_c_v, cache_c_idx_k,
           cache_d_k, cache_d_v, cache_d_logf, p_prompt, p_sample, g_in, w_in, g_qk, g_rope, a_rel_bias, b_g_cq,
           b_w_uq, b_g_ckv, b_w_ukv, d_b_f, w_out, g_ple, w_ple_gate, w_ple_proj):
    depth = w_in.shape[0]
    b, t = x_prompt.shape[:2]
    bs, ts = x_sample.shape[:2]
    past_len = cache_b_ckv.shape[2]
    tab_p = _rot_tables(jnp.arange(t))
    tab_s = jnp.tile(_rot_tables(past_len + jnp.arange(ts)), (bs, 1))
    xp, xs = x_prompt, x_sample
    states_p, states_s = [], []
    for i in range(depth):
        lw = _prep_layer(i, g_in, w_in, g_qk, g_rope, a_rel_bias, b_g_cq, b_w_uq, b_g_ckv, b_w_ukv, d_b_f, w_out,
                         g_ple, w_ple_gate, w_ple_proj)
        xp, st_p = _layer(xp, p_prompt[i], None, lw, tab_p, fold_batch=False, tm=256, tq=256, tq_c=512, tk=256,
                           a_group=4)
        past = (cache_a_k[i], cache_a_v[i], cache_b_ckv[i], cache_b_krope[i], cache_c_k[i], cache_c_v[i],
                cache_c_idx_k[i], cache_d_k[i], cache_d_v[i], cache_d_logf[i])
        xs, st_s = _layer(xs, p_sample[i], past, lw, tab_s, fold_batch=True, tm=bs * ts, tq=128, tq_c=128, tk=256,
                           a_group=1)
        states_p.append(st_p)
        states_s.append(st_s)
    sp = [jnp.stack(z) for z in zip(*states_p)]
    ss = [jnp.stack(z) for z in zip(*states_s)]
    keep = min(A_BAND, t)
    sp[0] = sp[0][:, :, t - keep:]
    sp[1] = sp[1][:, :, t - keep:]
    return (xp, xs, *sp, *ss)
```

```python
import functools

import numpy as np
import jax
import jax.numpy as jnp
from jax import lax
from jax.experimental import pallas as pl
from jax.experimental.pallas import tpu as pltpu

F32 = jnp.float32
BF16 = jnp.bfloat16
I32 = jnp.int32

D_MODEL = 1024
CHUNK = 64
EPS = 1e-6
HEAD_DIM = 64
N_HEADS = 4
GROUP_WIDTH = 256
ROT_DIM = 16
ROPE_THETA = 500000.0
A_BAND = 8 * CHUNK
REL_CLIP = 128
MLA_Q_LORA = 384
MLA_KV_LORA = 128
MLA_NOPE = 64
MLA_ROPE = 32
IDX_HEADS = 8
IDX_DIM = 32
DSA_TOPK = 256
PLE_DIM = 256

_REF_SPLITS = (
    ('a_q', 256), ('a_k', 256), ('a_v', 256), ('a_g', 256),
    ('b_cq', 384), ('b_ckv', 128), ('b_kr', 32), ('b_g', 256),
    ('c_q', 256), ('c_k', 64), ('c_v', 64), ('c_iq', 256), ('c_ik', 32), ('c_iw', 8), ('c_g', 256),
    ('d_q', 256), ('d_k', 256), ('d_v', 256), ('d_f', 4), ('d_g', 256),
)
_REF_OFF = {}
_o = 0
for _n, _w in _REF_SPLITS:
    _REF_OFF[_n] = (_o, _w)
    _o += _w

_MY_ORDER = ('a_q', 'a_k', 'a_v', 'a_g', 'b_cq', 'b_ckv', 'b_g', 'c_q', 'c_iq', 'c_g', 'd_q', 'd_k', 'd_v', 'd_g',
             'c_k', 'c_v', 'b_kr', 'c_ik', 'c_iw', 'pad56')
_MY_OFF = {}
_o = 0
for _n in _MY_ORDER:
    _w = 56 if _n == 'pad56' else _REF_OFF[_n][1]
    _MY_OFF[_n] = _o
    _o += _w
W_IN_COLS = _o
_PROJ_GROUPS = ((0, _MY_OFF['b_cq']), (_MY_OFF['b_cq'], _MY_OFF['c_q']), (_MY_OFF['c_q'], _MY_OFF['d_q']),
                (_MY_OFF['d_q'], _MY_OFF['c_k']), (_MY_OFF['c_k'], W_IN_COLS))

XT_DV, XT_CV, XT_IW, XT_DF, XT_ROWS = 0, 256, 320, 328, 336

TAB_QB, TAB_CQ, TAB_S1, TAB_S2 = 0, 1536, 2304, 2688
TAB_W = 3072

V_ROWS = HEAD_DIM + 16
NEG = -1e30
LOG2E = 1.4426950408889634
MOST_NEGATIVE_CODE = -2139095040
VMEM_LIMIT = 56 * 1024 * 1024


def _cparams(n_axes):
    return pltpu.CompilerParams(dimension_semantics=("arbitrary",) * n_axes, vmem_limit_bytes=VMEM_LIMIT)


def _dot(a, b):
    return jnp.dot(a, b, preferred_element_type=F32)


def _dot_nt(a, b):
    return lax.dot_general(a, b, (((1,), (1,)), ((), ())), preferred_element_type=F32)


def _seg_mean_sq(t, seg):
    sq = t * t
    hi = sq.astype(BF16)
    lo = (sq - hi.astype(F32)).astype(BF16)
    return _dot(hi, seg) + _dot(lo, seg)


def _rotate(t, cos, sin_up, sin_dn, half):
    w = t.shape[-1]
    return t * cos + pltpu.roll(t, half, 1) * sin_up + pltpu.roll(t, w - half, 1) * sin_dn


def _silu(g):
    return g * (1.0 / (1.0 + jnp.exp(-g)))


def _log_sigmoid(v):
    return jnp.minimum(v, 0.0) - jnp.log1p(jnp.exp(-jnp.abs(v)))


def _inproj_kernel(x_ref, tab_ref, gin_ref, w_ref, wt_ref, bcol_ref, g4_ref, hseg_ref, gcq_ref, wuq_ref, segq_ref,
                   gqb_ref, gckv_ref, srow_ref, akz_ref, avz_ref,
                   aq_o, akb_o, avb_o, akf_o, avf_o, qb_o, ckvf_o, cq_o, ciq_o, ck_o, cv_o, s2_o, bkr_o, cik_o,
                   dq_o, dkb_o, dkf_o, dvf_o, xt_o, gate_o, dva_o, cva_o, krep_o, ikrep_o):
    del akz_ref, avz_ref
    x = x_ref[0]
    ms = jnp.mean(x * x, axis=-1, keepdims=True)
    xn = (x * lax.rsqrt(ms + EPS) * gin_ref[...]).astype(BF16)

    group_out = {}

    def proj(name, n):
        c0 = _MY_OFF[name]
        g0, g1 = next((a, b) for a, b in _PROJ_GROUPS if a <= c0 < b)
        if g0 not in group_out:
            group_out[g0] = _dot(xn, w_ref[:, g0:g1])
        return group_out[g0][:, c0 - g0:c0 - g0 + n]

    hseg = hseg_ref[...]

    def headnorm(t, row):
        return t * lax.rsqrt(_seg_mean_sq(t, hseg) + EPS) * g4_ref[row:row + 1, :]

    def fullnorm(t, g):
        return t * lax.rsqrt(jnp.mean(t * t, axis=-1, keepdims=True) + EPS) * g

    aq_o[0] = headnorm(proj('a_q', 256), 0).astype(BF16)
    ak = headnorm(proj('a_k', 256), 1)
    akf_o[0] = ak
    akb_o[0] = ak.astype(BF16)
    av = proj('a_v', 256)
    avf_o[0] = av
    avb_o[0] = av.astype(BF16)
    gate_o[0, :, 0:256] = _silu(proj('a_g', 256)).astype(BF16)

    cqn = fullnorm(proj('b_cq', MLA_Q_LORA), gcq_ref[...]).astype(BF16)
    qb = _dot(cqn, wuq_ref[...])
    segq = segq_ref[...]
    ms_q = jnp.concatenate([_seg_mean_sq(qb[:, :256], segq), _seg_mean_sq(qb[:, 256:], segq)], axis=1)
    qbn = qb * lax.rsqrt(ms_q + EPS) * gqb_ref[...]
    qbn = _rotate(qbn, tab_ref[:, TAB_QB:TAB_QB + 512], tab_ref[:, TAB_QB + 512:TAB_QB + 1024],
                  tab_ref[:, TAB_QB + 1024:TAB_QB + 1536], MLA_ROPE // 2)
    qb_o[0] = qbn.astype(BF16)
    ckvf_o[0] = fullnorm(proj('b_ckv', MLA_KV_LORA), gckv_ref[...])
    gate_o[0, :, 256:512] = _silu(proj('b_g', 256)).astype(BF16)

    cq = headnorm(proj('c_q', 256), 2)
    cq = _rotate(cq, tab_ref[:, TAB_CQ:TAB_CQ + 256], tab_ref[:, TAB_CQ + 256:TAB_CQ + 512],
                 tab_ref[:, TAB_CQ + 512:TAB_CQ + 768], ROT_DIM // 2)
    cq_o[0] = cq.astype(BF16)
    ciq_o[0] = proj('c_iq', 256).astype(BF16)
    gate_o[0, :, 512:768] = _silu(proj('c_g', 256)).astype(BF16)

    dq_o[0] = headnorm(proj('d_q', 256), 3).astype(BF16)
    dk = headnorm(proj('d_k', 256), 4)
    dkf_o[0] = dk
    dkb_o[0] = dk.astype(BF16)
    dvf_o[0] = proj('d_v', 256)
    gate_o[0, :, 768:1024] = _silu(proj('d_g', 256)).astype(BF16)

    lane = lax.broadcasted_iota(I32, (1, 128), 1)
    t = proj('c_k', 128)
    m64 = lane < HEAD_DIM
    ms1 = jnp.sum(jnp.where(m64, t * t, 0.0), axis=-1, keepdims=True) * (1.0 / HEAD_DIM)
    t = jnp.where(m64, t * lax.rsqrt(ms1 + EPS), t) * srow_ref[0:1, :]
    t = _rotate(t, tab_ref[:, TAB_S1:TAB_S1 + 128], tab_ref[:, TAB_S1 + 128:TAB_S1 + 256],
                tab_ref[:, TAB_S1 + 256:TAB_S1 + 384], ROT_DIM // 2)
    ck_o[0] = t[:, :HEAD_DIM]
    krep_o[0] = jnp.concatenate([t[:, :HEAD_DIM]] * N_HEADS, axis=1).astype(BF16)
    cv_o[0] = t[:, HEAD_DIM:]
    t = proj('b_kr', 128)
    m32 = lane < MLA_ROPE
    ms2 = jnp.sum(jnp.where(m32, t * t, 0.0), axis=-1, keepdims=True) * (1.0 / MLA_ROPE)
    t = jnp.where(m32, t * lax.rsqrt(ms2 + EPS), t) * srow_ref[1:2, :]
    t = _rotate(t, tab_ref[:, TAB_S2:TAB_S2 + 128], tab_ref[:, TAB_S2 + 128:TAB_S2 + 256],
                tab_ref[:, TAB_S2 + 256:TAB_S2 + 384], MLA_ROPE // 2)
    s2_o[0] = t
    bkr_o[0] = t[:, :MLA_ROPE]
    cik_o[0] = t[:, MLA_ROPE:MLA_ROPE + IDX_DIM]
    ikrep_o[0] = jnp.concatenate([t[:, MLA_ROPE:MLA_ROPE + IDX_DIM]] * IDX_HEADS, axis=1).astype(BF16)

    xt = _dot_nt(wt_ref[...], xn)
    xt_o[0, 0:XT_DF, :] = xt[0:XT_DF]
    xt_o[0, XT_DF:XT_ROWS, :] = _log_sigmoid(xt[XT_DF:XT_ROWS] + bcol_ref[...])
    _store_value_block(dva_o, xt[XT_DV:XT_CV].astype(BF16), N_HEADS)
    _store_value_block(cva_o, xt[XT_CV:XT_IW].astype(BF16), 1)


def _inproj_call(x, tab, lw, tm, a_front):
    bk, tk_, _ = x.shape
    front_blocks = a_front // tm
    grid = (tk_ // tm, bk)

    def const(shape):
        return pl.BlockSpec(shape, lambda i, b: (0,) * len(shape))

    def rows(w):
        return pl.BlockSpec((1, tm, w), lambda i, b: (b, i, 0))

    in_specs = [
        rows(D_MODEL),
        pl.BlockSpec((tm, TAB_W), lambda i, b: (i, 0)),
        const((1, D_MODEL)), const((D_MODEL, W_IN_COLS)), const((XT_ROWS, D_MODEL)), const((8, 1)),
        const((8, 256)), const((256, 256)), const((1, MLA_Q_LORA)), const((MLA_Q_LORA, 512)), const((256, 256)),
        const((1, 512)), const((1, MLA_KV_LORA)), const((8, 128)),
        pl.BlockSpec(memory_space=pl.ANY), pl.BlockSpec(memory_space=pl.ANY),
    ]
    band_zero = jnp.zeros((bk, a_front + tk_, 256), BF16)
    widths = [(256, BF16), (256, BF16), (256, BF16), (256, F32), (256, F32), (512, BF16), (128, F32), (256, BF16),
              (256, BF16), (64, F32), (64, F32), (128, F32), (32, F32), (32, F32), (256, BF16), (256, BF16), (256, F32),
              (256, F32)]
    out_shape = [jax.ShapeDtypeStruct((bk, tk_, w), dt) for w, dt in widths]
    out_specs = [rows(w) for w, _ in widths]
    for o in (1, 2):
        out_shape[o] = jax.ShapeDtypeStruct(band_zero.shape, BF16)
        out_specs[o] = pl.BlockSpec((1, tm, 256), lambda i, b: (b, i + front_blocks, 0))
    out_shape.append(jax.ShapeDtypeStruct((bk, XT_ROWS, tk_), F32))
    out_specs.append(pl.BlockSpec((1, XT_ROWS, tm), lambda i, b: (b, 0, i)))
    out_shape.append(jax.ShapeDtypeStruct((bk, tk_, 1024), BF16))
    out_specs.append(rows(1024))
    for heads in (N_HEADS, 1):
        out_shape.append(jax.ShapeDtypeStruct((bk, tk_ // tm, heads, V_ROWS, tm), BF16))
        out_specs.append(pl.BlockSpec((1, 1, heads, V_ROWS, tm), lambda i, b: (b, i, 0, 0, 0)))
    for _ in range(2):
        out_shape.append(jax.ShapeDtypeStruct((bk, tk_, 256), BF16))
        out_specs.append(rows(256))
    return pl.pallas_call(
        _inproj_kernel, grid=grid, in_specs=in_specs, out_specs=out_specs, out_shape=out_shape,
        input_output_aliases={14: 1, 15: 2}, compiler_params=_cparams(2), name="inproj",
    )(x, tab, lw['g_in'], lw['w_in'], lw['w_t'], lw['b_col'], lw['g4'], lw['hseg'], lw['g_cq'], lw['w_uq'],
      lw['segq'], lw['g_qb'], lw['g_ckv'], lw['srow'], band_zero, band_zero)


def _store_value_block(va_o, vt, heads):
    tk = vt.shape[1]
    for h in range(heads):
        va_o[0, 0, h, 0:HEAD_DIM, :] = vt[h * HEAD_DIM:(h + 1) * HEAD_DIM]
        va_o[0, 0, h, HEAD_DIM:V_ROWS, :] = jnp.ones((V_ROWS - HEAD_DIM, tk), BF16)


def _bkv_kernel(ckv_ref, kr_ref, wk_ref, wvt_ref, e_ref, segq_ref, gk_ref, kb_o, va_o):
    c = ckv_ref[0].astype(BF16)
    kn = _dot(c, wk_ref[...])
    segq = segq_ref[...]
    ms = jnp.concatenate([_seg_mean_sq(kn[:, :256], segq), _seg_mean_sq(kn[:, 256:], segq)], axis=1)
    kn = kn * lax.rsqrt(ms + EPS) * gk_ref[...]
    kr = _dot(kr_ref[0].astype(BF16), e_ref[...])
    kb_o[0] = (kn + kr).astype(BF16)
    vt = _dot_nt(wvt_ref[...], c).astype(BF16)
    _store_value_block(va_o, vt, N_HEADS)


def _bkv_call(ckv, kr, lw, tm):
    b, lp, _ = ckv.shape

    def const(shape):
        return pl.BlockSpec(shape, lambda bb, i: (0,) * len(shape))

    def rows(w):
        return pl.BlockSpec((1, tm, w), lambda bb, i: (bb, i, 0))

    return pl.pallas_call(
        _bkv_kernel, grid=(b, lp // tm),
        in_specs=[rows(128), rows(128), const((128, 512)), const((256, 128)), const((128, 512)), const((256, 256)),
                  const((1, 512))],
        out_specs=[rows(512), pl.BlockSpec((1, 1, N_HEADS, V_ROWS, tm), lambda bb, i: (bb, i, 0, 0, 0))],
        out_shape=[jax.ShapeDtypeStruct((b, lp, 512), BF16),
                   jax.ShapeDtypeStruct((b, lp // tm, N_HEADS, V_ROWS, tm), BF16)],
        compiler_params=_cparams(2), name="mla_kv",
    )(ckv, kr, lw['w_uk'], lw['w_uvt'], lw['e_kr'], lw['segq'], lw['g_kb'])


def _head_lane_id(width=256):
    return lax.broadcasted_iota(I32, (1, width), 1) // HEAD_DIM


def _keep_lanes(x, pred):
    return jnp.where(pred, x.astype(F32), 0.0).astype(BF16)


def _flash_run(nfull, nkb, nkb_max, *, qm, k_block, v_block, col_sub, row_add, mask_fn, s_refs, acc_ref, tq):
    s_a, s_b = s_refs

    track_max = nfull is not None

    def produce(j, s_out):
        jc = jnp.minimum(j, nkb_max - 1)
        tops = []
        for h in range(N_HEADS):
            s = _dot_nt(k_block(jc, h), qm[h])
            if col_sub is not None:
                s = s - col_sub(jc, h)
            s_out[h] = s
            tops.append(jnp.max(s, axis=0, keepdims=True) if track_max else jnp.zeros((1, tq), F32))
        return tuple(tops)

    def half(j, c, s_in, s_out, masked):
        ms, ls, tops = c
        next_tops = produce(j + 1, s_out)
        valid = mask_fn(j) if masked else None
        new_m, new_l = [], []
        for h in range(N_HEADS):
            t = s_in[h]
            if masked:
                t = jnp.where(valid, t, NEG)
                m_cur = jnp.max(t, axis=0, keepdims=True)
            else:
                m_cur = tops[h]
            if row_add is not None:
                m_cur = m_cur + row_add[h]
            m_new = jnp.maximum(ms[h], m_cur)
            alpha = jnp.exp2(ms[h] - m_new)
            off = m_new if row_add is None else m_new - row_add[h]
            p = jnp.exp2(t - off).astype(BF16)
            pv = _dot(v_block(j, h), p)
            r0 = h * HEAD_DIM
            acc_ref[r0:r0 + HEAD_DIM, :] = acc_ref[r0:r0 + HEAD_DIM, :] * alpha + pv[:HEAD_DIM]
            new_l.append(ls[h] * alpha + pv[HEAD_DIM:HEAD_DIM + 1])
            new_m.append(m_new)
        return tuple(new_m), tuple(new_l), next_tops

    def step(j, c, s_in, s_out, may_end):
        def run(c):
            if nfull is None:
                return half(j, c, s_in, s_out, True)
            return lax.cond(j >= nfull, lambda cc: half(j, cc, s_in, s_out, True),
                            lambda cc: half(j, cc, s_in, s_out, False), c)
        if not may_end:
            return run(c)
        return lax.cond(j >= nkb, lambda cc: cc, run, c)

    main_masked = nfull is None
    n_main = (nkb if main_masked else nfull) // 2

    def main_body(i, c):
        c = half(2 * i, c, s_a, s_b, main_masked)
        return half(2 * i + 1, c, s_b, s_a, main_masked)

    def tail_body(i, c):
        c = step(2 * i, c, s_a, s_b, False)
        return step(2 * i + 1, c, s_b, s_a, True)

    acc_ref[...] = jnp.zeros(acc_ref.shape, F32)
    init = (tuple(jnp.full((1, tq), NEG, F32) for _ in range(N_HEADS)),
            tuple(jnp.zeros((1, tq), F32) for _ in range(N_HEADS)), produce(0, s_a))
    c = lax.fori_loop(0, n_main, main_body, init)
    _, ls, _ = lax.fori_loop(n_main, (nkb + 1) // 2, tail_body, c)
    return ls


def _flash_scratch(tq, tk):
    return [pltpu.VMEM((N_HEADS, tk, tq), F32), pltpu.VMEM((N_HEADS, tk, tq), F32), pltpu.VMEM((256, tq), F32)]


def _flash_finish(o_ref, g_ref, acc_ref, ls):
    for h in range(N_HEADS):
        r0 = h * HEAD_DIM
        acc_ref[r0:r0 + HEAD_DIM, :] = acc_ref[r0:r0 + HEAD_DIM, :] / ls[h]
    y = acc_ref[...].T
    o_ref[0] = (y * g_ref[0].astype(F32)).astype(BF16)


A_WIN = A_BAND + 2 * CHUNK


def _attn_a_kernel(q_ref, k_ref, v_ref, bias_ref, g_ref, o_ref, *, row0, lo_valid, hi_valid, group):
    hid = _head_lane_id()
    for gi in range(group):
        c = pl.program_id(1) * group + gi
        start = pl.multiple_of(row0 + c * CHUNK, CHUNK)
        q = q_ref[0, gi * CHUNK:(gi + 1) * CHUNK, :]
        qs = jnp.concatenate([_keep_lanes(q, hid == h) for h in range(N_HEADS)], axis=0)
        kb = k_ref[0, pl.ds(start, A_WIN), :]
        vb = v_ref[0, pl.ds(start, A_WIN), :]
        s = _dot_nt(qs, kb) + bias_ref[...]
        row = start + lax.broadcasted_iota(I32, (1, A_WIN), 1)
        valid = (row >= lo_valid) & (row < hi_valid)
        s = jnp.where(valid, s, NEG)
        m = jnp.max(s, axis=-1, keepdims=True)
        p = jnp.where(valid, jnp.exp(s - m), 0.0)
        l = jnp.sum(p, axis=-1, keepdims=True)
        o = _dot(p.astype(BF16), vb) / l
        y = o[(N_HEADS - 1) * CHUNK:]
        for h in range(N_HEADS - 2, -1, -1):
            y = jnp.where(hid == h, o[h * CHUNK:(h + 1) * CHUNK], y)
        gate = g_ref[0, gi * CHUNK:(gi + 1) * CHUNK, :].astype(F32)
        o_ref[0, gi * CHUNK:(gi + 1) * CHUNK, :] = (y * gate).astype(BF16)


def _attn_a_call(q, kfull, vfull, bias, gates, row0, lo_valid, hi_valid, group):
    b, tp, _ = q.shape
    rows_kv = kfull.shape[1]
    assert row0 % CHUNK == 0 and row0 + q.shape[1] - CHUNK + A_WIN <= kfull.shape[1]
    kern = functools.partial(_attn_a_kernel, row0=row0, lo_valid=lo_valid, hi_valid=hi_valid, group=group)
    rows = CHUNK * group
    return pl.pallas_call(
        kern, grid=(b, tp // rows),
        in_specs=[pl.BlockSpec((1, rows, 256), lambda bb, c: (bb, c, 0)),
                  pl.BlockSpec((1, rows_kv, 256), lambda bb, c: (bb, 0, 0)),
                  pl.BlockSpec((1, rows_kv, 256), lambda bb, c: (bb, 0, 0)),
                  pl.BlockSpec((N_HEADS * CHUNK, A_WIN), lambda bb, c: (0, 0)),
                  pl.BlockSpec((1, rows, 256), lambda bb, c: (bb, c, 0))],
        out_specs=pl.BlockSpec((1, rows, 256), lambda bb, c: (bb, c, 0)),
        out_shape=jax.ShapeDtypeStruct((b, tp, 256), BF16),
        compiler_params=_cparams(2), name="attn_band",
    )(q, kfull, vfull, bias, gates)


def _attn_b_kernel(q_ref, k_ref, v_ref, g_ref, o_ref, sa_ref, sb_ref, acc_ref, *, p_len, l_len, tq, tk, nq):
    i = pl.program_id(1) if nq > 1 else 0
    q0 = p_len + i * tq
    lane = lax.broadcasted_iota(I32, (1, 256), 1)
    qm = []
    for h in range(N_HEADS):
        qg = q_ref[0, :, (h // 2) * 256:(h // 2) * 256 + 256]
        lo = (h % 2) * (MLA_NOPE + MLA_ROPE)
        qm.append(_keep_lanes(qg, (lane >= lo) & (lane < lo + MLA_NOPE + MLA_ROPE)))
    qchunk = (q0 + lax.broadcasted_iota(I32, (1, tq), 1)) // CHUNK
    nkb = jnp.minimum((q0 + tq + tk - 1) // tk, k_ref.shape[1] // tk)
    nfull = jnp.minimum(((q0 // CHUNK + 1) * CHUNK) // tk, nkb)

    def k_block(j, h):
        return k_ref[0, pl.ds(pl.multiple_of(j * tk, tk), tk), (h // 2) * 256:(h // 2) * 256 + 256]

    def mask_fn(j):
        kpos = j * tk + lax.broadcasted_iota(I32, (tk, 1), 0)
        return ((kpos // CHUNK) <= qchunk) & (kpos < l_len)

    ls = _flash_run(nfull, nkb, k_ref.shape[1] // tk, qm=qm, k_block=k_block, v_block=lambda j, h: v_ref[0, j, h],
                    col_sub=None, row_add=None, mask_fn=mask_fn, s_refs=(sa_ref, sb_ref), acc_ref=acc_ref, tq=tq)
    _flash_finish(o_ref, g_ref, acc_ref, ls)


def _attn_b_call(qb, kb, vaug, gates, p_len, l_len, tq, tk):
    b, t, _ = qb.shape
    lp = kb.shape[1]
    nq = t // tq
    kern = functools.partial(_attn_b_kernel, p_len=p_len, l_len=l_len, tq=tq, tk=tk, nq=nq)
    return pl.pallas_call(
        kern, grid=(b, nq),
        in_specs=[pl.BlockSpec((1, tq, 512), lambda bb, i: (bb, i, 0)),
                  pl.BlockSpec((1, lp, 512), lambda bb, i: (bb, 0, 0)),
                  pl.BlockSpec((1,) + vaug.shape[1:], lambda bb, i: (bb, 0, 0, 0, 0)),
                  pl.BlockSpec((1, tq, 256), lambda bb, i: (bb, i, 1))],
        out_specs=pl.BlockSpec((1, tq, 256), lambda bb, i: (bb, i, 0)),
        out_shape=jax.ShapeDtypeStruct((b, t, 256), BF16),
        scratch_shapes=_flash_scratch(tq, tk),
        compiler_params=_cparams(2), name="attn_latent",
    )(qb, kb, vaug, gates)


def _attn_c_kernel(q_ref, iq_ref, iw_ref, k_ref, v_ref, ik_ref, g_ref, o_ref, key_ref, top_ref, sa_ref, sb_ref,
                   acc_ref,
                   *, p_len, l_len, k_sel, tq, tk, nq):
    i = pl.program_id(1) if nq > 1 else 0
    q0 = p_len + i * tq
    nkb_max = k_ref.shape[1] // tk
    nkb = jnp.minimum((q0 + tq + tk - 1) // tk, nkb_max)
    qchunk = (q0 + lax.broadcasted_iota(I32, (1, tq), 1)) // CHUNK
    hid = _head_lane_id()

    iq = iq_ref[0]
    ihid = lax.broadcasted_iota(I32, (1, 256), 1) // IDX_DIM
    iqm = [_keep_lanes(iq, ihid == h) for h in range(IDX_HEADS)]
    iw = iw_ref[0] * (IDX_DIM ** -0.5 * IDX_HEADS ** -0.5)

    def score_body(edge):
        def body(j, carry):
            ks = pl.multiple_of(j * tk, tk)
            ikb = ik_ref[0, pl.ds(ks, tk), :]
            score = jnp.zeros((tk, tq), F32)
            for h in range(IDX_HEADS):
                score = score + iw[h:h + 1, :] * jnp.maximum(_dot_nt(ikb, iqm[h]), 0.0)
            if edge:
                kpos = ks + lax.broadcasted_iota(I32, (tk, 1), 0)
                score = jnp.where(((kpos // CHUNK) <= qchunk) & (kpos < l_len), score, -jnp.inf)
            key_ref[j] = score
            top_ref[j] = pltpu.bitcast(pltpu.bitcast(score, I32) & -65536, F32).astype(BF16)
            return carry
        return body

    n_free = jnp.minimum(((q0 // CHUNK + 1) * CHUNK) // tk, nkb)
    lax.fori_loop(0, n_free, score_body(False), 0)
    lax.fori_loop(n_free, nkb, score_body(True), 0)

    def cand_value(cand):
        bits = jnp.where(cand < 0, cand ^ 0x7FFFFFFF, cand)
        bits = jnp.where((bits > 0) & (bits < 0x00800000), 0x00800000, bits)
        return pltpu.bitcast(bits, F32)

    def count(pred_fn):
        def cbody(j, acc):
            pf = jnp.where(pred_fn(key_ref[j], j), 1.0, 0.0)
            for r in range(tk // 8):
                acc = acc + pf[r * 8:(r + 1) * 8]
            return acc
        acc = lax.fori_loop(0, nkb, cbody, jnp.zeros((8, tq), F32))
        return jnp.sum(acc, axis=0, keepdims=True)

    def count_top(cand_b):
        one, zero = jnp.ones((16, tq), BF16), jnp.zeros((16, tq), BF16)

        def cbody(j, acc):
            part = zero
            for r in range(tk // 16):
                part = part + jnp.where(top_ref[j, r * 16:(r + 1) * 16, :] >= cand_b, one, zero)
            return acc + part.astype(F32)
        acc = lax.fori_loop(0, nkb, cbody, jnp.zeros((16, tq), F32))
        return jnp.sum(acc, axis=0, keepdims=True)

    def top_body(it, c):
        t, n_ge = c
        cand = t + jnp.left_shift(jnp.int32(1), 15 - it)
        cbits = pltpu.bitcast(cand_value(cand * 65536), I32) & -65536
        cnt = count_top(jnp.broadcast_to(pltpu.bitcast(cbits, F32), (16, tq)).astype(BF16))
        ok = cnt >= k_sel
        return jnp.where(ok, cand, t), jnp.where(ok, cnt, n_ge)

    t16, n_ge = lax.fori_loop(0, 16, top_body, (jnp.full((1, tq), -32768, I32), jnp.zeros((1, tq), F32)))

    def bis_body(it, c):
        t, n_ge = c
        cand = t + jnp.left_shift(jnp.int32(1), 15 - it)
        cval = cand_value(cand)
        cnt = count(lambda kb, j: kb >= cval)
        ok = cnt >= k_sel
        return jnp.where(ok, cand, t), jnp.where(ok, cnt, n_ge)

    tcode, n_ge = lax.fori_loop(0, 16, bis_body, (t16 * 65536, n_ge))
    thr = cand_value(jnp.maximum(tcode, MOST_NEGATIVE_CODE))

    def idx_of(j):
        return j * tk + lax.broadcasted_iota(I32, (tk, tq), 0)

    def tie_cut(_):
        need = k_sel - count(lambda kb, j: kb > thr)

        def tie_body(it, jc):
            cand = jc + jnp.left_shift(jnp.int32(1), 12 - it)
            cnt = count(lambda kb, j: (kb == thr) & (idx_of(j) < cand))
            return jnp.where(cnt < need, cand, jc)

        return lax.fori_loop(0, 13, tie_body, jnp.zeros((1, tq), I32))

    has_tie = jnp.max(jnp.where(n_ge > k_sel, 1.0, 0.0)) > 0.0
    jcut = lax.cond(has_tie, tie_cut, lambda _: jnp.full((1, tq), 2 ** 30, I32), 0)

    q = q_ref[0]
    qm = [_keep_lanes(q, hid == h) for h in range(N_HEADS)]

    def mask_fn(j):
        kb = key_ref[j]
        return (kb > thr) | ((kb == thr) & (idx_of(j) <= jcut))

    ls = _flash_run(None, nkb, nkb_max, qm=qm,
                    k_block=lambda j, h: k_ref[0, pl.ds(pl.multiple_of(j * tk, tk), tk), :],
                    v_block=lambda j, h: v_ref[0, j, 0], col_sub=None, row_add=None, mask_fn=mask_fn,
                    s_refs=(sa_ref, sb_ref), acc_ref=acc_ref, tq=tq)
    _flash_finish(o_ref, g_ref, acc_ref, ls)


def _attn_c_call(cq, ciq, iwt, krep, vaug, ikrep, gates, p_len, l_len, k_sel, tq, tk):
    b, t, _ = cq.shape
    lp = krep.shape[1]
    assert lp <= 8192
    nq = t // tq
    kern = functools.partial(_attn_c_kernel, p_len=p_len, l_len=l_len, k_sel=float(k_sel), tq=tq, tk=tk, nq=nq)
    return pl.pallas_call(
        kern, grid=(b, nq),
        in_specs=[pl.BlockSpec((1, tq, 256), lambda bb, i: (bb, i, 0)),
                  pl.BlockSpec((1, tq, 256), lambda bb, i: (bb, i, 0)),
                  pl.BlockSpec((1, 8, tq), lambda bb, i: (bb, 0, i)),
                  pl.BlockSpec((1, lp, 256), lambda bb, i: (bb, 0, 0)),
                  pl.BlockSpec((1,) + vaug.shape[1:], lambda bb, i: (bb, 0, 0, 0, 0)),
                  pl.BlockSpec((1, lp, 256), lambda bb, i: (bb, 0, 0)),
                  pl.BlockSpec((1, tq, 256), lambda bb, i: (bb, i, 2))],
        out_specs=pl.BlockSpec((1, tq, 256), lambda bb, i: (bb, i, 0)),
        out_shape=jax.ShapeDtypeStruct((b, t, 256), BF16),
        scratch_shapes=[pltpu.VMEM((lp // tk, tk, tq), F32), pltpu.VMEM((lp // tk, tk, tq), BF16)]
        + _flash_scratch(tq, tk),
        compiler_params=_cparams(2), name="attn_sparse",
    )(cq, ciq, iwt, krep, vaug, ikrep, gates)


def _attn_d_kernel(q_ref, k_ref, v_ref, lf_ref, g_ref, o_ref, fc_ref, fk_ref, sa_ref, sb_ref, acc_ref,
                   *, p_len, tq, tk, nq):
    i = pl.program_id(1) if nq > 1 else 0
    lp = k_ref.shape[1]
    nch = lp // 128

    def cumulate():
        x = lf_ref[0]
        lane = lax.broadcasted_iota(I32, (1, lp), 1)
        step = 1
        while step < lp:
            x = x + jnp.where(lane >= step, pltpu.roll(x, step, 1), 0.0)
            step *= 2
        x = x * LOG2E
        for c in range(nch):
            fc_ref[c] = x[:, c * 128:(c + 1) * 128]

        eye = lax.broadcasted_iota(I32, (128, 128), 0) == lax.broadcasted_iota(I32, (128, 128), 1)

        def spread(c, carry):
            rows = fc_ref[c]
            for h in range(N_HEADS):
                col = jnp.sum(jnp.where(eye, rows[h:h + 1, :], 0.0), axis=1, keepdims=True)
                fk_ref[h, pl.ds(pl.multiple_of(c * 128, 128), 128), :] = jnp.broadcast_to(col, (128, 128))
            return carry

        lax.fori_loop(0, nch, spread, 0)

    if nq > 1:
        pl.when(i == 0)(cumulate)
    else:
        cumulate()

    q0 = p_len + i * tq
    hid = _head_lane_id()
    q = q_ref[0]
    qm = [_keep_lanes(q, hid == h) for h in range(N_HEADS)]
    c0 = q0 // 128
    fq_rows = [fc_ref[c0 + c] for c in range(tq // 128)]
    fq = [jnp.concatenate([r[h:h + 1, :] for r in fq_rows], axis=1) for h in range(N_HEADS)]
    qpos = q0 + lax.broadcasted_iota(I32, (1, tq), 1)
    nkb = jnp.minimum((q0 + tq + tk - 1) // tk, lp // tk)
    nfull = jnp.minimum((q0 + 1) // tk, nkb)

    def col_sub(j, h):
        fk = fk_ref[h, pl.ds(pl.multiple_of(j * tk, tk), tk), :]
        return jnp.concatenate([fk] * (tq // 128), axis=1)

    def mask_fn(j):
        return (j * tk + lax.broadcasted_iota(I32, (tk, 1), 0)) <= qpos

    ls = _flash_run(nfull, nkb, lp // tk, qm=qm,
                    k_block=lambda j, h: k_ref[0, pl.ds(pl.multiple_of(j * tk, tk), tk), :],
                    v_block=lambda j, h: v_ref[0, j, h], col_sub=col_sub, row_add=fq, mask_fn=mask_fn,
                    s_refs=(sa_ref, sb_ref), acc_ref=acc_ref, tq=tq)
    _flash_finish(o_ref, g_ref, acc_ref, ls)


def _attn_d_call(dq, kd, vaug, lft, gates, p_len, tq, tk):
    b, t, _ = dq.shape
    lp = kd.shape[1]
    nq = t // tq
    assert p_len % 128 == 0 and tq % 128 == 0 and p_len + t <= lp
    kern = functools.partial(_attn_d_kernel, p_len=p_len, tq=tq, tk=tk, nq=nq)
    return pl.pallas_call(
        kern, grid=(b, nq),
        in_specs=[pl.BlockSpec((1, tq, 256), lambda bb, i: (bb, i, 0)),
                  pl.BlockSpec((1, lp, 256), lambda bb, i: (bb, 0, 0)),
                  pl.BlockSpec((1,) + vaug.shape[1:], lambda bb, i: (bb, 0, 0, 0, 0)),
                  pl.BlockSpec((1, 8, lp), lambda bb, i: (bb, 0, 0)),
                  pl.BlockSpec((1, tq, 256), lambda bb, i: (bb, i, 3))],
        out_specs=pl.BlockSpec((1, tq, 256), lambda bb, i: (bb, i, 0)),
        out_shape=jax.ShapeDtypeStruct((b, t, 256), BF16),
        scratch_shapes=[pltpu.VMEM((lp // 128, 8, 128), F32), pltpu.VMEM((N_HEADS, lp, 128), F32)]
        + _flash_scratch(tq, tk),
        compiler_params=_cparams(2), name="attn_forget",
    )(dq, kd, vaug, lft, gates)


def _out_kernel(ya_ref, yb_ref, yc_ref, yd_ref, x_ref, p_ref, wo_ref, gple_ref, wg_ref, wp_ref, o_ref):
    mixed = (_dot(ya_ref[0], wo_ref[0:256, :]) + _dot(yb_ref[0], wo_ref[256:512, :])
             + _dot(yc_ref[0], wo_ref[512:768, :]) + _dot(yd_ref[0], wo_ref[768:1024, :]))
    x1 = x_ref[0] + mixed
    ms = jnp.mean(x1 * x1, axis=-1, keepdims=True)
    xn = (x1 * lax.rsqrt(ms + EPS) * gple_ref[...]).astype(BF16)
    gate = 1.0 / (1.0 + jnp.exp(-_dot(xn, wg_ref[...])))
    o_ref[0] = x1 + gate * _dot(p_ref[0].astype(BF16), wp_ref[...])


def _out_call(ys, x, p, lw, tm):
    b, t, _ = x.shape

    def const(shape):
        return pl.BlockSpec(shape, lambda bb, i: (0,) * len(shape))

    def rows(w):
        return pl.BlockSpec((1, tm, w), lambda bb, i: (bb, i, 0))

    return pl.pallas_call(
        _out_kernel, grid=(b, t // tm),
        in_specs=[rows(256)] * 4 + [rows(D_MODEL), rows(PLE_DIM), const((1024, D_MODEL)), const((1, D_MODEL)),
                                    const((D_MODEL, D_MODEL)), const((PLE_DIM, D_MODEL))],
        out_specs=rows(D_MODEL),
        out_shape=jax.ShapeDtypeStruct((b, t, D_MODEL), F32),
        compiler_params=_cparams(2), name="out_proj",
    )(*ys, x, p, lw['w_out'], lw['g_ple'], lw['w_ple_gate'], lw['w_ple_proj'])


def _seg_matrix(segments):
    m = np.zeros((256, 256), np.float32)
    for lo, n in segments:
        m[lo:lo + n, lo:lo + n] = 1.0 / n
    return jnp.asarray(m, BF16)


_PAIR_SEGS = ((0, 64), (64, 32), (96, 64), (160, 32), (192, 64))


def _pair_cols(head):
    return (head // 2) * 256 + (head % 2) * (MLA_NOPE + MLA_ROPE)


def _prep_layer(i, g_in, w_in, g_qk, g_rope, a_rel_bias, b_g_cq, b_w_uq, b_g_ckv, b_w_ukv, d_b_f, w_out, g_ple,
                w_ple_gate, w_ple_proj):
    w = w_in[i]
    cols = []
    for n in _MY_ORDER:
        if n == 'pad56':
            cols.append(jnp.zeros((D_MODEL, 56), F32))
        else:
            o, wd = _REF_OFF[n]
            cols.append(w[:, o:o + wd])
    lw = {'w_in': jnp.concatenate(cols, axis=1).astype(BF16), 'g_in': g_in[i][None, :]}
    wt = [w[:, _REF_OFF[n][0]:_REF_OFF[n][0] + _REF_OFF[n][1]] for n in ('d_v', 'c_v', 'c_iw', 'd_f')]
    wt.append(jnp.zeros((D_MODEL, XT_ROWS - XT_DF - N_HEADS), F32))
    lw['w_t'] = jnp.concatenate(wt, axis=1).T.astype(BF16)
    lw['b_col'] = jnp.concatenate([d_b_f[i], jnp.zeros((8 - N_HEADS,), F32)])[:, None]
    g = g_qk[i]
    sc = HEAD_DIM ** -0.5
    rows = [jnp.tile(g[0], 4) * sc, jnp.tile(g[1], 4), jnp.tile(g[2], 4) * (sc * LOG2E),
            jnp.tile(g[4], 4) * (sc * LOG2E), jnp.tile(g[5], 4)]
    lw['g4'] = jnp.stack(rows + [jnp.zeros((256,), F32)] * 3)
    lw['hseg'] = _seg_matrix(tuple((h * 64, 64) for h in range(4)))
    lw['segq'] = _seg_matrix(_PAIR_SEGS)
    lw['g_cq'] = b_g_cq[i][None, :]
    lw['g_ckv'] = b_g_ckv[i][None, :]
    qscale = (MLA_NOPE + MLA_ROPE) ** -0.5 * LOG2E
    hw = MLA_NOPE + MLA_ROPE
    wuq, gqb, wuk, gkb, wuv = [], [], [], [], []
    e_kr = np.zeros((128, 512), np.float32)
    for h in range(N_HEADS):
        wuq.append(b_w_uq[i][:, h * hw:(h + 1) * hw])
        gqb += [g[6] * qscale, g_rope[i][0] * qscale]
        srck = h * (MLA_NOPE + HEAD_DIM)
        wuk += [b_w_ukv[i][:, srck:srck + MLA_NOPE], jnp.zeros((MLA_KV_LORA, MLA_ROPE), F32)]
        gkb += [g[7], jnp.zeros((MLA_ROPE,), F32)]
        wuv.append(b_w_ukv[i][:, srck + MLA_NOPE:srck + MLA_NOPE + HEAD_DIM])
        e_kr[np.arange(MLA_ROPE), _pair_cols(h) + MLA_NOPE + np.arange(MLA_ROPE)] = 1.0
        if h % 2 == 1:
            wuq.append(jnp.zeros((MLA_Q_LORA, 256 - 2 * hw), F32))
            wuk.append(jnp.zeros((MLA_KV_LORA, 256 - 2 * hw), F32))
            gqb.append(jnp.zeros((256 - 2 * hw,), F32))
            gkb.append(jnp.zeros((256 - 2 * hw,), F32))
    lw['w_uq'] = jnp.concatenate(wuq, axis=1).astype(BF16)
    lw['g_qb'] = jnp.concatenate(gqb)[None, :]
    lw['w_uk'] = jnp.concatenate(wuk, axis=1).astype(BF16)
    lw['g_kb'] = jnp.concatenate(gkb)[None, :]
    lw['e_kr'] = jnp.asarray(e_kr, BF16)
    lw['w_uvt'] = jnp.concatenate(wuv, axis=1).T.astype(BF16)
    one = jnp.ones((128,), F32)
    lw['srow'] = jnp.stack([jnp.concatenate([g[3], one[HEAD_DIM:]]), jnp.concatenate([g_rope[i][1], one[MLA_ROPE:]])]
                           + [one] * 6)
    ab = a_rel_bias[i]
    ext = jnp.concatenate([jnp.broadcast_to(ab[:, 2 * REL_CLIP:], (N_HEADS, A_WIN - REL_CLIP - 1)),
                           ab[:, REL_CLIP - CHUNK + 1:][:, ::-1]], axis=1)
    lw['a_bias'] = jnp.stack([ext[:, CHUNK - 1 - q:CHUNK - 1 - q + A_WIN] for q in range(CHUNK)],
                             axis=1).reshape(N_HEADS * CHUNK, A_WIN)
    lw['w_out'] = w_out[i].astype(BF16)
    lw['g_ple'] = g_ple[i][None, :]
    lw['w_ple_gate'] = w_ple_gate[i].astype(BF16)
    lw['w_ple_proj'] = w_ple_proj[i].astype(BF16)
    return lw


def _cos_sin(pos, rot_dim):
    half = rot_dim // 2
    inv = jnp.float32(ROPE_THETA) ** (-jnp.arange(half, dtype=F32) / half)
    ang = pos.astype(F32)[:, None] * inv[None, :]
    return [jnp.cos(ang), jnp.sin(ang)]


def _rot_placement():
    src = {MLA_ROPE: 1, ROT_DIM: 1 + MLA_ROPE}
    e = np.zeros((1 + MLA_ROPE + ROT_DIM, TAB_W), np.float32)
    sections = ((TAB_QB, 512, [_pair_cols(h) + MLA_NOPE for h in range(N_HEADS)], MLA_ROPE),
                (TAB_CQ, 256, [h * HEAD_DIM for h in range(N_HEADS)], ROT_DIM),
                (TAB_S1, 128, [0], ROT_DIM), (TAB_S2, 128, [0], MLA_ROPE))
    for base, width, starts, rot_dim in sections:
        half = rot_dim // 2
        e[0, base:base + width] = 1.0
        for st in starts:
            for i in range(half):
                c, s = src[rot_dim] + i, src[rot_dim] + half + i
                e[0, base + st + i] = e[0, base + st + half + i] = 0.0
                e[c, base + st + i] = e[c, base + st + half + i] = 1.0
                e[s, base + width + st + half + i] = 1.0
                e[s, base + 2 * width + st + i] = -1.0
    return jnp.asarray(e)


def _rot_tables(pos):
    src = jnp.concatenate([jnp.ones((pos.shape[0], 1), F32)] + _cos_sin(pos, MLA_ROPE) + _cos_sin(pos, ROT_DIM),
                          axis=1)
    hi = src.astype(BF16)
    lo = (src - hi.astype(F32)).astype(BF16)
    place = _rot_placement().astype(BF16)
    return jnp.dot(jnp.concatenate([hi, lo], axis=1), jnp.concatenate([place, place], axis=0),
                   preferred_element_type=F32)


def _pad_axis(a, size, axis):
    if a.shape[axis] == size:
        return a
    pads = [(0, 0)] * a.ndim
    pads[axis] = (0, size - a.shape[axis])
    return jnp.pad(a, pads)


def _cat_rows(past, new, rows):
    a = new if past is None else jnp.concatenate([past.astype(new.dtype), new], axis=1)
    return _pad_axis(a, rows, 1)


def _cat_lanes(past_t, new_t, lanes):
    a = new_t if past_t is None else jnp.concatenate([past_t.astype(new_t.dtype), new_t], axis=2)
    return _pad_axis(a, lanes, 2)


def _value_blocks(vt, tk):
    b, r, lp = vt.shape
    h = r // HEAD_DIM
    v = vt.astype(BF16).reshape(b, h, HEAD_DIM, lp)
    v = jnp.concatenate([v, jnp.ones((b, h, V_ROWS - HEAD_DIM, lp), BF16)], axis=2)
    return v.reshape(b, h, V_ROWS, lp // tk, tk).transpose(0, 3, 1, 2, 4)


def _layer(x, p, past, lw, tab, *, fold_batch, tm, tq, tq_c, tk, a_group):
    b, t, _ = x.shape
    p_len = 0 if past is None else past[2].shape[1]
    l_len = p_len + t
    tqp = -(-t // tq) * tq
    tqp_c = -(-t // tq_c) * tq_c
    lp = -(-(p_len + max(tqp, tqp_c)) // tk) * tk

    xin = x.reshape(1, b * t, D_MODEL) if fold_batch else x
    a_direct = past is None and not fold_batch
    a_front = -(-(CHUNK + A_BAND) // tm) * tm if a_direct else 0
    outs = _inproj_call(xin, tab, lw, tm, a_front)
    if fold_batch:
        xt = outs[18][0].reshape(XT_ROWS, b, t).transpose(1, 0, 2)
        outs = ([o.reshape((b, t) + o.shape[2:]) for o in outs[:18]] + [xt, outs[19].reshape(b, t, 1024), None, None]
                + [o.reshape(b, t, 256) for o in outs[22:24]])
    (aq, akb, avb, akf, avf, qb, ckvf, cq, ciq, c_k, c_v, s2, b_kr, c_ik, dq, dkb, dkf, dvf, xt, gates,
     dva, cva, krep_new, ikrep_new) = outs
    direct_values = past is None and not fold_batch and tm == tk

    lft = xt[:, XT_DF:XT_ROWS]
    d_lf = lft[:, :N_HEADS, :].transpose(0, 2, 1)
    state = (akf.reshape(b, t, N_HEADS, HEAD_DIM), avf.reshape(b, t, N_HEADS, HEAD_DIM), ckvf, b_kr, c_k, c_v, c_ik,
             dkf.reshape(b, t, N_HEADS, HEAD_DIM), dvf.reshape(b, t, N_HEADS, HEAD_DIM), d_lf)

    if past is None:
        pa_k = pa_v = pb_ckv = pb_kr = pc_k = pc_v = pc_ik = pd_k = pd_v = pd_lf = None
        pa = 0
    else:
        pa_k, pa_v, pb_ckv, pb_kr, pc_k, pc_v, pc_ik, pd_k, pd_v, pd_lf = past
        pa = pa_k.shape[1]
        pa_k = pa_k.reshape(b, pa, 256)
        pa_v = pa_v.reshape(b, pa, 256)
        pd_k = pd_k.reshape(b, p_len, 256)
        pd_v = pd_v.reshape(b, p_len, 256)

    tp = -(-t // (CHUNK * a_group)) * (CHUNK * a_group)
    if a_direct:
        assert tp == t
        a_keys, a_vals = akb, avb
        row0, front = a_front - (CHUNK + A_BAND), a_front
    else:
        front = CHUNK + A_BAND - pa
        a_rows = CHUNK + A_BAND + tp

        def band_src(pst, new):
            parts = [jnp.zeros((b, front, 256), BF16)]
            if pst is not None:
                parts.append(pst.astype(BF16))
            parts.append(new)
            return _pad_axis(jnp.concatenate(parts, axis=1), a_rows, 1)

        a_keys, a_vals, row0 = band_src(pa_k, akb), band_src(pa_v, avb), 0
    ya = _attn_a_call(_pad_axis(aq, tp, 1), a_keys, a_vals, lw['a_bias'], _pad_axis(gates, tp, 1), row0, front,
                      front + pa + t, a_group)[:, :t]

    gates_q = _pad_axis(gates, tqp, 1)

    ckv_all = _cat_rows(pb_ckv, ckvf, lp)
    kr_new = s2 if past is None else _pad_axis(b_kr, 128, 2)
    kr_past = None if past is None else _pad_axis(pb_kr, 128, 2)
    kr_all = _cat_rows(kr_past, kr_new, lp)
    kb, vba = _bkv_call(ckv_all, kr_all, lw, tk)
    yb = _attn_b_call(_pad_axis(qb, tqp, 1), kb, vba, gates_q, p_len, l_len, tq, tk)[:, :t]

    krep = _cat_rows(None if past is None else jnp.tile(pc_k.astype(BF16), (1, 1, N_HEADS)), krep_new, lp)
    ikrep = _cat_rows(None if past is None else jnp.tile(pc_ik.astype(BF16), (1, 1, IDX_HEADS)), ikrep_new, lp)
    if not direct_values:
        cva = _value_blocks(_cat_lanes(None if past is None else pc_v.transpose(0, 2, 1), xt[:, XT_CV:XT_IW], lp), tk)
    iwt = _pad_axis(xt[:, XT_IW:XT_DF], tqp_c, 2)
    k_sel = min(DSA_TOPK, l_len // 4)
    yc = _attn_c_call(_pad_axis(cq, tqp_c, 1), _pad_axis(ciq, tqp_c, 1), iwt, krep, cva, ikrep,
                      _pad_axis(gates, tqp_c, 1), p_len, l_len, k_sel, tq_c, tk)[:, :t]

    kd = _cat_rows(None if past is None else pd_k.astype(BF16), dkb, lp)
    if not direct_values:
        dva = _value_blocks(_cat_lanes(None if past is None else pd_v.transpose(0, 2, 1), xt[:, XT_DV:XT_CV], lp), tk)
    lf_past = None if past is None else _pad_axis(pd_lf.transpose(0, 2, 1), 8, 1)
    lf_all = _cat_lanes(lf_past, lft, lp)
    yd = _attn_d_call(_pad_axis(dq, tqp, 1), kd, dva, lf_all, gates_q, p_len, tq, tk)[:, :t]

    if fold_batch:
        x_new = _out_call([y.reshape(1, b * t, 256) for y in (ya, yb, yc, yd)], x.reshape(1, b * t, D_MODEL),
                          p.reshape(1, b * t, PLE_DIM), lw, tm).reshape(b, t, D_MODEL)
    else:
        x_new = _out_call([ya, yb, yc, yd], x, p, lw, tm)
    return x_new, state


def kernel(x_prompt, x_sample, cache_a_k, cache_a_v, cache_b_ckv, cache_b_krope, cache_c_k, cache_c_v, cache_c_idx_k,
           cache_d_k, cache_d_v, cache_d_logf, p_prompt, p_sample, g_in, w_in, g_qk, g_rope, a_rel_bias, b_g_cq,
           b_w_uq, b_g_ckv, b_w_ukv, d_b_f, w_out, g_ple, w_ple_gate, w_ple_proj):
    depth = w_in.shape[0]
    b, t = x_prompt.shape[:2]
    bs, ts = x_sample.shape[:2]
    past_len = cache_b_ckv.shape[2]
    tab_p = _rot_tables(jnp.arange(t))
    tab_s = jnp.tile(_rot_tables(past_len + jnp.arange(ts)), (bs, 1))
    xp, xs = x_prompt, x_sample
    states_p, states_s = [], []
    for i in range(depth):
        lw = _prep_layer(i, g_in, w_in, g_qk, g_rope, a_rel_bias, b_g_cq, b_w_uq, b_g_ckv, b_w_ukv, d_b_f, w_out,
                         g_ple, w_ple_gate, w_ple_proj)
        xp, st_p = _layer(xp, p_prompt[i], None, lw, tab_p, fold_batch=False, tm=256, tq=256, tq_c=512, tk=256,
                           a_group=4)
        past = (cache_a_k[i], cache_a_v[i], cache_b_ckv[i], cache_b_krope[i], cache_c_k[i], cache_c_v[i],
                cache_c_idx_k[i], cache_d_k[i], cache_d_v[i], cache_d_logf[i])
        xs, st_s = _layer(xs, p_sample[i], past, lw, tab_s, fold_batch=True, tm=bs * ts, tq=128, tq_c=128, tk=256,
                           a_group=1)
        states_p.append(st_p)
        states_s.append(st_s)
    sp = [jnp.stack(z) for z in zip(*states_p)]
    ss = [jnp.stack(z) for z in zip(*states_s)]
    keep = min(A_BAND, t)
    sp[0] = sp[0][:, :, t - keep:]
    sp[1] = sp[1][:, :, t - keep:]
    return (xp, xs, *sp, *ss)
```

```python
import functools

import numpy as np
import jax
import jax.numpy as jnp
from jax import lax
from jax.experimental import pallas as pl
from jax.experimental.pallas import tpu as pltpu

F32 = jnp.float32
BF16 = jnp.bfloat16
I32 = jnp.int32

D_MODEL = 1024
CHUNK = 64
EPS = 1e-6
HEAD_DIM = 64
N_HEADS = 4
GROUP_WIDTH = 256
ROT_DIM = 16
ROPE_THETA = 500000.0
A_BAND = 8 * CHUNK
REL_CLIP = 128
MLA_Q_LORA = 384
MLA_KV_LORA = 128
MLA_NOPE = 64
MLA_ROPE = 32
IDX_HEADS = 8
IDX_DIM = 32
DSA_TOPK = 256
PLE_DIM = 256

_REF_SPLITS = (
    ('a_q', 256), ('a_k', 256), ('a_v', 256), ('a_g', 256),
    ('b_cq', 384), ('b_ckv', 128), ('b_kr', 32), ('b_g', 256),
    ('c_q', 256), ('c_k', 64), ('c_v', 64), ('c_iq', 256), ('c_ik', 32), ('c_iw', 8), ('c_g', 256),
    ('d_q', 256), ('d_k', 256), ('d_v', 256), ('d_f', 4), ('d_g', 256),
)
_REF_OFF = {}
_o = 0
for _n, _w in _REF_SPLITS:
    _REF_OFF[_n] = (_o, _w)
    _o += _w

_MY_ORDER = ('a_q', 'a_k', 'a_v', 'a_g', 'b_cq', 'b_ckv', 'b_g', 'c_q', 'c_iq', 'c_g', 'd_q', 'd_k', 'd_v', 'd_g',
             'c_k', 'c_v', 'b_kr', 'c_ik', 'c_iw', 'pad56')
_MY_OFF = {}
_o = 0
for _n in _MY_ORDER:
    _w = 56 if _n == 'pad56' else _REF_OFF[_n][1]
    _MY_OFF[_n] = _o
    _o += _w
W_IN_COLS = _o
_PROJ_GROUPS = ((0, _MY_OFF['b_cq']), (_MY_OFF['b_cq'], _MY_OFF['c_q']), (_MY_OFF['c_q'], _MY_OFF['d_q']),
                (_MY_OFF['d_q'], _MY_OFF['c_k']), (_MY_OFF['c_k'], W_IN_COLS))

XT_DV, XT_CV, XT_IW, XT_DF, XT_ROWS = 0, 256, 320, 328, 336

TAB_QB, TAB_CQ, TAB_S1, TAB_S2 = 0, 1536, 2304, 2688
TAB_W = 3072

V_ROWS = HEAD_DIM + 16
NEG = -1e30
LOG2E = 1.4426950408889634
MOST_NEGATIVE_CODE = -2139095040
VMEM_LIMIT = 56 * 1024 * 1024


def _cparams(n_axes):
    return pltpu.CompilerParams(dimension_semantics=("arbitrary",) * n_axes, vmem_limit_bytes=VMEM_LIMIT)


def _dot(a, b):
    return jnp.dot(a, b, preferred_element_type=F32)


def _dot_nt(a, b):
    return lax.dot_general(a, b, (((1,), (1,)), ((), ())), preferred_element_type=F32)


def _seg_mean_sq(t, seg):
    sq = t * t
    hi = sq.astype(BF16)
    lo = (sq - hi.astype(F32)).astype(BF16)
    return _dot(hi, seg) + _dot(lo, seg)


def _rotate(t, cos, sin_up, sin_dn, half):
    w = t.shape[-1]
    return t * cos + pltpu.roll(t, half, 1) * sin_up + pltpu.roll(t, w - half, 1) * sin_dn


def _silu(g):
    return g * (1.0 / (1.0 + jnp.exp(-g)))


def _log_sigmoid(v):
    return jnp.minimum(v, 0.0) - jnp.log1p(jnp.exp(-jnp.abs(v)))


N_INPROJ_IN = 16
NARROW_OUTS = (9, 10, 12, 13)


def _inproj_kernel(*refs, n_prev):
    n_in = N_INPROJ_IN + (len(NARROW_OUTS) if n_prev else 0)
    _inproj_body(refs[N_INPROJ_IN:n_in], n_prev, *refs[:N_INPROJ_IN], *refs[n_in:])


def _inproj_body(prev, n_prev,
                 x_ref, tab_ref, gin_ref, w_ref, wt_ref, bcol_ref, g4_ref, hseg_ref, gcq_ref, wuq_ref, segq_ref,
                   gqb_ref, gckv_ref, srow_ref, akz_ref, avz_ref,
                   aq_o, akb_o, avb_o, akf_o, avf_o, qb_o, ckvf_o, cq_o, ciq_o, ck_o, cv_o, s2_o, bkr_o, cik_o,
                   dq_o, dkb_o, dkf_o, dvf_o, xt_o, gate_o, dva_o, cva_o, krep_o, ikrep_o):
    del akz_ref, avz_ref

    def store_narrow(slot, o_ref, rows):
        for layer in range(n_prev):
            o_ref[layer, 0] = prev[slot][layer, 0]
        o_ref[n_prev, 0] = rows

    x = x_ref[0]
    ms = jnp.mean(x * x, axis=-1, keepdims=True)
    xn = (x * lax.rsqrt(ms + EPS) * gin_ref[...]).astype(BF16)

    group_out = {}

    def proj(name, n):
        c0 = _MY_OFF[name]
        g0, g1 = next((a, b) for a, b in _PROJ_GROUPS if a <= c0 < b)
        if g0 not in group_out:
            group_out[g0] = _dot(xn, w_ref[:, g0:g1])
        return group_out[g0][:, c0 - g0:c0 - g0 + n]

    hseg = hseg_ref[...]

    def headnorm(t, row):
        return t * lax.rsqrt(_seg_mean_sq(t, hseg) + EPS) * g4_ref[row:row + 1, :]

    def fullnorm(t, g):
        return t * lax.rsqrt(jnp.mean(t * t, axis=-1, keepdims=True) + EPS) * g

    aq_o[0] = headnorm(proj('a_q', 256), 0).astype(BF16)
    ak = headnorm(proj('a_k', 256), 1)
    akf_o[0] = ak
    akb_o[0] = ak.astype(BF16)
    av = proj('a_v', 256)
    avf_o[0] = av
    avb_o[0] = av.astype(BF16)
    gate_o[0, :, 0:256] = _silu(proj('a_g', 256)).astype(BF16)

    cqn = fullnorm(proj('b_cq', MLA_Q_LORA), gcq_ref[...]).astype(BF16)
    qb = _dot(cqn, wuq_ref[...])
    segq = segq_ref[...]
    ms_q = jnp.concatenate([_seg_mean_sq(qb[:, :256], segq), _seg_mean_sq(qb[:, 256:], segq)], axis=1)
    qbn = qb * lax.rsqrt(ms_q + EPS) * gqb_ref[...]
    qbn = _rotate(qbn, tab_ref[:, TAB_QB:TAB_QB + 512], tab_ref[:, TAB_QB + 512:TAB_QB + 1024],
                  tab_ref[:, TAB_QB + 1024:TAB_QB + 1536], MLA_ROPE // 2)
    qb_o[0] = qbn.astype(BF16)
    ckvf_o[0] = fullnorm(proj('b_ckv', MLA_KV_LORA), gckv_ref[...])
    gate_o[0, :, 256:512] = _silu(proj('b_g', 256)).astype(BF16)

    cq = headnorm(proj('c_q', 256), 2)
    cq = _rotate(cq, tab_ref[:, TAB_CQ:TAB_CQ + 256], tab_ref[:, TAB_CQ + 256:TAB_CQ + 512],
                 tab_ref[:, TAB_CQ + 512:TAB_CQ + 768], ROT_DIM // 2)
    cq_o[0] = cq.astype(BF16)
    ciq_o[0] = proj('c_iq', 256).astype(BF16)
    gate_o[0, :, 512:768] = _silu(proj('c_g', 256)).astype(BF16)

    dq_o[0] = headnorm(proj('d_q', 256), 3).astype(BF16)
    dk = headnorm(proj('d_k', 256), 4)
    dkf_o[0] = dk
    dkb_o[0] = dk.astype(BF16)
    dvf_o[0] = proj('d_v', 256)
    gate_o[0, :, 768:1024] = _silu(proj('d_g', 256)).astype(BF16)

    lane = lax.broadcasted_iota(I32, (1, 128), 1)
    t = proj('c_k', 128)
    m64 = lane < HEAD_DIM
    ms1 = jnp.sum(jnp.where(m64, t * t, 0.0), axis=-1, keepdims=True) * (1.0 / HEAD_DIM)
    t = jnp.where(m64, t * lax.rsqrt(ms1 + EPS), t) * srow_ref[0:1, :]
    t = _rotate(t, tab_ref[:, TAB_S1:TAB_S1 + 128], tab_ref[:, TAB_S1 + 128:TAB_S1 + 256],
                tab_ref[:, TAB_S1 + 256:TAB_S1 + 384], ROT_DIM // 2)
    store_narrow(0, ck_o, t[:, :HEAD_DIM])
    krep_o[0] = jnp.concatenate([t[:, :HEAD_DIM]] * N_HEADS, axis=1).astype(BF16)
    store_narrow(1, cv_o, t[:, HEAD_DIM:])
    t = proj('b_kr', 128)
    m32 = lane < MLA_ROPE
    ms2 = jnp.sum(jnp.where(m32, t * t, 0.0), axis=-1, keepdims=True) * (1.0 / MLA_ROPE)
    t = jnp.where(m32, t * lax.rsqrt(ms2 + EPS), t) * srow_ref[1:2, :]
    t = _rotate(t, tab_ref[:, TAB_S2:TAB_S2 + 128], tab_ref[:, TAB_S2 + 128:TAB_S2 + 256],
                tab_ref[:, TAB_S2 + 256:TAB_S2 + 384], MLA_ROPE // 2)
    s2_o[0] = t
    store_narrow(2, bkr_o, t[:, :MLA_ROPE])
    store_narrow(3, cik_o, t[:, MLA_ROPE:MLA_ROPE + IDX_DIM])
    ikrep_o[0] = jnp.concatenate([t[:, MLA_ROPE:MLA_ROPE + IDX_DIM]] * IDX_HEADS, axis=1).astype(BF16)

    xt = _dot_nt(wt_ref[...], xn)
    xt_o[0, 0:XT_DF, :] = xt[0:XT_DF]
    xt_o[0, XT_DF:XT_ROWS, :] = _log_sigmoid(xt[XT_DF:XT_ROWS] + bcol_ref[...])
    _store_value_block(dva_o, xt[XT_DV:XT_CV].astype(BF16), N_HEADS)
    _store_value_block(cva_o, xt[XT_CV:XT_IW].astype(BF16), 1)


def _inproj_call(x, tab, lw, tm, a_front, prev_narrow):
    n_prev = 0 if prev_narrow is None else prev_narrow[0].shape[0]
    bk, tk_, _ = x.shape
    front_blocks = a_front // tm
    grid = (tk_ // tm, bk)

    def const(shape):
        return pl.BlockSpec(shape, lambda i, b: (0,) * len(shape))

    def rows(w):
        return pl.BlockSpec((1, tm, w), lambda i, b: (b, i, 0))

    in_specs = [
        rows(D_MODEL),
        pl.BlockSpec((tm, TAB_W), lambda i, b: (i, 0)),
        const((1, D_MODEL)), const((D_MODEL, W_IN_COLS)), const((XT_ROWS, D_MODEL)), const((8, 1)),
        const((8, 256)), const((256, 256)), const((1, MLA_Q_LORA)), const((MLA_Q_LORA, 512)), const((256, 256)),
        const((1, 512)), const((1, MLA_KV_LORA)), const((8, 128)),
        pl.BlockSpec(memory_space=pl.ANY), pl.BlockSpec(memory_space=pl.ANY),
    ]
    assert len(in_specs) == N_INPROJ_IN
    if n_prev:
        in_specs += [pl.BlockSpec((n_prev, 1, tm, a.shape[-1]), lambda i, b: (0, b, i, 0)) for a in prev_narrow]
    band_zero = jnp.zeros((bk, a_front + tk_, 256), BF16)
    widths = [(256, BF16), (256, BF16), (256, BF16), (256, F32), (256, F32), (512, BF16), (128, F32), (256, BF16),
              (256, BF16), (64, F32), (64, F32), (128, F32), (32, F32), (32, F32), (256, BF16), (256, BF16), (256, F32),
              (256, F32)]
    out_shape = [jax.ShapeDtypeStruct((bk, tk_, w), dt) for w, dt in widths]
    out_specs = [rows(w) for w, _ in widths]
    for o in NARROW_OUTS:
        w = widths[o][0]
        out_shape[o] = jax.ShapeDtypeStruct((n_prev + 1, bk, tk_, w), F32)
        out_specs[o] = pl.BlockSpec((n_prev + 1, 1, tm, w), lambda i, b: (0, b, i, 0))
    for o in (1, 2):
        out_shape[o] = jax.ShapeDtypeStruct(band_zero.shape, BF16)
        out_specs[o] = pl.BlockSpec((1, tm, 256), lambda i, b: (b, i + front_blocks, 0))
    out_shape.append(jax.ShapeDtypeStruct((bk, XT_ROWS, tk_), F32))
    out_specs.append(pl.BlockSpec((1, XT_ROWS, tm), lambda i, b: (b, 0, i)))
    out_shape.append(jax.ShapeDtypeStruct((bk, tk_, 1024), BF16))
    out_specs.append(rows(1024))
    for heads in (N_HEADS, 1):
        out_shape.append(jax.ShapeDtypeStruct((bk, tk_ // tm, heads, V_ROWS, tm), BF16))
        out_specs.append(pl.BlockSpec((1, 1, heads, V_ROWS, tm), lambda i, b: (b, i, 0, 0, 0)))
    for _ in range(2):
        out_shape.append(jax.ShapeDtypeStruct((bk, tk_, 256), BF16))
        out_specs.append(rows(256))
    return pl.pallas_call(
        functools.partial(_inproj_kernel, n_prev=n_prev), grid=grid, in_specs=in_specs, out_specs=out_specs,
        out_shape=out_shape,
        input_output_aliases={14: 1, 15: 2}, compiler_params=_cparams(2), name="inproj",
    )(x, tab, lw['g_in'], lw['w_in'], lw['w_t'], lw['b_col'], lw['g4'], lw['hseg'], lw['g_cq'], lw['w_uq'],
      lw['segq'], lw['g_qb'], lw['g_ckv'], lw['srow'], band_zero, band_zero, *(prev_narrow or ()))


def _store_value_block(va_o, vt, heads):
    tk = vt.shape[1]
    for h in range(heads):
        va_o[0, 0, h, 0:HEAD_DIM, :] = vt[h * HEAD_DIM:(h + 1) * HEAD_DIM]
        va_o[0, 0, h, HEAD_DIM:V_ROWS, :] = jnp.ones((V_ROWS - HEAD_DIM, tk), BF16)


def _bkv_kernel(ckv_ref, kr_ref, wk_ref, wvt_ref, e_ref, segq_ref, gk_ref, kb_o, va_o):
    c = ckv_ref[0].astype(BF16)
    kn = _dot(c, wk_ref[...])
    segq = segq_ref[...]
    ms = jnp.concatenate([_seg_mean_sq(kn[:, :256], segq), _seg_mean_sq(kn[:, 256:], segq)], axis=1)
    kn = kn * lax.rsqrt(ms + EPS) * gk_ref[...]
    kr = _dot(kr_ref[0].astype(BF16), e_ref[...])
    kb_o[0] = (kn + kr).astype(BF16)
    vt = _dot_nt(wvt_ref[...], c).astype(BF16)
    _store_value_block(va_o, vt, N_HEADS)


def _bkv_call(ckv, kr, lw, tm):
    b, lp, _ = ckv.shape

    def const(shape):
        return pl.BlockSpec(shape, lambda bb, i: (0,) * len(shape))

    def rows(w):
        return pl.BlockSpec((1, tm, w), lambda bb, i: (bb, i, 0))

    return pl.pallas_call(
        _bkv_kernel, grid=(b, lp // tm),
        in_specs=[rows(128), rows(128), const((128, 512)), const((256, 128)), const((128, 512)), const((256, 256)),
                  const((1, 512))],
        out_specs=[rows(512), pl.BlockSpec((1, 1, N_HEADS, V_ROWS, tm), lambda bb, i: (bb, i, 0, 0, 0))],
        out_shape=[jax.ShapeDtypeStruct((b, lp, 512), BF16),
                   jax.ShapeDtypeStruct((b, lp // tm, N_HEADS, V_ROWS, tm), BF16)],
        compiler_params=_cparams(2), name="mla_kv",
    )(ckv, kr, lw['w_uk'], lw['w_uvt'], lw['e_kr'], lw['segq'], lw['g_kb'])


def _head_lane_id(width=256):
    return lax.broadcasted_iota(I32, (1, width), 1) // HEAD_DIM


def _keep_lanes(x, pred):
    return jnp.where(pred, x.astype(F32), 0.0).astype(BF16)


def _flash_run(nfull, nkb, nkb_max, *, qm, k_block, v_block, col_sub, row_add, mask_fn, s_refs, acc_ref, tq):
    s_a, s_b = s_refs

    track_max = nfull is not None

    def produce(j, s_out):
        jc = jnp.minimum(j, nkb_max - 1)
        tops = []
        for h in range(N_HEADS):
            s = _dot_nt(k_block(jc, h), qm[h])
            if col_sub is not None:
                s = s - col_sub(jc, h)
            s_out[h] = s
            tops.append(jnp.max(s, axis=0, keepdims=True) if track_max else jnp.zeros((1, tq), F32))
        return tuple(tops)

    def half(j, c, s_in, s_out, masked):
        ms, ls, tops = c
        next_tops = produce(j + 1, s_out)
        valid = mask_fn(j) if masked else None
        new_m, new_l = [], []
        for h in range(N_HEADS):
            t = s_in[h]
            if masked:
                t = jnp.where(valid, t, NEG)
                m_cur = jnp.max(t, axis=0, keepdims=True)
            else:
                m_cur = tops[h]
            if row_add is not None:
                m_cur = m_cur + row_add[h]
            m_new = jnp.maximum(ms[h], m_cur)
            alpha = jnp.exp2(ms[h] - m_new)
            off = m_new if row_add is None else m_new - row_add[h]
            p = jnp.exp2(t - off).astype(BF16)
            pv = _dot(v_block(j, h), p)
            r0 = h * HEAD_DIM
            acc_ref[r0:r0 + HEAD_DIM, :] = acc_ref[r0:r0 + HEAD_DIM, :] * alpha + pv[:HEAD_DIM]
            new_l.append(ls[h] * alpha + pv[HEAD_DIM:HEAD_DIM + 1])
            new_m.append(m_new)
        return tuple(new_m), tuple(new_l), next_tops

    def step(j, c, s_in, s_out, may_end):
        def run(c):
            if nfull is None:
                return half(j, c, s_in, s_out, True)
            return lax.cond(j >= nfull, lambda cc: half(j, cc, s_in, s_out, True),
                            lambda cc: half(j, cc, s_in, s_out, False), c)
        if not may_end:
            return run(c)
        return lax.cond(j >= nkb, lambda cc: cc, run, c)

    main_masked = nfull is None
    n_main = (nkb if main_masked else nfull) // 2

    def main_body(i, c):
        c = half(2 * i, c, s_a, s_b, main_masked)
        return half(2 * i + 1, c, s_b, s_a, main_masked)

    def tail_body(i, c):
        c = step(2 * i, c, s_a, s_b, False)
        return step(2 * i + 1, c, s_b, s_a, True)

    acc_ref[...] = jnp.zeros(acc_ref.shape, F32)
    init = (tuple(jnp.full((1, tq), NEG, F32) for _ in range(N_HEADS)),
            tuple(jnp.zeros((1, tq), F32) for _ in range(N_HEADS)), produce(0, s_a))
    c = lax.fori_loop(0, n_main, main_body, init)
    _, ls, _ = lax.fori_loop(n_main, (nkb + 1) // 2, tail_body, c)
    return ls


def _flash_scratch(tq, tk):
    return [pltpu.VMEM((N_HEADS, tk, tq), F32), pltpu.VMEM((N_HEADS, tk, tq), F32), pltpu.VMEM((256, tq), F32)]


def _flash_finish(o_ref, g_ref, acc_ref, ls):
    for h in range(N_HEADS):
        r0 = h * HEAD_DIM
        acc_ref[r0:r0 + HEAD_DIM, :] = acc_ref[r0:r0 + HEAD_DIM, :] / ls[h]
    y = acc_ref[...].T
    o_ref[0] = (y * g_ref[0].astype(F32)).astype(BF16)


A_WIN = A_BAND + 2 * CHUNK


def _attn_a_kernel(q_ref, k_ref, v_ref, bias_ref, g_ref, o_ref, *, row0, lo_valid, hi_valid, group):
    hid = _head_lane_id()
    for gi in range(group):
        c = pl.program_id(1) * group + gi
        start = pl.multiple_of(row0 + c * CHUNK, CHUNK)
        q = q_ref[0, gi * CHUNK:(gi + 1) * CHUNK, :]
        qs = jnp.concatenate([_keep_lanes(q, hid == h) for h in range(N_HEADS)], axis=0)
        kb = k_ref[0, pl.ds(start, A_WIN), :]
        vb = v_ref[0, pl.ds(start, A_WIN), :]
        s = _dot_nt(qs, kb) + bias_ref[...]
        row = start + lax.broadcasted_iota(I32, (1, A_WIN), 1)
        valid = (row >= lo_valid) & (row < hi_valid)
        s = jnp.where(valid, s, NEG)
        m = jnp.max(s, axis=-1, keepdims=True)
        p = jnp.where(valid, jnp.exp(s - m), 0.0)
        l = jnp.sum(p, axis=-1, keepdims=True)
        o = _dot(p.astype(BF16), vb) / l
        y = o[(N_HEADS - 1) * CHUNK:]
        for h in range(N_HEADS - 2, -1, -1):
            y = jnp.where(hid == h, o[h * CHUNK:(h + 1) * CHUNK], y)
        gate = g_ref[0, gi * CHUNK:(gi + 1) * CHUNK, :].astype(F32)
        o_ref[0, gi * CHUNK:(gi + 1) * CHUNK, :] = (y * gate).astype(BF16)


def _attn_a_call(q, kfull, vfull, bias, gates, row0, lo_valid, hi_valid, group):
    b, tp, _ = q.shape
    rows_kv = kfull.shape[1]
    assert row0 % CHUNK == 0 and row0 + q.shape[1] - CHUNK + A_WIN <= kfull.shape[1]
    kern = functools.partial(_attn_a_kernel, row0=row0, lo_valid=lo_valid, hi_valid=hi_valid, group=group)
    rows = CHUNK * group
    return pl.pallas_call(
        kern, grid=(b, tp // rows),
        in_specs=[pl.BlockSpec((1, rows, 256), lambda bb, c: (bb, c, 0)),
                  pl.BlockSpec((1, rows_kv, 256), lambda bb, c: (bb, 0, 0)),
                  pl.BlockSpec((1, rows_kv, 256), lambda bb, c: (bb, 0, 0)),
                  pl.BlockSpec((N_HEADS * CHUNK, A_WIN), lambda bb, c: (0, 0)),
                  pl.BlockSpec((1, rows, 256), lambda bb, c: (bb, c, 0))],
        out_specs=pl.BlockSpec((1, rows, 256), lambda bb, c: (bb, c, 0)),
        out_shape=jax.ShapeDtypeStruct((b, tp, 256), BF16),
        compiler_params=_cparams(2), name="attn_band",
    )(q, kfull, vfull, bias, gates)


def _attn_b_kernel(q_ref, k_ref, v_ref, g_ref, o_ref, sa_ref, sb_ref, acc_ref, *, p_len, l_len, tq, tk, nq):
    i = pl.program_id(1) if nq > 1 else 0
    q0 = p_len + i * tq
    lane = lax.broadcasted_iota(I32, (1, 256), 1)
    qm = []
    for h in range(N_HEADS):
        qg = q_ref[0, :, (h // 2) * 256:(h // 2) * 256 + 256]
        lo = (h % 2) * (MLA_NOPE + MLA_ROPE)
        qm.append(_keep_lanes(qg, (lane >= lo) & (lane < lo + MLA_NOPE + MLA_ROPE)))
    qchunk = (q0 + lax.broadcasted_iota(I32, (1, tq), 1)) // CHUNK
    nkb = jnp.minimum((q0 + tq + tk - 1) // tk, k_ref.shape[1] // tk)
    nfull = jnp.minimum(((q0 // CHUNK + 1) * CHUNK) // tk, nkb)

    def k_block(j, h):
        return k_ref[0, pl.ds(pl.multiple_of(j * tk, tk), tk), (h // 2) * 256:(h // 2) * 256 + 256]

    def mask_fn(j):
        kpos = j * tk + lax.broadcasted_iota(I32, (tk, 1), 0)
        return ((kpos // CHUNK) <= qchunk) & (kpos < l_len)

    ls = _flash_run(nfull, nkb, k_ref.shape[1] // tk, qm=qm, k_block=k_block, v_block=lambda j, h: v_ref[0, j, h],
                    col_sub=None, row_add=None, mask_fn=mask_fn, s_refs=(sa_ref, sb_ref), acc_ref=acc_ref, tq=tq)
    _flash_finish(o_ref, g_ref, acc_ref, ls)


def _attn_b_call(qb, kb, vaug, gates, p_len, l_len, tq, tk):
    b, t, _ = qb.shape
    lp = kb.shape[1]
    nq = t // tq
    kern = functools.partial(_attn_b_kernel, p_len=p_len, l_len=l_len, tq=tq, tk=tk, nq=nq)
    return pl.pallas_call(
        kern, grid=(b, nq),
        in_specs=[pl.BlockSpec((1, tq, 512), lambda bb, i: (bb, i, 0)),
                  pl.BlockSpec((1, lp, 512), lambda bb, i: (bb, 0, 0)),
                  pl.BlockSpec((1,) + vaug.shape[1:], lambda bb, i: (bb, 0, 0, 0, 0)),
                  pl.BlockSpec((1, tq, 256), lambda bb, i: (bb, i, 1))],
        out_specs=pl.BlockSpec((1, tq, 256), lambda bb, i: (bb, i, 0)),
        out_shape=jax.ShapeDtypeStruct((b, t, 256), BF16),
        scratch_shapes=_flash_scratch(tq, tk),
        compiler_params=_cparams(2), name="attn_latent",
    )(qb, kb, vaug, gates)


def _attn_c_kernel(q_ref, iq_ref, iw_ref, k_ref, v_ref, ik_ref, g_ref, o_ref, key_ref, top_ref, sa_ref, sb_ref,
                   acc_ref,
                   *, p_len, l_len, k_sel, tq, tk, nq):
    i = pl.program_id(1) if nq > 1 else 0
    q0 = p_len + i * tq
    nkb_max = k_ref.shape[1] // tk
    nkb = jnp.minimum((q0 + tq + tk - 1) // tk, nkb_max)
    qchunk = (q0 + lax.broadcasted_iota(I32, (1, tq), 1)) // CHUNK
    hid = _head_lane_id()

    iq = iq_ref[0]
    ihid = lax.broadcasted_iota(I32, (1, 256), 1) // IDX_DIM
    iqm = [_keep_lanes(iq, ihid == h) for h in range(IDX_HEADS)]
    iw = iw_ref[0] * (IDX_DIM ** -0.5 * IDX_HEADS ** -0.5)

    def score_body(edge):
        def body(j, carry):
            ks = pl.multiple_of(j * tk, tk)
            ikb = ik_ref[0, pl.ds(ks, tk), :]
            score = jnp.zeros((tk, tq), F32)
            for h in range(IDX_HEADS):
                score = score + iw[h:h + 1, :] * jnp.maximum(_dot_nt(ikb, iqm[h]), 0.0)
            if edge:
                kpos = ks + lax.broadcasted_iota(I32, (tk, 1), 0)
                score = jnp.where(((kpos // CHUNK) <= qchunk) & (kpos < l_len), score, -jnp.inf)
            key_ref[j] = score
            top_ref[j] = pltpu.bitcast(pltpu.bitcast(score, I32) & -65536, F32).astype(BF16)
            return carry
        return body

    n_free = jnp.minimum(((q0 // CHUNK + 1) * CHUNK) // tk, nkb)
    lax.fori_loop(0, n_free, score_body(False), 0)
    lax.fori_loop(n_free, nkb, score_body(True), 0)

    def cand_value(cand):
        bits = jnp.where(cand < 0, cand ^ 0x7FFFFFFF, cand)
        bits = jnp.where((bits > 0) & (bits < 0x00800000), 0x00800000, bits)
        return pltpu.bitcast(bits, F32)

    def count(pred_fn):
        def cbody(j, acc):
            pf = jnp.where(pred_fn(key_ref[j], j), 1.0, 0.0)
            for r in range(tk // 8):
                acc = acc + pf[r * 8:(r + 1) * 8]
            return acc
        acc = lax.fori_loop(0, nkb, cbody, jnp.zeros((8, tq), F32))
        return jnp.sum(acc, axis=0, keepdims=True)

    def count_top(cand_b):
        one, zero = jnp.ones((16, tq), BF16), jnp.zeros((16, tq), BF16)

        def cbody(j, acc):
            part = zero
            for r in range(tk // 16):
                part = part + jnp.where(top_ref[j, r * 16:(r + 1) * 16, :] >= cand_b, one, zero)
            return acc + part.astype(F32)
        acc = lax.fori_loop(0, nkb, cbody, jnp.zeros((16, tq), F32))
        return jnp.sum(acc, axis=0, keepdims=True)

    def top_body(it, c):
        t, n_ge = c
        cand = t + jnp.left_shift(jnp.int32(1), 15 - it)
        cbits = pltpu.bitcast(cand_value(cand * 65536), I32) & -65536
        cnt = count_top(jnp.broadcast_to(pltpu.bitcast(cbits, F32), (16, tq)).astype(BF16))
        ok = cnt >= k_sel
        return jnp.where(ok, cand, t), jnp.where(ok, cnt, n_ge)

    t16, n_ge = lax.fori_loop(0, 16, top_body, (jnp.full((1, tq), -32768, I32), jnp.zeros((1, tq), F32)))

    def bis_body(it, c):
        t, n_ge = c
        cand = t + jnp.left_shift(jnp.int32(1), 15 - it)
        cval = cand_value(cand)
        cnt = count(lambda kb, j: kb >= cval)
        ok = cnt >= k_sel
        return jnp.where(ok, cand, t), jnp.where(ok, cnt, n_ge)

    tcode, n_ge = lax.fori_loop(0, 16, bis_body, (t16 * 65536, n_ge))
    thr = cand_value(jnp.maximum(tcode, MOST_NEGATIVE_CODE))

    def idx_of(j):
        return j * tk + lax.broadcasted_iota(I32, (tk, tq), 0)

    def tie_cut(_):
        need = k_sel - count(lambda kb, j: kb > thr)

        def tie_body(it, jc):
            cand = jc + jnp.left_shift(jnp.int32(1), 12 - it)
            cnt = count(lambda kb, j: (kb == thr) & (idx_of(j) < cand))
            return jnp.where(cnt < need, cand, jc)

        return lax.fori_loop(0, 13, tie_body, jnp.zeros((1, tq), I32))

    has_tie = jnp.max(jnp.where(n_ge > k_sel, 1.0, 0.0)) > 0.0
    jcut = lax.cond(has_tie, tie_cut, lambda _: jnp.full((1, tq), 2 ** 30, I32), 0)

    q = q_ref[0]
    qm = [_keep_lanes(q, hid == h) for h in range(N_HEADS)]

    def attend(mask_fn):
        return _flash_run(None, nkb, nkb_max, qm=qm,
                          k_block=lambda j, h: k_ref[0, pl.ds(pl.multiple_of(j * tk, tk), tk), :],
                          v_block=lambda j, h: v_ref[0, j, 0], col_sub=None, row_add=None, mask_fn=mask_fn,
                          s_refs=(sa_ref, sb_ref), acc_ref=acc_ref, tq=tq)

    def with_ties(_):
        def mask_fn(j):
            kb = key_ref[j]
            return (kb > thr) | ((kb == thr) & (idx_of(j) <= jcut))
        return attend(mask_fn)

    ls = lax.cond(has_tie, with_ties, lambda _: attend(lambda j: key_ref[j] >= thr), 0)
    _flash_finish(o_ref, g_ref, acc_ref, ls)


def _attn_c_call(cq, ciq, iwt, krep, vaug, ikrep, gates, p_len, l_len, k_sel, tq, tk):
    b, t, _ = cq.shape
    lp = krep.shape[1]
    assert lp <= 8192
    nq = t // tq
    kern = functools.partial(_attn_c_kernel, p_len=p_len, l_len=l_len, k_sel=float(k_sel), tq=tq, tk=tk, nq=nq)
    return pl.pallas_call(
        kern, grid=(b, nq),
        in_specs=[pl.BlockSpec((1, tq, 256), lambda bb, i: (bb, i, 0)),
                  pl.BlockSpec((1, tq, 256), lambda bb, i: (bb, i, 0)),
                  pl.BlockSpec((1, 8, tq), lambda bb, i: (bb, 0, i)),
                  pl.BlockSpec((1, lp, 256), lambda bb, i: (bb, 0, 0)),
                  pl.BlockSpec((1,) + vaug.shape[1:], lambda bb, i: (bb, 0, 0, 0, 0)),
                  pl.BlockSpec((1, lp, 256), lambda bb, i: (bb, 0, 0)),
                  pl.BlockSpec((1, tq, 256), lambda bb, i: (bb, i, 2))],
        out_specs=pl.BlockSpec((1, tq, 256), lambda bb, i: (bb, i, 0)),
        out_shape=jax.ShapeDtypeStruct((b, t, 256), BF16),
        scratch_shapes=[pltpu.VMEM((lp // tk, tk, tq), F32), pltpu.VMEM((lp // tk, tk, tq), BF16)]
        + _flash_scratch(tq, tk),
        compiler_params=_cparams(2), name="attn_sparse",
    )(cq, ciq, iwt, krep, vaug, ikrep, gates)


def _attn_d_kernel(q_ref, k_ref, v_ref, lf_ref, g_ref, o_ref, fc_ref, fk_ref, sa_ref, sb_ref, acc_ref,
                   *, p_len, tq, tk, nq):
    i = pl.program_id(1) if nq > 1 else 0
    lp = k_ref.shape[1]
    nch = lp // 128

    def cumulate():
        x = lf_ref[0]
        lane = lax.broadcasted_iota(I32, (1, lp), 1)
        step = 1
        while step < lp:
            x = x + jnp.where(lane >= step, pltpu.roll(x, step, 1), 0.0)
            step *= 2
        x = x * LOG2E
        for c in range(nch):
            fc_ref[c] = x[:, c * 128:(c + 1) * 128]

        eye = lax.broadcasted_iota(I32, (128, 128), 0) == lax.broadcasted_iota(I32, (128, 128), 1)

        def spread(c, carry):
            rows = fc_ref[c]
            for h in range(N_HEADS):
                col = jnp.sum(jnp.where(eye, rows[h:h + 1, :], 0.0), axis=1, keepdims=True)
                fk_ref[h, pl.ds(pl.multiple_of(c * 128, 128), 128), :] = jnp.broadcast_to(col, (128, 128))
            return carry

        lax.fori_loop(0, nch, spread, 0)

    if nq > 1:
        pl.when(i == 0)(cumulate)
    else:
        cumulate()

    q0 = p_len + i * tq
    hid = _head_lane_id()
    q = q_ref[0]
    qm = [_keep_lanes(q, hid == h) for h in range(N_HEADS)]
    c0 = q0 // 128
    fq_rows = [fc_ref[c0 + c] for c in range(tq // 128)]
    fq = [jnp.concatenate([r[h:h + 1, :] for r in fq_rows], axis=1) for h in range(N_HEADS)]
    qpos = q0 + lax.broadcasted_iota(I32, (1, tq), 1)
    nkb = jnp.minimum((q0 + tq + tk - 1) // tk, lp // tk)
    nfull = jnp.minimum((q0 + 1) // tk, nkb)

    def col_sub(j, h):
        fk = fk_ref[h, pl.ds(pl.multiple_of(j * tk, tk), tk), :]
        return jnp.concatenate([fk] * (tq // 128), axis=1)

    def mask_fn(j):
        return (j * tk + lax.broadcasted_iota(I32, (tk, 1), 0)) <= qpos

    ls = _flash_run(nfull, nkb, lp // tk, qm=qm,
                    k_block=lambda j, h: k_ref[0, pl.ds(pl.multiple_of(j * tk, tk), tk), :],
                    v_block=lambda j, h: v_ref[0, j, h], col_sub=col_sub, row_add=fq, mask_fn=mask_fn,
                    s_refs=(sa_ref, sb_ref), acc_ref=acc_ref, tq=tq)
    _flash_finish(o_ref, g_ref, acc_ref, ls)


def _attn_d_call(dq, kd, vaug, lft, gates, p_len, tq, tk):
    b, t, _ = dq.shape
    lp = kd.shape[1]
    nq = t // tq
    assert p_len % 128 == 0 and tq % 128 == 0 and p_len + t <= lp
    kern = functools.partial(_attn_d_kernel, p_len=p_len, tq=tq, tk=tk, nq=nq)
    return pl.pallas_call(
        kern, grid=(b, nq),
        in_specs=[pl.BlockSpec((1, tq, 256), lambda bb, i: (bb, i, 0)),
                  pl.BlockSpec((1, lp, 256), lambda bb, i: (bb, 0, 0)),
                  pl.BlockSpec((1,) + vaug.shape[1:], lambda bb, i: (bb, 0, 0, 0, 0)),
                  pl.BlockSpec((1, 8, lp), lambda bb, i: (bb, 0, 0)),
                  pl.BlockSpec((1, tq, 256), lambda bb, i: (bb, i, 3))],
        out_specs=pl.BlockSpec((1, tq, 256), lambda bb, i: (bb, i, 0)),
        out_shape=jax.ShapeDtypeStruct((b, t, 256), BF16),
        scratch_shapes=[pltpu.VMEM((lp // 128, 8, 128), F32), pltpu.VMEM((N_HEADS, lp, 128), F32)]
        + _flash_scratch(tq, tk),
        compiler_params=_cparams(2), name="attn_forget",
    )(dq, kd, vaug, lft, gates)


def _out_kernel(ya_ref, yb_ref, yc_ref, yd_ref, x_ref, p_ref, wo_ref, gple_ref, wg_ref, wp_ref, o_ref):
    mixed = (_dot(ya_ref[0], wo_ref[0:256, :]) + _dot(yb_ref[0], wo_ref[256:512, :])
             + _dot(yc_ref[0], wo_ref[512:768, :]) + _dot(yd_ref[0], wo_ref[768:1024, :]))
    x1 = x_ref[0] + mixed
    ms = jnp.mean(x1 * x1, axis=-1, keepdims=True)
    xn = (x1 * lax.rsqrt(ms + EPS) * gple_ref[...]).astype(BF16)
    gate = 1.0 / (1.0 + jnp.exp(-_dot(xn, wg_ref[...])))
    o_ref[0] = x1 + gate * _dot(p_ref[0].astype(BF16), wp_ref[...])


def _out_call(ys, x, p, lw, tm):
    b, t, _ = x.shape

    def const(shape):
        return pl.BlockSpec(shape, lambda bb, i: (0,) * len(shape))

    def rows(w):
        return pl.BlockSpec((1, tm, w), lambda bb, i: (bb, i, 0))

    return pl.pallas_call(
        _out_kernel, grid=(b, t // tm),
        in_specs=[rows(256)] * 4 + [rows(D_MODEL), rows(PLE_DIM), const((1024, D_MODEL)), const((1, D_MODEL)),
                                    const((D_MODEL, D_MODEL)), const((PLE_DIM, D_MODEL))],
        out_specs=rows(D_MODEL),
        out_shape=jax.ShapeDtypeStruct((b, t, D_MODEL), F32),
        compiler_params=_cparams(2), name="out_proj",
    )(*ys, x, p, lw['w_out'], lw['g_ple'], lw['w_ple_gate'], lw['w_ple_proj'])


def _seg_matrix(segments):
    m = np.zeros((256, 256), np.float32)
    for lo, n in segments:
        m[lo:lo + n, lo:lo + n] = 1.0 / n
    return jnp.asarray(m, BF16)


_PAIR_SEGS = ((0, 64), (64, 32), (96, 64), (160, 32), (192, 64))


def _pair_cols(head):
    return (head // 2) * 256 + (head % 2) * (MLA_NOPE + MLA_ROPE)


def _prep_layer(i, g_in, w_in, g_qk, g_rope, a_rel_bias, b_g_cq, b_w_uq, b_g_ckv, b_w_ukv, d_b_f, w_out, g_ple,
                w_ple_gate, w_ple_proj):
    w = w_in[i]
    cols = []
    for n in _MY_ORDER:
        if n == 'pad56':
            cols.append(jnp.zeros((D_MODEL, 56), F32))
        else:
            o, wd = _REF_OFF[n]
            cols.append(w[:, o:o + wd])
    lw = {'w_in': jnp.concatenate(cols, axis=1).astype(BF16), 'g_in': g_in[i][None, :]}
    wt = [w[:, _REF_OFF[n][0]:_REF_OFF[n][0] + _REF_OFF[n][1]] for n in ('d_v', 'c_v', 'c_iw', 'd_f')]
    wt.append(jnp.zeros((D_MODEL, XT_ROWS - XT_DF - N_HEADS), F32))
    lw['w_t'] = jnp.concatenate(wt, axis=1).T.astype(BF16)
    lw['b_col'] = jnp.concatenate([d_b_f[i], jnp.zeros((8 - N_HEADS,), F32)])[:, None]
    g = g_qk[i]
    sc = HEAD_DIM ** -0.5
    rows = [jnp.tile(g[0], 4) * sc, jnp.tile(g[1], 4), jnp.tile(g[2], 4) * (sc * LOG2E),
            jnp.tile(g[4], 4) * (sc * LOG2E), jnp.tile(g[5], 4)]
    lw['g4'] = jnp.stack(rows + [jnp.zeros((256,), F32)] * 3)
    lw['hseg'] = _seg_matrix(tuple((h * 64, 64) for h in range(4)))
    lw['segq'] = _seg_matrix(_PAIR_SEGS)
    lw['g_cq'] = b_g_cq[i][None, :]
    lw['g_ckv'] = b_g_ckv[i][None, :]
    qscale = (MLA_NOPE + MLA_ROPE) ** -0.5 * LOG2E
    hw = MLA_NOPE + MLA_ROPE
    wuq, gqb, wuk, gkb, wuv = [], [], [], [], []
    e_kr = np.zeros((128, 512), np.float32)
    for h in range(N_HEADS):
        wuq.append(b_w_uq[i][:, h * hw:(h + 1) * hw])
        gqb += [g[6] * qscale, g_rope[i][0] * qscale]
        srck = h * (MLA_NOPE + HEAD_DIM)
        wuk += [b_w_ukv[i][:, srck:srck + MLA_NOPE], jnp.zeros((MLA_KV_LORA, MLA_ROPE), F32)]
        gkb += [g[7], jnp.zeros((MLA_ROPE,), F32)]
        wuv.append(b_w_ukv[i][:, srck + MLA_NOPE:srck + MLA_NOPE + HEAD_DIM])
        e_kr[np.arange(MLA_ROPE), _pair_cols(h) + MLA_NOPE + np.arange(MLA_ROPE)] = 1.0
        if h % 2 == 1:
            wuq.append(jnp.zeros((MLA_Q_LORA, 256 - 2 * hw), F32))
            wuk.append(jnp.zeros((MLA_KV_LORA, 256 - 2 * hw), F32))
            gqb.append(jnp.zeros((256 - 2 * hw,), F32))
            gkb.append(jnp.zeros((256 - 2 * hw,), F32))
    lw['w_uq'] = jnp.concatenate(wuq, axis=1).astype(BF16)
    lw['g_qb'] = jnp.concatenate(gqb)[None, :]
    lw['w_uk'] = jnp.concatenate(wuk, axis=1).astype(BF16)
    lw['g_kb'] = jnp.concatenate(gkb)[None, :]
    lw['e_kr'] = jnp.asarray(e_kr, BF16)
    lw['w_uvt'] = jnp.concatenate(wuv, axis=1).T.astype(BF16)
    one = jnp.ones((128,), F32)
    lw['srow'] = jnp.stack([jnp.concatenate([g[3], one[HEAD_DIM:]]), jnp.concatenate([g_rope[i][1], one[MLA_ROPE:]])]
                           + [one] * 6)
    ab = a_rel_bias[i]
    ext = jnp.concatenate([jnp.broadcast_to(ab[:, 2 * REL_CLIP:], (N_HEADS, A_WIN - REL_CLIP - 1)),
                           ab[:, REL_CLIP - CHUNK + 1:][:, ::-1]], axis=1)
    lw['a_bias'] = jnp.stack([ext[:, CHUNK - 1 - q:CHUNK - 1 - q + A_WIN] for q in range(CHUNK)],
                             axis=1).reshape(N_HEADS * CHUNK, A_WIN)
    lw['w_out'] = w_out[i].astype(BF16)
    lw['g_ple'] = g_ple[i][None, :]
    lw['w_ple_gate'] = w_ple_gate[i].astype(BF16)
    lw['w_ple_proj'] = w_ple_proj[i].astype(BF16)
    return lw


def _cos_sin(pos, rot_dim):
    half = rot_dim // 2
    inv = jnp.float32(ROPE_THETA) ** (-jnp.arange(half, dtype=F32) / half)
    ang = pos.astype(F32)[:, None] * inv[None, :]
    return [jnp.cos(ang), jnp.sin(ang)]


def _rot_placement():
    src = {MLA_ROPE: 1, ROT_DIM: 1 + MLA_ROPE}
    e = np.zeros((1 + MLA_ROPE + ROT_DIM, TAB_W), np.float32)
    sections = ((TAB_QB, 512, [_pair_cols(h) + MLA_NOPE for h in range(N_HEADS)], MLA_ROPE),
                (TAB_CQ, 256, [h * HEAD_DIM for h in range(N_HEADS)], ROT_DIM),
                (TAB_S1, 128, [0], ROT_DIM), (TAB_S2, 128, [0], MLA_ROPE))
    for base, width, starts, rot_dim in sections:
        half = rot_dim // 2
        e[0, base:base + width] = 1.0
        for st in starts:
            for i in range(half):
                c, s = src[rot_dim] + i, src[rot_dim] + half + i
                e[0, base + st + i] = e[0, base + st + half + i] = 0.0
                e[c, base + st + i] = e[c, base + st + half + i] = 1.0
                e[s, base + width + st + half + i] = 1.0
                e[s, base + 2 * width + st + i] = -1.0
    return jnp.asarray(e)


def _rot_tables(pos):
    src = jnp.concatenate([jnp.ones((pos.shape[0], 1), F32)] + _cos_sin(pos, MLA_ROPE) + _cos_sin(pos, ROT_DIM),
                          axis=1)
    hi = src.astype(BF16)
    lo = (src - hi.astype(F32)).astype(BF16)
    place = _rot_placement().astype(BF16)
    return jnp.dot(jnp.concatenate([hi, lo], axis=1), jnp.concatenate([place, place], axis=0),
                   preferred_element_type=F32)


def _pad_axis(a, size, axis):
    if a.shape[axis] == size:
        return a
    pads = [(0, 0)] * a.ndim
    pads[axis] = (0, size - a.shape[axis])
    return jnp.pad(a, pads)


def _cat_rows(past, new, rows):
    a = new if past is None else jnp.concatenate([past.astype(new.dtype), new], axis=1)
    return _pad_axis(a, rows, 1)


def _cat_lanes(past_t, new_t, lanes):
    a = new_t if past_t is None else jnp.concatenate([past_t.astype(new_t.dtype), new_t], axis=2)
    return _pad_axis(a, lanes, 2)


def _value_blocks(vt, tk):
    b, r, lp = vt.shape
    h = r // HEAD_DIM
    v = vt.astype(BF16).reshape(b, h, HEAD_DIM, lp)
    v = jnp.concatenate([v, jnp.ones((b, h, V_ROWS - HEAD_DIM, lp), BF16)], axis=2)
    return v.reshape(b, h, V_ROWS, lp // tk, tk).transpose(0, 3, 1, 2, 4)


def _layer(x, p, past, lw, tab, prev_narrow, *, fold_batch, tm, tq, tq_c, tk, a_group):
    b, t, _ = x.shape
    p_len = 0 if past is None else past[2].shape[1]
    l_len = p_len + t
    tqp = -(-t // tq) * tq
    tqp_c = -(-t // tq_c) * tq_c
    lp = -(-(p_len + max(tqp, tqp_c)) // tk) * tk

    xin = x.reshape(1, b * t, D_MODEL) if fold_batch else x
    a_direct = past is None and not fold_batch
    a_front = -(-(CHUNK + A_BAND) // tm) * tm if a_direct else 0
    outs = list(_inproj_call(xin, tab, lw, tm, a_front, prev_narrow))
    narrow = tuple(outs[o] for o in NARROW_OUTS)
    for o in NARROW_OUTS:
        outs[o] = None
    if fold_batch:
        xt = outs[18][0].reshape(XT_ROWS, b, t).transpose(1, 0, 2)
        outs = ([None if o is None else o.reshape((b, t) + o.shape[2:]) for o in outs[:18]]
                + [xt, outs[19].reshape(b, t, 1024), None, None] + [o.reshape(b, t, 256) for o in outs[22:24]])
    (aq, akb, avb, akf, avf, qb, ckvf, cq, ciq, c_k, c_v, s2, b_kr, c_ik, dq, dkb, dkf, dvf, xt, gates,
     dva, cva, krep_new, ikrep_new) = outs
    direct_values = past is None and not fold_batch and tm == tk

    lft = xt[:, XT_DF:XT_ROWS]
    d_lf = lft[:, :N_HEADS, :].transpose(0, 2, 1)
    state = (akf.reshape(b, t, N_HEADS, HEAD_DIM), avf.reshape(b, t, N_HEADS, HEAD_DIM), ckvf, b_kr, c_k, c_v, c_ik,
             dkf.reshape(b, t, N_HEADS, HEAD_DIM), dvf.reshape(b, t, N_HEADS, HEAD_DIM), d_lf)

    if past is None:
        pa_k = pa_v = pb_ckv = pb_kr = pc_k = pc_v = pc_ik = pd_k = pd_v = pd_lf = None
        pa = 0
    else:
        pa_k, pa_v, pb_ckv, pb_kr, pc_k, pc_v, pc_ik, pd_k, pd_v, pd_lf = past
        pa = pa_k.shape[1]
        pa_k = pa_k.reshape(b, pa, 256)
        pa_v = pa_v.reshape(b, pa, 256)
        pd_k = pd_k.reshape(b, p_len, 256)
        pd_v = pd_v.reshape(b, p_len, 256)

    tp = -(-t // (CHUNK * a_group)) * (CHUNK * a_group)
    if a_direct:
        assert tp == t
        a_keys, a_vals = akb, avb
        row0, front = a_front - (CHUNK + A_BAND), a_front
    else:
        front = CHUNK + A_BAND - pa
        a_rows = CHUNK + A_BAND + tp

        def band_src(pst, new):
            parts = [jnp.zeros((b, front, 256), BF16)]
            if pst is not None:
                parts.append(pst.astype(BF16))
            parts.append(new)
            return _pad_axis(jnp.concatenate(parts, axis=1), a_rows, 1)

        a_keys, a_vals, row0 = band_src(pa_k, akb), band_src(pa_v, avb), 0
    ya = _attn_a_call(_pad_axis(aq, tp, 1), a_keys, a_vals, lw['a_bias'], _pad_axis(gates, tp, 1), row0, front,
                      front + pa + t, a_group)[:, :t]

    gates_q = _pad_axis(gates, tqp, 1)

    ckv_all = _cat_rows(pb_ckv, ckvf, lp)
    kr_new = s2
    kr_past = None if past is None else _pad_axis(pb_kr, 128, 2)
    kr_all = _cat_rows(kr_past, kr_new, lp)
    kb, vba = _bkv_call(ckv_all, kr_all, lw, tk)
    yb = _attn_b_call(_pad_axis(qb, tqp, 1), kb, vba, gates_q, p_len, l_len, tq, tk)[:, :t]

    krep = _cat_rows(None if past is None else jnp.tile(pc_k.astype(BF16), (1, 1, N_HEADS)), krep_new, lp)
    ikrep = _cat_rows(None if past is None else jnp.tile(pc_ik.astype(BF16), (1, 1, IDX_HEADS)), ikrep_new, lp)
    if not direct_values:
        cva = _value_blocks(_cat_lanes(None if past is None else pc_v.transpose(0, 2, 1), xt[:, XT_CV:XT_IW], lp), tk)
    iwt = _pad_axis(xt[:, XT_IW:XT_DF], tqp_c, 2)
    k_sel = min(DSA_TOPK, l_len // 4)
    yc = _attn_c_call(_pad_axis(cq, tqp_c, 1), _pad_axis(ciq, tqp_c, 1), iwt, krep, cva, ikrep,
                      _pad_axis(gates, tqp_c, 1), p_len, l_len, k_sel, tq_c, tk)[:, :t]

    kd = _cat_rows(None if past is None else pd_k.astype(BF16), dkb, lp)
    if not direct_values:
        dva = _value_blocks(_cat_lanes(None if past is None else pd_v.transpose(0, 2, 1), xt[:, XT_DV:XT_CV], lp), tk)
    lf_past = None if past is None else _pad_axis(pd_lf.transpose(0, 2, 1), 8, 1)
    lf_all = _cat_lanes(lf_past, lft, lp)
    yd = _attn_d_call(_pad_axis(dq, tqp, 1), kd, dva, lf_all, gates_q, p_len, tq, tk)[:, :t]

    if fold_batch:
        x_new = _out_call([y.reshape(1, b * t, 256) for y in (ya, yb, yc, yd)], x.reshape(1, b * t, D_MODEL),
                          p.reshape(1, b * t, PLE_DIM), lw, tm).reshape(b, t, D_MODEL)
    else:
        x_new = _out_call([ya, yb, yc, yd], x, p, lw, tm)
    return x_new, state, narrow


def kernel(x_prompt, x_sample, cache_a_k, cache_a_v, cache_b_ckv, cache_b_krope, cache_c_k, cache_c_v, cache_c_idx_k,
           cache_d_k, cache_d_v, cache_d_logf, p_prompt, p_sample, g_in, w_in, g_qk, g_rope, a_rel_bias, b_g_cq,
           b_w_uq, b_g_ckv, b_w_ukv, d_b_f, w_out, g_ple, w_ple_gate, w_ple_proj):
    depth = w_in.shape[0]
    b, t = x_prompt.shape[:2]
    bs, ts = x_sample.shape[:2]
    past_len = cache_b_ckv.shape[2]
    tab_p = _rot_tables(jnp.arange(t))
    tab_s = jnp.tile(_rot_tables(past_len + jnp.arange(ts)), (bs, 1))
    xp, xs = x_prompt, x_sample
    states_p, states_s = [], []
    narrow_p = narrow_s = None
    for i in range(depth):
        lw = _prep_layer(i, g_in, w_in, g_qk, g_rope, a_rel_bias, b_g_cq, b_w_uq, b_g_ckv, b_w_ukv, d_b_f, w_out,
                         g_ple, w_ple_gate, w_ple_proj)
        xp, st_p, narrow_p = _layer(xp, p_prompt[i], None, lw, tab_p, narrow_p, fold_batch=False, tm=256, tq=512,
                                     tq_c=512, tk=256, a_group=8)
        past = (cache_a_k[i], cache_a_v[i], cache_b_ckv[i], cache_b_krope[i], cache_c_k[i], cache_c_v[i],
                cache_c_idx_k[i], cache_d_k[i], cache_d_v[i], cache_d_logf[i])
        xs, st_s, narrow_s = _layer(xs, p_sample[i], past, lw, tab_s, narrow_s, fold_batch=True, tm=bs * ts, tq=128,
                                     tq_c=128, tk=256, a_group=1)
        states_p.append(st_p)
        states_s.append(st_s)
    keep = min(A_BAND, t)
    states_p = [(st[0][:, t - keep:], st[1][:, t - keep:]) + tuple(st[2:]) for st in states_p]

    def assemble(states, narrow, bb, tt):
        out = [None if z[0] is None else jnp.stack(z) for z in zip(*states)]
        c_k, c_v, b_kr, c_ik = (a.reshape(depth, bb, tt, a.shape[-1]) for a in narrow)
        out[3], out[4], out[5], out[6] = b_kr, c_k, c_v, c_ik
        return out

    return (xp, xs, *assemble(states_p, narrow_p, b, t), *assemble(states_s, narrow_s, bs, ts))
```

```python
import functools

import numpy as np
import jax
import jax.numpy as jnp
from jax import lax
from jax.experimental import pallas as pl
from jax.experimental.pallas import tpu as pltpu

F32 = jnp.float32
BF16 = jnp.bfloat16
I32 = jnp.int32

D_MODEL = 1024
CHUNK = 64
EPS = 1e-6
HEAD_DIM = 64
N_HEADS = 4
GROUP_WIDTH = 256
ROT_DIM = 16
ROPE_THETA = 500000.0
A_BAND = 8 * CHUNK
REL_CLIP = 128
MLA_Q_LORA = 384
MLA_KV_LORA = 128
MLA_NOPE = 64
MLA_ROPE = 32
IDX_HEADS = 8
IDX_DIM = 32
DSA_TOPK = 256
PLE_DIM = 256

_REF_SPLITS = (
    ('a_q', 256), ('a_k', 256), ('a_v', 256), ('a_g', 256),
    ('b_cq', 384), ('b_ckv', 128), ('b_kr', 32), ('b_g', 256),
    ('c_q', 256), ('c_k', 64), ('c_v', 64), ('c_iq', 256), ('c_ik', 32), ('c_iw', 8), ('c_g', 256),
    ('d_q', 256), ('d_k', 256), ('d_v', 256), ('d_f', 4), ('d_g', 256),
)
_REF_OFF = {}
_o = 0
for _n, _w in _REF_SPLITS:
    _REF_OFF[_n] = (_o, _w)
    _o += _w

_MY_ORDER = ('a_q', 'a_k', 'a_v', 'a_g', 'b_cq', 'b_ckv', 'b_g', 'c_q', 'c_iq', 'c_g', 'd_q', 'd_k', 'd_v', 'd_g',
             'c_k', 'c_v', 'b_kr', 'c_ik', 'c_iw', 'pad56')
_MY_OFF = {}
_o = 0
for _n in _MY_ORDER:
    _w = 56 if _n == 'pad56' else _REF_OFF[_n][1]
    _MY_OFF[_n] = _o
    _o += _w
W_IN_COLS = _o
_PROJ_GROUPS = ((0, _MY_OFF['b_cq']), (_MY_OFF['b_cq'], _MY_OFF['c_q']), (_MY_OFF['c_q'], _MY_OFF['d_q']),
                (_MY_OFF['d_q'], _MY_OFF['c_k']), (_MY_OFF['c_k'], W_IN_COLS))

XT_DV, XT_CV, XT_IW, XT_DF, XT_ROWS = 0, 256, 320, 328, 336

TAB_QB, TAB_CQ, TAB_S1, TAB_S2 = 0, 1536, 2304, 2688
TAB_W = 3072

V_ROWS = HEAD_DIM + 16
NEG = -1e30
LOG2E = 1.4426950408889634
MOST_NEGATIVE_CODE = -2139095040
VMEM_LIMIT = 56 * 1024 * 1024


def _cparams(n_axes):
    return pltpu.CompilerParams(dimension_semantics=("arbitrary",) * n_axes, vmem_limit_bytes=VMEM_LIMIT)


def _dot(a, b):
    return jnp.dot(a, b, preferred_element_type=F32)


def _dot_nt(a, b):
    return lax.dot_general(a, b, (((1,), (1,)), ((), ())), preferred_element_type=F32)


def _seg_mean_sq(t, seg):
    sq = t * t
    hi = sq.astype(BF16)
    lo = (sq - hi.astype(F32)).astype(BF16)
    return _dot(hi, seg) + _dot(lo, seg)


def _rotate(t, cos, sin_up, sin_dn, half):
    w = t.shape[-1]
    return t * cos + pltpu.roll(t, half, 1) * sin_up + pltpu.roll(t, w - half, 1) * sin_dn


def _silu(g):
    return g * (1.0 / (1.0 + jnp.exp(-g)))


def _log_sigmoid(v):
    return jnp.minimum(v, 0.0) - jnp.log1p(jnp.exp(-jnp.abs(v)))


N_INPROJ_IN = 16
NARROW_OUTS = (9, 10, 12, 13, 16, 17)


def _inproj_kernel(*refs, n_prev):
    n_in = N_INPROJ_IN + (len(NARROW_OUTS) if n_prev else 0)
    _inproj_body(refs[N_INPROJ_IN:n_in], n_prev, *refs[:N_INPROJ_IN], *refs[n_in:])


def _inproj_body(prev, n_prev,
                 x_ref, tab_ref, gin_ref, w_ref, wt_ref, bcol_ref, g4_ref, hseg_ref, gcq_ref, wuq_ref, segq_ref,
                   gqb_ref, gckv_ref, srow_ref, akz_ref, avz_ref,
                   aq_o, akb_o, avb_o, akf_o, avf_o, qb_o, ckvf_o, cq_o, ciq_o, ck_o, cv_o, s2_o, bkr_o, cik_o,
                   dq_o, dkb_o, dkf_o, dvf_o, xt_o, gate_o, dva_o, cva_o, krep_o, ikrep_o):
    del akz_ref, avz_ref

    def store_narrow(slot, o_ref, rows):
        for layer in range(n_prev):
            o_ref[layer, 0] = prev[slot][layer, 0]
        o_ref[n_prev, 0] = rows

    x = x_ref[0]
    ms = jnp.mean(x * x, axis=-1, keepdims=True)
    xn = (x * lax.rsqrt(ms + EPS) * gin_ref[...]).astype(BF16)

    group_out = {}

    def proj(name, n):
        c0 = _MY_OFF[name]
        g0, g1 = next((a, b) for a, b in _PROJ_GROUPS if a <= c0 < b)
        if g0 not in group_out:
            group_out[g0] = _dot(xn, w_ref[:, g0:g1])
        return group_out[g0][:, c0 - g0:c0 - g0 + n]

    hseg = hseg_ref[...]

    def headnorm(t, row):
        return t * lax.rsqrt(_seg_mean_sq(t, hseg) + EPS) * g4_ref[row:row + 1, :]

    def fullnorm(t, g):
        return t * lax.rsqrt(jnp.mean(t * t, axis=-1, keepdims=True) + EPS) * g

    aq_o[0] = headnorm(proj('a_q', 256), 0).astype(BF16)
    ak = headnorm(proj('a_k', 256), 1)
    akf_o[0] = ak
    akb_o[0] = ak.astype(BF16)
    av = proj('a_v', 256)
    avf_o[0] = av
    avb_o[0] = av.astype(BF16)
    gate_o[0, :, 0:256] = _silu(proj('a_g', 256)).astype(BF16)

    cqn = fullnorm(proj('b_cq', MLA_Q_LORA), gcq_ref[...]).astype(BF16)
    qb = _dot(cqn, wuq_ref[...])
    segq = segq_ref[...]
    ms_q = jnp.concatenate([_seg_mean_sq(qb[:, :256], segq), _seg_mean_sq(qb[:, 256:], segq)], axis=1)
    qbn = qb * lax.rsqrt(ms_q + EPS) * gqb_ref[...]
    qbn = _rotate(qbn, tab_ref[:, TAB_QB:TAB_QB + 512], tab_ref[:, TAB_QB + 512:TAB_QB + 1024],
                  tab_ref[:, TAB_QB + 1024:TAB_QB + 1536], MLA_ROPE // 2)
    qb_o[0] = qbn.astype(BF16)
    ckvf_o[0] = fullnorm(proj('b_ckv', MLA_KV_LORA), gckv_ref[...])
    gate_o[0, :, 256:512] = _silu(proj('b_g', 256)).astype(BF16)

    cq = headnorm(proj('c_q', 256), 2)
    cq = _rotate(cq, tab_ref[:, TAB_CQ:TAB_CQ + 256], tab_ref[:, TAB_CQ + 256:TAB_CQ + 512],
                 tab_ref[:, TAB_CQ + 512:TAB_CQ + 768], ROT_DIM // 2)
    cq_o[0] = cq.astype(BF16)
    ciq_o[0] = proj('c_iq', 256).astype(BF16)
    gate_o[0, :, 512:768] = _silu(proj('c_g', 256)).astype(BF16)

    dq_o[0] = headnorm(proj('d_q', 256), 3).astype(BF16)
    dk = headnorm(proj('d_k', 256), 4)
    store_narrow(4, dkf_o, dk)
    dkb_o[0] = dk.astype(BF16)
    store_narrow(5, dvf_o, proj('d_v', 256))
    gate_o[0, :, 768:1024] = _silu(proj('d_g', 256)).astype(BF16)

    lane = lax.broadcasted_iota(I32, (1, 128), 1)
    t = proj('c_k', 128)
    m64 = lane < HEAD_DIM
    ms1 = jnp.sum(jnp.where(m64, t * t, 0.0), axis=-1, keepdims=True) * (1.0 / HEAD_DIM)
    t = jnp.where(m64, t * lax.rsqrt(ms1 + EPS), t) * srow_ref[0:1, :]
    t = _rotate(t, tab_ref[:, TAB_S1:TAB_S1 + 128], tab_ref[:, TAB_S1 + 128:TAB_S1 + 256],
                tab_ref[:, TAB_S1 + 256:TAB_S1 + 384], ROT_DIM // 2)
    store_narrow(0, ck_o, t[:, :HEAD_DIM])
    krep_o[0] = jnp.concatenate([t[:, :HEAD_DIM]] * N_HEADS, axis=1).astype(BF16)
    store_narrow(1, cv_o, t[:, HEAD_DIM:])
    t = proj('b_kr', 128)
    m32 = lane < MLA_ROPE
    ms2 = jnp.sum(jnp.where(m32, t * t, 0.0), axis=-1, keepdims=True) * (1.0 / MLA_ROPE)
    t = jnp.where(m32, t * lax.rsqrt(ms2 + EPS), t) * srow_ref[1:2, :]
    t = _rotate(t, tab_ref[:, TAB_S2:TAB_S2 + 128], tab_ref[:, TAB_S2 + 128:TAB_S2 + 256],
                tab_ref[:, TAB_S2 + 256:TAB_S2 + 384], MLA_ROPE // 2)
    s2_o[0] = t
    store_narrow(2, bkr_o, t[:, :MLA_ROPE])
    store_narrow(3, cik_o, t[:, MLA_ROPE:MLA_ROPE + IDX_DIM])
    ikrep_o[0] = jnp.concatenate([t[:, MLA_ROPE:MLA_ROPE + IDX_DIM]] * IDX_HEADS, axis=1).astype(BF16)

    xt = _dot_nt(wt_ref[...], xn)
    xt_o[0, 0:XT_DF, :] = xt[0:XT_DF]
    xt_o[0, XT_DF:XT_ROWS, :] = _log_sigmoid(xt[XT_DF:XT_ROWS] + bcol_ref[...])
    _store_value_block(dva_o, xt[XT_DV:XT_CV].astype(BF16), N_HEADS)
    _store_value_block(cva_o, xt[XT_CV:XT_IW].astype(BF16), 1)


def _inproj_call(x, tab, lw, tm, a_front, prev_narrow):
    n_prev = 0 if prev_narrow is None else prev_narrow[0].shape[0]
    bk, tk_, _ = x.shape
    front_blocks = a_front // tm
    grid = (tk_ // tm, bk)

    def const(shape):
        return pl.BlockSpec(shape, lambda i, b: (0,) * len(shape))

    def rows(w):
        return pl.BlockSpec((1, tm, w), lambda i, b: (b, i, 0))

    in_specs = [
        rows(D_MODEL),
        pl.BlockSpec((tm, TAB_W), lambda i, b: (i, 0)),
        const((1, D_MODEL)), const((D_MODEL, W_IN_COLS)), const((XT_ROWS, D_MODEL)), const((8, 1)),
        const((8, 256)), const((256, 256)), const((1, MLA_Q_LORA)), const((MLA_Q_LORA, 512)), const((256, 256)),
        const((1, 512)), const((1, MLA_KV_LORA)), const((8, 128)),
        pl.BlockSpec(memory_space=pl.ANY), pl.BlockSpec(memory_space=pl.ANY),
    ]
    assert len(in_specs) == N_INPROJ_IN
    if n_prev:
        in_specs += [pl.BlockSpec((n_prev, 1, tm, a.shape[-1]), lambda i, b: (0, b, i, 0)) for a in prev_narrow]
    band_zero = jnp.zeros((bk, a_front + tk_, 256), BF16)
    widths = [(256, BF16), (256, BF16), (256, BF16), (256, F32), (256, F32), (512, BF16), (128, F32), (256, BF16),
              (256, BF16), (64, F32), (64, F32), (128, F32), (32, F32), (32, F32), (256, BF16), (256, BF16), (256, F32),
              (256, F32)]
    out_shape = [jax.ShapeDtypeStruct((bk, tk_, w), dt) for w, dt in widths]
    out_specs = [rows(w) for w, _ in widths]
    for o in NARROW_OUTS:
        w = widths[o][0]
        out_shape[o] = jax.ShapeDtypeStruct((n_prev + 1, bk, tk_, w), F32)
        out_specs[o] = pl.BlockSpec((n_prev + 1, 1, tm, w), lambda i, b: (0, b, i, 0))
    for o in (1, 2):
        out_shape[o] = jax.ShapeDtypeStruct(band_zero.shape, BF16)
        out_specs[o] = pl.BlockSpec((1, tm, 256), lambda i, b: (b, i + front_blocks, 0))
    out_shape.append(jax.ShapeDtypeStruct((bk, XT_ROWS, tk_), F32))
    out_specs.append(pl.BlockSpec((1, XT_ROWS, tm), lambda i, b: (b, 0, i)))
    out_shape.append(jax.ShapeDtypeStruct((bk, tk_, 1024), BF16))
    out_specs.append(rows(1024))
    for heads in (N_HEADS, 1):
        out_shape.append(jax.ShapeDtypeStruct((bk, tk_ // tm, heads, V_ROWS, tm), BF16))
        out_specs.append(pl.BlockSpec((1, 1, heads, V_ROWS, tm), lambda i, b: (b, i, 0, 0, 0)))
    for _ in range(2):
        out_shape.append(jax.ShapeDtypeStruct((bk, tk_, 256), BF16))
        out_specs.append(rows(256))
    return pl.pallas_call(
        functools.partial(_inproj_kernel, n_prev=n_prev), grid=grid, in_specs=in_specs, out_specs=out_specs,
        out_shape=out_shape,
        input_output_aliases={14: 1, 15: 2}, compiler_params=_cparams(2), name="inproj",
    )(x, tab, lw['g_in'], lw['w_in'], lw['w_t'], lw['b_col'], lw['g4'], lw['hseg'], lw['g_cq'], lw['w_uq'],
      lw['segq'], lw['g_qb'], lw['g_ckv'], lw['srow'], band_zero, band_zero, *(prev_narrow or ()))


def _store_value_block(va_o, vt, heads):
    tk = vt.shape[1]
    for h in range(heads):
        va_o[0, 0, h, 0:HEAD_DIM, :] = vt[h * HEAD_DIM:(h + 1) * HEAD_DIM]
        va_o[0, 0, h, HEAD_DIM:V_ROWS, :] = jnp.ones((V_ROWS - HEAD_DIM, tk), BF16)


def _bkv_kernel(ckv_ref, kr_ref, wk_ref, wvt_ref, e_ref, segq_ref, gk_ref, kb_o, va_o):
    c = ckv_ref[0].astype(BF16)
    kn = _dot(c, wk_ref[...])
    segq = segq_ref[...]
    ms = jnp.concatenate([_seg_mean_sq(kn[:, :256], segq), _seg_mean_sq(kn[:, 256:], segq)], axis=1)
    kn = kn * lax.rsqrt(ms + EPS) * gk_ref[...]
    kr = _dot(kr_ref[0].astype(BF16), e_ref[...])
    kb_o[0] = (kn + kr).astype(BF16)
    vt = _dot_nt(wvt_ref[...], c).astype(BF16)
    _store_value_block(va_o, vt, N_HEADS)


def _bkv_call(ckv, kr, lw, tm):
    b, lp, _ = ckv.shape

    def const(shape):
        return pl.BlockSpec(shape, lambda bb, i: (0,) * len(shape))

    def rows(w):
        return pl.BlockSpec((1, tm, w), lambda bb, i: (bb, i, 0))

    return pl.pallas_call(
        _bkv_kernel, grid=(b, lp // tm),
        in_specs=[rows(128), rows(128), const((128, 512)), const((256, 128)), const((128, 512)), const((256, 256)),
                  const((1, 512))],
        out_specs=[rows(512), pl.BlockSpec((1, 1, N_HEADS, V_ROWS, tm), lambda bb, i: (bb, i, 0, 0, 0))],
        out_shape=[jax.ShapeDtypeStruct((b, lp, 512), BF16),
                   jax.ShapeDtypeStruct((b, lp // tm, N_HEADS, V_ROWS, tm), BF16)],
        compiler_params=_cparams(2), name="mla_kv",
    )(ckv, kr, lw['w_uk'], lw['w_uvt'], lw['e_kr'], lw['segq'], lw['g_kb'])


def _head_lane_id(width=256):
    return lax.broadcasted_iota(I32, (1, width), 1) // HEAD_DIM


def _keep_lanes(x, pred):
    return jnp.where(pred, x.astype(F32), 0.0).astype(BF16)


def _flash_run(nfull, nkb, nkb_max, *, qm, k_block, v_block, col_sub, row_add, mask_fn, s_refs, acc_ref, tq):
    s_a, s_b = s_refs

    track_max = nfull is not None

    def produce(j, s_out):
        jc = jnp.minimum(j, nkb_max - 1)
        tops = []
        for h in range(N_HEADS):
            s = _dot_nt(k_block(jc, h), qm[h])
            if col_sub is not None:
                s = s - col_sub(jc, h)
            s_out[h] = s
            tops.append(jnp.max(s, axis=0, keepdims=True) if track_max else jnp.zeros((1, tq), F32))
        return tuple(tops)

    def half(j, c, s_in, s_out, masked):
        ms, ls, tops = c
        next_tops = produce(j + 1, s_out)
        valid = mask_fn(j) if masked else None
        new_m, new_l = [], []
        for h in range(N_HEADS):
            t = s_in[h]
            if masked:
                t = jnp.where(valid, t, NEG)
                m_cur = jnp.max(t, axis=0, keepdims=True)
            else:
                m_cur = tops[h]
            if row_add is not None:
                m_cur = m_cur + row_add[h]
            m_new = jnp.maximum(ms[h], m_cur)
            alpha = jnp.exp2(ms[h] - m_new)
            off = m_new if row_add is None else m_new - row_add[h]
            p = jnp.exp2(t - off).astype(BF16)
            pv = _dot(v_block(j, h), p)
            r0 = h * HEAD_DIM
            acc_ref[r0:r0 + HEAD_DIM, :] = acc_ref[r0:r0 + HEAD_DIM, :] * alpha + pv[:HEAD_DIM]
            new_l.append(ls[h] * alpha + pv[HEAD_DIM:HEAD_DIM + 1])
            new_m.append(m_new)
        return tuple(new_m), tuple(new_l), next_tops

    def step(j, c, s_in, s_out, may_end):
        def run(c):
            if nfull is None:
                return half(j, c, s_in, s_out, True)
            return lax.cond(j >= nfull, lambda cc: half(j, cc, s_in, s_out, True),
                            lambda cc: half(j, cc, s_in, s_out, False), c)
        if not may_end:
            return run(c)
        return lax.cond(j >= nkb, lambda cc: cc, run, c)

    main_masked = nfull is None
    n_main = (nkb if main_masked else nfull) // 2

    def main_body(i, c):
        c = half(2 * i, c, s_a, s_b, main_masked)
        return half(2 * i + 1, c, s_b, s_a, main_masked)

    def tail_body(i, c):
        c = step(2 * i, c, s_a, s_b, False)
        return step(2 * i + 1, c, s_b, s_a, True)

    acc_ref[...] = jnp.zeros(acc_ref.shape, F32)
    init = (tuple(jnp.full((1, tq), NEG, F32) for _ in range(N_HEADS)),
            tuple(jnp.zeros((1, tq), F32) for _ in range(N_HEADS)), produce(0, s_a))
    c = lax.fori_loop(0, n_main, main_body, init)
    _, ls, _ = lax.fori_loop(n_main, (nkb + 1) // 2, tail_body, c)
    return ls


def _flash_scratch(tq, tk):
    return [pltpu.VMEM((N_HEADS, tk, tq), F32), pltpu.VMEM((N_HEADS, tk, tq), F32), pltpu.VMEM((256, tq), F32)]


def _flash_finish(o_ref, g_ref, acc_ref, ls):
    for h in range(N_HEADS):
        r0 = h * HEAD_DIM
        acc_ref[r0:r0 + HEAD_DIM, :] = acc_ref[r0:r0 + HEAD_DIM, :] / ls[h]
    y = acc_ref[...].T
    o_ref[0] = (y * g_ref[0].astype(F32)).astype(BF16)


A_WIN = A_BAND + 2 * CHUNK


def _attn_a_kernel(q_ref, k_ref, v_ref, bias_ref, g_ref, o_ref, *, row0, lo_valid, hi_valid, group):
    hid = _head_lane_id()
    for gi in range(group):
        c = pl.program_id(1) * group + gi
        start = pl.multiple_of(row0 + c * CHUNK, CHUNK)
        q = q_ref[0, gi * CHUNK:(gi + 1) * CHUNK, :]
        qs = jnp.concatenate([_keep_lanes(q, hid == h) for h in range(N_HEADS)], axis=0)
        kb = k_ref[0, pl.ds(start, A_WIN), :]
        vb = v_ref[0, pl.ds(start, A_WIN), :]
        s = _dot_nt(qs, kb) + bias_ref[...]
        row = start + lax.broadcasted_iota(I32, (1, A_WIN), 1)
        valid = (row >= lo_valid) & (row < hi_valid)
        s = jnp.where(valid, s, NEG)
        m = jnp.max(s, axis=-1, keepdims=True)
        p = jnp.where(valid, jnp.exp(s - m), 0.0)
        l = jnp.sum(p, axis=-1, keepdims=True)
        o = _dot(p.astype(BF16), vb) / l
        y = o[(N_HEADS - 1) * CHUNK:]
        for h in range(N_HEADS - 2, -1, -1):
            y = jnp.where(hid == h, o[h * CHUNK:(h + 1) * CHUNK], y)
        gate = g_ref[0, gi * CHUNK:(gi + 1) * CHUNK, :].astype(F32)
        o_ref[0, gi * CHUNK:(gi + 1) * CHUNK, :] = (y * gate).astype(BF16)


def _attn_a_call(q, kfull, vfull, bias, gates, row0, lo_valid, hi_valid, group):
    b, tp, _ = q.shape
    rows_kv = kfull.shape[1]
    assert row0 % CHUNK == 0 and row0 + q.shape[1] - CHUNK + A_WIN <= kfull.shape[1]
    kern = functools.partial(_attn_a_kernel, row0=row0, lo_valid=lo_valid, hi_valid=hi_valid, group=group)
    rows = CHUNK * group
    return pl.pallas_call(
        kern, grid=(b, tp // rows),
        in_specs=[pl.BlockSpec((1, rows, 256), lambda bb, c: (bb, c, 0)),
                  pl.BlockSpec((1, rows_kv, 256), lambda bb, c: (bb, 0, 0)),
                  pl.BlockSpec((1, rows_kv, 256), lambda bb, c: (bb, 0, 0)),
                  pl.BlockSpec((N_HEADS * CHUNK, A_WIN), lambda bb, c: (0, 0)),
                  pl.BlockSpec((1, rows, 256), lambda bb, c: (bb, c, 0))],
        out_specs=pl.BlockSpec((1, rows, 256), lambda bb, c: (bb, c, 0)),
        out_shape=jax.ShapeDtypeStruct((b, tp, 256), BF16),
        compiler_params=_cparams(2), name="attn_band",
    )(q, kfull, vfull, bias, gates)


def _attn_b_kernel(q_ref, k_ref, v_ref, g_ref, o_ref, sa_ref, sb_ref, acc_ref, *, p_len, l_len, tq, tk, nq):
    i = pl.program_id(1) if nq > 1 else 0
    q0 = p_len + i * tq
    lane = lax.broadcasted_iota(I32, (1, 256), 1)
    qm = []
    for h in range(N_HEADS):
        qg = q_ref[0, :, (h // 2) * 256:(h // 2) * 256 + 256]
        lo = (h % 2) * (MLA_NOPE + MLA_ROPE)
        qm.append(_keep_lanes(qg, (lane >= lo) & (lane < lo + MLA_NOPE + MLA_ROPE)))
    qchunk = (q0 + lax.broadcasted_iota(I32, (1, tq), 1)) // CHUNK
    nkb = jnp.minimum((q0 + tq + tk - 1) // tk, k_ref.shape[1] // tk)
    nfull = jnp.minimum(((q0 // CHUNK + 1) * CHUNK) // tk, nkb)

    def k_block(j, h):
        return k_ref[0, pl.ds(pl.multiple_of(j * tk, tk), tk), (h // 2) * 256:(h // 2) * 256 + 256]

    def mask_fn(j):
        kpos = j * tk + lax.broadcasted_iota(I32, (tk, 1), 0)
        return ((kpos // CHUNK) <= qchunk) & (kpos < l_len)

    ls = _flash_run(nfull, nkb, k_ref.shape[1] // tk, qm=qm, k_block=k_block, v_block=lambda j, h: v_ref[0, j, h],
                    col_sub=None, row_add=None, mask_fn=mask_fn, s_refs=(sa_ref, sb_ref), acc_ref=acc_ref, tq=tq)
    _flash_finish(o_ref, g_ref, acc_ref, ls)


def _attn_b_call(qb, kb, vaug, gates, p_len, l_len, tq, tk):
    b, t, _ = qb.shape
    lp = kb.shape[1]
    nq = t // tq
    kern = functools.partial(_attn_b_kernel, p_len=p_len, l_len=l_len, tq=tq, tk=tk, nq=nq)
    return pl.pallas_call(
        kern, grid=(b, nq),
        in_specs=[pl.BlockSpec((1, tq, 512), lambda bb, i: (bb, i, 0)),
                  pl.BlockSpec((1, lp, 512), lambda bb, i: (bb, 0, 0)),
                  pl.BlockSpec((1,) + vaug.shape[1:], lambda bb, i: (bb, 0, 0, 0, 0)),
                  pl.BlockSpec((1, tq, 256), lambda bb, i: (bb, i, 1))],
        out_specs=pl.BlockSpec((1, tq, 256), lambda bb, i: (bb, i, 0)),
        out_shape=jax.ShapeDtypeStruct((b, t, 256), BF16),
        scratch_shapes=_flash_scratch(tq, tk),
        compiler_params=_cparams(2), name="attn_latent",
    )(qb, kb, vaug, gates)


def _attn_c_kernel(q_ref, iq_ref, iw_ref, k_ref, v_ref, ik_ref, g_ref, o_ref, key_ref, top_ref, sa_ref, sb_ref,
                   acc_ref,
                   *, p_len, l_len, k_sel, tq, tk, nq):
    i = pl.program_id(1) if nq > 1 else 0
    q0 = p_len + i * tq
    nkb_max = k_ref.shape[1] // tk
    nkb = jnp.minimum((q0 + tq + tk - 1) // tk, nkb_max)
    qchunk = (q0 + lax.broadcasted_iota(I32, (1, tq), 1)) // CHUNK
    hid = _head_lane_id()

    iq = iq_ref[0]
    ihid = lax.broadcasted_iota(I32, (1, 256), 1) // IDX_DIM
    iqm = [_keep_lanes(iq, ihid == h) for h in range(IDX_HEADS)]
    iw = iw_ref[0] * (IDX_DIM ** -0.5 * IDX_HEADS ** -0.5)

    def score_body(edge):
        def body(j, carry):
            ks = pl.multiple_of(j * tk, tk)
            ikb = ik_ref[0, pl.ds(ks, tk), :]
            score = jnp.zeros((tk, tq), F32)
            for h in range(IDX_HEADS):
                score = score + iw[h:h + 1, :] * jnp.maximum(_dot_nt(ikb, iqm[h]), 0.0)
            if edge:
                kpos = ks + lax.broadcasted_iota(I32, (tk, 1), 0)
                score = jnp.where(((kpos // CHUNK) <= qchunk) & (kpos < l_len), score, -jnp.inf)
            key_ref[j] = score
            top_ref[j] = pltpu.bitcast(pltpu.bitcast(score, I32) & -65536, F32).astype(BF16)
            return carry
        return body

    n_free = jnp.minimum(((q0 // CHUNK + 1) * CHUNK) // tk, nkb)
    lax.fori_loop(0, n_free, score_body(False), 0)
    lax.fori_loop(n_free, nkb, score_body(True), 0)

    def cand_value(cand):
        bits = jnp.where(cand < 0, cand ^ 0x7FFFFFFF, cand)
        bits = jnp.where((bits > 0) & (bits < 0x00800000), 0x00800000, bits)
        return pltpu.bitcast(bits, F32)

    def count(pred_fn):
        def cbody(j, acc):
            pf = jnp.where(pred_fn(key_ref[j], j), 1.0, 0.0)
            for r in range(tk // 8):
                acc = acc + pf[r * 8:(r + 1) * 8]
            return acc
        acc = lax.fori_loop(0, nkb, cbody, jnp.zeros((8, tq), F32))
        return jnp.sum(acc, axis=0, keepdims=True)

    def count_top(cand_b):
        one, zero = jnp.ones((16, tq), BF16), jnp.zeros((16, tq), BF16)

        def cbody(j, acc):
            part = zero
            for r in range(tk // 16):
                part = part + jnp.where(top_ref[j, r * 16:(r + 1) * 16, :] >= cand_b, one, zero)
            return acc + part.astype(F32)
        acc = lax.fori_loop(0, nkb, cbody, jnp.zeros((16, tq), F32))
        return jnp.sum(acc, axis=0, keepdims=True)

    def top_body(it, c):
        t, n_ge = c
        cand = t + jnp.left_shift(jnp.int32(1), 15 - it)
        cbits = pltpu.bitcast(cand_value(cand * 65536), I32) & -65536
        cnt = count_top(jnp.broadcast_to(pltpu.bitcast(cbits, F32), (16, tq)).astype(BF16))
        ok = cnt >= k_sel
        return jnp.where(ok, cand, t), jnp.where(ok, cnt, n_ge)

    t16, n_ge = lax.fori_loop(0, 16, top_body, (jnp.full((1, tq), -32768, I32), jnp.zeros((1, tq), F32)))

    def bis_body(it, c):
        t, n_ge = c
        cand = t + jnp.left_shift(jnp.int32(1), 15 - it)
        cval = cand_value(cand)
        cnt = count(lambda kb, j: kb >= cval)
        ok = cnt >= k_sel
        return jnp.where(ok, cand, t), jnp.where(ok, cnt, n_ge)

    tcode, n_ge = lax.fori_loop(0, 16, bis_body, (t16 * 65536, n_ge))
    thr = cand_value(jnp.maximum(tcode, MOST_NEGATIVE_CODE))

    def idx_of(j):
        return j * tk + lax.broadcasted_iota(I32, (tk, tq), 0)

    def tie_cut(_):
        need = k_sel - count(lambda kb, j: kb > thr)

        def tie_body(it, jc):
            cand = jc + jnp.left_shift(jnp.int32(1), 12 - it)
            cnt = count(lambda kb, j: (kb == thr) & (idx_of(j) < cand))
            return jnp.where(cnt < need, cand, jc)

        return lax.fori_loop(0, 13, tie_body, jnp.zeros((1, tq), I32))

    has_tie = jnp.max(jnp.where(n_ge > k_sel, 1.0, 0.0)) > 0.0
    jcut = lax.cond(has_tie, tie_cut, lambda _: jnp.full((1, tq), 2 ** 30, I32), 0)

    q = q_ref[0]
    qm = [_keep_lanes(q, hid == h) for h in range(N_HEADS)]

    def attend(mask_fn):
        return _flash_run(None, nkb, nkb_max, qm=qm,
                          k_block=lambda j, h: k_ref[0, pl.ds(pl.multiple_of(j * tk, tk), tk), :],
                          v_block=lambda j, h: v_ref[0, j, 0], col_sub=None, row_add=None, mask_fn=mask_fn,
                          s_refs=(sa_ref, sb_ref), acc_ref=acc_ref, tq=tq)

    def with_ties(_):
        def mask_fn(j):
            kb = key_ref[j]
            return (kb > thr) | ((kb == thr) & (idx_of(j) <= jcut))
        return attend(mask_fn)

    ls = lax.cond(has_tie, with_ties, lambda _: attend(lambda j: key_ref[j] >= thr), 0)
    _flash_finish(o_ref, g_ref, acc_ref, ls)


def _attn_c_call(cq, ciq, iwt, krep, vaug, ikrep, gates, p_len, l_len, k_sel, tq, tk):
    b, t, _ = cq.shape
    lp = krep.shape[1]
    assert lp <= 8192
    nq = t // tq
    kern = functools.partial(_attn_c_kernel, p_len=p_len, l_len=l_len, k_sel=float(k_sel), tq=tq, tk=tk, nq=nq)
    return pl.pallas_call(
        kern, grid=(b, nq),
        in_specs=[pl.BlockSpec((1, tq, 256), lambda bb, i: (bb, i, 0)),
                  pl.BlockSpec((1, tq, 256), lambda bb, i: (bb, i, 0)),
                  pl.BlockSpec((1, 8, tq), lambda bb, i: (bb, 0, i)),
                  pl.BlockSpec((1, lp, 256), lambda bb, i: (bb, 0, 0)),
                  pl.BlockSpec((1,) + vaug.shape[1:], lambda bb, i: (bb, 0, 0, 0, 0)),
                  pl.BlockSpec((1, lp, 256), lambda bb, i: (bb, 0, 0)),
                  pl.BlockSpec((1, tq, 256), lambda bb, i: (bb, i, 2))],
        out_specs=pl.BlockSpec((1, tq, 256), lambda bb, i: (bb, i, 0)),
        out_shape=jax.ShapeDtypeStruct((b, t, 256), BF16),
        scratch_shapes=[pltpu.VMEM((lp // tk, tk, tq), F32), pltpu.VMEM((lp // tk, tk, tq), BF16)]
        + _flash_scratch(tq, tk),
        compiler_params=_cparams(2), name="attn_sparse",
    )(cq, ciq, iwt, krep, vaug, ikrep, gates)


def _attn_d_kernel(q_ref, k_ref, v_ref, lf_ref, g_ref, o_ref, fc_ref, fk_ref, sa_ref, sb_ref, acc_ref,
                   *, p_len, tq, tk, nq):
    i = pl.program_id(1) if nq > 1 else 0
    lp = k_ref.shape[1]
    nch = lp // 128

    def cumulate():
        x = lf_ref[0]
        lane = lax.broadcasted_iota(I32, (1, lp), 1)
        step = 1
        while step < lp:
            x = x + jnp.where(lane >= step, pltpu.roll(x, step, 1), 0.0)
            step *= 2
        x = x * LOG2E
        for c in range(nch):
            fc_ref[c] = x[:, c * 128:(c + 1) * 128]

        eye = lax.broadcasted_iota(I32, (128, 128), 0) == lax.broadcasted_iota(I32, (128, 128), 1)

        def spread(c, carry):
            rows = fc_ref[c]
            for h in range(N_HEADS):
                col = jnp.sum(jnp.where(eye, rows[h:h + 1, :], 0.0), axis=1, keepdims=True)
                fk_ref[h, pl.ds(pl.multiple_of(c * 128, 128), 128), :] = jnp.broadcast_to(col, (128, 128))
            return carry

        lax.fori_loop(0, nch, spread, 0)

    if nq > 1:
        pl.when(i == 0)(cumulate)
    else:
        cumulate()

    q0 = p_len + i * tq
    hid = _head_lane_id()
    q = q_ref[0]
    qm = [_keep_lanes(q, hid == h) for h in range(N_HEADS)]
    c0 = q0 // 128
    fq_rows = [fc_ref[c0 + c] for c in range(tq // 128)]
    fq = [jnp.concatenate([r[h:h + 1, :] for r in fq_rows], axis=1) for h in range(N_HEADS)]
    qpos = q0 + lax.broadcasted_iota(I32, (1, tq), 1)
    nkb = jnp.minimum((q0 + tq + tk - 1) // tk, lp // tk)
    nfull = jnp.minimum((q0 + 1) // tk, nkb)

    def col_sub(j, h):
        fk = fk_ref[h, pl.ds(pl.multiple_of(j * tk, tk), tk), :]
        return jnp.concatenate([fk] * (tq // 128), axis=1)

    def mask_fn(j):
        return (j * tk + lax.broadcasted_iota(I32, (tk, 1), 0)) <= qpos

    ls = _flash_run(nfull, nkb, lp // tk, qm=qm,
                    k_block=lambda j, h: k_ref[0, pl.ds(pl.multiple_of(j * tk, tk), tk), :],
                    v_block=lambda j, h: v_ref[0, j, h], col_sub=col_sub, row_add=fq, mask_fn=mask_fn,
                    s_refs=(sa_ref, sb_ref), acc_ref=acc_ref, tq=tq)
    _flash_finish(o_ref, g_ref, acc_ref, ls)


def _attn_d_call(dq, kd, vaug, lft, gates, p_len, tq, tk):
    b, t, _ = dq.shape
    lp = kd.shape[1]
    nq = t // tq
    assert p_len % 128 == 0 and tq % 128 == 0 and p_len + t <= lp
    kern = functools.partial(_attn_d_kernel, p_len=p_len, tq=tq, tk=tk, nq=nq)
    return pl.pallas_call(
        kern, grid=(b, nq),
        in_specs=[pl.BlockSpec((1, tq, 256), lambda bb, i: (bb, i, 0)),
                  pl.BlockSpec((1, lp, 256), lambda bb, i: (bb, 0, 0)),
                  pl.BlockSpec((1,) + vaug.shape[1:], lambda bb, i: (bb, 0, 0, 0, 0)),
                  pl.BlockSpec((1, 8, lp), lambda bb, i: (bb, 0, 0)),
                  pl.BlockSpec((1, tq, 256), lambda bb, i: (bb, i, 3))],
        out_specs=pl.BlockSpec((1, tq, 256), lambda bb, i: (bb, i, 0)),
        out_shape=jax.ShapeDtypeStruct((b, t, 256), BF16),
        scratch_shapes=[pltpu.VMEM((lp // 128, 8, 128), F32), pltpu.VMEM((N_HEADS, lp, 128), F32)]
        + _flash_scratch(tq, tk),
        compiler_params=_cparams(2), name="attn_forget",
    )(dq, kd, vaug, lft, gates)


def _out_kernel(ya_ref, yb_ref, yc_ref, yd_ref, x_ref, p_ref, wo_ref, gple_ref, wg_ref, wp_ref, o_ref):
    mixed = (_dot(ya_ref[0], wo_ref[0:256, :]) + _dot(yb_ref[0], wo_ref[256:512, :])
             + _dot(yc_ref[0], wo_ref[512:768, :]) + _dot(yd_ref[0], wo_ref[768:1024, :]))
    x1 = x_ref[0] + mixed
    ms = jnp.mean(x1 * x1, axis=-1, keepdims=True)
    xn = (x1 * lax.rsqrt(ms + EPS) * gple_ref[...]).astype(BF16)
    gate = 1.0 / (1.0 + jnp.exp(-_dot(xn, wg_ref[...])))
    o_ref[0] = x1 + gate * _dot(p_ref[0, 0].astype(BF16), wp_ref[...])


def _out_call(ys, x, p_all, layer, lw, tm):
    b, t, _ = x.shape

    def const(shape):
        return pl.BlockSpec(shape, lambda bb, i: (0,) * len(shape))

    def rows(w):
        return pl.BlockSpec((1, tm, w), lambda bb, i: (bb, i, 0))

    return pl.pallas_call(
        _out_kernel, grid=(b, t // tm),
        in_specs=[rows(256)] * 4 + [rows(D_MODEL), pl.BlockSpec((1, 1, tm, PLE_DIM), lambda bb, i: (layer, bb, i, 0)),
                                    const((1024, D_MODEL)), const((1, D_MODEL)),
                                    const((D_MODEL, D_MODEL)), const((PLE_DIM, D_MODEL))],
        out_specs=rows(D_MODEL),
        out_shape=jax.ShapeDtypeStruct((b, t, D_MODEL), F32),
        compiler_params=_cparams(2), name="out_proj",
    )(*ys, x, p_all, lw['w_out'], lw['g_ple'], lw['w_ple_gate'], lw['w_ple_proj'])


def _seg_matrix(segments):
    m = np.zeros((256, 256), np.float32)
    for lo, n in segments:
        m[lo:lo + n, lo:lo + n] = 1.0 / n
    return jnp.asarray(m, BF16)


_PAIR_SEGS = ((0, 64), (64, 32), (96, 64), (160, 32), (192, 64))


def _pair_cols(head):
    return (head // 2) * 256 + (head % 2) * (MLA_NOPE + MLA_ROPE)


def _prep_layer(i, g_in, w_in, g_qk, g_rope, a_rel_bias, b_g_cq, b_w_uq, b_g_ckv, b_w_ukv, d_b_f, w_out, g_ple,
                w_ple_gate, w_ple_proj):
    w = w_in[i]
    cols = []
    for n in _MY_ORDER:
        if n == 'pad56':
            cols.append(jnp.zeros((D_MODEL, 56), F32))
        else:
            o, wd = _REF_OFF[n]
            cols.append(w[:, o:o + wd])
    lw = {'w_in': jnp.concatenate(cols, axis=1).astype(BF16), 'g_in': g_in[i][None, :]}
    wt = [w[:, _REF_OFF[n][0]:_REF_OFF[n][0] + _REF_OFF[n][1]] for n in ('d_v', 'c_v', 'c_iw', 'd_f')]
    wt.append(jnp.zeros((D_MODEL, XT_ROWS - XT_DF - N_HEADS), F32))
    lw['w_t'] = jnp.concatenate(wt, axis=1).T.astype(BF16)
    lw['b_col'] = jnp.concatenate([d_b_f[i], jnp.zeros((8 - N_HEADS,), F32)])[:, None]
    g = g_qk[i]
    sc = HEAD_DIM ** -0.5
    rows = [jnp.tile(g[0], 4) * sc, jnp.tile(g[1], 4), jnp.tile(g[2], 4) * (sc * LOG2E),
            jnp.tile(g[4], 4) * (sc * LOG2E), jnp.tile(g[5], 4)]
    lw['g4'] = jnp.stack(rows + [jnp.zeros((256,), F32)] * 3)
    lw['hseg'] = _seg_matrix(tuple((h * 64, 64) for h in range(4)))
    lw['segq'] = _seg_matrix(_PAIR_SEGS)
    lw['g_cq'] = b_g_cq[i][None, :]
    lw['g_ckv'] = b_g_ckv[i][None, :]
    qscale = (MLA_NOPE + MLA_ROPE) ** -0.5 * LOG2E
    hw = MLA_NOPE + MLA_ROPE
    wuq, gqb, wuk, gkb, wuv = [], [], [], [], []
    e_kr = np.zeros((128, 512), np.float32)
    for h in range(N_HEADS):
        wuq.append(b_w_uq[i][:, h * hw:(h + 1) * hw])
        gqb += [g[6] * qscale, g_rope[i][0] * qscale]
        srck = h * (MLA_NOPE + HEAD_DIM)
        wuk += [b_w_ukv[i][:, srck:srck + MLA_NOPE], jnp.zeros((MLA_KV_LORA, MLA_ROPE), F32)]
        gkb += [g[7], jnp.zeros((MLA_ROPE,), F32)]
        wuv.append(b_w_ukv[i][:, srck + MLA_NOPE:srck + MLA_NOPE + HEAD_DIM])
        e_kr[np.arange(MLA_ROPE), _pair_cols(h) + MLA_NOPE + np.arange(MLA_ROPE)] = 1.0
        if h % 2 == 1:
            wuq.append(jnp.zeros((MLA_Q_LORA, 256 - 2 * hw), F32))
            wuk.append(jnp.zeros((MLA_KV_LORA, 256 - 2 * hw), F32))
            gqb.append(jnp.zeros((256 - 2 * hw,), F32))
            gkb.append(jnp.zeros((256 - 2 * hw,), F32))
    lw['w_uq'] = jnp.concatenate(wuq, axis=1).astype(BF16)
    lw['g_qb'] = jnp.concatenate(gqb)[None, :]
    lw['w_uk'] = jnp.concatenate(wuk, axis=1).astype(BF16)
    lw['g_kb'] = jnp.concatenate(gkb)[None, :]
    lw['e_kr'] = jnp.asarray(e_kr, BF16)
    lw['w_uvt'] = jnp.concatenate(wuv, axis=1).T.astype(BF16)
    one = jnp.ones((128,), F32)
    lw['srow'] = jnp.stack([jnp.concatenate([g[3], one[HEAD_DIM:]]), jnp.concatenate([g_rope[i][1], one[MLA_ROPE:]])]
                           + [one] * 6)
    ab = a_rel_bias[i]
    ext = jnp.concatenate([jnp.broadcast_to(ab[:, 2 * REL_CLIP:], (N_HEADS, A_WIN - REL_CLIP - 1)),
                           ab[:, REL_CLIP - CHUNK + 1:][:, ::-1]], axis=1)
    lw['a_bias'] = jnp.stack([ext[:, CHUNK - 1 - q:CHUNK - 1 - q + A_WIN] for q in range(CHUNK)],
                             axis=1).reshape(N_HEADS * CHUNK, A_WIN)
    lw['w_out'] = w_out[i].astype(BF16)
    lw['g_ple'] = g_ple[i][None, :]
    lw['w_ple_gate'] = w_ple_gate[i].astype(BF16)
    lw['w_ple_proj'] = w_ple_proj[i].astype(BF16)
    return lw


def _cos_sin(pos, rot_dim):
    half = rot_dim // 2
    inv = jnp.float32(ROPE_THETA) ** (-jnp.arange(half, dtype=F32) / half)
    ang = pos.astype(F32)[:, None] * inv[None, :]
    return [jnp.cos(ang), jnp.sin(ang)]


def _rot_placement():
    src = {MLA_ROPE: 1, ROT_DIM: 1 + MLA_ROPE}
    e = np.zeros((1 + MLA_ROPE + ROT_DIM, TAB_W), np.float32)
    sections = ((TAB_QB, 512, [_pair_cols(h) + MLA_NOPE for h in range(N_HEADS)], MLA_ROPE),
                (TAB_CQ, 256, [h * HEAD_DIM for h in range(N_HEADS)], ROT_DIM),
                (TAB_S1, 128, [0], ROT_DIM), (TAB_S2, 128, [0], MLA_ROPE))
    for base, width, starts, rot_dim in sections:
        half = rot_dim // 2
        e[0, base:base + width] = 1.0
        for st in starts:
            for i in range(half):
                c, s = src[rot_dim] + i, src[rot_dim] + half + i
                e[0, base + st + i] = e[0, base + st + half + i] = 0.0
                e[c, base + st + i] = e[c, base + st + half + i] = 1.0
                e[s, base + width + st + half + i] = 1.0
                e[s, base + 2 * width + st + i] = -1.0
    return jnp.asarray(e)


def _rot_tables(pos):
    src = jnp.concatenate([jnp.ones((pos.shape[0], 1), F32)] + _cos_sin(pos, MLA_ROPE) + _cos_sin(pos, ROT_DIM),
                          axis=1)
    hi = src.astype(BF16)
    lo = (src - hi.astype(F32)).astype(BF16)
    place = _rot_placement().astype(BF16)
    return jnp.dot(jnp.concatenate([hi, lo], axis=1), jnp.concatenate([place, place], axis=0),
                   preferred_element_type=F32)


def _pad_axis(a, size, axis):
    if a.shape[axis] == size:
        return a
    pads = [(0, 0)] * a.ndim
    pads[axis] = (0, size - a.shape[axis])
    return jnp.pad(a, pads)


def _cat_rows(past, new, rows):
    a = new if past is None else jnp.concatenate([past.astype(new.dtype), new], axis=1)
    return _pad_axis(a, rows, 1)


def _cat_lanes(past_t, new_t, lanes):
    a = new_t if past_t is None else jnp.concatenate([past_t.astype(new_t.dtype), new_t], axis=2)
    return _pad_axis(a, lanes, 2)


def _value_blocks(vt, tk):
    b, r, lp = vt.shape
    h = r // HEAD_DIM
    v = vt.astype(BF16).reshape(b, h, HEAD_DIM, lp)
    v = jnp.concatenate([v, jnp.ones((b, h, V_ROWS - HEAD_DIM, lp), BF16)], axis=2)
    return v.reshape(b, h, V_ROWS, lp // tk, tk).transpose(0, 3, 1, 2, 4)


def _layer(x, p_all, layer, past, lw, tab, prev_narrow, *, fold_batch, tm, tq, tq_c, tk, a_group):
    b, t, _ = x.shape
    p_len = 0 if past is None else past[2].shape[1]
    l_len = p_len + t
    tqp = -(-t // tq) * tq
    tqp_c = -(-t // tq_c) * tq_c
    lp = -(-(p_len + max(tqp, tqp_c)) // tk) * tk

    xin = x.reshape(1, b * t, D_MODEL) if fold_batch else x
    a_direct = past is None and not fold_batch
    a_front = -(-(CHUNK + A_BAND) // tm) * tm if a_direct else 0
    outs = list(_inproj_call(xin, tab, lw, tm, a_front, prev_narrow))
    narrow = tuple(outs[o] for o in NARROW_OUTS)
    for o in NARROW_OUTS:
        outs[o] = None
    if fold_batch:
        xt = outs[18][0].reshape(XT_ROWS, b, t).transpose(1, 0, 2)
        outs = ([None if o is None else o.reshape((b, t) + o.shape[2:]) for o in outs[:18]]
                + [xt, outs[19].reshape(b, t, 1024), None, None] + [o.reshape(b, t, 256) for o in outs[22:24]])
    (aq, akb, avb, akf, avf, qb, ckvf, cq, ciq, c_k, c_v, s2, b_kr, c_ik, dq, dkb, dkf, dvf, xt, gates,
     dva, cva, krep_new, ikrep_new) = outs
    direct_values = past is None and not fold_batch and tm == tk

    lft = xt[:, XT_DF:XT_ROWS]
    d_lf = lft[:, :N_HEADS, :].transpose(0, 2, 1)
    state = (akf.reshape(b, t, N_HEADS, HEAD_DIM), avf.reshape(b, t, N_HEADS, HEAD_DIM), ckvf, b_kr, c_k, c_v, c_ik,
             dkf, dvf, d_lf)

    if past is None:
        pa_k = pa_v = pb_ckv = pb_kr = pc_k = pc_v = pc_ik = pd_k = pd_v = pd_lf = None
        pa = 0
    else:
        pa_k, pa_v, pb_ckv, pb_kr, pc_k, pc_v, pc_ik, pd_k, pd_v, pd_lf = past
        pa = pa_k.shape[1]
        pa_k = pa_k.reshape(b, pa, 256)
        pa_v = pa_v.reshape(b, pa, 256)
        pd_k = pd_k.reshape(b, p_len, 256)
        pd_v = pd_v.reshape(b, p_len, 256)

    tp = -(-t // (CHUNK * a_group)) * (CHUNK * a_group)
    if a_direct:
        assert tp == t
        a_keys, a_vals = akb, avb
        row0, front = a_front - (CHUNK + A_BAND), a_front
    else:
        front = CHUNK + A_BAND - pa
        a_rows = CHUNK + A_BAND + tp

        def band_src(pst, new):
            parts = [jnp.zeros((b, front, 256), BF16)]
            if pst is not None:
                parts.append(pst.astype(BF16))
            parts.append(new)
            return _pad_axis(jnp.concatenate(parts, axis=1), a_rows, 1)

        a_keys, a_vals, row0 = band_src(pa_k, akb), band_src(pa_v, avb), 0
    ya = _attn_a_call(_pad_axis(aq, tp, 1), a_keys, a_vals, lw['a_bias'], _pad_axis(gates, tp, 1), row0, front,
                      front + pa + t, a_group)[:, :t]

    gates_q = _pad_axis(gates, tqp, 1)

    ckv_all = _cat_rows(pb_ckv, ckvf, lp)
    kr_new = s2
    kr_past = None if past is None else _pad_axis(pb_kr, 128, 2)
    kr_all = _cat_rows(kr_past, kr_new, lp)
    kb, vba = _bkv_call(ckv_all, kr_all, lw, tk)
    yb = _attn_b_call(_pad_axis(qb, tqp, 1), kb, vba, gates_q, p_len, l_len, tq, tk)[:, :t]

    krep = _cat_rows(None if past is None else jnp.tile(pc_k.astype(BF16), (1, 1, N_HEADS)), krep_new, lp)
    ikrep = _cat_rows(None if past is None else jnp.tile(pc_ik.astype(BF16), (1, 1, IDX_HEADS)), ikrep_new, lp)
    if not direct_values:
        cva = _value_blocks(_cat_lanes(None if past is None else pc_v.transpose(0, 2, 1), xt[:, XT_CV:XT_IW], lp), tk)
    iwt = _pad_axis(xt[:, XT_IW:XT_DF], tqp_c, 2)
    k_sel = min(DSA_TOPK, l_len // 4)
    yc = _attn_c_call(_pad_axis(cq, tqp_c, 1), _pad_axis(ciq, tqp_c, 1), iwt, krep, cva, ikrep,
                      _pad_axis(gates, tqp_c, 1), p_len, l_len, k_sel, tq_c, tk)[:, :t]

    kd = _cat_rows(None if past is None else pd_k.astype(BF16), dkb, lp)
    if not direct_values:
        dva = _value_blocks(_cat_lanes(None if past is None else pd_v.transpose(0, 2, 1), xt[:, XT_DV:XT_CV], lp), tk)
    lf_past = None if past is None else _pad_axis(pd_lf.transpose(0, 2, 1), 8, 1)
    lf_all = _cat_lanes(lf_past, lft, lp)
    yd = _attn_d_call(_pad_axis(dq, tqp, 1), kd, dva, lf_all, gates_q, p_len, tq, tk)[:, :t]

    if fold_batch:
        x_new = _out_call([y.reshape(1, b * t, 256) for y in (ya, yb, yc, yd)], x.reshape(1, b * t, D_MODEL),
                          p_all.reshape(p_all.shape[0], 1, b * t, PLE_DIM), layer, lw, tm).reshape(b, t, D_MODEL)
    else:
        x_new = _out_call([ya, yb, yc, yd], x, p_all, layer, lw, tm)
    return x_new, state, narrow


def kernel(x_prompt, x_sample, cache_a_k, cache_a_v, cache_b_ckv, cache_b_krope, cache_c_k, cache_c_v, cache_c_idx_k,
           cache_d_k, cache_d_v, cache_d_logf, p_prompt, p_sample, g_in, w_in, g_qk, g_rope, a_rel_bias, b_g_cq,
           b_w_uq, b_g_ckv, b_w_ukv, d_b_f, w_out, g_ple, w_ple_gate, w_ple_proj):
    depth = w_in.shape[0]
    b, t = x_prompt.shape[:2]
    bs, ts = x_sample.shape[:2]
    past_len = cache_b_ckv.shape[2]
    tab_p = _rot_tables(jnp.arange(t))
    tab_s = jnp.tile(_rot_tables(past_len + jnp.arange(ts)), (bs, 1))
    xp, xs = x_prompt, x_sample
    states_p, states_s = [], []
    narrow_p = narrow_s = None
    for i in range(depth):
        lw = _prep_layer(i, g_in, w_in, g_qk, g_rope, a_rel_bias, b_g_cq, b_w_uq, b_g_ckv, b_w_ukv, d_b_f, w_out,
                         g_ple, w_ple_gate, w_ple_proj)
        xp, st_p, narrow_p = _layer(xp, p_prompt, i, None, lw, tab_p, narrow_p, fold_batch=False, tm=256, tq=512,
                                     tq_c=512, tk=256, a_group=8)
        past = (cache_a_k[i], cache_a_v[i], cache_b_ckv[i], cache_b_krope[i], cache_c_k[i], cache_c_v[i],
                cache_c_idx_k[i], cache_d_k[i], cache_d_v[i], cache_d_logf[i])
        xs, st_s, narrow_s = _layer(xs, p_sample, i, past, lw, tab_s, narrow_s, fold_batch=True, tm=bs * ts, tq=128,
                                     tq_c=128, tk=256, a_group=1)
        states_p.append(st_p)
        states_s.append(st_s)
    keep = min(A_BAND, t)
    states_p = [(st[0][:, t - keep:], st[1][:, t - keep:]) + tuple(st[2:]) for st in states_p]

    def assemble(states, narrow, bb, tt):
        out = [None if z[0] is None else jnp.stack(z) for z in zip(*states)]
        c_k, c_v, b_kr, c_ik, d_k, d_v = (a.reshape(depth, bb, tt, a.shape[-1]) for a in narrow)
        out[3], out[4], out[5], out[6] = b_kr, c_k, c_v, c_ik
        out[7] = d_k.reshape(depth, bb, tt, N_HEADS, HEAD_DIM)
        out[8] = d_v.reshape(depth, bb, tt, N_HEADS, HEAD_DIM)
        return out

    return (xp, xs, *assemble(states_p, narrow_p, b, t), *assemble(states_s, narrow_s, bs, ts))
```

```python
import functools

import numpy as np
import jax
import jax.numpy as jnp
from jax import lax
from jax.experimental import pallas as pl
from jax.experimental.pallas import tpu as pltpu

F32 = jnp.float32
BF16 = jnp.bfloat16
I32 = jnp.int32

D_MODEL = 1024
CHUNK = 64
EPS = 1e-6
HEAD_DIM = 64
N_HEADS = 4
GROUP_WIDTH = 256
ROT_DIM = 16
ROPE_THETA = 500000.0
A_BAND = 8 * CHUNK
REL_CLIP = 128
MLA_Q_LORA = 384
MLA_KV_LORA = 128
MLA_NOPE = 64
MLA_ROPE = 32
IDX_HEADS = 8
IDX_DIM = 32
DSA_TOPK = 256
PLE_DIM = 256

_REF_SPLITS = (
    ('a_q', 256), ('a_k', 256), ('a_v', 256), ('a_g', 256),
    ('b_cq', 384), ('b_ckv', 128), ('b_kr', 32), ('b_g', 256),
    ('c_q', 256), ('c_k', 64), ('c_v', 64), ('c_iq', 256), ('c_ik', 32), ('c_iw', 8), ('c_g', 256),
    ('d_q', 256), ('d_k', 256), ('d_v', 256), ('d_f', 4), ('d_g', 256),
)
_REF_OFF = {}
_o = 0
for _n, _w in _REF_SPLITS:
    _REF_OFF[_n] = (_o, _w)
    _o += _w

_MY_ORDER = ('a_q', 'a_k', 'a_v', 'a_g', 'b_cq', 'b_ckv', 'b_g', 'c_q', 'c_iq', 'c_g', 'd_q', 'd_k', 'd_v', 'd_g',
             'c_k', 'c_v', 'b_kr', 'c_ik', 'c_iw', 'pad56')
_MY_OFF = {}
_o = 0
for _n in _MY_ORDER:
    _w = 56 if _n == 'pad56' else _REF_OFF[_n][1]
    _MY_OFF[_n] = _o
    _o += _w
W_IN_COLS = _o
_PROJ_GROUPS = ((0, _MY_OFF['b_cq']), (_MY_OFF['b_cq'], _MY_OFF['c_q']), (_MY_OFF['c_q'], _MY_OFF['d_q']),
                (_MY_OFF['d_q'], _MY_OFF['c_k']), (_MY_OFF['c_k'], W_IN_COLS))

XT_DV, XT_CV, XT_IW, XT_DF, XT_ROWS = 0, 256, 320, 328, 336

TAB_QB, TAB_CQ, TAB_S1, TAB_S2 = 0, 1536, 2304, 2688
TAB_W = 3072

V_ROWS = HEAD_DIM + 16
NEG = -1e30
LOG2E = 1.4426950408889634
MOST_NEGATIVE_CODE = -2139095040
VMEM_LIMIT = 56 * 1024 * 1024


def _cparams(n_axes):
    return pltpu.CompilerParams(dimension_semantics=("arbitrary",) * n_axes, vmem_limit_bytes=VMEM_LIMIT)


def _dot(a, b):
    return jnp.dot(a, b, preferred_element_type=F32)


def _dot_nt(a, b):
    return lax.dot_general(a, b, (((1,), (1,)), ((), ())), preferred_element_type=F32)


def _seg_mean_sq(t, seg):
    sq = t * t
    hi = sq.astype(BF16)
    lo = (sq - hi.astype(F32)).astype(BF16)
    return _dot(hi, seg) + _dot(lo, seg)


def _rotate(t, cos, sin_up, sin_dn, half):
    w = t.shape[-1]
    return t * cos + pltpu.roll(t, half, 1) * sin_up + pltpu.roll(t, w - half, 1) * sin_dn


def _silu(g):
    return g * (1.0 / (1.0 + jnp.exp(-g)))


def _log_sigmoid(v):
    return jnp.minimum(v, 0.0) - jnp.log1p(jnp.exp(-jnp.abs(v)))


N_INPROJ_IN = 16
NARROW_OUTS = (9, 10, 12, 13, 16, 17)


def _inproj_kernel(*refs, n_prev):
    n_in = N_INPROJ_IN + (len(NARROW_OUTS) if n_prev else 0)
    _inproj_body(refs[N_INPROJ_IN:n_in], n_prev, *refs[:N_INPROJ_IN], *refs[n_in:])


def _inproj_body(prev, n_prev,
                 x_ref, tab_ref, gin_ref, w_ref, wt_ref, bcol_ref, g4_ref, hseg_ref, gcq_ref, wuq_ref, segq_ref,
                   gqb_ref, gckv_ref, srow_ref, akz_ref, avz_ref,
                   aq_o, akb_o, avb_o, akf_o, avf_o, qb_o, ckvf_o, cq_o, ciq_o, ck_o, cv_o, s2_o, bkr_o, cik_o,
                   dq_o, dkb_o, dkf_o, dvf_o, xt_o, gate_o, dva_o, cva_o, krep_o, ikrep_o):
    del akz_ref, avz_ref

    def store_narrow(slot, o_ref, rows):
        for layer in range(n_prev):
            o_ref[layer, 0] = prev[slot][layer, 0]
        o_ref[n_prev, 0] = rows

    x = x_ref[0]
    ms = jnp.mean(x * x, axis=-1, keepdims=True)
    xn = (x * lax.rsqrt(ms + EPS) * gin_ref[...]).astype(BF16)

    group_out = {}

    def proj(name, n):
        c0 = _MY_OFF[name]
        g0, g1 = next((a, b) for a, b in _PROJ_GROUPS if a <= c0 < b)
        if g0 not in group_out:
            group_out[g0] = _dot(xn, w_ref[:, g0:g1])
        return group_out[g0][:, c0 - g0:c0 - g0 + n]

    hseg = hseg_ref[...]

    def headnorm(t, row):
        return t * lax.rsqrt(_seg_mean_sq(t, hseg) + EPS) * g4_ref[row:row + 1, :]

    def fullnorm(t, g):
        return t * lax.rsqrt(jnp.mean(t * t, axis=-1, keepdims=True) + EPS) * g

    aq_o[0] = headnorm(proj('a_q', 256), 0).astype(BF16)
    ak = headnorm(proj('a_k', 256), 1)
    akf_o[0] = ak
    akb_o[0] = ak.astype(BF16)
    av = proj('a_v', 256)
    avf_o[0] = av
    avb_o[0] = av.astype(BF16)
    gate_o[0, :, 0:256] = _silu(proj('a_g', 256)).astype(BF16)

    cqn = fullnorm(proj('b_cq', MLA_Q_LORA), gcq_ref[...]).astype(BF16)
    qb = _dot(cqn, wuq_ref[...])
    segq = segq_ref[...]
    ms_q = jnp.concatenate([_seg_mean_sq(qb[:, :256], segq), _seg_mean_sq(qb[:, 256:], segq)], axis=1)
    qbn = qb * lax.rsqrt(ms_q + EPS) * gqb_ref[...]
    qbn = _rotate(qbn, tab_ref[:, TAB_QB:TAB_QB + 512], tab_ref[:, TAB_QB + 512:TAB_QB + 1024],
                  tab_ref[:, TAB_QB + 1024:TAB_QB + 1536], MLA_ROPE // 2)
    qb_o[0] = qbn.astype(BF16)
    ckvf_o[0] = fullnorm(proj('b_ckv', MLA_KV_LORA), gckv_ref[...])
    gate_o[0, :, 256:512] = _silu(proj('b_g', 256)).astype(BF16)

    cq = headnorm(proj('c_q', 256), 2)
    cq = _rotate(cq, tab_ref[:, TAB_CQ:TAB_CQ + 256], tab_ref[:, TAB_CQ + 256:TAB_CQ + 512],
                 tab_ref[:, TAB_CQ + 512:TAB_CQ + 768], ROT_DIM // 2)
    cq_o[0] = cq.astype(BF16)
    ciq_o[0] = proj('c_iq', 256).astype(BF16)
    gate_o[0, :, 512:768] = _silu(proj('c_g', 256)).astype(BF16)

    dq_o[0] = headnorm(proj('d_q', 256), 3).astype(BF16)
    dk = headnorm(proj('d_k', 256), 4)
    store_narrow(4, dkf_o, dk)
    dkb_o[0] = dk.astype(BF16)
    store_narrow(5, dvf_o, proj('d_v', 256))
    gate_o[0, :, 768:1024] = _silu(proj('d_g', 256)).astype(BF16)

    lane = lax.broadcasted_iota(I32, (1, 128), 1)
    t = proj('c_k', 128)
    m64 = lane < HEAD_DIM
    ms1 = jnp.sum(jnp.where(m64, t * t, 0.0), axis=-1, keepdims=True) * (1.0 / HEAD_DIM)
    t = jnp.where(m64, t * lax.rsqrt(ms1 + EPS), t) * srow_ref[0:1, :]
    t = _rotate(t, tab_ref[:, TAB_S1:TAB_S1 + 128], tab_ref[:, TAB_S1 + 128:TAB_S1 + 256],
                tab_ref[:, TAB_S1 + 256:TAB_S1 + 384], ROT_DIM // 2)
    store_narrow(0, ck_o, t[:, :HEAD_DIM])
    krep_o[0] = jnp.concatenate([t[:, :HEAD_DIM]] * N_HEADS, axis=1).astype(BF16)
    store_narrow(1, cv_o, t[:, HEAD_DIM:])
    t = proj('b_kr', 128)
    m32 = lane < MLA_ROPE
    ms2 = jnp.sum(jnp.where(m32, t * t, 0.0), axis=-1, keepdims=True) * (1.0 / MLA_ROPE)
    t = jnp.where(m32, t * lax.rsqrt(ms2 + EPS), t) * srow_ref[1:2, :]
    t = _rotate(t, tab_ref[:, TAB_S2:TAB_S2 + 128], tab_ref[:, TAB_S2 + 128:TAB_S2 + 256],
                tab_ref[:, TAB_S2 + 256:TAB_S2 + 384], MLA_ROPE // 2)
    s2_o[0] = t
    store_narrow(2, bkr_o, t[:, :MLA_ROPE])
    store_narrow(3, cik_o, t[:, MLA_ROPE:MLA_ROPE + IDX_DIM])
    ikrep_o[0] = jnp.concatenate([t[:, MLA_ROPE:MLA_ROPE + IDX_DIM]] * IDX_HEADS, axis=1).astype(BF16)

    xt = _dot_nt(wt_ref[...], xn)
    xt_o[0, 0:XT_DF, :] = xt[0:XT_DF]
    xt_o[0, XT_DF:XT_ROWS, :] = _log_sigmoid(xt[XT_DF:XT_ROWS] + bcol_ref[...])
    _store_value_block(dva_o, xt[XT_DV:XT_CV].astype(BF16), N_HEADS)
    _store_value_block(cva_o, xt[XT_CV:XT_IW].astype(BF16), 1)


def _inproj_call(x, tab, lw, tm, a_front, prev_narrow):
    n_prev = 0 if prev_narrow is None else prev_narrow[0].shape[0]
    bk, tk_, _ = x.shape
    front_blocks = a_front // tm
    grid = (tk_ // tm, bk)

    def const(shape):
        return pl.BlockSpec(shape, lambda i, b: (0,) * len(shape))

    def rows(w):
        return pl.BlockSpec((1, tm, w), lambda i, b: (b, i, 0))

    in_specs = [
        rows(D_MODEL),
        pl.BlockSpec((tm, TAB_W), lambda i, b: (i, 0)),
        const((1, D_MODEL)), const((D_MODEL, W_IN_COLS)), const((XT_ROWS, D_MODEL)), const((8, 1)),
        const((8, 256)), const((256, 256)), const((1, MLA_Q_LORA)), const((MLA_Q_LORA, 512)), const((256, 256)),
        const((1, 512)), const((1, MLA_KV_LORA)), const((8, 128)),
        pl.BlockSpec(memory_space=pl.ANY), pl.BlockSpec(memory_space=pl.ANY),
    ]
    assert len(in_specs) == N_INPROJ_IN
    if n_prev:
        in_specs += [pl.BlockSpec((n_prev, 1, tm, a.shape[-1]), lambda i, b: (0, b, i, 0)) for a in prev_narrow]
    band_zero = jnp.zeros((bk, a_front + tk_, 256), BF16)
    widths = [(256, BF16), (256, BF16), (256, BF16), (256, F32), (256, F32), (512, BF16), (128, F32), (256, BF16),
              (256, BF16), (64, F32), (64, F32), (128, F32), (32, F32), (32, F32), (256, BF16), (256, BF16), (256, F32),
              (256, F32)]
    out_shape = [jax.ShapeDtypeStruct((bk, tk_, w), dt) for w, dt in widths]
    out_specs = [rows(w) for w, _ in widths]
    for o in NARROW_OUTS:
        w = widths[o][0]
        out_shape[o] = jax.ShapeDtypeStruct((n_prev + 1, bk, tk_, w), F32)
        out_specs[o] = pl.BlockSpec((n_prev + 1, 1, tm, w), lambda i, b: (0, b, i, 0))
    for o in (1, 2):
        out_shape[o] = jax.ShapeDtypeStruct(band_zero.shape, BF16)
        out_specs[o] = pl.BlockSpec((1, tm, 256), lambda i, b: (b, i + front_blocks, 0))
    out_shape.append(jax.ShapeDtypeStruct((bk, XT_ROWS, tk_), F32))
    out_specs.append(pl.BlockSpec((1, XT_ROWS, tm), lambda i, b: (b, 0, i)))
    out_shape.append(jax.ShapeDtypeStruct((bk, tk_, 1024), BF16))
    out_specs.append(rows(1024))
    for heads in (N_HEADS, 1):
        out_shape.append(jax.ShapeDtypeStruct((bk, tk_ // tm, heads, V_ROWS, tm), BF16))
        out_specs.append(pl.BlockSpec((1, 1, heads, V_ROWS, tm), lambda i, b: (b, i, 0, 0, 0)))
    for _ in range(2):
        out_shape.append(jax.ShapeDtypeStruct((bk, tk_, 256), BF16))
        out_specs.append(rows(256))
    return pl.pallas_call(
        functools.partial(_inproj_kernel, n_prev=n_prev), grid=grid, in_specs=in_specs, out_specs=out_specs,
        out_shape=out_shape,
        input_output_aliases={14: 1, 15: 2}, compiler_params=_cparams(2), name="inproj",
    )(x, tab, lw['g_in'], lw['w_in'], lw['w_t'], lw['b_col'], lw['g4'], lw['hseg'], lw['g_cq'], lw['w_uq'],
      lw['segq'], lw['g_qb'], lw['g_ckv'], lw['srow'], band_zero, band_zero, *(prev_narrow or ()))


def _store_value_block(va_o, vt, heads):
    tk = vt.shape[1]
    for h in range(heads):
        va_o[0, 0, h, 0:HEAD_DIM, :] = vt[h * HEAD_DIM:(h + 1) * HEAD_DIM]
        va_o[0, 0, h, HEAD_DIM:V_ROWS, :] = jnp.ones((V_ROWS - HEAD_DIM, tk), BF16)


def _bkv_kernel(ckv_ref, kr_ref, wk_ref, wvt_ref, e_ref, segq_ref, gk_ref, kb_o, va_o):
    c = ckv_ref[0].astype(BF16)
    kn = _dot(c, wk_ref[...])
    segq = segq_ref[...]
    ms = jnp.concatenate([_seg_mean_sq(kn[:, :256], segq), _seg_mean_sq(kn[:, 256:], segq)], axis=1)
    kn = kn * lax.rsqrt(ms + EPS) * gk_ref[...]
    kr = _dot(kr_ref[0].astype(BF16), e_ref[...])
    kb_o[0] = (kn + kr).astype(BF16)
    vt = _dot_nt(wvt_ref[...], c).astype(BF16)
    _store_value_block(va_o, vt, N_HEADS)


def _bkv_call(ckv, kr, lw, tm):
    b, lp, _ = ckv.shape

    def const(shape):
        return pl.BlockSpec(shape, lambda bb, i: (0,) * len(shape))

    def rows(w):
        return pl.BlockSpec((1, tm, w), lambda bb, i: (bb, i, 0))

    return pl.pallas_call(
        _bkv_kernel, grid=(b, lp // tm),
        in_specs=[rows(128), rows(128), const((128, 512)), const((256, 128)), const((128, 512)), const((256, 256)),
                  const((1, 512))],
        out_specs=[rows(512), pl.BlockSpec((1, 1, N_HEADS, V_ROWS, tm), lambda bb, i: (bb, i, 0, 0, 0))],
        out_shape=[jax.ShapeDtypeStruct((b, lp, 512), BF16),
                   jax.ShapeDtypeStruct((b, lp // tm, N_HEADS, V_ROWS, tm), BF16)],
        compiler_params=_cparams(2), name="mla_kv",
    )(ckv, kr, lw['w_uk'], lw['w_uvt'], lw['e_kr'], lw['segq'], lw['g_kb'])


def _head_lane_id(width=256):
    return lax.broadcasted_iota(I32, (1, width), 1) // HEAD_DIM


def _keep_lanes(x, pred):
    return jnp.where(pred, x.astype(F32), 0.0).astype(BF16)


def _flash_run(nfull, nkb, nkb_max, *, qm, k_block, v_block, col_sub, row_add, mask_fn, s_refs, acc_ref, tq):
    s_a, s_b = s_refs

    track_max = nfull is not None

    def produce(j, s_out):
        jc = jnp.minimum(j, nkb_max - 1)
        tops = []
        for h in range(N_HEADS):
            s = _dot_nt(k_block(jc, h), qm[h])
            if col_sub is not None:
                s = s - col_sub(jc, h)
            s_out[h] = s
            tops.append(jnp.max(s, axis=0, keepdims=True) if track_max else jnp.zeros((1, tq), F32))
        return tuple(tops)

    def half(j, c, s_in, s_out, masked):
        ms, ls, tops = c
        next_tops = produce(j + 1, s_out)
        valid = mask_fn(j) if masked else None
        new_m, new_l = [], []
        for h in range(N_HEADS):
            t = s_in[h]
            if masked:
                t = jnp.where(valid, t, NEG)
                m_cur = jnp.max(t, axis=0, keepdims=True)
            else:
                m_cur = tops[h]
            if row_add is not None:
                m_cur = m_cur + row_add[h]
            m_new = jnp.maximum(ms[h], m_cur)
            alpha = jnp.exp2(ms[h] - m_new)
            off = m_new if row_add is None else m_new - row_add[h]
            p = jnp.exp2(t - off).astype(BF16)
            pv = _dot(v_block(j, h), p)
            r0 = h * HEAD_DIM
            acc_ref[r0:r0 + HEAD_DIM, :] = acc_ref[r0:r0 + HEAD_DIM, :] * alpha + pv[:HEAD_DIM]
            new_l.append(ls[h] * alpha + pv[HEAD_DIM:HEAD_DIM + 1])
            new_m.append(m_new)
        return tuple(new_m), tuple(new_l), next_tops

    def step(j, c, s_in, s_out, may_end):
        def run(c):
            if nfull is None:
                return half(j, c, s_in, s_out, True)
            return lax.cond(j >= nfull, lambda cc: half(j, cc, s_in, s_out, True),
                            lambda cc: half(j, cc, s_in, s_out, False), c)
        if not may_end:
            return run(c)
        return lax.cond(j >= nkb, lambda cc: cc, run, c)

    main_masked = nfull is None
    n_main = (nkb if main_masked else nfull) // 2

    def main_body(i, c):
        c = half(2 * i, c, s_a, s_b, main_masked)
        return half(2 * i + 1, c, s_b, s_a, main_masked)

    def tail_body(i, c):
        c = step(2 * i, c, s_a, s_b, False)
        return step(2 * i + 1, c, s_b, s_a, True)

    acc_ref[...] = jnp.zeros(acc_ref.shape, F32)
    init = (tuple(jnp.full((1, tq), NEG, F32) for _ in range(N_HEADS)),
            tuple(jnp.zeros((1, tq), F32) for _ in range(N_HEADS)), produce(0, s_a))
    c = lax.fori_loop(0, n_main, main_body, init)
    _, ls, _ = lax.fori_loop(n_main, (nkb + 1) // 2, tail_body, c)
    return ls


def _flash_scratch(tq, tk):
    return [pltpu.VMEM((N_HEADS, tk, tq), F32), pltpu.VMEM((N_HEADS, tk, tq), F32), pltpu.VMEM((256, tq), F32)]


def _flash_finish(o_ref, g_ref, acc_ref, ls):
    for h in range(N_HEADS):
        r0 = h * HEAD_DIM
        acc_ref[r0:r0 + HEAD_DIM, :] = acc_ref[r0:r0 + HEAD_DIM, :] / ls[h]
    y = acc_ref[...].T
    o_ref[0] = (y * g_ref[0].astype(F32)).astype(BF16)


A_WIN = A_BAND + 2 * CHUNK


def _attn_a_kernel(q_ref, k_ref, v_ref, bias_ref, g_ref, o_ref, *, row0, lo_valid, hi_valid, group):
    hid = _head_lane_id()
    for gi in range(group):
        c = pl.program_id(1) * group + gi
        start = pl.multiple_of(row0 + c * CHUNK, CHUNK)
        q = q_ref[0, gi * CHUNK:(gi + 1) * CHUNK, :]
        qs = jnp.concatenate([_keep_lanes(q, hid == h) for h in range(N_HEADS)], axis=0)
        kb = k_ref[0, pl.ds(start, A_WIN), :]
        vb = v_ref[0, pl.ds(start, A_WIN), :]
        s = _dot_nt(qs, kb) + bias_ref[...]
        row = start + lax.broadcasted_iota(I32, (1, A_WIN), 1)
        valid = (row >= lo_valid) & (row < hi_valid)
        s = jnp.where(valid, s, NEG)
        m = jnp.max(s, axis=-1, keepdims=True)
        p = jnp.where(valid, jnp.exp(s - m), 0.0)
        l = jnp.sum(p, axis=-1, keepdims=True)
        o = _dot(p.astype(BF16), vb) / l
        y = o[(N_HEADS - 1) * CHUNK:]
        for h in range(N_HEADS - 2, -1, -1):
            y = jnp.where(hid == h, o[h * CHUNK:(h + 1) * CHUNK], y)
        gate = g_ref[0, gi * CHUNK:(gi + 1) * CHUNK, :].astype(F32)
        o_ref[0, gi * CHUNK:(gi + 1) * CHUNK, :] = (y * gate).astype(BF16)


def _attn_a_call(q, kfull, vfull, bias, gates, row0, lo_valid, hi_valid, group):
    b, tp, _ = q.shape
    rows_kv = kfull.shape[1]
    assert row0 % CHUNK == 0 and row0 + q.shape[1] - CHUNK + A_WIN <= kfull.shape[1]
    kern = functools.partial(_attn_a_kernel, row0=row0, lo_valid=lo_valid, hi_valid=hi_valid, group=group)
    rows = CHUNK * group
    return pl.pallas_call(
        kern, grid=(b, tp // rows),
        in_specs=[pl.BlockSpec((1, rows, 256), lambda bb, c: (bb, c, 0)),
                  pl.BlockSpec((1, rows_kv, 256), lambda bb, c: (bb, 0, 0)),
                  pl.BlockSpec((1, rows_kv, 256), lambda bb, c: (bb, 0, 0)),
                  pl.BlockSpec((N_HEADS * CHUNK, A_WIN), lambda bb, c: (0, 0)),
                  pl.BlockSpec((1, rows, 256), lambda bb, c: (bb, c, 0))],
        out_specs=pl.BlockSpec((1, rows, 256), lambda bb, c: (bb, c, 0)),
        out_shape=jax.ShapeDtypeStruct((b, tp, 256), BF16),
        compiler_params=_cparams(2), name="attn_band",
    )(q, kfull, vfull, bias, gates)


def _attn_b_kernel(q_ref, k_ref, v_ref, g_ref, o_ref, sa_ref, sb_ref, acc_ref, *, p_len, l_len, tq, tk, nq):
    i = pl.program_id(1) if nq > 1 else 0
    q0 = p_len + i * tq
    lane = lax.broadcasted_iota(I32, (1, 256), 1)
    qm = []
    for h in range(N_HEADS):
        qg = q_ref[0, :, (h // 2) * 256:(h // 2) * 256 + 256]
        lo = (h % 2) * (MLA_NOPE + MLA_ROPE)
        qm.append(_keep_lanes(qg, (lane >= lo) & (lane < lo + MLA_NOPE + MLA_ROPE)))
    qchunk = (q0 + lax.broadcasted_iota(I32, (1, tq), 1)) // CHUNK
    nkb = jnp.minimum((q0 + tq + tk - 1) // tk, k_ref.shape[1] // tk)
    nfull = jnp.minimum(((q0 // CHUNK + 1) * CHUNK) // tk, nkb)

    def k_block(j, h):
        return k_ref[0, pl.ds(pl.multiple_of(j * tk, tk), tk), (h // 2) * 256:(h // 2) * 256 + 256]

    def mask_fn(j):
        kpos = j * tk + lax.broadcasted_iota(I32, (tk, 1), 0)
        return ((kpos // CHUNK) <= qchunk) & (kpos < l_len)

    ls = _flash_run(nfull, nkb, k_ref.shape[1] // tk, qm=qm, k_block=k_block, v_block=lambda j, h: v_ref[0, j, h],
                    col_sub=None, row_add=None, mask_fn=mask_fn, s_refs=(sa_ref, sb_ref), acc_ref=acc_ref, tq=tq)
    _flash_finish(o_ref, g_ref, acc_ref, ls)


def _attn_b_call(qb, kb, vaug, gates, p_len, l_len, tq, tk):
    b, t, _ = qb.shape
    lp = kb.shape[1]
    nq = t // tq
    kern = functools.partial(_attn_b_kernel, p_len=p_len, l_len=l_len, tq=tq, tk=tk, nq=nq)
    return pl.pallas_call(
        kern, grid=(b, nq),
        in_specs=[pl.BlockSpec((1, tq, 512), lambda bb, i: (bb, i, 0)),
                  pl.BlockSpec((1, lp, 512), lambda bb, i: (bb, 0, 0)),
                  pl.BlockSpec((1,) + vaug.shape[1:], lambda bb, i: (bb, 0, 0, 0, 0)),
                  pl.BlockSpec((1, tq, 256), lambda bb, i: (bb, i, 1))],
        out_specs=pl.BlockSpec((1, tq, 256), lambda bb, i: (bb, i, 0)),
        out_shape=jax.ShapeDtypeStruct((b, t, 256), BF16),
        scratch_shapes=_flash_scratch(tq, tk),
        compiler_params=_cparams(2), name="attn_latent",
    )(qb, kb, vaug, gates)


def _attn_c_kernel(q_ref, iq_ref, iw_ref, k_ref, v_ref, ik_ref, g_ref, o_ref, key_ref, top_ref, sa_ref, sb_ref,
                   acc_ref,
                   *, p_len, l_len, k_sel, tq, tk, nq):
    i = pl.program_id(1) if nq > 1 else 0
    q0 = p_len + i * tq
    nkb_max = k_ref.shape[1] // tk
    nkb = jnp.minimum((q0 + tq + tk - 1) // tk, nkb_max)
    qchunk = (q0 + lax.broadcasted_iota(I32, (1, tq), 1)) // CHUNK
    hid = _head_lane_id()

    iq = iq_ref[0]
    ihid = lax.broadcasted_iota(I32, (1, 256), 1) // IDX_DIM
    iqm = [_keep_lanes(iq, ihid == h) for h in range(IDX_HEADS)]
    iw = iw_ref[0] * (IDX_DIM ** -0.5 * IDX_HEADS ** -0.5)

    def score_body(edge):
        def body(j, carry):
            ks = pl.multiple_of(j * tk, tk)
            ikb = ik_ref[0, pl.ds(ks, tk), :]
            score = jnp.zeros((tk, tq), F32)
            for h in range(IDX_HEADS):
                score = score + iw[h:h + 1, :] * jnp.maximum(_dot_nt(ikb, iqm[h]), 0.0)
            if edge:
                kpos = ks + lax.broadcasted_iota(I32, (tk, 1), 0)
                score = jnp.where(((kpos // CHUNK) <= qchunk) & (kpos < l_len), score, -jnp.inf)
            key_ref[j] = score
            top_ref[j] = pltpu.bitcast(pltpu.bitcast(score, I32) & -65536, F32).astype(BF16)
            return carry
        return body

    n_free = jnp.minimum(((q0 // CHUNK + 1) * CHUNK) // tk, nkb)
    lax.fori_loop(0, n_free, score_body(False), 0)
    lax.fori_loop(n_free, nkb, score_body(True), 0)

    def cand_value(cand):
        bits = jnp.where(cand < 0, cand ^ 0x7FFFFFFF, cand)
        bits = jnp.where((bits > 0) & (bits < 0x00800000), 0x00800000, bits)
        return pltpu.bitcast(bits, F32)

    def count(pred_fn):
        def cbody(j, acc):
            pf = jnp.where(pred_fn(key_ref[j], j), 1.0, 0.0)
            for r in range(tk // 8):
                acc = acc + pf[r * 8:(r + 1) * 8]
            return acc
        acc = lax.fori_loop(0, nkb, cbody, jnp.zeros((8, tq), F32))
        return jnp.sum(acc, axis=0, keepdims=True)

    def count_top(cand_b):
        one, zero = jnp.ones((16, tq), BF16), jnp.zeros((16, tq), BF16)

        def cbody(j, acc):
            part = zero
            for r in range(tk // 16):
                part = part + jnp.where(top_ref[j, r * 16:(r + 1) * 16, :] >= cand_b, one, zero)
            return acc + part.astype(F32)
        acc = lax.fori_loop(0, nkb, cbody, jnp.zeros((16, tq), F32))
        return jnp.sum(acc, axis=0, keepdims=True)

    def top_body(it, c):
        t, n_ge = c
        cand = t + jnp.left_shift(jnp.int32(1), 15 - it)
        cbits = pltpu.bitcast(cand_value(cand * 65536), I32) & -65536
        cnt = count_top(jnp.broadcast_to(pltpu.bitcast(cbits, F32), (16, tq)).astype(BF16))
        ok = cnt >= k_sel
        return jnp.where(ok, cand, t), jnp.where(ok, cnt, n_ge)

    t16, n_ge = lax.fori_loop(0, 16, top_body, (jnp.full((1, tq), -32768, I32), jnp.zeros((1, tq), F32)))

    def bis_body(it, c):
        t, n_ge = c
        cand = t + jnp.left_shift(jnp.int32(1), 15 - it)
        cval = cand_value(cand)
        cnt = count(lambda kb, j: kb >= cval)
        ok = cnt >= k_sel
        return jnp.where(ok, cand, t), jnp.where(ok, cnt, n_ge)

    tcode, n_ge = lax.fori_loop(0, 16, bis_body, (t16 * 65536, n_ge))
    thr = cand_value(jnp.maximum(tcode, MOST_NEGATIVE_CODE))

    def idx_of(j):
        return j * tk + lax.broadcasted_iota(I32, (tk, tq), 0)

    def tie_cut(_):
        need = k_sel - count(lambda kb, j: kb > thr)

        def tie_body(it, jc):
            cand = jc + jnp.left_shift(jnp.int32(1), 12 - it)
            cnt = count(lambda kb, j: (kb == thr) & (idx_of(j) < cand))
            return jnp.where(cnt < need, cand, jc)

        return lax.fori_loop(0, 13, tie_body, jnp.zeros((1, tq), I32))

    has_tie = jnp.max(jnp.where(n_ge > k_sel, 1.0, 0.0)) > 0.0
    jcut = lax.cond(has_tie, tie_cut, lambda _: jnp.full((1, tq), 2 ** 30, I32), 0)

    q = q_ref[0]
    qm = [_keep_lanes(q, hid == h) for h in range(N_HEADS)]

    def attend(mask_fn):
        return _flash_run(None, nkb, nkb_max, qm=qm,
                          k_block=lambda j, h: k_ref[0, pl.ds(pl.multiple_of(j * tk, tk), tk), :],
                          v_block=lambda j, h: v_ref[0, j, 0], col_sub=None, row_add=None, mask_fn=mask_fn,
                          s_refs=(sa_ref, sb_ref), acc_ref=acc_ref, tq=tq)

    def with_ties(_):
        def mask_fn(j):
            kb = key_ref[j]
            return (kb > thr) | ((kb == thr) & (idx_of(j) <= jcut))
        return attend(mask_fn)

    ls = lax.cond(has_tie, with_ties, lambda _: attend(lambda j: key_ref[j] >= thr), 0)
    _flash_finish(o_ref, g_ref, acc_ref, ls)


def _attn_c_call(cq, ciq, iwt, krep, vaug, ikrep, gates, p_len, l_len, k_sel, tq, tk):
    b, t, _ = cq.shape
    lp = krep.shape[1]
    assert lp <= 8192
    nq = t // tq
    kern = functools.partial(_attn_c_kernel, p_len=p_len, l_len=l_len, k_sel=float(k_sel), tq=tq, tk=tk, nq=nq)
    return pl.pallas_call(
        kern, grid=(b, nq),
        in_specs=[pl.BlockSpec((1, tq, 256), lambda bb, i: (bb, i, 0)),
                  pl.BlockSpec((1, tq, 256), lambda bb, i: (bb, i, 0)),
                  pl.BlockSpec((1, 8, tq), lambda bb, i: (bb, 0, i)),
                  pl.BlockSpec((1, lp, 256), lambda bb, i: (bb, 0, 0)),
                  pl.BlockSpec((1,) + vaug.shape[1:], lambda bb, i: (bb, 0, 0, 0, 0)),
                  pl.BlockSpec((1, lp, 256), lambda bb, i: (bb, 0, 0)),
                  pl.BlockSpec((1, tq, 256), lambda bb, i: (bb, i, 2))],
        out_specs=pl.BlockSpec((1, tq, 256), lambda bb, i: (bb, i, 0)),
        out_shape=jax.ShapeDtypeStruct((b, t, 256), BF16),
        scratch_shapes=[pltpu.VMEM((lp // tk, tk, tq), F32), pltpu.VMEM((lp // tk, tk, tq), BF16)]
        + _flash_scratch(tq, tk),
        compiler_params=_cparams(2), name="attn_sparse",
    )(cq, ciq, iwt, krep, vaug, ikrep, gates)


def _attn_d_kernel(q_ref, k_ref, v_ref, lf_ref, g_ref, o_ref, fc_ref, fk_ref, sa_ref, sb_ref, acc_ref,
                   *, p_len, tq, tk, nq):
    i = pl.program_id(1) if nq > 1 else 0
    lp = k_ref.shape[1]
    nch = lp // 128

    def cumulate():
        x = lf_ref[0]
        lane = lax.broadcasted_iota(I32, (1, lp), 1)
        step = 1
        while step < lp:
            x = x + jnp.where(lane >= step, pltpu.roll(x, step, 1), 0.0)
            step *= 2
        x = x * LOG2E
        for c in range(nch):
            fc_ref[c] = x[:, c * 128:(c + 1) * 128]

        eye = lax.broadcasted_iota(I32, (128, 128), 0) == lax.broadcasted_iota(I32, (128, 128), 1)

        def spread(c, carry):
            rows = fc_ref[c]
            for h in range(N_HEADS):
                col = jnp.sum(jnp.where(eye, rows[h:h + 1, :], 0.0), axis=1, keepdims=True)
                fk_ref[h, pl.ds(pl.multiple_of(c * 128, 128), 128), :] = jnp.broadcast_to(col, (128, 128))
            return carry

        lax.fori_loop(0, nch, spread, 0)

    if nq > 1:
        pl.when(i == 0)(cumulate)
    else:
        cumulate()

    q0 = p_len + i * tq
    hid = _head_lane_id()
    q = q_ref[0]
    qm = [_keep_lanes(q, hid == h) for h in range(N_HEADS)]
    c0 = q0 // 128
    fq_rows = [fc_ref[c0 + c] for c in range(tq // 128)]
    fq = [jnp.concatenate([r[h:h + 1, :] for r in fq_rows], axis=1) for h in range(N_HEADS)]
    qpos = q0 + lax.broadcasted_iota(I32, (1, tq), 1)
    nkb = jnp.minimum((q0 + tq + tk - 1) // tk, lp // tk)
    nfull = jnp.minimum((q0 + 1) // tk, nkb)

    def col_sub(j, h):
        fk = fk_ref[h, pl.ds(pl.multiple_of(j * tk, tk), tk), :]
        return jnp.concatenate([fk] * (tq // 128), axis=1)

    def mask_fn(j):
        return (j * tk + lax.broadcasted_iota(I32, (tk, 1), 0)) <= qpos

    ls = _flash_run(nfull, nkb, lp // tk, qm=qm,
                    k_block=lambda j, h: k_ref[0, pl.ds(pl.multiple_of(j * tk, tk), tk), :],
                    v_block=lambda j, h: v_ref[0, j, h], col_sub=col_sub, row_add=fq, mask_fn=mask_fn,
                    s_refs=(sa_ref, sb_ref), acc_ref=acc_ref, tq=tq)
    _flash_finish(o_ref, g_ref, acc_ref, ls)


def _attn_d_call(dq, kd, vaug, lft, gates, p_len, tq, tk):
    b, t, _ = dq.shape
    lp = kd.shape[1]
    nq = t // tq
    assert p_len % 128 == 0 and tq % 128 == 0 and p_len + t <= lp
    kern = functools.partial(_attn_d_kernel, p_len=p_len, tq=tq, tk=tk, nq=nq)
    return pl.pallas_call(
        kern, grid=(b, nq),
        in_specs=[pl.BlockSpec((1, tq, 256), lambda bb, i: (bb, i, 0)),
                  pl.BlockSpec((1, lp, 256), lambda bb, i: (bb, 0, 0)),
                  pl.BlockSpec((1,) + vaug.shape[1:], lambda bb, i: (bb, 0, 0, 0, 0)),
                  pl.BlockSpec((1, 8, lp), lambda bb, i: (bb, 0, 0)),
                  pl.BlockSpec((1, tq, 256), lambda bb, i: (bb, i, 3))],
        out_specs=pl.BlockSpec((1, tq, 256), lambda bb, i: (bb, i, 0)),
        out_shape=jax.ShapeDtypeStruct((b, t, 256), BF16),
        scratch_shapes=[pltpu.VMEM((lp // 128, 8, 128), F32), pltpu.VMEM((N_HEADS, lp, 128), F32)]
        + _flash_scratch(tq, tk),
        compiler_params=_cparams(2), name="attn_forget",
    )(dq, kd, vaug, lft, gates)


def _out_kernel(ya_ref, yb_ref, yc_ref, yd_ref, x_ref, p_ref, wo_ref, gple_ref, wg_ref, wp_ref, o_ref):
    mixed = (_dot(ya_ref[0], wo_ref[0:256, :]) + _dot(yb_ref[0], wo_ref[256:512, :])
             + _dot(yc_ref[0], wo_ref[512:768, :]) + _dot(yd_ref[0], wo_ref[768:1024, :]))
    x1 = x_ref[0] + mixed
    ms = jnp.mean(x1 * x1, axis=-1, keepdims=True)
    xn = (x1 * lax.rsqrt(ms + EPS) * gple_ref[...]).astype(BF16)
    gate = 1.0 / (1.0 + jnp.exp(-_dot(xn, wg_ref[...])))
    o_ref[0] = x1 + gate * _dot(p_ref[0, 0].astype(BF16), wp_ref[...])


def _out_call(ys, x, p_all, layer, lw, tm):
    b, t, _ = x.shape

    def const(shape):
        return pl.BlockSpec(shape, lambda bb, i: (0,) * len(shape))

    def rows(w):
        return pl.BlockSpec((1, tm, w), lambda bb, i: (bb, i, 0))

    return pl.pallas_call(
        _out_kernel, grid=(b, t // tm),
        in_specs=[rows(256)] * 4 + [rows(D_MODEL), pl.BlockSpec((1, 1, tm, PLE_DIM), lambda bb, i: (layer, bb, i, 0)),
                                    const((1024, D_MODEL)), const((1, D_MODEL)),
                                    const((D_MODEL, D_MODEL)), const((PLE_DIM, D_MODEL))],
        out_specs=rows(D_MODEL),
        out_shape=jax.ShapeDtypeStruct((b, t, D_MODEL), F32),
        compiler_params=_cparams(2), name="out_proj",
    )(*ys, x, p_all, lw['w_out'], lw['g_ple'], lw['w_ple_gate'], lw['w_ple_proj'])


def _seg_matrix(segments):
    m = np.zeros((256, 256), np.float32)
    for lo, n in segments:
        m[lo:lo + n, lo:lo + n] = 1.0 / n
    return jnp.asarray(m, BF16)


_PAIR_SEGS = ((0, 64), (64, 32), (96, 64), (160, 32), (192, 64))


def _pair_cols(head):
    return (head // 2) * 256 + (head % 2) * (MLA_NOPE + MLA_ROPE)


def _prep_layer(i, g_in, w_in, g_qk, g_rope, a_rel_bias, b_g_cq, b_w_uq, b_g_ckv, b_w_ukv, d_b_f, w_out, g_ple,
                w_ple_gate, w_ple_proj):
    w = w_in[i]
    cols = []
    for n in _MY_ORDER:
        if n == 'pad56':
            cols.append(jnp.zeros((D_MODEL, 56), F32))
        else:
            o, wd = _REF_OFF[n]
            cols.append(w[:, o:o + wd])
    lw = {'w_in': jnp.concatenate(cols, axis=1).astype(BF16), 'g_in': g_in[i][None, :]}
    wt = [w[:, _REF_OFF[n][0]:_REF_OFF[n][0] + _REF_OFF[n][1]] for n in ('d_v', 'c_v', 'c_iw', 'd_f')]
    wt.append(jnp.zeros((D_MODEL, XT_ROWS - XT_DF - N_HEADS), F32))
    lw['w_t'] = jnp.concatenate(wt, axis=1).T.astype(BF16)
    lw['b_col'] = jnp.concatenate([d_b_f[i], jnp.zeros((8 - N_HEADS,), F32)])[:, None]
    g = g_qk[i]
    sc = HEAD_DIM ** -0.5
    rows = [jnp.tile(g[0], 4) * sc, jnp.tile(g[1], 4), jnp.tile(g[2], 4) * (sc * LOG2E),
            jnp.tile(g[4], 4) * (sc * LOG2E), jnp.tile(g[5], 4)]
    lw['g4'] = jnp.stack(rows + [jnp.zeros((256,), F32)] * 3)
    lw['hseg'] = _seg_matrix(tuple((h * 64, 64) for h in range(4)))
    lw['segq'] = _seg_matrix(_PAIR_SEGS)
    lw['g_cq'] = b_g_cq[i][None, :]
    lw['g_ckv'] = b_g_ckv[i][None, :]
    qscale = (MLA_NOPE + MLA_ROPE) ** -0.5 * LOG2E
    hw = MLA_NOPE + MLA_ROPE
    wuq, gqb, wuk, gkb, wuv = [], [], [], [], []
    e_kr = np.zeros((128, 512), np.float32)
    for h in range(N_HEADS):
        wuq.append(b_w_uq[i][:, h * hw:(h + 1) * hw])
        gqb += [g[6] * qscale, g_rope[i][0] * qscale]
        srck = h * (MLA_NOPE + HEAD_DIM)
        wuk += [b_w_ukv[i][:, srck:srck + MLA_NOPE], jnp.zeros((MLA_KV_LORA, MLA_ROPE), F32)]
        gkb += [g[7], jnp.zeros((MLA_ROPE,), F32)]
        wuv.append(b_w_ukv[i][:, srck + MLA_NOPE:srck + MLA_NOPE + HEAD_DIM])
        e_kr[np.arange(MLA_ROPE), _pair_cols(h) + MLA_NOPE + np.arange(MLA_ROPE)] = 1.0
        if h % 2 == 1:
            wuq.append(jnp.zeros((MLA_Q_LORA, 256 - 2 * hw), F32))
            wuk.append(jnp.zeros((MLA_KV_LORA, 256 - 2 * hw), F32))
            gqb.append(jnp.zeros((256 - 2 * hw,), F32))
            gkb.append(jnp.zeros((256 - 2 * hw,), F32))
    lw['w_uq'] = jnp.concatenate(wuq, axis=1).astype(BF16)
    lw['g_qb'] = jnp.concatenate(gqb)[None, :]
    lw['w_uk'] = jnp.concatenate(wuk, axis=1).astype(BF16)
    lw['g_kb'] = jnp.concatenate(gkb)[None, :]
    lw['e_kr'] = jnp.asarray(e_kr, BF16)
    lw['w_uvt'] = jnp.concatenate(wuv, axis=1).T.astype(BF16)
    one = jnp.ones((128,), F32)
    lw['srow'] = jnp.stack([jnp.concatenate([g[3], one[HEAD_DIM:]]), jnp.concatenate([g_rope[i][1], one[MLA_ROPE:]])]
                           + [one] * 6)
    ab = a_rel_bias[i]
    ext = jnp.concatenate([jnp.broadcast_to(ab[:, 2 * REL_CLIP:], (N_HEADS, A_WIN - REL_CLIP - 1)),
                           ab[:, REL_CLIP - CHUNK + 1:][:, ::-1]], axis=1)
    lw['a_bias'] = jnp.stack([ext[:, CHUNK - 1 - q:CHUNK - 1 - q + A_WIN] for q in range(CHUNK)],
                             axis=1).reshape(N_HEADS * CHUNK, A_WIN)
    lw['w_out'] = w_out[i].astype(BF16)
    lw['g_ple'] = g_ple[i][None, :]
    lw['w_ple_gate'] = w_ple_gate[i].astype(BF16)
    lw['w_ple_proj'] = w_ple_proj[i].astype(BF16)
    return lw


def _cos_sin(pos, rot_dim):
    half = rot_dim // 2
    inv = jnp.float32(ROPE_THETA) ** (-jnp.arange(half, dtype=F32) / half)
    ang = pos.astype(F32)[:, None] * inv[None, :]
    return [jnp.cos(ang), jnp.sin(ang)]


def _rot_placement():
    src = {MLA_ROPE: 1, ROT_DIM: 1 + MLA_ROPE}
    e = np.zeros((1 + MLA_ROPE + ROT_DIM, TAB_W), np.float32)
    sections = ((TAB_QB, 512, [_pair_cols(h) + MLA_NOPE for h in range(N_HEADS)], MLA_ROPE),
                (TAB_CQ, 256, [h * HEAD_DIM for h in range(N_HEADS)], ROT_DIM),
                (TAB_S1, 128, [0], ROT_DIM), (TAB_S2, 128, [0], MLA_ROPE))
    for base, width, starts, rot_dim in sections:
        half = rot_dim // 2
        e[0, base:base + width] = 1.0
        for st in starts:
            for i in range(half):
                c, s = src[rot_dim] + i, src[rot_dim] + half + i
                e[0, base + st + i] = e[0, base + st + half + i] = 0.0
                e[c, base + st + i] = e[c, base + st + half + i] = 1.0
                e[s, base + width + st + half + i] = 1.0
                e[s, base + 2 * width + st + i] = -1.0
    return jnp.asarray(e)


def _rot_tables(pos):
    src = jnp.concatenate([jnp.ones((pos.shape[0], 1), F32)] + _cos_sin(pos, MLA_ROPE) + _cos_sin(pos, ROT_DIM),
                          axis=1)
    hi = src.astype(BF16)
    lo = (src - hi.astype(F32)).astype(BF16)
    place = _rot_placement().astype(BF16)
    return jnp.dot(jnp.concatenate([hi, lo], axis=1), jnp.concatenate([place, place], axis=0),
                   preferred_element_type=F32)


def _pad_axis(a, size, axis):
    if a.shape[axis] == size:
        return a
    pads = [(0, 0)] * a.ndim
    pads[axis] = (0, size - a.shape[axis])
    return jnp.pad(a, pads)


def _cat_rows(past, new, rows):
    a = new if past is None else jnp.concatenate([past.astype(new.dtype), new], axis=1)
    return _pad_axis(a, rows, 1)


def _cat_lanes(past_t, new_t, lanes):
    a = new_t if past_t is None else jnp.concatenate([past_t.astype(new_t.dtype), new_t], axis=2)
    return _pad_axis(a, lanes, 2)


def _value_blocks(vt, tk):
    b, r, lp = vt.shape
    h = r // HEAD_DIM
    v = vt.astype(BF16).reshape(b, h, HEAD_DIM, lp)
    v = jnp.concatenate([v, jnp.ones((b, h, V_ROWS - HEAD_DIM, lp), BF16)], axis=2)
    return v.reshape(b, h, V_ROWS, lp // tk, tk).transpose(0, 3, 1, 2, 4)


def _layer(x, p_all, layer, past, lw, tab, prev_narrow, *, fold_batch, tm, tq, tq_c, tk, a_group):
    b, t, _ = x.shape
    p_len = 0 if past is None else past[2].shape[1]
    l_len = p_len + t
    tqp = -(-t // tq) * tq
    tqp_c = -(-t // tq_c) * tq_c
    lp = -(-(p_len + max(tqp, tqp_c)) // tk) * tk

    xin = x.reshape(1, b * t, D_MODEL) if fold_batch else x
    a_direct = past is None and not fold_batch
    a_front = -(-(CHUNK + A_BAND) // tm) * tm if a_direct else 0
    outs = list(_inproj_call(xin, tab, lw, tm, a_front, prev_narrow))
    narrow = tuple(outs[o] for o in NARROW_OUTS)
    for o in NARROW_OUTS:
        outs[o] = None
    if fold_batch:
        xt = outs[18][0].reshape(XT_ROWS, b, t).transpose(1, 0, 2)
        outs = ([None if o is None else o.reshape((b, t) + o.shape[2:]) for o in outs[:18]]
                + [xt, outs[19].reshape(b, t, 1024), None, None] + [o.reshape(b, t, 256) for o in outs[22:24]])
    (aq, akb, avb, akf, avf, qb, ckvf, cq, ciq, c_k, c_v, s2, b_kr, c_ik, dq, dkb, dkf, dvf, xt, gates,
     dva, cva, krep_new, ikrep_new) = outs
    direct_values = past is None and not fold_batch and tm == tk

    lft = xt[:, XT_DF:XT_ROWS]
    d_lf = lft[:, :N_HEADS, :].transpose(0, 2, 1)
    state = (akf.reshape(b, t, N_HEADS, HEAD_DIM), avf.reshape(b, t, N_HEADS, HEAD_DIM), ckvf, b_kr, c_k, c_v, c_ik,
             dkf, dvf, d_lf)

    if past is None:
        pa_k = pa_v = pb_ckv = pb_kr = pc_k = pc_v = pc_ik = pd_k = pd_v = pd_lf = None
        pa = 0
    else:
        pa_k, pa_v, pb_ckv, pb_kr, pc_k, pc_v, pc_ik, pd_k, pd_v, pd_lf = past
        pa = pa_k.shape[1]
        pa_k = pa_k.reshape(b, pa, 256)
        pa_v = pa_v.reshape(b, pa, 256)
        pd_k = pd_k.reshape(b, p_len, 256)
        pd_v = pd_v.reshape(b, p_len, 256)

    tp = -(-t // (CHUNK * a_group)) * (CHUNK * a_group)
    if a_direct:
        assert tp == t
        a_keys, a_vals = akb, avb
        row0, front = a_front - (CHUNK + A_BAND), a_front
    else:
        front = CHUNK + A_BAND - pa
        a_rows = CHUNK + A_BAND + tp

        def band_src(pst, new):
            parts = [jnp.zeros((b, front, 256), BF16)]
            if pst is not None:
                parts.append(pst.astype(BF16))
            parts.append(new)
            return _pad_axis(jnp.concatenate(parts, axis=1), a_rows, 1)

        a_keys, a_vals, row0 = band_src(pa_k, akb), band_src(pa_v, avb), 0
    ya = _attn_a_call(_pad_axis(aq, tp, 1), a_keys, a_vals, lw['a_bias'], _pad_axis(gates, tp, 1), row0, front,
                      front + pa + t, a_group)[:, :t]

    gates_q = _pad_axis(gates, tqp, 1)

    ckv_all = _cat_rows(pb_ckv, ckvf, lp)
    kr_new = s2
    kr_past = None if past is None else _pad_axis(pb_kr, 128, 2)
    kr_all = _cat_rows(kr_past, kr_new, lp)
    kb, vba = _bkv_call(ckv_all, kr_all, lw, tk)
    yb = _attn_b_call(_pad_axis(qb, tqp, 1), kb, vba, gates_q, p_len, l_len, tq, tk)[:, :t]

    krep = _cat_rows(None if past is None else jnp.tile(pc_k.astype(BF16), (1, 1, N_HEADS)), krep_new, lp)
    ikrep = _cat_rows(None if past is None else jnp.tile(pc_ik.astype(BF16), (1, 1, IDX_HEADS)), ikrep_new, lp)
    if not direct_values:
        cva = _value_blocks(_cat_lanes(None if past is None else pc_v.transpose(0, 2, 1), xt[:, XT_CV:XT_IW], lp), tk)
    iwt = _pad_axis(xt[:, XT_IW:XT_DF], tqp_c, 2)
    k_sel = min(DSA_TOPK, l_len // 4)
    yc = _attn_c_call(_pad_axis(cq, tqp_c, 1), _pad_axis(ciq, tqp_c, 1), iwt, krep, cva, ikrep,
                      _pad_axis(gates, tqp_c, 1), p_len, l_len, k_sel, tq_c, tk)[:, :t]

    kd = _cat_rows(None if past is None else pd_k.astype(BF16), dkb, lp)
    if not direct_values:
        dva = _value_blocks(_cat_lanes(None if past is None else pd_v.transpose(0, 2, 1), xt[:, XT_DV:XT_CV], lp), tk)
    lf_past = None if past is None else _pad_axis(pd_lf.transpose(0, 2, 1), 8, 1)
    lf_all = _cat_lanes(lf_past, lft, lp)
    yd = _attn_d_call(_pad_axis(dq, tqp, 1), kd, dva, lf_all, gates_q, p_len, tq, tk)[:, :t]

    if fold_batch:
        x_new = _out_call([y.reshape(1, b * t, 256) for y in (ya, yb, yc, yd)], x.reshape(1, b * t, D_MODEL),
                          p_all.reshape(p_all.shape[0], 1, b * t, PLE_DIM), layer, lw, tm).reshape(b, t, D_MODEL)
    else:
        x_new = _out_call([ya, yb, yc, yd], x, p_all, layer, lw, 2 * tm)
    return x_new, state, narrow


def kernel(x_prompt, x_sample, cache_a_k, cache_a_v, cache_b_ckv, cache_b_krope, cache_c_k, cache_c_v, cache_c_idx_k,
           cache_d_k, cache_d_v, cache_d_logf, p_prompt, p_sample, g_in, w_in, g_qk, g_rope, a_rel_bias, b_g_cq,
           b_w_uq, b_g_ckv, b_w_ukv, d_b_f, w_out, g_ple, w_ple_gate, w_ple_proj):
    depth = w_in.shape[0]
    b, t = x_prompt.shape[:2]
    bs, ts = x_sample.shape[:2]
    past_len = cache_b_ckv.shape[2]
    tab_p = _rot_tables(jnp.arange(t))
    tab_s = jnp.tile(_rot_tables(past_len + jnp.arange(ts)), (bs, 1))
    xp, xs = x_prompt, x_sample
    states_p, states_s = [], []
    narrow_p = narrow_s = None
    for i in range(depth):
        lw = _prep_layer(i, g_in, w_in, g_qk, g_rope, a_rel_bias, b_g_cq, b_w_uq, b_g_ckv, b_w_ukv, d_b_f, w_out,
                         g_ple, w_ple_gate, w_ple_proj)
        xp, st_p, narrow_p = _layer(xp, p_prompt, i, None, lw, tab_p, narrow_p, fold_batch=False, tm=256, tq=512,
                                     tq_c=512, tk=256, a_group=16)
        past = (cache_a_k[i], cache_a_v[i], cache_b_ckv[i], cache_b_krope[i], cache_c_k[i], cache_c_v[i],
                cache_c_idx_k[i], cache_d_k[i], cache_d_v[i], cache_d_logf[i])
        xs, st_s, narrow_s = _layer(xs, p_sample, i, past, lw, tab_s, narrow_s, fold_batch=True, tm=bs * ts, tq=128,
                                     tq_c=128, tk=256, a_group=1)
        states_p.append(st_p)
        states_s.append(st_s)
    keep = min(A_BAND, t)
    states_p = [(st[0][:, t - keep:], st[1][:, t - keep:]) + tuple(st[2:]) for st in states_p]

    def assemble(states, narrow, bb, tt):
        out = [None if z[0] is None else jnp.stack(z) for z in zip(*states)]
        c_k, c_v, b_kr, c_ik, d_k, d_v = (a.reshape(depth, bb, tt, a.shape[-1]) for a in narrow)
        out[3], out[4], out[5], out[6] = b_kr, c_k, c_v, c_ik
        out[7] = d_k.reshape(depth, bb, tt, N_HEADS, HEAD_DIM)
        out[8] = d_v.reshape(depth, bb, tt, N_HEADS, HEAD_DIM)
        return out

    return (xp, xs, *assemble(states_p, narrow_p, b, t), *assemble(states_s, narrow_s, bs, ts))
```

```python
import functools

import numpy as np
import jax
import jax.numpy as jnp
from jax import lax
from jax.experimental import pallas as pl
from jax.experimental.pallas import tpu as pltpu

F32 = jnp.float32
BF16 = jnp.bfloat16
I32 = jnp.int32

D_MODEL = 1024
CHUNK = 64
EPS = 1e-6
HEAD_DIM = 64
N_HEADS = 4
GROUP_WIDTH = 256
ROT_DIM = 16
ROPE_THETA = 500000.0
A_BAND = 8 * CHUNK
REL_CLIP = 128
MLA_Q_LORA = 384
MLA_KV_LORA = 128
MLA_NOPE = 64
MLA_ROPE = 32
IDX_HEADS = 8
IDX_DIM = 32
DSA_TOPK = 256
PLE_DIM = 256

_REF_SPLITS = (
    ('a_q', 256), ('a_k', 256), ('a_v', 256), ('a_g', 256),
    ('b_cq', 384), ('b_ckv', 128), ('b_kr', 32), ('b_g', 256),
    ('c_q', 256), ('c_k', 64), ('c_v', 64), ('c_iq', 256), ('c_ik', 32), ('c_iw', 8), ('c_g', 256),
    ('d_q', 256), ('d_k', 256), ('d_v', 256), ('d_f', 4), ('d_g', 256),
)
_REF_OFF = {}
_o = 0
for _n, _w in _REF_SPLITS:
    _REF_OFF[_n] = (_o, _w)
    _o += _w

_MY_ORDER = ('a_q', 'a_k', 'a_v', 'a_g', 'b_cq', 'b_ckv', 'b_g', 'c_q', 'c_iq', 'c_g', 'd_q', 'd_k', 'd_v', 'd_g',
             'c_k', 'c_v', 'b_kr', 'c_ik', 'c_iw', 'pad56')
_MY_OFF = {}
_o = 0
for _n in _MY_ORDER:
    _w = 56 if _n == 'pad56' else _REF_OFF[_n][1]
    _MY_OFF[_n] = _o
    _o += _w
W_IN_COLS = _o
_PROJ_GROUPS = ((0, _MY_OFF['b_cq']), (_MY_OFF['b_cq'], _MY_OFF['c_q']), (_MY_OFF['c_q'], _MY_OFF['d_q']),
                (_MY_OFF['d_q'], _MY_OFF['c_k']), (_MY_OFF['c_k'], W_IN_COLS))

XT_DV, XT_CV, XT_IW, XT_DF, XT_ROWS = 0, 256, 320, 328, 336

TAB_QB, TAB_CQ, TAB_S1, TAB_S2 = 0, 1536, 2304, 2688
TAB_W = 3072

V_ROWS = HEAD_DIM + 16
NEG = -1e30
LOG2E = 1.4426950408889634
MOST_NEGATIVE_CODE = -2139095040
VMEM_LIMIT = 56 * 1024 * 1024


def _cparams(n_axes):
    return pltpu.CompilerParams(dimension_semantics=("arbitrary",) * n_axes, vmem_limit_bytes=VMEM_LIMIT)


def _dot(a, b):
    return jnp.dot(a, b, preferred_element_type=F32)


def _dot_nt(a, b):
    return lax.dot_general(a, b, (((1,), (1,)), ((), ())), preferred_element_type=F32)


def _seg_mean_sq(t, seg):
    sq = t * t
    hi = sq.astype(BF16)
    lo = (sq - hi.astype(F32)).astype(BF16)
    return _dot(hi, seg) + _dot(lo, seg)


def _rotate(t, cos, sin_up, sin_dn, half):
    w = t.shape[-1]
    return t * cos + pltpu.roll(t, half, 1) * sin_up + pltpu.roll(t, w - half, 1) * sin_dn


def _silu(g):
    return g * (1.0 / (1.0 + jnp.exp(-g)))


def _log_sigmoid(v):
    return jnp.minimum(v, 0.0) - jnp.log1p(jnp.exp(-jnp.abs(v)))


N_INPROJ_IN = 16
NARROW_OUTS = (9, 10, 12, 13, 16, 17)


def _inproj_kernel(*refs, n_prev):
    n_in = N_INPROJ_IN + (len(NARROW_OUTS) if n_prev else 0)
    _inproj_body(refs[N_INPROJ_IN:n_in], n_prev, *refs[:N_INPROJ_IN], *refs[n_in:])


def _inproj_body(prev, n_prev,
                 x_ref, tab_ref, gin_ref, w_ref, wt_ref, bcol_ref, g4_ref, hseg_ref, gcq_ref, wuq_ref, segq_ref,
                   gqb_ref, gckv_ref, srow_ref, akz_ref, avz_ref,
                   aq_o, akb_o, avb_o, akf_o, avf_o, qb_o, ckvf_o, cq_o, ciq_o, ck_o, cv_o, s2_o, bkr_o, cik_o,
                   dq_o, dkb_o, dkf_o, dvf_o, xt_o, gate_o, dva_o, cva_o, krep_o, ikrep_o):
    del akz_ref, avz_ref

    def store_narrow(slot, o_ref, rows):
        for layer in range(n_prev):
            o_ref[layer, 0] = prev[slot][layer, 0]
        o_ref[n_prev, 0] = rows

    x = x_ref[0]
    ms = jnp.mean(x * x, axis=-1, keepdims=True)
    xn = (x * lax.rsqrt(ms + EPS) * gin_ref[...]).astype(BF16)

    group_out = {}

    def proj(name, n):
        c0 = _MY_OFF[name]
        g0, g1 = next((a, b) for a, b in _PROJ_GROUPS if a <= c0 < b)
        if g0 not in group_out:
            group_out[g0] = _dot(xn, w_ref[:, g0:g1])
        return group_out[g0][:, c0 - g0:c0 - g0 + n]

    hseg = hseg_ref[...]

    def headnorm(t, row):
        return t * lax.rsqrt(_seg_mean_sq(t, hseg) + EPS) * g4_ref[row:row + 1, :]

    def fullnorm(t, g):
        return t * lax.rsqrt(jnp.mean(t * t, axis=-1, keepdims=True) + EPS) * g

    aq_o[0] = headnorm(proj('a_q', 256), 0).astype(BF16)
    ak = headnorm(proj('a_k', 256), 1)
    akf_o[0] = ak
    akb_o[0] = ak.astype(BF16)
    av = proj('a_v', 256)
    avf_o[0] = av
    avb_o[0] = av.astype(BF16)
    gate_o[0, :, 0:256] = _silu(proj('a_g', 256)).astype(BF16)

    cqn = fullnorm(proj('b_cq', MLA_Q_LORA), gcq_ref[...]).astype(BF16)
    qb = _dot(cqn, wuq_ref[...])
    segq = segq_ref[...]
    ms_q = jnp.concatenate([_seg_mean_sq(qb[:, :256], segq), _seg_mean_sq(qb[:, 256:], segq)], axis=1)
    qbn = qb * lax.rsqrt(ms_q + EPS) * gqb_ref[...]
    qbn = _rotate(qbn, tab_ref[:, TAB_QB:TAB_QB + 512], tab_ref[:, TAB_QB + 512:TAB_QB + 1024],
                  tab_ref[:, TAB_QB + 1024:TAB_QB + 1536], MLA_ROPE // 2)
    qb_o[0] = qbn.astype(BF16)
    ckvf_o[0] = fullnorm(proj('b_ckv', MLA_KV_LORA), gckv_ref[...])
    gate_o[0, :, 256:512] = _silu(proj('b_g', 256)).astype(BF16)

    cq = headnorm(proj('c_q', 256), 2)
    cq = _rotate(cq, tab_ref[:, TAB_CQ:TAB_CQ + 256], tab_ref[:, TAB_CQ + 256:TAB_CQ + 512],
                 tab_ref[:, TAB_CQ + 512:TAB_CQ + 768], ROT_DIM // 2)
    cq_o[0] = cq.astype(BF16)
    ciq_o[0] = proj('c_iq', 256).astype(BF16)
    gate_o[0, :, 512:768] = _silu(proj('c_g', 256)).astype(BF16)

    dq_o[0] = headnorm(proj('d_q', 256), 3).astype(BF16)
    dk = headnorm(proj('d_k', 256), 4)
    store_narrow(4, dkf_o, dk)
    dkb_o[0] = dk.astype(BF16)
    store_narrow(5, dvf_o, proj('d_v', 256))
    gate_o[0, :, 768:1024] = _silu(proj('d_g', 256)).astype(BF16)

    lane = lax.broadcasted_iota(I32, (1, 128), 1)
    t = proj('c_k', 128)
    m64 = lane < HEAD_DIM
    ms1 = jnp.sum(jnp.where(m64, t * t, 0.0), axis=-1, keepdims=True) * (1.0 / HEAD_DIM)
    t = jnp.where(m64, t * lax.rsqrt(ms1 + EPS), t) * srow_ref[0:1, :]
    t = _rotate(t, tab_ref[:, TAB_S1:TAB_S1 + 128], tab_ref[:, TAB_S1 + 128:TAB_S1 + 256],
                tab_ref[:, TAB_S1 + 256:TAB_S1 + 384], ROT_DIM // 2)
    store_narrow(0, ck_o, t[:, :HEAD_DIM])
    krep_o[0] = jnp.concatenate([t[:, :HEAD_DIM]] * N_HEADS, axis=1).astype(BF16)
    store_narrow(1, cv_o, t[:, HEAD_DIM:])
    t = proj('b_kr', 128)
    m32 = lane < MLA_ROPE
    ms2 = jnp.sum(jnp.where(m32, t * t, 0.0), axis=-1, keepdims=True) * (1.0 / MLA_ROPE)
    t = jnp.where(m32, t * lax.rsqrt(ms2 + EPS), t) * srow_ref[1:2, :]
    t = _rotate(t, tab_ref[:, TAB_S2:TAB_S2 + 128], tab_ref[:, TAB_S2 + 128:TAB_S2 + 256],
                tab_ref[:, TAB_S2 + 256:TAB_S2 + 384], MLA_ROPE // 2)
    s2_o[0] = t
    store_narrow(2, bkr_o, t[:, :MLA_ROPE])
    store_narrow(3, cik_o, t[:, MLA_ROPE:MLA_ROPE + IDX_DIM])
    ikrep_o[0] = jnp.concatenate([t[:, MLA_ROPE:MLA_ROPE + IDX_DIM]] * IDX_HEADS, axis=1).astype(BF16)

    xt = _dot_nt(wt_ref[...], xn)
    xt_o[0, 0:XT_DF, :] = xt[0:XT_DF]
    xt_o[0, XT_DF:XT_ROWS, :] = _log_sigmoid(xt[XT_DF:XT_ROWS] + bcol_ref[...])
    _store_value_block(dva_o, xt[XT_DV:XT_CV].astype(BF16), N_HEADS)
    _store_value_block(cva_o, xt[XT_CV:XT_IW].astype(BF16), 1)


def _inproj_call(x, tab, lw, tm, a_front, prev_narrow):
    n_prev = 0 if prev_narrow is None else prev_narrow[0].shape[0]
    bk, tk_, _ = x.shape
    front_blocks = a_front // tm
    grid = (tk_ // tm, bk)

    def const(shape):
        return pl.BlockSpec(shape, lambda i, b: (0,) * len(shape))

    def rows(w):
        return pl.BlockSpec((1, tm, w), lambda i, b: (b, i, 0))

    in_specs = [
        rows(D_MODEL),
        pl.BlockSpec((tm, TAB_W), lambda i, b: (i, 0)),
        const((1, D_MODEL)), const((D_MODEL, W_IN_COLS)), const((XT_ROWS, D_MODEL)), const((8, 1)),
        const((8, 256)), const((256, 256)), const((1, MLA_Q_LORA)), const((MLA_Q_LORA, 512)), const((256, 256)),
        const((1, 512)), const((1, MLA_KV_LORA)), const((8, 128)),
        pl.BlockSpec(memory_space=pl.ANY), pl.BlockSpec(memory_space=pl.ANY),
    ]
    assert len(in_specs) == N_INPROJ_IN
    if n_prev:
        in_specs += [pl.BlockSpec((n_prev, 1, tm, a.shape[-1]), lambda i, b: (0, b, i, 0)) for a in prev_narrow]
    band_zero = jnp.zeros((bk, a_front + tk_, 256), BF16)
    widths = [(256, BF16), (256, BF16), (256, BF16), (256, F32), (256, F32), (512, BF16), (128, F32), (256, BF16),
              (256, BF16), (64, F32), (64, F32), (128, F32), (32, F32), (32, F32), (256, BF16), (256, BF16), (256, F32),
              (256, F32)]
    out_shape = [jax.ShapeDtypeStruct((bk, tk_, w), dt) for w, dt in widths]
    out_specs = [rows(w) for w, _ in widths]
    for o in NARROW_OUTS:
        w = widths[o][0]
        out_shape[o] = jax.ShapeDtypeStruct((n_prev + 1, bk, tk_, w), F32)
        out_specs[o] = pl.BlockSpec((n_prev + 1, 1, tm, w), lambda i, b: (0, b, i, 0))
    for o in (1, 2):
        out_shape[o] = jax.ShapeDtypeStruct(band_zero.shape, BF16)
        out_specs[o] = pl.BlockSpec((1, tm, 256), lambda i, b: (b, i + front_blocks, 0))
    out_shape.append(jax.ShapeDtypeStruct((bk, XT_ROWS, tk_), F32))
    out_specs.append(pl.BlockSpec((1, XT_ROWS, tm), lambda i, b: (b, 0, i)))
    out_shape.append(jax.ShapeDtypeStruct((bk, tk_, 1024), BF16))
    out_specs.append(rows(1024))
    for heads in (N_HEADS, 1):
        out_shape.append(jax.ShapeDtypeStruct((bk, tk_ // tm, heads, V_ROWS, tm), BF16))
        out_specs.append(pl.BlockSpec((1, 1, heads, V_ROWS, tm), lambda i, b: (b, i, 0, 0, 0)))
    for _ in range(2):
        out_shape.append(jax.ShapeDtypeStruct((bk, tk_, 256), BF16))
        out_specs.append(rows(256))
    return pl.pallas_call(
        functools.partial(_inproj_kernel, n_prev=n_prev), grid=grid, in_specs=in_specs, out_specs=out_specs,
        out_shape=out_shape,
        input_output_aliases={14: 1, 15: 2}, compiler_params=_cparams(2), name="inproj",
    )(x, tab, lw['g_in'], lw['w_in'], lw['w_t'], lw['b_col'], lw['g4'], lw['hseg'], lw['g_cq'], lw['w_uq'],
      lw['segq'], lw['g_qb'], lw['g_ckv'], lw['srow'], band_zero, band_zero, *(prev_narrow or ()))


def _store_value_block(va_o, vt, heads):
    tk = vt.shape[1]
    for h in range(heads):
        va_o[0, 0, h, 0:HEAD_DIM, :] = vt[h * HEAD_DIM:(h + 1) * HEAD_DIM]
        va_o[0, 0, h, HEAD_DIM:V_ROWS, :] = jnp.ones((V_ROWS - HEAD_DIM, tk), BF16)


def _bkv_kernel(ckv_ref, kr_ref, wk_ref, wvt_ref, e_ref, segq_ref, gk_ref, kb_o, va_o):
    c = ckv_ref[0].astype(BF16)
    kn = _dot(c, wk_ref[...])
    segq = segq_ref[...]
    ms = jnp.concatenate([_seg_mean_sq(kn[:, :256], segq), _seg_mean_sq(kn[:, 256:], segq)], axis=1)
    kn = kn * lax.rsqrt(ms + EPS) * gk_ref[...]
    kr = _dot(kr_ref[0].astype(BF16), e_ref[...])
    kb_o[0] = (kn + kr).astype(BF16)
    vt = _dot_nt(wvt_ref[...], c).astype(BF16)
    _store_value_block(va_o, vt, N_HEADS)


def _bkv_call(ckv, kr, lw, tm):
    b, lp, _ = ckv.shape

    def const(shape):
        return pl.BlockSpec(shape, lambda bb, i: (0,) * len(shape))

    def rows(w):
        return pl.BlockSpec((1, tm, w), lambda bb, i: (bb, i, 0))

    return pl.pallas_call(
        _bkv_kernel, grid=(b, lp // tm),
        in_specs=[rows(128), rows(128), const((128, 512)), const((256, 128)), const((128, 512)), const((256, 256)),
                  const((1, 512))],
        out_specs=[rows(512), pl.BlockSpec((1, 1, N_HEADS, V_ROWS, tm), lambda bb, i: (bb, i, 0, 0, 0))],
        out_shape=[jax.ShapeDtypeStruct((b, lp, 512), BF16),
                   jax.ShapeDtypeStruct((b, lp // tm, N_HEADS, V_ROWS, tm), BF16)],
        compiler_params=_cparams(2), name="mla_kv",
    )(ckv, kr, lw['w_uk'], lw['w_uvt'], lw['e_kr'], lw['segq'], lw['g_kb'])


def _head_lane_id(width=256):
    return lax.broadcasted_iota(I32, (1, width), 1) // HEAD_DIM


def _keep_lanes(x, pred):
    return jnp.where(pred, x.astype(F32), 0.0).astype(BF16)


def _flash_run(nfull, nkb, nkb_max, *, qm, k_block, v_block, col_sub, row_add, mask_fn, s_refs, acc_ref, tq):
    s_a, s_b = s_refs

    track_max = nfull is not None

    def produce(j, s_out):
        jc = jnp.minimum(j, nkb_max - 1)
        tops = []
        for h in range(N_HEADS):
            s = _dot_nt(k_block(jc, h), qm[h])
            if col_sub is not None:
                s = s - col_sub(jc, h)
            s_out[h] = s
            tops.append(jnp.max(s, axis=0, keepdims=True) if track_max else jnp.zeros((1, tq), F32))
        return tuple(tops)

    def half(j, c, s_in, s_out, masked):
        ms, ls, tops = c
        next_tops = produce(j + 1, s_out)
        valid = mask_fn(j) if masked else None
        new_m, new_l = [], []
        for h in range(N_HEADS):
            t = s_in[h]
            if masked:
                t = jnp.where(valid, t, NEG)
                m_cur = jnp.max(t, axis=0, keepdims=True)
            else:
                m_cur = tops[h]
            if row_add is not None:
                m_cur = m_cur + row_add[h]
            m_new = jnp.maximum(ms[h], m_cur)
            alpha = jnp.exp2(ms[h] - m_new)
            off = m_new if row_add is None else m_new - row_add[h]
            p = jnp.exp2(t - off).astype(BF16)
            pv = _dot(v_block(j, h), p)
            r0 = h * HEAD_DIM
            acc_ref[r0:r0 + HEAD_DIM, :] = acc_ref[r0:r0 + HEAD_DIM, :] * alpha + pv[:HEAD_DIM]
            new_l.append(ls[h] * alpha + pv[HEAD_DIM:HEAD_DIM + 1])
            new_m.append(m_new)
        return tuple(new_m), tuple(new_l), next_tops

    def step(j, c, s_in, s_out, may_end):
        def run(c):
            if nfull is None:
                return half(j, c, s_in, s_out, True)
            return lax.cond(j >= nfull, lambda cc: half(j, cc, s_in, s_out, True),
                            lambda cc: half(j, cc, s_in, s_out, False), c)
        if not may_end:
            return run(c)
        return lax.cond(j >= nkb, lambda cc: cc, run, c)

    main_masked = nfull is None
    n_main = (nkb if main_masked else nfull) // 2

    def main_body(i, c):
        c = half(2 * i, c, s_a, s_b, main_masked)
        return half(2 * i + 1, c, s_b, s_a, main_masked)

    def tail_body(i, c):
        c = step(2 * i, c, s_a, s_b, False)
        return step(2 * i + 1, c, s_b, s_a, True)

    acc_ref[...] = jnp.zeros(acc_ref.shape, F32)
    init = (tuple(jnp.full((1, tq), NEG, F32) for _ in range(N_HEADS)),
            tuple(jnp.zeros((1, tq), F32) for _ in range(N_HEADS)), produce(0, s_a))
    c = lax.fori_loop(0, n_main, main_body, init)
    _, ls, _ = lax.fori_loop(n_main, (nkb + 1) // 2, tail_body, c)
    return ls


def _flash_scratch(tq, tk):
    return [pltpu.VMEM((N_HEADS, tk, tq), F32), pltpu.VMEM((N_HEADS, tk, tq), F32), pltpu.VMEM((256, tq), F32)]


def _flash_finish(o_ref, g_ref, acc_ref, ls):
    for h in range(N_HEADS):
        r0 = h * HEAD_DIM
        acc_ref[r0:r0 + HEAD_DIM, :] = acc_ref[r0:r0 + HEAD_DIM, :] / ls[h]
    y = acc_ref[...].T
    o_ref[0] = (y * g_ref[0].astype(F32)).astype(BF16)


A_WIN = A_BAND + 2 * CHUNK


def _attn_a_kernel(q_ref, k_ref, v_ref, bias_ref, g_ref, o_ref, *, row0, lo_valid, hi_valid, group):
    hid = _head_lane_id()
    for gi in range(group):
        c = pl.program_id(1) * group + gi
        start = pl.multiple_of(row0 + c * CHUNK, CHUNK)
        q = q_ref[0, gi * CHUNK:(gi + 1) * CHUNK, :]
        qs = jnp.concatenate([_keep_lanes(q, hid == h) for h in range(N_HEADS)], axis=0)
        kb = k_ref[0, pl.ds(start, A_WIN), :]
        vb = v_ref[0, pl.ds(start, A_WIN), :]
        s = _dot_nt(qs, kb) + bias_ref[...]
        row = start + lax.broadcasted_iota(I32, (1, A_WIN), 1)
        valid = (row >= lo_valid) & (row < hi_valid)
        s = jnp.where(valid, s, NEG)
        m = jnp.max(s, axis=-1, keepdims=True)
        p = jnp.where(valid, jnp.exp(s - m), 0.0)
        l = jnp.sum(p, axis=-1, keepdims=True)
        o = _dot(p.astype(BF16), vb) / l
        y = o[(N_HEADS - 1) * CHUNK:]
        for h in range(N_HEADS - 2, -1, -1):
            y = jnp.where(hid == h, o[h * CHUNK:(h + 1) * CHUNK], y)
        gate = g_ref[0, gi * CHUNK:(gi + 1) * CHUNK, :].astype(F32)
        o_ref[0, gi * CHUNK:(gi + 1) * CHUNK, :] = (y * gate).astype(BF16)


def _attn_a_call(q, kfull, vfull, bias, gates, row0, lo_valid, hi_valid, group):
    b, tp, _ = q.shape
    rows_kv = kfull.shape[1]
    assert row0 % CHUNK == 0 and row0 + q.shape[1] - CHUNK + A_WIN <= kfull.shape[1]
    kern = functools.partial(_attn_a_kernel, row0=row0, lo_valid=lo_valid, hi_valid=hi_valid, group=group)
    rows = CHUNK * group
    return pl.pallas_call(
        kern, grid=(b, tp // rows),
        in_specs=[pl.BlockSpec((1, rows, 256), lambda bb, c: (bb, c, 0)),
                  pl.BlockSpec((1, rows_kv, 256), lambda bb, c: (bb, 0, 0)),
                  pl.BlockSpec((1, rows_kv, 256), lambda bb, c: (bb, 0, 0)),
                  pl.BlockSpec((N_HEADS * CHUNK, A_WIN), lambda bb, c: (0, 0)),
                  pl.BlockSpec((1, rows, 256), lambda bb, c: (bb, c, 0))],
        out_specs=pl.BlockSpec((1, rows, 256), lambda bb, c: (bb, c, 0)),
        out_shape=jax.ShapeDtypeStruct((b, tp, 256), BF16),
        compiler_params=_cparams(2), name="attn_band",
    )(q, kfull, vfull, bias, gates)


def _attn_b_kernel(q_ref, k_ref, v_ref, g_ref, o_ref, sa_ref, sb_ref, acc_ref, *, p_len, l_len, tq, tk, nq):
    i = pl.program_id(1) if nq > 1 else 0
    q0 = p_len + i * tq
    lane = lax.broadcasted_iota(I32, (1, 256), 1)
    qm = []
    for h in range(N_HEADS):
        qg = q_ref[0, :, (h // 2) * 256:(h // 2) * 256 + 256]
        lo = (h % 2) * (MLA_NOPE + MLA_ROPE)
        qm.append(_keep_lanes(qg, (lane >= lo) & (lane < lo + MLA_NOPE + MLA_ROPE)))
    qchunk = (q0 + lax.broadcasted_iota(I32, (1, tq), 1)) // CHUNK
    nkb = jnp.minimum((q0 + tq + tk - 1) // tk, k_ref.shape[1] // tk)
    nfull = jnp.minimum(((q0 // CHUNK + 1) * CHUNK) // tk, nkb)

    def k_block(j, h):
        return k_ref[0, pl.ds(pl.multiple_of(j * tk, tk), tk), (h // 2) * 256:(h // 2) * 256 + 256]

    def mask_fn(j):
        kpos = j * tk + lax.broadcasted_iota(I32, (tk, 1), 0)
        return ((kpos // CHUNK) <= qchunk) & (kpos < l_len)

    ls = _flash_run(nfull, nkb, k_ref.shape[1] // tk, qm=qm, k_block=k_block, v_block=lambda j, h: v_ref[0, j, h],
                    col_sub=None, row_add=None, mask_fn=mask_fn, s_refs=(sa_ref, sb_ref), acc_ref=acc_ref, tq=tq)
    _flash_finish(o_ref, g_ref, acc_ref, ls)


def _attn_b_call(qb, kb, vaug, gates, p_len, l_len, tq, tk):
    b, t, _ = qb.shape
    lp = kb.shape[1]
    nq = t // tq
    kern = functools.partial(_attn_b_kernel, p_len=p_len, l_len=l_len, tq=tq, tk=tk, nq=nq)
    return pl.pallas_call(
        kern, grid=(b, nq),
        in_specs=[pl.BlockSpec((1, tq, 512), lambda bb, i: (bb, i, 0)),
                  pl.BlockSpec((1, lp, 512), lambda bb, i: (bb, 0, 0)),
                  pl.BlockSpec((1,) + vaug.shape[1:], lambda bb, i: (bb, 0, 0, 0, 0)),
                  pl.BlockSpec((1, tq, 256), lambda bb, i: (bb, i, 1))],
        out_specs=pl.BlockSpec((1, tq, 256), lambda bb, i: (bb, i, 0)),
        out_shape=jax.ShapeDtypeStruct((b, t, 256), BF16),
        scratch_shapes=_flash_scratch(tq, tk),
        compiler_params=_cparams(2), name="attn_latent",
    )(qb, kb, vaug, gates)


def _attn_c_kernel(q_ref, iq_ref, iw_ref, k_ref, v_ref, ik_ref, g_ref, o_ref, key_ref, top_ref, sa_ref, sb_ref,
                   acc_ref,
                   *, p_len, l_len, k_sel, tq, tk, nq):
    i = pl.program_id(1) if nq > 1 else 0
    q0 = p_len + i * tq
    nkb_max = k_ref.shape[1] // tk
    nkb = jnp.minimum((q0 + tq + tk - 1) // tk, nkb_max)
    qchunk = (q0 + lax.broadcasted_iota(I32, (1, tq), 1)) // CHUNK
    hid = _head_lane_id()

    iq = iq_ref[0]
    ihid = lax.broadcasted_iota(I32, (1, 256), 1) // IDX_DIM
    iqm = [_keep_lanes(iq, ihid == h) for h in range(IDX_HEADS)]
    iw = iw_ref[0] * (IDX_DIM ** -0.5 * IDX_HEADS ** -0.5)

    def score_body(edge):
        def body(j, carry):
            ks = pl.multiple_of(j * tk, tk)
            ikb = ik_ref[0, pl.ds(ks, tk), :]
            score = jnp.zeros((tk, tq), F32)
            for h in range(IDX_HEADS):
                score = score + iw[h:h + 1, :] * jnp.maximum(_dot_nt(ikb, iqm[h]), 0.0)
            if edge:
                kpos = ks + lax.broadcasted_iota(I32, (tk, 1), 0)
                score = jnp.where(((kpos // CHUNK) <= qchunk) & (kpos < l_len), score, -jnp.inf)
            key_ref[j] = score
            top_ref[j] = pltpu.bitcast(pltpu.bitcast(score, I32) & -65536, F32).astype(BF16)
            return carry
        return body

    n_free = jnp.minimum(((q0 // CHUNK + 1) * CHUNK) // tk, nkb)
    lax.fori_loop(0, n_free, score_body(False), 0)
    lax.fori_loop(n_free, nkb, score_body(True), 0)

    def cand_value(cand):
        bits = jnp.where(cand < 0, cand ^ 0x7FFFFFFF, cand)
        bits = jnp.where((bits > 0) & (bits < 0x00800000), 0x00800000, bits)
        return pltpu.bitcast(bits, F32)

    def count(pred_fn):
        def cbody(j, acc):
            pf = jnp.where(pred_fn(key_ref[j], j), 1.0, 0.0)
            for r in range(tk // 8):
                acc = acc + pf[r * 8:(r + 1) * 8]
            return acc
        acc = lax.fori_loop(0, nkb, cbody, jnp.zeros((8, tq), F32))
        return jnp.sum(acc, axis=0, keepdims=True)

    def count_top(cand_b):
        one, zero = jnp.ones((16, tq), BF16), jnp.zeros((16, tq), BF16)

        def cbody(j, acc):
            part = zero
            for r in range(tk // 16):
                part = part + jnp.where(top_ref[j, r * 16:(r + 1) * 16, :] >= cand_b, one, zero)
            return acc + part.astype(F32)
        acc = lax.fori_loop(0, nkb, cbody, jnp.zeros((16, tq), F32))
        return jnp.sum(acc, axis=0, keepdims=True)

    def top_body(it, c):
        t, n_ge = c
        cand = t + jnp.left_shift(jnp.int32(1), 15 - it)
        cbits = pltpu.bitcast(cand_value(cand * 65536), I32) & -65536
        cnt = count_top(jnp.broadcast_to(pltpu.bitcast(cbits, F32), (16, tq)).astype(BF16))
        ok = cnt >= k_sel
        return jnp.where(ok, cand, t), jnp.where(ok, cnt, n_ge)

    t16, n_ge = lax.fori_loop(0, 16, top_body, (jnp.full((1, tq), -32768, I32), jnp.zeros((1, tq), F32)))

    def bis_body(it, c):
        t, n_ge = c
        cand = t + jnp.left_shift(jnp.int32(1), 15 - it)
        cval = cand_value(cand)
        cnt = count(lambda kb, j: kb >= cval)
        ok = cnt >= k_sel
        return jnp.where(ok, cand, t), jnp.where(ok, cnt, n_ge)

    tcode, n_ge = lax.fori_loop(0, 16, bis_body, (t16 * 65536, n_ge))
    thr = cand_value(jnp.maximum(tcode, MOST_NEGATIVE_CODE))

    def idx_of(j):
        return j * tk + lax.broadcasted_iota(I32, (tk, tq), 0)

    def tie_cut(_):
        need = k_sel - count(lambda kb, j: kb > thr)

        def tie_body(it, jc):
            cand = jc + jnp.left_shift(jnp.int32(1), 12 - it)
            cnt = count(lambda kb, j: (kb == thr) & (idx_of(j) < cand))
            return jnp.where(cnt < need, cand, jc)

        return lax.fori_loop(0, 13, tie_body, jnp.zeros((1, tq), I32))

    has_tie = jnp.max(jnp.where(n_ge > k_sel, 1.0, 0.0)) > 0.0
    jcut = lax.cond(has_tie, tie_cut, lambda _: jnp.full((1, tq), 2 ** 30, I32), 0)

    q = q_ref[0]
    qm = [_keep_lanes(q, hid == h) for h in range(N_HEADS)]

    def attend(mask_fn):
        return _flash_run(None, nkb, nkb_max, qm=qm,
                          k_block=lambda j, h: k_ref[0, pl.ds(pl.multiple_of(j * tk, tk), tk), :],
                          v_block=lambda j, h: v_ref[0, j, 0], col_sub=None, row_add=None, mask_fn=mask_fn,
                          s_refs=(sa_ref, sb_ref), acc_ref=acc_ref, tq=tq)

    def with_ties(_):
        def mask_fn(j):
            kb = key_ref[j]
            return (kb > thr) | ((kb == thr) & (idx_of(j) <= jcut))
        return attend(mask_fn)

    ls = lax.cond(has_tie, with_ties, lambda _: attend(lambda j: key_ref[j] >= thr), 0)
    _flash_finish(o_ref, g_ref, acc_ref, ls)


def _attn_c_call(cq, ciq, iwt, krep, vaug, ikrep, gates, p_len, l_len, k_sel, tq, tk):
    b, t, _ = cq.shape
    lp = krep.shape[1]
    assert lp <= 8192
    nq = t // tq
    kern = functools.partial(_attn_c_kernel, p_len=p_len, l_len=l_len, k_sel=float(k_sel), tq=tq, tk=tk, nq=nq)
    return pl.pallas_call(
        kern, grid=(b, nq),
        in_specs=[pl.BlockSpec((1, tq, 256), lambda bb, i: (bb, i, 0)),
                  pl.BlockSpec((1, tq, 256), lambda bb, i: (bb, i, 0)),
                  pl.BlockSpec((1, 8, tq), lambda bb, i: (bb, 0, i)),
                  pl.BlockSpec((1, lp, 256), lambda bb, i: (bb, 0, 0)),
                  pl.BlockSpec((1,) + vaug.shape[1:], lambda bb, i: (bb, 0, 0, 0, 0)),
                  pl.BlockSpec((1, lp, 256), lambda bb, i: (bb, 0, 0)),
                  pl.BlockSpec((1, tq, 256), lambda bb, i: (bb, i, 2))],
        out_specs=pl.BlockSpec((1, tq, 256), lambda bb, i: (bb, i, 0)),
        out_shape=jax.ShapeDtypeStruct((b, t, 256), BF16),
        scratch_shapes=[pltpu.VMEM((lp // tk, tk, tq), F32), pltpu.VMEM((lp // tk, tk, tq), BF16)]
        + _flash_scratch(tq, tk),
        compiler_params=_cparams(2), name="attn_sparse",
    )(cq, ciq, iwt, krep, vaug, ikrep, gates)


def _attn_d_kernel(q_ref, k_ref, v_ref, lf_ref, g_ref, o_ref, fc_ref, fk_ref, sa_ref, sb_ref, acc_ref,
                   *, p_len, tq, tk, nq):
    i = pl.program_id(1) if nq > 1 else 0
    lp = k_ref.shape[1]
    nch = lp // 128

    def cumulate():
        x = lf_ref[0]
        lane = lax.broadcasted_iota(I32, (1, lp), 1)
        step = 1
        while step < lp:
            x = x + jnp.where(lane >= step, pltpu.roll(x, step, 1), 0.0)
            step *= 2
        x = x * LOG2E
        for c in range(nch):
            fc_ref[c] = x[:, c * 128:(c + 1) * 128]

        eye = lax.broadcasted_iota(I32, (128, 128), 0) == lax.broadcasted_iota(I32, (128, 128), 1)

        def spread(c, carry):
            rows = fc_ref[c]
            for h in range(N_HEADS):
                col = jnp.sum(jnp.where(eye, rows[h:h + 1, :], 0.0), axis=1, keepdims=True)
                fk_ref[h, pl.ds(pl.multiple_of(c * 128, 128), 128), :] = jnp.broadcast_to(col, (128, 128))
            return carry

        lax.fori_loop(0, nch, spread, 0)

    if nq > 1:
        pl.when(i == 0)(cumulate)
    else:
        cumulate()

    q0 = p_len + i * tq
    hid = _head_lane_id()
    q = q_ref[0]
    qm = [_keep_lanes(q, hid == h) for h in range(N_HEADS)]
    c0 = q0 // 128
    fq_rows = [fc_ref[c0 + c] for c in range(tq // 128)]
    fq = [jnp.concatenate([r[h:h + 1, :] for r in fq_rows], axis=1) for h in range(N_HEADS)]
    qpos = q0 + lax.broadcasted_iota(I32, (1, tq), 1)
    nkb = jnp.minimum((q0 + tq + tk - 1) // tk, lp // tk)
    nfull = jnp.minimum((q0 + 1) // tk, nkb)

    def col_sub(j, h):
        fk = fk_ref[h, pl.ds(pl.multiple_of(j * tk, tk), tk), :]
        return jnp.concatenate([fk] * (tq // 128), axis=1)

    def mask_fn(j):
        return (j * tk + lax.broadcasted_iota(I32, (tk, 1), 0)) <= qpos

    ls = _flash_run(nfull, nkb, lp // tk, qm=qm,
                    k_block=lambda j, h: k_ref[0, pl.ds(pl.multiple_of(j * tk, tk), tk), :],
                    v_block=lambda j, h: v_ref[0, j, h], col_sub=col_sub, row_add=fq, mask_fn=mask_fn,
                    s_refs=(sa_ref, sb_ref), acc_ref=acc_ref, tq=tq)
    _flash_finish(o_ref, g_ref, acc_ref, ls)


def _attn_d_call(dq, kd, vaug, lft, gates, p_len, tq, tk):
    b, t, _ = dq.shape
    lp = kd.shape[1]
    nq = t // tq
    assert p_len % 128 == 0 and tq % 128 == 0 and p_len + t <= lp
    kern = functools.partial(_attn_d_kernel, p_len=p_len, tq=tq, tk=tk, nq=nq)
    return pl.pallas_call(
        kern, grid=(b, nq),
        in_specs=[pl.BlockSpec((1, tq, 256), lambda bb, i: (bb, i, 0)),
                  pl.BlockSpec((1, lp, 256), lambda bb, i: (bb, 0, 0)),
                  pl.BlockSpec((1,) + vaug.shape[1:], lambda bb, i: (bb, 0, 0, 0, 0)),
                  pl.BlockSpec((1, 8, lp), lambda bb, i: (bb, 0, 0)),
                  pl.BlockSpec((1, tq, 256), lambda bb, i: (bb, i, 3))],
        out_specs=pl.BlockSpec((1, tq, 256), lambda bb, i: (bb, i, 0)),
        out_shape=jax.ShapeDtypeStruct((b, t, 256), BF16),
        scratch_shapes=[pltpu.VMEM((lp // 128, 8, 128), F32), pltpu.VMEM((N_HEADS, lp, 128), F32)]
        + _flash_scratch(tq, tk),
        compiler_params=_cparams(2), name="attn_forget",
    )(dq, kd, vaug, lft, gates)


def _out_kernel(ya_ref, yb_ref, yc_ref, yd_ref, x_ref, p_ref, wo_ref, gple_ref, wg_ref, wp_ref, o_ref):
    mixed = (_dot(ya_ref[0], wo_ref[0:256, :]) + _dot(yb_ref[0], wo_ref[256:512, :])
             + _dot(yc_ref[0], wo_ref[512:768, :]) + _dot(yd_ref[0], wo_ref[768:1024, :]))
    x1 = x_ref[0] + mixed
    ms = jnp.mean(x1 * x1, axis=-1, keepdims=True)
    xn = (x1 * lax.rsqrt(ms + EPS) * gple_ref[...]).astype(BF16)
    gate = 1.0 / (1.0 + jnp.exp(-_dot(xn, wg_ref[...])))
    o_ref[0] = x1 + gate * _dot(p_ref[0, 0].astype(BF16), wp_ref[...])


def _out_call(ys, x, p_all, layer, lw, tm):
    b, t, _ = x.shape

    def const(shape):
        return pl.BlockSpec(shape, lambda bb, i: (0,) * len(shape))

    def rows(w):
        return pl.BlockSpec((1, tm, w), lambda bb, i: (bb, i, 0))

    return pl.pallas_call(
        _out_kernel, grid=(b, t // tm),
        in_specs=[rows(256)] * 4 + [rows(D_MODEL), pl.BlockSpec((1, 1, tm, PLE_DIM), lambda bb, i: (layer, bb, i, 0)),
                                    const((1024, D_MODEL)), const((1, D_MODEL)),
                                    const((D_MODEL, D_MODEL)), const((PLE_DIM, D_MODEL))],
        out_specs=rows(D_MODEL),
        out_shape=jax.ShapeDtypeStruct((b, t, D_MODEL), F32),
        compiler_params=_cparams(2), name="out_proj",
    )(*ys, x, p_all, lw['w_out'], lw['g_ple'], lw['w_ple_gate'], lw['w_ple_proj'])


def _seg_matrix(segments):
    m = np.zeros((256, 256), np.float32)
    for lo, n in segments:
        m[lo:lo + n, lo:lo + n] = 1.0 / n
    return jnp.asarray(m, BF16)


_PAIR_SEGS = ((0, 64), (64, 32), (96, 64), (160, 32), (192, 64))


def _pair_cols(head):
    return (head // 2) * 256 + (head % 2) * (MLA_NOPE + MLA_ROPE)


def _prep_layer(i, g_in, w_in, g_qk, g_rope, a_rel_bias, b_g_cq, b_w_uq, b_g_ckv, b_w_ukv, d_b_f, w_out, g_ple,
                w_ple_gate, w_ple_proj):
    w = w_in[i]
    cols = []
    for n in _MY_ORDER:
        if n == 'pad56':
            cols.append(jnp.zeros((D_MODEL, 56), F32))
        else:
            o, wd = _REF_OFF[n]
            cols.append(w[:, o:o + wd])
    lw = {'w_in': jnp.concatenate(cols, axis=1).astype(BF16), 'g_in': g_in[i][None, :]}
    wt = [w[:, _REF_OFF[n][0]:_REF_OFF[n][0] + _REF_OFF[n][1]] for n in ('d_v', 'c_v', 'c_iw', 'd_f')]
    wt.append(jnp.zeros((D_MODEL, XT_ROWS - XT_DF - N_HEADS), F32))
    lw['w_t'] = jnp.concatenate(wt, axis=1).T.astype(BF16)
    lw['b_col'] = jnp.concatenate([d_b_f[i], jnp.zeros((8 - N_HEADS,), F32)])[:, None]
    g = g_qk[i]
    sc = HEAD_DIM ** -0.5
    rows = [jnp.tile(g[0], 4) * sc, jnp.tile(g[1], 4), jnp.tile(g[2], 4) * (sc * LOG2E),
            jnp.tile(g[4], 4) * (sc * LOG2E), jnp.tile(g[5], 4)]
    lw['g4'] = jnp.stack(rows + [jnp.zeros((256,), F32)] * 3)
    lw['hseg'] = _seg_matrix(tuple((h * 64, 64) for h in range(4)))
    lw['segq'] = _seg_matrix(_PAIR_SEGS)
    lw['g_cq'] = b_g_cq[i][None, :]
    lw['g_ckv'] = b_g_ckv[i][None, :]
    qscale = (MLA_NOPE + MLA_ROPE) ** -0.5 * LOG2E
    hw = MLA_NOPE + MLA_ROPE
    wuq, gqb, wuk, gkb, wuv = [], [], [], [], []
    e_kr = np.zeros((128, 512), np.float32)
    for h in range(N_HEADS):
        wuq.append(b_w_uq[i][:, h * hw:(h + 1) * hw])
        gqb += [g[6] * qscale, g_rope[i][0] * qscale]
        srck = h * (MLA_NOPE + HEAD_DIM)
        wuk += [b_w_ukv[i][:, srck:srck + MLA_NOPE], jnp.zeros((MLA_KV_LORA, MLA_ROPE), F32)]
        gkb += [g[7], jnp.zeros((MLA_ROPE,), F32)]
        wuv.append(b_w_ukv[i][:, srck + MLA_NOPE:srck + MLA_NOPE + HEAD_DIM])
        e_kr[np.arange(MLA_ROPE), _pair_cols(h) + MLA_NOPE + np.arange(MLA_ROPE)] = 1.0
        if h % 2 == 1:
            wuq.append(jnp.zeros((MLA_Q_LORA, 256 - 2 * hw), F32))
            wuk.append(jnp.zeros((MLA_KV_LORA, 256 - 2 * hw), F32))
            gqb.append(jnp.zeros((256 - 2 * hw,), F32))
            gkb.append(jnp.zeros((256 - 2 * hw,), F32))
    lw['w_uq'] = jnp.concatenate(wuq, axis=1).astype(BF16)
    lw['g_qb'] = jnp.concatenate(gqb)[None, :]
    lw['w_uk'] = jnp.concatenate(wuk, axis=1).astype(BF16)
    lw['g_kb'] = jnp.concatenate(gkb)[None, :]
    lw['e_kr'] = jnp.asarray(e_kr, BF16)
    lw['w_uvt'] = jnp.concatenate(wuv, axis=1).T.astype(BF16)
    one = jnp.ones((128,), F32)
    lw['srow'] = jnp.stack([jnp.concatenate([g[3], one[HEAD_DIM:]]), jnp.concatenate([g_rope[i][1], one[MLA_ROPE:]])]
                           + [one] * 6)
    ab = a_rel_bias[i]
    ext = jnp.concatenate([jnp.broadcast_to(ab[:, 2 * REL_CLIP:], (N_HEADS, A_WIN - REL_CLIP - 1)),
                           ab[:, REL_CLIP - CHUNK + 1:][:, ::-1]], axis=1)
    lw['a_bias'] = jnp.stack([ext[:, CHUNK - 1 - q:CHUNK - 1 - q + A_WIN] for q in range(CHUNK)],
                             axis=1).reshape(N_HEADS * CHUNK, A_WIN)
    lw['w_out'] = w_out[i].astype(BF16)
    lw['g_ple'] = g_ple[i][None, :]
    lw['w_ple_gate'] = w_ple_gate[i].astype(BF16)
    lw['w_ple_proj'] = w_ple_proj[i].astype(BF16)
    return lw


def _cos_sin(pos, rot_dim):
    half = rot_dim // 2
    inv = jnp.float32(ROPE_THETA) ** (-jnp.arange(half, dtype=F32) / half)
    ang = pos.astype(F32)[:, None] * inv[None, :]
    return [jnp.cos(ang), jnp.sin(ang)]


def _rot_placement():
    src = {MLA_ROPE: 1, ROT_DIM: 1 + MLA_ROPE}
    e = np.zeros((1 + MLA_ROPE + ROT_DIM, TAB_W), np.float32)
    sections = ((TAB_QB, 512, [_pair_cols(h) + MLA_NOPE for h in range(N_HEADS)], MLA_ROPE),
                (TAB_CQ, 256, [h * HEAD_DIM for h in range(N_HEADS)], ROT_DIM),
                (TAB_S1, 128, [0], ROT_DIM), (TAB_S2, 128, [0], MLA_ROPE))
    for base, width, starts, rot_dim in sections:
        half = rot_dim // 2
        e[0, base:base + width] = 1.0
        for st in starts:
            for i in range(half):
                c, s = src[rot_dim] + i, src[rot_dim] + half + i
                e[0, base + st + i] = e[0, base + st + half + i] = 0.0
                e[c, base + st + i] = e[c, base + st + half + i] = 1.0
                e[s, base + width + st + half + i] = 1.0
                e[s, base + 2 * width + st + i] = -1.0
    return jnp.asarray(e)


def _rot_tables(pos):
    src = jnp.concatenate([jnp.ones((pos.shape[0], 1), F32)] + _cos_sin(pos, MLA_ROPE) + _cos_sin(pos, ROT_DIM),
                          axis=1)
    hi = src.astype(BF16)
    lo = (src - hi.astype(F32)).astype(BF16)
    place = _rot_placement().astype(BF16)
    return jnp.dot(jnp.concatenate([hi, lo], axis=1), jnp.concatenate([place, place], axis=0),
                   preferred_element_type=F32)


def _pad_axis(a, size, axis):
    if a.shape[axis] == size:
        return a
    pads = [(0, 0)] * a.ndim
    pads[axis] = (0, size - a.shape[axis])
    return jnp.pad(a, pads)


def _cat_rows(past, new, rows):
    a = new if past is None else jnp.concatenate([past.astype(new.dtype), new], axis=1)
    return _pad_axis(a, rows, 1)


def _cat_lanes(past_t, new_t, lanes):
    a = new_t if past_t is None else jnp.concatenate([past_t.astype(new_t.dtype), new_t], axis=2)
    return _pad_axis(a, lanes, 2)


def _value_blocks(vt, tk):
    b, r, lp = vt.shape
    h = r // HEAD_DIM
    v = vt.astype(BF16).reshape(b, h, HEAD_DIM, lp)
    v = jnp.concatenate([v, jnp.ones((b, h, V_ROWS - HEAD_DIM, lp), BF16)], axis=2)
    return v.reshape(b, h, V_ROWS, lp // tk, tk).transpose(0, 3, 1, 2, 4)


def _layer(x, p_all, layer, past, lw, tab, prev_narrow, *, fold_batch, tm, tq, tq_c, tk, a_group):
    b, t, _ = x.shape
    p_len = 0 if past is None else past[2].shape[1]
    l_len = p_len + t
    tqp = -(-t // tq) * tq
    tqp_c = -(-t // tq_c) * tq_c
    lp = -(-(p_len + max(tqp, tqp_c)) // tk) * tk

    xin = x.reshape(1, b * t, D_MODEL) if fold_batch else x
    a_direct = past is None and not fold_batch
    a_front = -(-(CHUNK + A_BAND) // tm) * tm if a_direct else 0
    outs = list(_inproj_call(xin, tab, lw, tm, a_front, prev_narrow))
    narrow = tuple(outs[o] for o in NARROW_OUTS)
    for o in NARROW_OUTS:
        outs[o] = None
    if fold_batch:
        xt = outs[18][0].reshape(XT_ROWS, b, t).transpose(1, 0, 2)
        outs = ([None if o is None else o.reshape((b, t) + o.shape[2:]) for o in outs[:18]]
                + [xt, outs[19].reshape(b, t, 1024), None, None] + [o.reshape(b, t, 256) for o in outs[22:24]])
    (aq, akb, avb, akf, avf, qb, ckvf, cq, ciq, c_k, c_v, s2, b_kr, c_ik, dq, dkb, dkf, dvf, xt, gates,
     dva, cva, krep_new, ikrep_new) = outs
    direct_values = past is None and not fold_batch and tm == tk

    lft = xt[:, XT_DF:XT_ROWS]
    d_lf = lft[:, :N_HEADS, :].transpose(0, 2, 1)
    state = (akf.reshape(b, t, N_HEADS, HEAD_DIM), avf.reshape(b, t, N_HEADS, HEAD_DIM), ckvf, b_kr, c_k, c_v, c_ik,
             dkf, dvf, d_lf)

    if past is None:
        pa_k = pa_v = pb_ckv = pb_kr = pc_k = pc_v = pc_ik = pd_k = pd_v = pd_lf = None
        pa = 0
    else:
        pa_k, pa_v, pb_ckv, pb_kr, pc_k, pc_v, pc_ik, pd_k, pd_v, pd_lf = past
        pa = pa_k.shape[1]
        pa_k = pa_k.reshape(b, pa, 256)
        pa_v = pa_v.reshape(b, pa, 256)
        pd_k = pd_k.reshape(b, p_len, 256)
        pd_v = pd_v.reshape(b, p_len, 256)

    tp = -(-t // (CHUNK * a_group)) * (CHUNK * a_group)
    if a_direct:
        assert tp == t
        a_keys, a_vals = akb, avb
        row0, front = a_front - (CHUNK + A_BAND), a_front
    else:
        front = CHUNK + A_BAND - pa
        a_rows = CHUNK + A_BAND + tp

        def band_src(pst, new):
            parts = [jnp.zeros((b, front, 256), BF16)]
            if pst is not None:
                parts.append(pst.astype(BF16))
            parts.append(new)
            return _pad_axis(jnp.concatenate(parts, axis=1), a_rows, 1)

        a_keys, a_vals, row0 = band_src(pa_k, akb), band_src(pa_v, avb), 0
    ya = _attn_a_call(_pad_axis(aq, tp, 1), a_keys, a_vals, lw['a_bias'], _pad_axis(gates, tp, 1), row0, front,
                      front + pa + t, a_group)[:, :t]

    gates_q = _pad_axis(gates, tqp, 1)

    ckv_all = _cat_rows(pb_ckv, ckvf, lp)
    kr_new = s2
    kr_past = None if past is None else _pad_axis(pb_kr, 128, 2)
    kr_all = _cat_rows(kr_past, kr_new, lp)
    kb, vba = _bkv_call(ckv_all, kr_all, lw, tk)
    yb = _attn_b_call(_pad_axis(qb, tqp, 1), kb, vba, gates_q, p_len, l_len, tq, tk)[:, :t]

    krep = _cat_rows(None if past is None else jnp.tile(pc_k.astype(BF16), (1, 1, N_HEADS)), krep_new, lp)
    ikrep = _cat_rows(None if past is None else jnp.tile(pc_ik.astype(BF16), (1, 1, IDX_HEADS)), ikrep_new, lp)
    if not direct_values:
        cva = _value_blocks(_cat_lanes(None if past is None else pc_v.transpose(0, 2, 1), xt[:, XT_CV:XT_IW], lp), tk)
    iwt = _pad_axis(xt[:, XT_IW:XT_DF], tqp_c, 2)
    k_sel = min(DSA_TOPK, l_len // 4)
    yc = _attn_c_call(_pad_axis(cq, tqp_c, 1), _pad_axis(ciq, tqp_c, 1), iwt, krep, cva, ikrep,
                      _pad_axis(gates, tqp_c, 1), p_len, l_len, k_sel, tq_c, tk)[:, :t]

    kd = _cat_rows(None if past is None else pd_k.astype(BF16), dkb, lp)
    if not direct_values:
        dva = _value_blocks(_cat_lanes(None if past is None else pd_v.transpose(0, 2, 1), xt[:, XT_DV:XT_CV], lp), tk)
    lf_past = None if past is None else _pad_axis(pd_lf.transpose(0, 2, 1), 8, 1)
    lf_all = _cat_lanes(lf_past, lft, lp)
    yd = _attn_d_call(_pad_axis(dq, tqp, 1), kd, dva, lf_all, gates_q, p_len, tq, tk)[:, :t]

    if fold_batch:
        x_new = _out_call([y.reshape(1, b * t, 256) for y in (ya, yb, yc, yd)], x.reshape(1, b * t, D_MODEL),
                          p_all.reshape(p_all.shape[0], 1, b * t, PLE_DIM), layer, lw, tm).reshape(b, t, D_MODEL)
    else:
        x_new = _out_call([ya, yb, yc, yd], x, p_all, layer, lw, 4 * tm)
    return x_new, state, narrow


def kernel(x_prompt, x_sample, cache_a_k, cache_a_v, cache_b_ckv, cache_b_krope, cache_c_k, cache_c_v, cache_c_idx_k,
           cache_d_k, cache_d_v, cache_d_logf, p_prompt, p_sample, g_in, w_in, g_qk, g_rope, a_rel_bias, b_g_cq,
           b_w_uq, b_g_ckv, b_w_ukv, d_b_f, w_out, g_ple, w_ple_gate, w_ple_proj):
    depth = w_in.shape[0]
    b, t = x_prompt.shape[:2]
    bs, ts = x_sample.shape[:2]
    past_len = cache_b_ckv.shape[2]
    tab_p = _rot_tables(jnp.arange(t))
    tab_s = jnp.tile(_rot_tables(past_len + jnp.arange(ts)), (bs, 1))
    xp, xs = x_prompt, x_sample
    states_p, states_s = [], []
    narrow_p = narrow_s = None
    for i in range(depth):
        lw = _prep_layer(i, g_in, w_in, g_qk, g_rope, a_rel_bias, b_g_cq, b_w_uq, b_g_ckv, b_w_ukv, d_b_f, w_out,
                         g_ple, w_ple_gate, w_ple_proj)
        xp, st_p, narrow_p = _layer(xp, p_prompt, i, None, lw, tab_p, narrow_p, fold_batch=False, tm=256, tq=512,
                                     tq_c=512, tk=256, a_group=32)
        past = (cache_a_k[i], cache_a_v[i], cache_b_ckv[i], cache_b_krope[i], cache_c_k[i], cache_c_v[i],
                cache_c_idx_k[i], cache_d_k[i], cache_d_v[i], cache_d_logf[i])
        xs, st_s, narrow_s = _layer(xs, p_sample, i, past, lw, tab_s, narrow_s, fold_batch=True, tm=bs * ts, tq=128,
                                     tq_c=128, tk=256, a_group=1)
        states_p.append(st_p)
        states_s.append(st_s)
    keep = min(A_BAND, t)
    states_p = [(st[0][:, t - keep:], st[1][:, t - keep:]) + tuple(st[2:]) for st in states_p]

    def assemble(states, narrow, bb, tt):
        out = [None if z[0] is None else jnp.stack(z) for z in zip(*states)]
        c_k, c_v, b_kr, c_ik, d_k, d_v = (a.reshape(depth, bb, tt, a.shape[-1]) for a in narrow)
        out[3], out[4], out[5], out[6] = b_kr, c_k, c_v, c_ik
        out[7] = d_k.reshape(depth, bb, tt, N_HEADS, HEAD_DIM)
        out[8] = d_v.reshape(depth, bb, tt, N_HEADS, HEAD_DIM)
        return out

    return (xp, xs, *assemble(states_p, narrow_p, b, t), *assemble(states_s, narrow_s, bs, ts))
```

```python
import functools

import numpy as np
import jax
import jax.numpy as jnp
from jax import lax
from jax.experimental import pallas as pl
from jax.experimental.pallas import tpu as pltpu

F32 = jnp.float32
BF16 = jnp.bfloat16
I32 = jnp.int32

D_MODEL = 1024
CHUNK = 64
EPS = 1e-6
HEAD_DIM = 64
N_HEADS = 4
GROUP_WIDTH = 256
ROT_DIM = 16
ROPE_THETA = 500000.0
A_BAND = 8 * CHUNK
REL_CLIP = 128
MLA_Q_LORA = 384
MLA_KV_LORA = 128
MLA_NOPE = 64
MLA_ROPE = 32
IDX_HEADS = 8
IDX_DIM = 32
DSA_TOPK = 256
PLE_DIM = 256

_REF_SPLITS = (
    ('a_q', 256), ('a_k', 256), ('a_v', 256), ('a_g', 256),
    ('b_cq', 384), ('b_ckv', 128), ('b_kr', 32), ('b_g', 256),
    ('c_q', 256), ('c_k', 64), ('c_v', 64), ('c_iq', 256), ('c_ik', 32), ('c_iw', 8), ('c_g', 256),
    ('d_q', 256), ('d_k', 256), ('d_v', 256), ('d_f', 4), ('d_g', 256),
)
_REF_OFF = {}
_o = 0
for _n, _w in _REF_SPLITS:
    _REF_OFF[_n] = (_o, _w)
    _o += _w

_MY_ORDER = ('a_q', 'a_k', 'a_v', 'a_g', 'b_cq', 'b_ckv', 'b_g', 'c_q', 'c_iq', 'c_g', 'd_q', 'd_k', 'd_v', 'd_g',
             'c_k', 'c_v', 'b_kr', 'c_ik', 'c_iw', 'pad56')
_MY_OFF = {}
_o = 0
for _n in _MY_ORDER:
    _w = 56 if _n == 'pad56' else _REF_OFF[_n][1]
    _MY_OFF[_n] = _o
    _o += _w
W_IN_COLS = _o
_PROJ_GROUPS = ((0, _MY_OFF['b_cq']), (_MY_OFF['b_cq'], _MY_OFF['c_q']), (_MY_OFF['c_q'], _MY_OFF['d_q']),
                (_MY_OFF['d_q'], _MY_OFF['c_k']), (_MY_OFF['c_k'], W_IN_COLS))

XT_DV, XT_CV, XT_IW, XT_DF, XT_ROWS = 0, 256, 320, 328, 336

TAB_QB, TAB_CQ, TAB_S1, TAB_S2 = 0, 1536, 2304, 2688
TAB_W = 3072

V_ROWS = HEAD_DIM + 16
NEG = -1e30
LOG2E = 1.4426950408889634
MOST_NEGATIVE_CODE = -2139095040
VMEM_LIMIT = 56 * 1024 * 1024


def _cparams(n_axes):
    return pltpu.CompilerParams(dimension_semantics=("arbitrary",) * n_axes, vmem_limit_bytes=VMEM_LIMIT)


def _dot(a, b):
    return jnp.dot(a, b, preferred_element_type=F32)


def _dot_nt(a, b):
    return lax.dot_general(a, b, (((1,), (1,)), ((), ())), preferred_element_type=F32)


def _seg_mean_sq(t, seg):
    sq = t * t
    hi = sq.astype(BF16)
    lo = (sq - hi.astype(F32)).astype(BF16)
    return _dot(hi, seg) + _dot(lo, seg)


def _rotate(t, cos, sin_up, sin_dn, half):
    w = t.shape[-1]
    return t * cos + pltpu.roll(t, half, 1) * sin_up + pltpu.roll(t, w - half, 1) * sin_dn


def _silu(g):
    return g * (1.0 / (1.0 + jnp.exp(-g)))


def _log_sigmoid(v):
    return jnp.minimum(v, 0.0) - jnp.log1p(jnp.exp(-jnp.abs(v)))


N_INPROJ_IN = 16
NARROW_OUTS = (9, 10, 12, 13, 16, 17)


def _inproj_kernel(*refs, n_prev):
    n_in = N_INPROJ_IN + (len(NARROW_OUTS) if n_prev else 0)
    _inproj_body(refs[N_INPROJ_IN:n_in], n_prev, *refs[:N_INPROJ_IN], *refs[n_in:])


def _inproj_body(prev, n_prev,
                 x_ref, tab_ref, gin_ref, w_ref, wt_ref, bcol_ref, g4_ref, hseg_ref, gcq_ref, wuq_ref, segq_ref,
                   gqb_ref, gckv_ref, srow_ref, akz_ref, avz_ref,
                   aq_o, akb_o, avb_o, akf_o, avf_o, qb_o, ckvf_o, cq_o, ciq_o, ck_o, cv_o, s2_o, bkr_o, cik_o,
                   dq_o, dkb_o, dkf_o, dvf_o, xt_o, gate_o, dva_o, cva_o, krep_o, ikrep_o):
    del akz_ref, avz_ref

    def store_narrow(slot, o_ref, rows):
        for layer in range(n_prev):
            o_ref[layer, 0] = prev[slot][layer, 0]
        o_ref[n_prev, 0] = rows

    x = x_ref[0]
    ms = jnp.mean(x * x, axis=-1, keepdims=True)
    xn = (x * lax.rsqrt(ms + EPS) * gin_ref[...]).astype(BF16)

    group_out = {}

    def proj(name, n):
        c0 = _MY_OFF[name]
        g0, g1 = next((a, b) for a, b in _PROJ_GROUPS if a <= c0 < b)
        if g0 not in group_out:
            group_out[g0] = _dot(xn, w_ref[:, g0:g1])
        return group_out[g0][:, c0 - g0:c0 - g0 + n]

    hseg = hseg_ref[...]

    def headnorm(t, row):
        return t * lax.rsqrt(_seg_mean_sq(t, hseg) + EPS) * g4_ref[row:row + 1, :]

    def fullnorm(t, g):
        return t * lax.rsqrt(jnp.mean(t * t, axis=-1, keepdims=True) + EPS) * g

    aq_o[0] = headnorm(proj('a_q', 256), 0).astype(BF16)
    ak = headnorm(proj('a_k', 256), 1)
    akf_o[0] = ak
    akb_o[0] = ak.astype(BF16)
    av = proj('a_v', 256)
    avf_o[0] = av
    avb_o[0] = av.astype(BF16)
    gate_o[0, :, 0:256] = _silu(proj('a_g', 256)).astype(BF16)

    cqn = fullnorm(proj('b_cq', MLA_Q_LORA), gcq_ref[...]).astype(BF16)
    qb = _dot(cqn, wuq_ref[...])
    segq = segq_ref[...]
    ms_q = jnp.concatenate([_seg_mean_sq(qb[:, :256], segq), _seg_mean_sq(qb[:, 256:], segq)], axis=1)
    qbn = qb * lax.rsqrt(ms_q + EPS) * gqb_ref[...]
    qbn = _rotate(qbn, tab_ref[:, TAB_QB:TAB_QB + 512], tab_ref[:, TAB_QB + 512:TAB_QB + 1024],
                  tab_ref[:, TAB_QB + 1024:TAB_QB + 1536], MLA_ROPE // 2)
    qb_o[0] = qbn.astype(BF16)
    ckvf_o[0] = fullnorm(proj('b_ckv', MLA_KV_LORA), gckv_ref[...])
    gate_o[0, :, 256:512] = _silu(proj('b_g', 256)).astype(BF16)

    cq = headnorm(proj('c_q', 256), 2)
    cq = _rotate(cq, tab_ref[:, TAB_CQ:TAB_CQ + 256], tab_ref[:, TAB_CQ + 256:TAB_CQ + 512],
                 tab_ref[:, TAB_CQ + 512:TAB_CQ + 768], ROT_DIM // 2)
    cq_o[0] = cq.astype(BF16)
    ciq_o[0] = proj('c_iq', 256).astype(BF16)
    gate_o[0, :, 512:768] = _silu(proj('c_g', 256)).astype(BF16)

    dq_o[0] = headnorm(proj('d_q', 256), 3).astype(BF16)
    dk = headnorm(proj('d_k', 256), 4)
    store_narrow(4, dkf_o, dk)
    dkb_o[0] = dk.astype(BF16)
    store_narrow(5, dvf_o, proj('d_v', 256))
    gate_o[0, :, 768:1024] = _silu(proj('d_g', 256)).astype(BF16)

    lane = lax.broadcasted_iota(I32, (1, 128), 1)
    t = proj('c_k', 128)
    m64 = lane < HEAD_DIM
    ms1 = jnp.sum(jnp.where(m64, t * t, 0.0), axis=-1, keepdims=True) * (1.0 / HEAD_DIM)
    t = jnp.where(m64, t * lax.rsqrt(ms1 + EPS), t) * srow_ref[0:1, :]
    t = _rotate(t, tab_ref[:, TAB_S1:TAB_S1 + 128], tab_ref[:, TAB_S1 + 128:TAB_S1 + 256],
                tab_ref[:, TAB_S1 + 256:TAB_S1 + 384], ROT_DIM // 2)
    store_narrow(0, ck_o, t[:, :HEAD_DIM])
    krep_o[0] = jnp.concatenate([t[:, :HEAD_DIM]] * N_HEADS, axis=1).astype(BF16)
    store_narrow(1, cv_o, t[:, HEAD_DIM:])
    t = proj('b_kr', 128)
    m32 = lane < MLA_ROPE
    ms2 = jnp.sum(jnp.where(m32, t * t, 0.0), axis=-1, keepdims=True) * (1.0 / MLA_ROPE)
    t = jnp.where(m32, t * lax.rsqrt(ms2 + EPS), t) * srow_ref[1:2, :]
    t = _rotate(t, tab_ref[:, TAB_S2:TAB_S2 + 128], tab_ref[:, TAB_S2 + 128:TAB_S2 + 256],
                tab_ref[:, TAB_S2 + 256:TAB_S2 + 384], MLA_ROPE // 2)
    s2_o[0] = t
    store_narrow(2, bkr_o, t[:, :MLA_ROPE])
    store_narrow(3, cik_o, t[:, MLA_ROPE:MLA_ROPE + IDX_DIM])
    ikrep_o[0] = jnp.concatenate([t[:, MLA_ROPE:MLA_ROPE + IDX_DIM]] * IDX_HEADS, axis=1).astype(BF16)

    xt = _dot_nt(wt_ref[...], xn)
    xt_o[0, 0:XT_DF, :] = xt[0:XT_DF]
    xt_o[0, XT_DF:XT_ROWS, :] = _log_sigmoid(xt[XT_DF:XT_ROWS] + bcol_ref[...])
    _store_value_block(dva_o, xt[XT_DV:XT_CV].astype(BF16), N_HEADS)
    _store_value_block(cva_o, xt[XT_CV:XT_IW].astype(BF16), 1)


def _inproj_call(x, tab, lw, tm, a_front, prev_narrow):
    n_prev = 0 if prev_narrow is None else prev_narrow[0].shape[0]
    bk, tk_, _ = x.shape
    front_blocks = a_front // tm
    grid = (tk_ // tm, bk)

    def const(shape):
        return pl.BlockSpec(shape, lambda i, b: (0,) * len(shape))

    def rows(w):
        return pl.BlockSpec((1, tm, w), lambda i, b: (b, i, 0))

    in_specs = [
        rows(D_MODEL),
        pl.BlockSpec((tm, TAB_W), lambda i, b: (i, 0)),
        const((1, D_MODEL)), const((D_MODEL, W_IN_COLS)), const((XT_ROWS, D_MODEL)), const((8, 1)),
        const((8, 256)), const((256, 256)), const((1, MLA_Q_LORA)), const((MLA_Q_LORA, 512)), const((256, 256)),
        const((1, 512)), const((1, MLA_KV_LORA)), const((8, 128)),
        pl.BlockSpec(memory_space=pl.ANY), pl.BlockSpec(memory_space=pl.ANY),
    ]
    assert len(in_specs) == N_INPROJ_IN
    if n_prev:
        in_specs += [pl.BlockSpec((n_prev, 1, tm, a.shape[-1]), lambda i, b: (0, b, i, 0)) for a in prev_narrow]
    band_zero = jnp.zeros((bk, a_front + tk_, 256), BF16)
    widths = [(256, BF16), (256, BF16), (256, BF16), (256, F32), (256, F32), (512, BF16), (128, F32), (256, BF16),
              (256, BF16), (64, F32), (64, F32), (128, F32), (32, F32), (32, F32), (256, BF16), (256, BF16), (256, F32),
              (256, F32)]
    out_shape = [jax.ShapeDtypeStruct((bk, tk_, w), dt) for w, dt in widths]
    out_specs = [rows(w) for w, _ in widths]
    for o in NARROW_OUTS:
        w = widths[o][0]
        out_shape[o] = jax.ShapeDtypeStruct((n_prev + 1, bk, tk_, w), F32)
        out_specs[o] = pl.BlockSpec((n_prev + 1, 1, tm, w), lambda i, b: (0, b, i, 0))
    for o in (1, 2):
        out_shape[o] = jax.ShapeDtypeStruct(band_zero.shape, BF16)
        out_specs[o] = pl.BlockSpec((1, tm, 256), lambda i, b: (b, i + front_blocks, 0))
    out_shape.append(jax.ShapeDtypeStruct((bk, XT_ROWS, tk_), F32))
    out_specs.append(pl.BlockSpec((1, XT_ROWS, tm), lambda i, b: (b, 0, i)))
    out_shape.append(jax.ShapeDtypeStruct((bk, tk_, 1024), BF16))
    out_specs.append(rows(1024))
    for heads in (N_HEADS, 1):
        out_shape.append(jax.ShapeDtypeStruct((bk, tk_ // tm, heads, V_ROWS, tm), BF16))
        out_specs.append(pl.BlockSpec((1, 1, heads, V_ROWS, tm), lambda i, b: (b, i, 0, 0, 0)))
    for _ in range(2):
        out_shape.append(jax.ShapeDtypeStruct((bk, tk_, 256), BF16))
        out_specs.append(rows(256))
    return pl.pallas_call(
        functools.partial(_inproj_kernel, n_prev=n_prev), grid=grid, in_specs=in_specs, out_specs=out_specs,
        out_shape=out_shape,
        input_output_aliases={14: 1, 15: 2}, compiler_params=_cparams(2), name="inproj",
    )(x, tab, lw['g_in'], lw['w_in'], lw['w_t'], lw['b_col'], lw['g4'], lw['hseg'], lw['g_cq'], lw['w_uq'],
      lw['segq'], lw['g_qb'], lw['g_ckv'], lw['srow'], band_zero, band_zero, *(prev_narrow or ()))


def _store_value_block(va_o, vt, heads, blk=0):
    tk = vt.shape[1]
    for h in range(heads):
        va_o[0, blk, h, 0:HEAD_DIM, :] = vt[h * HEAD_DIM:(h + 1) * HEAD_DIM]
        va_o[0, blk, h, HEAD_DIM:V_ROWS, :] = jnp.ones((V_ROWS - HEAD_DIM, tk), BF16)


def _bkv_kernel(ckv_ref, kr_ref, wk_ref, wvt_ref, e_ref, segq_ref, gk_ref, kb_o, va_o):
    c = ckv_ref[0].astype(BF16)
    kn = _dot(c, wk_ref[...])
    segq = segq_ref[...]
    ms = jnp.concatenate([_seg_mean_sq(kn[:, :256], segq), _seg_mean_sq(kn[:, 256:], segq)], axis=1)
    kn = kn * lax.rsqrt(ms + EPS) * gk_ref[...]
    kr = _dot(kr_ref[0].astype(BF16), e_ref[...])
    kb_o[0] = (kn + kr).astype(BF16)
    vt = _dot_nt(wvt_ref[...], c).astype(BF16)
    tk = va_o.shape[-1]
    for blk in range(vt.shape[1] // tk):
        _store_value_block(va_o, vt[:, blk * tk:(blk + 1) * tk], N_HEADS, blk)


def _bkv_call(ckv, kr, lw, tk):
    b, lp, _ = ckv.shape
    nb = 2 if (lp // tk) % 2 == 0 else 1
    tm = nb * tk

    def const(shape):
        return pl.BlockSpec(shape, lambda bb, i: (0,) * len(shape))

    def rows(w):
        return pl.BlockSpec((1, tm, w), lambda bb, i: (bb, i, 0))

    return pl.pallas_call(
        _bkv_kernel, grid=(b, lp // tm),
        in_specs=[rows(128), rows(128), const((128, 512)), const((256, 128)), const((128, 512)), const((256, 256)),
                  const((1, 512))],
        out_specs=[rows(512), pl.BlockSpec((1, nb, N_HEADS, V_ROWS, tk), lambda bb, i: (bb, i, 0, 0, 0))],
        out_shape=[jax.ShapeDtypeStruct((b, lp, 512), BF16),
                   jax.ShapeDtypeStruct((b, lp // tk, N_HEADS, V_ROWS, tk), BF16)],
        compiler_params=_cparams(2), name="mla_kv",
    )(ckv, kr, lw['w_uk'], lw['w_uvt'], lw['e_kr'], lw['segq'], lw['g_kb'])


def _head_lane_id(width=256):
    return lax.broadcasted_iota(I32, (1, width), 1) // HEAD_DIM


def _keep_lanes(x, pred):
    return jnp.where(pred, x.astype(F32), 0.0).astype(BF16)


def _flash_run(nfull, nkb, nkb_max, *, qm, k_block, v_block, col_sub, row_add, mask_fn, s_refs, acc_ref, tq):
    s_a, s_b = s_refs

    track_max = nfull is not None

    def produce(j, s_out):
        jc = jnp.minimum(j, nkb_max - 1)
        tops = []
        for h in range(N_HEADS):
            s = _dot_nt(k_block(jc, h), qm[h])
            if col_sub is not None:
                s = s - col_sub(jc, h)
            s_out[h] = s
            tops.append(jnp.max(s, axis=0, keepdims=True) if track_max else jnp.zeros((1, tq), F32))
        return tuple(tops)

    def half(j, c, s_in, s_out, masked):
        ms, ls, tops = c
        next_tops = produce(j + 1, s_out)
        valid = mask_fn(j) if masked else None
        new_m, new_l = [], []
        for h in range(N_HEADS):
            t = s_in[h]
            if masked:
                t = jnp.where(valid, t, NEG)
                m_cur = jnp.max(t, axis=0, keepdims=True)
            else:
                m_cur = tops[h]
            if row_add is not None:
                m_cur = m_cur + row_add[h]
            m_new = jnp.maximum(ms[h], m_cur)
            alpha = jnp.exp2(ms[h] - m_new)
            off = m_new if row_add is None else m_new - row_add[h]
            p = jnp.exp2(t - off).astype(BF16)
            pv = _dot(v_block(j, h), p)
            r0 = h * HEAD_DIM
            acc_ref[r0:r0 + HEAD_DIM, :] = acc_ref[r0:r0 + HEAD_DIM, :] * alpha + pv[:HEAD_DIM]
            new_l.append(ls[h] * alpha + pv[HEAD_DIM:HEAD_DIM + 1])
            new_m.append(m_new)
        return tuple(new_m), tuple(new_l), next_tops

    def step(j, c, s_in, s_out, may_end):
        def run(c):
            if nfull is None:
                return half(j, c, s_in, s_out, True)
            return lax.cond(j >= nfull, lambda cc: half(j, cc, s_in, s_out, True),
                            lambda cc: half(j, cc, s_in, s_out, False), c)
        if not may_end:
            return run(c)
        return lax.cond(j >= nkb, lambda cc: cc, run, c)

    main_masked = nfull is None
    n_main = (nkb if main_masked else nfull) // 2

    def main_body(i, c):
        c = half(2 * i, c, s_a, s_b, main_masked)
        return half(2 * i + 1, c, s_b, s_a, main_masked)

    def tail_body(i, c):
        c = step(2 * i, c, s_a, s_b, False)
        return step(2 * i + 1, c, s_b, s_a, True)

    acc_ref[...] = jnp.zeros(acc_ref.shape, F32)
    init = (tuple(jnp.full((1, tq), NEG, F32) for _ in range(N_HEADS)),
            tuple(jnp.zeros((1, tq), F32) for _ in range(N_HEADS)), produce(0, s_a))
    c = lax.fori_loop(0, n_main, main_body, init)
    _, ls, _ = lax.fori_loop(n_main, (nkb + 1) // 2, tail_body, c)
    return ls


def _flash_scratch(tq, tk):
    return [pltpu.VMEM((N_HEADS, tk, tq), F32), pltpu.VMEM((N_HEADS, tk, tq), F32), pltpu.VMEM((256, tq), F32)]


def _flash_finish(o_ref, g_ref, acc_ref, ls):
    for h in range(N_HEADS):
        r0 = h * HEAD_DIM
        acc_ref[r0:r0 + HEAD_DIM, :] = acc_ref[r0:r0 + HEAD_DIM, :] / ls[h]
    y = acc_ref[...].T
    o_ref[0] = (y * g_ref[0].astype(F32)).astype(BF16)


A_WIN = A_BAND + 2 * CHUNK


def _attn_a_kernel(q_ref, k_ref, v_ref, bias_ref, g_ref, o_ref, *, row0, lo_valid, hi_valid, group):
    hid = _head_lane_id()
    for gi in range(group):
        c = pl.program_id(1) * group + gi
        start = pl.multiple_of(row0 + c * CHUNK, CHUNK)
        q = q_ref[0, gi * CHUNK:(gi + 1) * CHUNK, :]
        qs = jnp.concatenate([_keep_lanes(q, hid == h) for h in range(N_HEADS)], axis=0)
        kb = k_ref[0, pl.ds(start, A_WIN), :]
        vb = v_ref[0, pl.ds(start, A_WIN), :]
        s = _dot_nt(qs, kb) + bias_ref[...]
        row = start + lax.broadcasted_iota(I32, (1, A_WIN), 1)
        valid = (row >= lo_valid) & (row < hi_valid)
        s = jnp.where(valid, s, NEG)
        m = jnp.max(s, axis=-1, keepdims=True)
        p = jnp.where(valid, jnp.exp(s - m), 0.0)
        l = jnp.sum(p, axis=-1, keepdims=True)
        o = _dot(p.astype(BF16), vb) / l
        y = o[(N_HEADS - 1) * CHUNK:]
        for h in range(N_HEADS - 2, -1, -1):
            y = jnp.where(hid == h, o[h * CHUNK:(h + 1) * CHUNK], y)
        gate = g_ref[0, gi * CHUNK:(gi + 1) * CHUNK, :].astype(F32)
        o_ref[0, gi * CHUNK:(gi + 1) * CHUNK, :] = (y * gate).astype(BF16)


def _attn_a_call(q, kfull, vfull, bias, gates, row0, lo_valid, hi_valid, group):
    b, tp, _ = q.shape
    rows_kv = kfull.shape[1]
    assert row0 % CHUNK == 0 and row0 + q.shape[1] - CHUNK + A_WIN <= kfull.shape[1]
    kern = functools.partial(_attn_a_kernel, row0=row0, lo_valid=lo_valid, hi_valid=hi_valid, group=group)
    rows = CHUNK * group
    return pl.pallas_call(
        kern, grid=(b, tp // rows),
        in_specs=[pl.BlockSpec((1, rows, 256), lambda bb, c: (bb, c, 0)),
                  pl.BlockSpec((1, rows_kv, 256), lambda bb, c: (bb, 0, 0)),
                  pl.BlockSpec((1, rows_kv, 256), lambda bb, c: (bb, 0, 0)),
                  pl.BlockSpec((N_HEADS * CHUNK, A_WIN), lambda bb, c: (0, 0)),
                  pl.BlockSpec((1, rows, 256), lambda bb, c: (bb, c, 0))],
        out_specs=pl.BlockSpec((1, rows, 256), lambda bb, c: (bb, c, 0)),
        out_shape=jax.ShapeDtypeStruct((b, tp, 256), BF16),
        compiler_params=_cparams(2), name="attn_band",
    )(q, kfull, vfull, bias, gates)


def _attn_b_kernel(q_ref, k_ref, v_ref, g_ref, o_ref, sa_ref, sb_ref, acc_ref, *, p_len, l_len, tq, tk, nq):
    i = pl.program_id(1) if nq > 1 else 0
    q0 = p_len + i * tq
    lane = lax.broadcasted_iota(I32, (1, 256), 1)
    qm = []
    for h in range(N_HEADS):
        qg = q_ref[0, :, (h // 2) * 256:(h // 2) * 256 + 256]
        lo = (h % 2) * (MLA_NOPE + MLA_ROPE)
        qm.append(_keep_lanes(qg, (lane >= lo) & (lane < lo + MLA_NOPE + MLA_ROPE)))
    qchunk = (q0 + lax.broadcasted_iota(I32, (1, tq), 1)) // CHUNK
    nkb = jnp.minimum((q0 + tq + tk - 1) // tk, k_ref.shape[1] // tk)
    nfull = jnp.minimum(((q0 // CHUNK + 1) * CHUNK) // tk, nkb)

    def k_block(j, h):
        return k_ref[0, pl.ds(pl.multiple_of(j * tk, tk), tk), (h // 2) * 256:(h // 2) * 256 + 256]

    def mask_fn(j):
        kpos = j * tk + lax.broadcasted_iota(I32, (tk, 1), 0)
        return ((kpos // CHUNK) <= qchunk) & (kpos < l_len)

    ls = _flash_run(nfull, nkb, k_ref.shape[1] // tk, qm=qm, k_block=k_block, v_block=lambda j, h: v_ref[0, j, h],
                    col_sub=None, row_add=None, mask_fn=mask_fn, s_refs=(sa_ref, sb_ref), acc_ref=acc_ref, tq=tq)
    _flash_finish(o_ref, g_ref, acc_ref, ls)


def _attn_b_call(qb, kb, vaug, gates, p_len, l_len, tq, tk):
    b, t, _ = qb.shape
    lp = kb.shape[1]
    nq = t // tq
    kern = functools.partial(_attn_b_kernel, p_len=p_len, l_len=l_len, tq=tq, tk=tk, nq=nq)
    return pl.pallas_call(
        kern, grid=(b, nq),
        in_specs=[pl.BlockSpec((1, tq, 512), lambda bb, i: (bb, i, 0)),
                  pl.BlockSpec((1, lp, 512), lambda bb, i: (bb, 0, 0)),
                  pl.BlockSpec((1,) + vaug.shape[1:], lambda bb, i: (bb, 0, 0, 0, 0)),
                  pl.BlockSpec((1, tq, 256), lambda bb, i: (bb, i, 1))],
        out_specs=pl.BlockSpec((1, tq, 256), lambda bb, i: (bb, i, 0)),
        out_shape=jax.ShapeDtypeStruct((b, t, 256), BF16),
        scratch_shapes=_flash_scratch(tq, tk),
        compiler_params=_cparams(2), name="attn_latent",
    )(qb, kb, vaug, gates)


def _attn_c_kernel(q_ref, iq_ref, iw_ref, k_ref, v_ref, ik_ref, g_ref, o_ref, key_ref, top_ref, sa_ref, sb_ref,
                   acc_ref,
                   *, p_len, l_len, k_sel, tq, tk, nq):
    i = pl.program_id(1) if nq > 1 else 0
    q0 = p_len + i * tq
    nkb_max = k_ref.shape[1] // tk
    nkb = jnp.minimum((q0 + tq + tk - 1) // tk, nkb_max)
    qchunk = (q0 + lax.broadcasted_iota(I32, (1, tq), 1)) // CHUNK
    hid = _head_lane_id()

    iq = iq_ref[0]
    ihid = lax.broadcasted_iota(I32, (1, 256), 1) // IDX_DIM
    iqm = [_keep_lanes(iq, ihid == h) for h in range(IDX_HEADS)]
    iw = iw_ref[0] * (IDX_DIM ** -0.5 * IDX_HEADS ** -0.5)

    def score_body(edge):
        def body(j, carry):
            ks = pl.multiple_of(j * tk, tk)
            ikb = ik_ref[0, pl.ds(ks, tk), :]
            score = jnp.zeros((tk, tq), F32)
            for h in range(IDX_HEADS):
                score = score + iw[h:h + 1, :] * jnp.maximum(_dot_nt(ikb, iqm[h]), 0.0)
            if edge:
                kpos = ks + lax.broadcasted_iota(I32, (tk, 1), 0)
                score = jnp.where(((kpos // CHUNK) <= qchunk) & (kpos < l_len), score, -jnp.inf)
            key_ref[j] = score
            top_ref[j] = pltpu.bitcast(pltpu.bitcast(score, I32) & -65536, F32).astype(BF16)
            return carry
        return body

    n_free = jnp.minimum(((q0 // CHUNK + 1) * CHUNK) // tk, nkb)
    lax.fori_loop(0, n_free, score_body(False), 0)
    lax.fori_loop(n_free, nkb, score_body(True), 0)

    def cand_value(cand):
        bits = jnp.where(cand < 0, cand ^ 0x7FFFFFFF, cand)
        bits = jnp.where((bits > 0) & (bits < 0x00800000), 0x00800000, bits)
        return pltpu.bitcast(bits, F32)

    def count(pred_fn):
        def cbody(j, acc):
            pf = jnp.where(pred_fn(key_ref[j], j), 1.0, 0.0)
            for r in range(tk // 8):
                acc = acc + pf[r * 8:(r + 1) * 8]
            return acc
        acc = lax.fori_loop(0, nkb, cbody, jnp.zeros((8, tq), F32))
        return jnp.sum(acc, axis=0, keepdims=True)

    def count_top(cand_b):
        one, zero = jnp.ones((16, tq), BF16), jnp.zeros((16, tq), BF16)

        def cbody(j, acc):
            part = zero
            for r in range(tk // 16):
                part = part + jnp.where(top_ref[j, r * 16:(r + 1) * 16, :] >= cand_b, one, zero)
            return acc + part.astype(F32)
        acc = lax.fori_loop(0, nkb, cbody, jnp.zeros((16, tq), F32))
        return jnp.sum(acc, axis=0, keepdims=True)

    def top_body(it, c):
        t, n_ge = c
        cand = t + jnp.left_shift(jnp.int32(1), 15 - it)
        cbits = pltpu.bitcast(cand_value(cand * 65536), I32) & -65536
        cnt = count_top(jnp.broadcast_to(pltpu.bitcast(cbits, F32), (16, tq)).astype(BF16))
        ok = cnt >= k_sel
        return jnp.where(ok, cand, t), jnp.where(ok, cnt, n_ge)

    t16, n_ge = lax.fori_loop(0, 16, top_body, (jnp.full((1, tq), -32768, I32), jnp.zeros((1, tq), F32)))

    def bis_body(it, c):
        t, n_ge = c
        cand = t + jnp.left_shift(jnp.int32(1), 15 - it)
        cval = cand_value(cand)
        cnt = count(lambda kb, j: kb >= cval)
        ok = cnt >= k_sel
        return jnp.where(ok, cand, t), jnp.where(ok, cnt, n_ge)

    tcode, n_ge = lax.fori_loop(0, 16, bis_body, (t16 * 65536, n_ge))
    thr = cand_value(jnp.maximum(tcode, MOST_NEGATIVE_CODE))

    def idx_of(j):
        return j * tk + lax.broadcasted_iota(I32, (tk, tq), 0)

    def tie_cut(_):
        need = k_sel - count(lambda kb, j: kb > thr)

        def tie_body(it, jc):
            cand = jc + jnp.left_shift(jnp.int32(1), 12 - it)
            cnt = count(lambda kb, j: (kb == thr) & (idx_of(j) < cand))
            return jnp.where(cnt < need, cand, jc)

        return lax.fori_loop(0, 13, tie_body, jnp.zeros((1, tq), I32))

    has_tie = jnp.max(jnp.where(n_ge > k_sel, 1.0, 0.0)) > 0.0
    jcut = lax.cond(has_tie, tie_cut, lambda _: jnp.full((1, tq), 2 ** 30, I32), 0)

    q = q_ref[0]
    qm = [_keep_lanes(q, hid == h) for h in range(N_HEADS)]

    def attend(mask_fn):
        return _flash_run(None, nkb, nkb_max, qm=qm,
                          k_block=lambda j, h: k_ref[0, pl.ds(pl.multiple_of(j * tk, tk), tk), :],
                          v_block=lambda j, h: v_ref[0, j, 0], col_sub=None, row_add=None, mask_fn=mask_fn,
                          s_refs=(sa_ref, sb_ref), acc_ref=acc_ref, tq=tq)

    def with_ties(_):
        def mask_fn(j):
            kb = key_ref[j]
            return (kb > thr) | ((kb == thr) & (idx_of(j) <= jcut))
        return attend(mask_fn)

    ls = lax.cond(has_tie, with_ties, lambda _: attend(lambda j: key_ref[j] >= thr), 0)
    _flash_finish(o_ref, g_ref, acc_ref, ls)


def _attn_c_call(cq, ciq, iwt, krep, vaug, ikrep, gates, p_len, l_len, k_sel, tq, tk):
    b, t, _ = cq.shape
    lp = krep.shape[1]
    assert lp <= 8192
    nq = t // tq
    kern = functools.partial(_attn_c_kernel, p_len=p_len, l_len=l_len, k_sel=float(k_sel), tq=tq, tk=tk, nq=nq)
    return pl.pallas_call(
        kern, grid=(b, nq),
        in_specs=[pl.BlockSpec((1, tq, 256), lambda bb, i: (bb, i, 0)),
                  pl.BlockSpec((1, tq, 256), lambda bb, i: (bb, i, 0)),
                  pl.BlockSpec((1, 8, tq), lambda bb, i: (bb, 0, i)),
                  pl.BlockSpec((1, lp, 256), lambda bb, i: (bb, 0, 0)),
                  pl.BlockSpec((1,) + vaug.shape[1:], lambda bb, i: (bb, 0, 0, 0, 0)),
                  pl.BlockSpec((1, lp, 256), lambda bb, i: (bb, 0, 0)),
                  pl.BlockSpec((1, tq, 256), lambda bb, i: (bb, i, 2))],
        out_specs=pl.BlockSpec((1, tq, 256), lambda bb, i: (bb, i, 0)),
        out_shape=jax.ShapeDtypeStruct((b, t, 256), BF16),
        scratch_shapes=[pltpu.VMEM((lp // tk, tk, tq), F32), pltpu.VMEM((lp // tk, tk, tq), BF16)]
        + _flash_scratch(tq, tk),
        compiler_params=_cparams(2), name="attn_sparse",
    )(cq, ciq, iwt, krep, vaug, ikrep, gates)


def _attn_d_kernel(q_ref, k_ref, v_ref, lf_ref, g_ref, o_ref, fc_ref, fk_ref, sa_ref, sb_ref, acc_ref,
                   *, p_len, tq, tk, nq):
    i = pl.program_id(1) if nq > 1 else 0
    lp = k_ref.shape[1]
    nch = lp // 128

    def cumulate():
        x = lf_ref[0]
        lane = lax.broadcasted_iota(I32, (1, lp), 1)
        step = 1
        while step < lp:
            x = x + jnp.where(lane >= step, pltpu.roll(x, step, 1), 0.0)
            step *= 2
        x = x * LOG2E
        for c in range(nch):
            fc_ref[c] = x[:, c * 128:(c + 1) * 128]

        eye = lax.broadcasted_iota(I32, (128, 128), 0) == lax.broadcasted_iota(I32, (128, 128), 1)

        def spread(c, carry):
            rows = fc_ref[c]
            for h in range(N_HEADS):
                col = jnp.sum(jnp.where(eye, rows[h:h + 1, :], 0.0), axis=1, keepdims=True)
                fk_ref[h, pl.ds(pl.multiple_of(c * 128, 128), 128), :] = jnp.broadcast_to(col, (128, 128))
            return carry

        lax.fori_loop(0, nch, spread, 0)

    if nq > 1:
        pl.when(i == 0)(cumulate)
    else:
        cumulate()

    q0 = p_len + i * tq
    hid = _head_lane_id()
    q = q_ref[0]
    qm = [_keep_lanes(q, hid == h) for h in range(N_HEADS)]
    c0 = q0 // 128
    fq_rows = [fc_ref[c0 + c] for c in range(tq // 128)]
    fq = [jnp.concatenate([r[h:h + 1, :] for r in fq_rows], axis=1) for h in range(N_HEADS)]
    qpos = q0 + lax.broadcasted_iota(I32, (1, tq), 1)
    nkb = jnp.minimum((q0 + tq + tk - 1) // tk, lp // tk)
    nfull = jnp.minimum((q0 + 1) // tk, nkb)

    def col_sub(j, h):
        fk = fk_ref[h, pl.ds(pl.multiple_of(j * tk, tk), tk), :]
        return jnp.concatenate([fk] * (tq // 128), axis=1)

    def mask_fn(j):
        return (j * tk + lax.broadcasted_iota(I32, (tk, 1), 0)) <= qpos

    ls = _flash_run(nfull, nkb, lp // tk, qm=qm,
                    k_block=lambda j, h: k_ref[0, pl.ds(pl.multiple_of(j * tk, tk), tk), :],
                    v_block=lambda j, h: v_ref[0, j, h], col_sub=col_sub, row_add=fq, mask_fn=mask_fn,
                    s_refs=(sa_ref, sb_ref), acc_ref=acc_ref, tq=tq)
    _flash_finish(o_ref, g_ref, acc_ref, ls)


def _attn_d_call(dq, kd, vaug, lft, gates, p_len, tq, tk):
    b, t, _ = dq.shape
    lp = kd.shape[1]
    nq = t // tq
    assert p_len % 128 == 0 and tq % 128 == 0 and p_len + t <= lp
    kern = functools.partial(_attn_d_kernel, p_len=p_len, tq=tq, tk=tk, nq=nq)
    return pl.pallas_call(
        kern, grid=(b, nq),
        in_specs=[pl.BlockSpec((1, tq, 256), lambda bb, i: (bb, i, 0)),
                  pl.BlockSpec((1, lp, 256), lambda bb, i: (bb, 0, 0)),
                  pl.BlockSpec((1,) + vaug.shape[1:], lambda bb, i: (bb, 0, 0, 0, 0)),
                  pl.BlockSpec((1, 8, lp), lambda bb, i: (bb, 0, 0)),
                  pl.BlockSpec((1, tq, 256), lambda bb, i: (bb, i, 3))],
        out_specs=pl.BlockSpec((1, tq, 256), lambda bb, i: (bb, i, 0)),
        out_shape=jax.ShapeDtypeStruct((b, t, 256), BF16),
        scratch_shapes=[pltpu.VMEM((lp // 128, 8, 128), F32), pltpu.VMEM((N_HEADS, lp, 128), F32)]
        + _flash_scratch(tq, tk),
        compiler_params=_cparams(2), name="attn_forget",
    )(dq, kd, vaug, lft, gates)


def _out_kernel(ya_ref, yb_ref, yc_ref, yd_ref, x_ref, p_ref, wo_ref, gple_ref, wg_ref, wp_ref, o_ref):
    mixed = (_dot(ya_ref[0], wo_ref[0:256, :]) + _dot(yb_ref[0], wo_ref[256:512, :])
             + _dot(yc_ref[0], wo_ref[512:768, :]) + _dot(yd_ref[0], wo_ref[768:1024, :]))
    x1 = x_ref[0] + mixed
    ms = jnp.mean(x1 * x1, axis=-1, keepdims=True)
    xn = (x1 * lax.rsqrt(ms + EPS) * gple_ref[...]).astype(BF16)
    gate = 1.0 / (1.0 + jnp.exp(-_dot(xn, wg_ref[...])))
    o_ref[0] = x1 + gate * _dot(p_ref[0, 0].astype(BF16), wp_ref[...])


def _out_call(ys, x, p_all, layer, lw, tm):
    b, t, _ = x.shape

    def const(shape):
        return pl.BlockSpec(shape, lambda bb, i: (0,) * len(shape))

    def rows(w):
        return pl.BlockSpec((1, tm, w), lambda bb, i: (bb, i, 0))

    return pl.pallas_call(
        _out_kernel, grid=(b, t // tm),
        in_specs=[rows(256)] * 4 + [rows(D_MODEL), pl.BlockSpec((1, 1, tm, PLE_DIM), lambda bb, i: (layer, bb, i, 0)),
                                    const((1024, D_MODEL)), const((1, D_MODEL)),
                                    const((D_MODEL, D_MODEL)), const((PLE_DIM, D_MODEL))],
        out_specs=rows(D_MODEL),
        out_shape=jax.ShapeDtypeStruct((b, t, D_MODEL), F32),
        compiler_params=_cparams(2), name="out_proj",
    )(*ys, x, p_all, lw['w_out'], lw['g_ple'], lw['w_ple_gate'], lw['w_ple_proj'])


def _seg_matrix(segments):
    m = np.zeros((256, 256), np.float32)
    for lo, n in segments:
        m[lo:lo + n, lo:lo + n] = 1.0 / n
    return jnp.asarray(m, BF16)


_PAIR_SEGS = ((0, 64), (64, 32), (96, 64), (160, 32), (192, 64))


def _pair_cols(head):
    return (head // 2) * 256 + (head % 2) * (MLA_NOPE + MLA_ROPE)


def _prep_layer(i, g_in, w_in, g_qk, g_rope, a_rel_bias, b_g_cq, b_w_uq, b_g_ckv, b_w_ukv, d_b_f, w_out, g_ple,
                w_ple_gate, w_ple_proj):
    w = w_in[i]
    cols = []
    for n in _MY_ORDER:
        if n == 'pad56':
            cols.append(jnp.zeros((D_MODEL, 56), F32))
        else:
            o, wd = _REF_OFF[n]
            cols.append(w[:, o:o + wd])
    lw = {'w_in': jnp.concatenate(cols, axis=1).astype(BF16), 'g_in': g_in[i][None, :]}
    wt = [w[:, _REF_OFF[n][0]:_REF_OFF[n][0] + _REF_OFF[n][1]] for n in ('d_v', 'c_v', 'c_iw', 'd_f')]
    wt.append(jnp.zeros((D_MODEL, XT_ROWS - XT_DF - N_HEADS), F32))
    lw['w_t'] = jnp.concatenate(wt, axis=1).T.astype(BF16)
    lw['b_col'] = jnp.concatenate([d_b_f[i], jnp.zeros((8 - N_HEADS,), F32)])[:, None]
    g = g_qk[i]
    sc = HEAD_DIM ** -0.5
    rows = [jnp.tile(g[0], 4) * sc, jnp.tile(g[1], 4), jnp.tile(g[2], 4) * (sc * LOG2E),
            jnp.tile(g[4], 4) * (sc * LOG2E), jnp.tile(g[5], 4)]
    lw['g4'] = jnp.stack(rows + [jnp.zeros((256,), F32)] * 3)
    lw['hseg'] = _seg_matrix(tuple((h * 64, 64) for h in range(4)))
    lw['segq'] = _seg_matrix(_PAIR_SEGS)
    lw['g_cq'] = b_g_cq[i][None, :]
    lw['g_ckv'] = b_g_ckv[i][None, :]
    qscale = (MLA_NOPE + MLA_ROPE) ** -0.5 * LOG2E
    hw = MLA_NOPE + MLA_ROPE
    wuq, gqb, wuk, gkb, wuv = [], [], [], [], []
    e_kr = np.zeros((128, 512), np.float32)
    for h in range(N_HEADS):
        wuq.append(b_w_uq[i][:, h * hw:(h + 1) * hw])
        gqb += [g[6] * qscale, g_rope[i][0] * qscale]
        srck = h * (MLA_NOPE + HEAD_DIM)
        wuk += [b_w_ukv[i][:, srck:srck + MLA_NOPE], jnp.zeros((MLA_KV_LORA, MLA_ROPE), F32)]
        gkb += [g[7], jnp.zeros((MLA_ROPE,), F32)]
        wuv.append(b_w_ukv[i][:, srck + MLA_NOPE:srck + MLA_NOPE + HEAD_DIM])
        e_kr[np.arange(MLA_ROPE), _pair_cols(h) + MLA_NOPE + np.arange(MLA_ROPE)] = 1.0
        if h % 2 == 1:
            wuq.append(jnp.zeros((MLA_Q_LORA, 256 - 2 * hw), F32))
            wuk.append(jnp.zeros((MLA_KV_LORA, 256 - 2 * hw), F32))
            gqb.append(jnp.zeros((256 - 2 * hw,), F32))
            gkb.append(jnp.zeros((256 - 2 * hw,), F32))
    lw['w_uq'] = jnp.concatenate(wuq, axis=1).astype(BF16)
    lw['g_qb'] = jnp.concatenate(gqb)[None, :]
    lw['w_uk'] = jnp.concatenate(wuk, axis=1).astype(BF16)
    lw['g_kb'] = jnp.concatenate(gkb)[None, :]
    lw['e_kr'] = jnp.asarray(e_kr, BF16)
    lw['w_uvt'] = jnp.concatenate(wuv, axis=1).T.astype(BF16)
    one = jnp.ones((128,), F32)
    lw['srow'] = jnp.stack([jnp.concatenate([g[3], one[HEAD_DIM:]]), jnp.concatenate([g_rope[i][1], one[MLA_ROPE:]])]
                           + [one] * 6)
    ab = a_rel_bias[i]
    ext = jnp.concatenate([jnp.broadcast_to(ab[:, 2 * REL_CLIP:], (N_HEADS, A_WIN - REL_CLIP - 1)),
                           ab[:, REL_CLIP - CHUNK + 1:][:, ::-1]], axis=1)
    lw['a_bias'] = jnp.stack([ext[:, CHUNK - 1 - q:CHUNK - 1 - q + A_WIN] for q in range(CHUNK)],
                             axis=1).reshape(N_HEADS * CHUNK, A_WIN)
    lw['w_out'] = w_out[i].astype(BF16)
    lw['g_ple'] = g_ple[i][None, :]
    lw['w_ple_gate'] = w_ple_gate[i].astype(BF16)
    lw['w_ple_proj'] = w_ple_proj[i].astype(BF16)
    return lw


def _cos_sin(pos, rot_dim):
    half = rot_dim // 2
    inv = jnp.float32(ROPE_THETA) ** (-jnp.arange(half, dtype=F32) / half)
    ang = pos.astype(F32)[:, None] * inv[None, :]
    return [jnp.cos(ang), jnp.sin(ang)]


def _rot_placement():
    src = {MLA_ROPE: 1, ROT_DIM: 1 + MLA_ROPE}
    e = np.zeros((1 + MLA_ROPE + ROT_DIM, TAB_W), np.float32)
    sections = ((TAB_QB, 512, [_pair_cols(h) + MLA_NOPE for h in range(N_HEADS)], MLA_ROPE),
                (TAB_CQ, 256, [h * HEAD_DIM for h in range(N_HEADS)], ROT_DIM),
                (TAB_S1, 128, [0], ROT_DIM), (TAB_S2, 128, [0], MLA_ROPE))
    for base, width, starts, rot_dim in sections:
        half = rot_dim // 2
        e[0, base:base + width] = 1.0
        for st in starts:
            for i in range(half):
                c, s = src[rot_dim] + i, src[rot_dim] + half + i
                e[0, base + st + i] = e[0, base + st + half + i] = 0.0
                e[c, base + st + i] = e[c, base + st + half + i] = 1.0
                e[s, base + width + st + half + i] = 1.0
                e[s, base + 2 * width + st + i] = -1.0
    return jnp.asarray(e)


def _rot_tables(pos):
    src = jnp.concatenate([jnp.ones((pos.shape[0], 1), F32)] + _cos_sin(pos, MLA_ROPE) + _cos_sin(pos, ROT_DIM),
                          axis=1)
    hi = src.astype(BF16)
    lo = (src - hi.astype(F32)).astype(BF16)
    place = _rot_placement().astype(BF16)
    return jnp.dot(jnp.concatenate([hi, lo], axis=1), jnp.concatenate([place, place], axis=0),
                   preferred_element_type=F32)


def _pad_axis(a, size, axis):
    if a.shape[axis] == size:
        return a
    pads = [(0, 0)] * a.ndim
    pads[axis] = (0, size - a.shape[axis])
    return jnp.pad(a, pads)


def _cat_rows(past, new, rows):
    a = new if past is None else jnp.concatenate([past.astype(new.dtype), new], axis=1)
    return _pad_axis(a, rows, 1)


def _cat_lanes(past_t, new_t, lanes):
    a = new_t if past_t is None else jnp.concatenate([past_t.astype(new_t.dtype), new_t], axis=2)
    return _pad_axis(a, lanes, 2)


def _value_blocks(vt, tk):
    b, r, lp = vt.shape
    h = r // HEAD_DIM
    v = vt.astype(BF16).reshape(b, h, HEAD_DIM, lp)
    v = jnp.concatenate([v, jnp.ones((b, h, V_ROWS - HEAD_DIM, lp), BF16)], axis=2)
    return v.reshape(b, h, V_ROWS, lp // tk, tk).transpose(0, 3, 1, 2, 4)


def _layer(x, p_all, layer, past, lw, tab, prev_narrow, *, fold_batch, tm, tq, tq_c, tk, a_group):
    b, t, _ = x.shape
    p_len = 0 if past is None else past[2].shape[1]
    l_len = p_len + t
    tqp = -(-t // tq) * tq
    tqp_c = -(-t // tq_c) * tq_c
    lp = -(-(p_len + max(tqp, tqp_c)) // tk) * tk

    xin = x.reshape(1, b * t, D_MODEL) if fold_batch else x
    a_direct = past is None and not fold_batch
    a_front = -(-(CHUNK + A_BAND) // tm) * tm if a_direct else 0
    outs = list(_inproj_call(xin, tab, lw, tm, a_front, prev_narrow))
    narrow = tuple(outs[o] for o in NARROW_OUTS)
    for o in NARROW_OUTS:
        outs[o] = None
    if fold_batch:
        xt = outs[18][0].reshape(XT_ROWS, b, t).transpose(1, 0, 2)
        outs = ([None if o is None else o.reshape((b, t) + o.shape[2:]) for o in outs[:18]]
                + [xt, outs[19].reshape(b, t, 1024), None, None] + [o.reshape(b, t, 256) for o in outs[22:24]])
    (aq, akb, avb, akf, avf, qb, ckvf, cq, ciq, c_k, c_v, s2, b_kr, c_ik, dq, dkb, dkf, dvf, xt, gates,
     dva, cva, krep_new, ikrep_new) = outs
    direct_values = past is None and not fold_batch and tm == tk

    lft = xt[:, XT_DF:XT_ROWS]
    d_lf = lft[:, :N_HEADS, :].transpose(0, 2, 1)
    state = (akf.reshape(b, t, N_HEADS, HEAD_DIM), avf.reshape(b, t, N_HEADS, HEAD_DIM), ckvf, b_kr, c_k, c_v, c_ik,
             dkf, dvf, d_lf)

    if past is None:
        pa_k = pa_v = pb_ckv = pb_kr = pc_k = pc_v = pc_ik = pd_k = pd_v = pd_lf = None
        pa = 0
    else:
        pa_k, pa_v, pb_ckv, pb_kr, pc_k, pc_v, pc_ik, pd_k, pd_v, pd_lf = past
        pa = pa_k.shape[1]
        pa_k = pa_k.reshape(b, pa, 256)
        pa_v = pa_v.reshape(b, pa, 256)
        pd_k = pd_k.reshape(b, p_len, 256)
        pd_v = pd_v.reshape(b, p_len, 256)

    tp = -(-t // (CHUNK * a_group)) * (CHUNK * a_group)
    if a_direct:
        assert tp == t
        a_keys, a_vals = akb, avb
        row0, front = a_front - (CHUNK + A_BAND), a_front
    else:
        front = CHUNK + A_BAND - pa
        a_rows = CHUNK + A_BAND + tp

        def band_src(pst, new):
            parts = [jnp.zeros((b, front, 256), BF16)]
            if pst is not None:
                parts.append(pst.astype(BF16))
            parts.append(new)
            return _pad_axis(jnp.concatenate(parts, axis=1), a_rows, 1)

        a_keys, a_vals, row0 = band_src(pa_k, akb), band_src(pa_v, avb), 0
    ya = _attn_a_call(_pad_axis(aq, tp, 1), a_keys, a_vals, lw['a_bias'], _pad_axis(gates, tp, 1), row0, front,
                      front + pa + t, a_group)[:, :t]

    gates_q = _pad_axis(gates, tqp, 1)

    ckv_all = _cat_rows(pb_ckv, ckvf, lp)
    kr_new = s2
    kr_past = None if past is None else _pad_axis(pb_kr, 128, 2)
    kr_all = _cat_rows(kr_past, kr_new, lp)
    kb, vba = _bkv_call(ckv_all, kr_all, lw, tk)
    yb = _attn_b_call(_pad_axis(qb, tqp, 1), kb, vba, gates_q, p_len, l_len, tq, tk)[:, :t]

    krep = _cat_rows(None if past is None else jnp.tile(pc_k.astype(BF16), (1, 1, N_HEADS)), krep_new, lp)
    ikrep = _cat_rows(None if past is None else jnp.tile(pc_ik.astype(BF16), (1, 1, IDX_HEADS)), ikrep_new, lp)
    if not direct_values:
        cva = _value_blocks(_cat_lanes(None if past is None else pc_v.transpose(0, 2, 1), xt[:, XT_CV:XT_IW], lp), tk)
    iwt = _pad_axis(xt[:, XT_IW:XT_DF], tqp_c, 2)
    k_sel = min(DSA_TOPK, l_len // 4)
    yc = _attn_c_call(_pad_axis(cq, tqp_c, 1), _pad_axis(ciq, tqp_c, 1), iwt, krep, cva, ikrep,
                      _pad_axis(gates, tqp_c, 1), p_len, l_len, k_sel, tq_c, tk)[:, :t]

    kd = _cat_rows(None if past is None else pd_k.astype(BF16), dkb, lp)
    if not direct_values:
        dva = _value_blocks(_cat_lanes(None if past is None else pd_v.transpose(0, 2, 1), xt[:, XT_DV:XT_CV], lp), tk)
    lf_past = None if past is None else _pad_axis(pd_lf.transpose(0, 2, 1), 8, 1)
    lf_all = _cat_lanes(lf_past, lft, lp)
    yd = _attn_d_call(_pad_axis(dq, tqp, 1), kd, dva, lf_all, gates_q, p_len, tq, tk)[:, :t]

    if fold_batch:
        x_new = _out_call([y.reshape(1, b * t, 256) for y in (ya, yb, yc, yd)], x.reshape(1, b * t, D_MODEL),
                          p_all.reshape(p_all.shape[0], 1, b * t, PLE_DIM), layer, lw, tm).reshape(b, t, D_MODEL)
    else:
        x_new = _out_call([ya, yb, yc, yd], x, p_all, layer, lw, 4 * tm)
    return x_new, state, narrow


def kernel(x_prompt, x_sample, cache_a_k, cache_a_v, cache_b_ckv, cache_b_krope, cache_c_k, cache_c_v, cache_c_idx_k,
           cache_d_k, cache_d_v, cache_d_logf, p_prompt, p_sample, g_in, w_in, g_qk, g_rope, a_rel_bias, b_g_cq,
           b_w_uq, b_g_ckv, b_w_ukv, d_b_f, w_out, g_ple, w_ple_gate, w_ple_proj):
    depth = w_in.shape[0]
    b, t = x_prompt.shape[:2]
    bs, ts = x_sample.shape[:2]
    past_len = cache_b_ckv.shape[2]
    tab_p = _rot_tables(jnp.arange(t))
    tab_s = jnp.tile(_rot_tables(past_len + jnp.arange(ts)), (bs, 1))
    xp, xs = x_prompt, x_sample
    states_p, states_s = [], []
    narrow_p = narrow_s = None
    for i in range(depth):
        lw = _prep_layer(i, g_in, w_in, g_qk, g_rope, a_rel_bias, b_g_cq, b_w_uq, b_g_ckv, b_w_ukv, d_b_f, w_out,
                         g_ple, w_ple_gate, w_ple_proj)
        xp, st_p, narrow_p = _layer(xp, p_prompt, i, None, lw, tab_p, narrow_p, fold_batch=False, tm=256, tq=512,
                                     tq_c=512, tk=256, a_group=32)
        past = (cache_a_k[i], cache_a_v[i], cache_b_ckv[i], cache_b_krope[i], cache_c_k[i], cache_c_v[i],
                cache_c_idx_k[i], cache_d_k[i], cache_d_v[i], cache_d_logf[i])
        xs, st_s, narrow_s = _layer(xs, p_sample, i, past, lw, tab_s, narrow_s, fold_batch=True, tm=bs * ts, tq=128,
                                     tq_c=128, tk=256, a_group=1)
        states_p.append(st_p)
        states_s.append(st_s)
    keep = min(A_BAND, t)
    states_p = [(st[0][:, t - keep:], st[1][:, t - keep:]) + tuple(st[2:]) for st in states_p]

    def assemble(states, narrow, bb, tt):
        out = [None if z[0] is None else jnp.stack(z) for z in zip(*states)]
        c_k, c_v, b_kr, c_ik, d_k, d_v = (a.reshape(depth, bb, tt, a.shape[-1]) for a in narrow)
        out[3], out[4], out[5], out[6] = b_kr, c_k, c_v, c_ik
        out[7] = d_k.reshape(depth, bb, tt, N_HEADS, HEAD_DIM)
        out[8] = d_v.reshape(depth, bb, tt, N_HEADS, HEAD_DIM)
        return out

    return (xp, xs, *assemble(states_p, narrow_p, b, t), *assemble(states_s, narrow_s, bs, ts))
```
